```python
import math
import jax
import jax.numpy as jnp
from jax import lax
import numpy as np

D_MODEL = 1024
BATCH = 8
SEQ = 2048
DEPTH = 2

MEM_LEN = 256
N_MIXERS = 4
GROUP_WIDTH = D_MODEL // 4
MIX_WIDTH = N_MIXERS * GROUP_WIDTH
HEAD_DIM = 64
N_HEADS_GROUP = GROUP_WIDTH // HEAD_DIM
NORM_EPS = 1e-6
CONV_WIDTH = 4
LRU_BLOCKS = N_HEADS_GROUP
LRU_C = 8.0
SB_QBLOCK = 128
SSM_HEADS = N_HEADS_GROUP
SSM_HEAD_DIM = HEAD_DIM
SSM_INNER = SSM_HEADS * SSM_HEAD_DIM
SSM_NGROUPS = 2
SSM_STATE = 128
SSM_CHUNK = 128
SSM_CONV_CH = SSM_INNER + 2 * SSM_NGROUPS * SSM_STATE
MOBA_BLOCK = 256
MOBA_TOPK = 3
MOBA_QCHUNK = 32
ROPE_THETA = 10000.0
XATTN_HEADS = 4
XATTN_HEAD_DIM = D_MODEL // XATTN_HEADS
MOE_GROUPS = 4
MOE_EXPERTS_PER_GROUP = 4
MOE_EXPERTS = MOE_GROUPS * MOE_EXPERTS_PER_GROUP
MOE_TOPK = 2
MOE_FF = D_MODEL // 4
IN_SPLITS = (GROUP_WIDTH, GROUP_WIDTH, 3 * GROUP_WIDTH, SSM_INNER, SSM_CONV_CH, SSM_HEADS, 3 * GROUP_WIDTH)
IN_WIDTH = sum(IN_SPLITS)

kernel_name = 'hybrid_hymba_rglru_sb_ssd_moba_hmoe'


def rmsnorm(x, g):
    xf = x.astype(jnp.float32)
    y = xf * lax.rsqrt(jnp.mean(xf * xf, axis=-1, keepdims=True) + NORM_EPS)
    return (y * g.astype(jnp.float32)).astype(x.dtype)


def causal_depthwise_conv(x, w, b):
    y = lax.conv_general_dilated(x, w[:, None, :].astype(x.dtype), window_strides=(1,),
                                 padding=[(CONV_WIDTH - 1, 0)],
                                 dimension_numbers=('NWC', 'WIO', 'NWC'),
                                 feature_group_count=x.shape[-1])
    return y + b.astype(x.dtype)


def rope(x, positions):
    half = x.shape[-1] // 2
    inv_freq = ROPE_THETA ** (-jnp.arange(half, dtype=jnp.float32) / half)
    ang = positions.astype(jnp.float32)[:, None] * inv_freq[None, :]
    cos = jnp.cos(ang)[None, :, None, :].astype(x.dtype)
    sin = jnp.sin(ang)[None, :, None, :].astype(x.dtype)
    x1, x2 = x[..., :half], x[..., half:]
    return jnp.concatenate([x1 * cos - x2 * sin, x2 * cos + x1 * sin], axis=-1)


def rglru_mixer(xb, gate, conv_w, conv_b, wr, br, wi, bi, lam):
    B, S, W = xb.shape
    xc = causal_depthwise_conv(xb, conv_w, conv_b)
    xh = xc.reshape(B, S, LRU_BLOCKS, W // LRU_BLOCKS)
    r = jax.nn.sigmoid(jnp.einsum('bshi,hij->bshj', xh, wr).reshape(B, S, W) + br).astype(jnp.float32)
    i = jax.nn.sigmoid(jnp.einsum('bshi,hij->bshj', xh, wi).reshape(B, S, W) + bi)
    log_a = LRU_C * r * jax.nn.log_sigmoid(lam.astype(jnp.float32))
    a = jnp.exp(log_a)
    u = jnp.sqrt(-jnp.expm1(2.0 * log_a)) * (i * xc).astype(jnp.float32)

    def combine(left, right):
        a_l, b_l = left
        a_r, b_r = right
        return a_l * a_r, a_r * b_l + b_r

    _, h = lax.associative_scan(combine, (a, u), axis=1)
    return h.astype(xb.dtype) * jax.nn.gelu(gate)


def stick_breaking_attention(q, k, v):
    B, S, H, Dh = q.shape
    scale = Dh ** -0.5
    q = q.transpose(0, 2, 1, 3)
    k = k.transpose(0, 2, 1, 3)
    v = v.transpose(0, 2, 1, 3)
    outs = []
    for blk in range(S // SB_QBLOCK):
        t0 = blk * SB_QBLOCK
        t1 = t0 + SB_QBLOCK
        z = jnp.einsum('bhtd,bhsd->bhts', q[:, :, t0:t1], k[:, :, :t1]).astype(jnp.float32) * scale
        past = jnp.arange(t1)[None, :] < jnp.arange(t0, t1)[:, None]
        log_fail = jnp.where(past, jax.nn.log_sigmoid(-z), 0.0)
        after = lax.cumsum(log_fail, axis=3, reverse=True) - log_fail
        w = jnp.where(past, jnp.exp(jax.nn.log_sigmoid(z) + after), 0.0)
        outs.append(jnp.einsum('bhts,bhsd->bhtd', w.astype(v.dtype), v[:, :, :t1]))
    o = jnp.concatenate(outs, axis=2)
    return o.transpose(0, 2, 1, 3).reshape(B, S, H * Dh)


def segsum(x):
    T = x.shape[-1]
    xr = jnp.broadcast_to(x[..., :, None], x.shape + (T,))
    strict = jnp.tril(jnp.ones((T, T), dtype=bool), -1)
    cs = jnp.cumsum(jnp.where(strict, xr, 0.0), axis=-2)
    return jnp.where(jnp.tril(jnp.ones((T, T), dtype=bool)), cs, -jnp.inf)


def ssd_chunked(xs, dt, A, Bm, Cm):
    Bsz, S, H, P = xs.shape
    N = Bm.shape[-1]
    nc = S // SSM_CHUNK
    Xc = (xs * dt[..., None]).reshape(Bsz, nc, SSM_CHUNK, H, P)
    Ac = (dt * A).reshape(Bsz, nc, SSM_CHUNK, H).transpose(0, 3, 1, 2)
    Bc = Bm.reshape(Bsz, nc, SSM_CHUNK, H, N)
    Cc = Cm.reshape(Bsz, nc, SSM_CHUNK, H, N)
    A_cs = jnp.cumsum(Ac, axis=-1)
    Lmat = jnp.exp(segsum(Ac))
    y_diag = jnp.einsum('bhcls,bcshp->bclhp', jnp.einsum('bclhn,bcshn->bhcls', Cc, Bc) * Lmat, Xc)
    decay_states = jnp.exp(A_cs[..., -1:] - A_cs)
    states = jnp.einsum('bclhn,bhcl,bclhp->bchpn', Bc, decay_states, Xc)
    states = jnp.concatenate([jnp.zeros_like(states[:, :1]), states], axis=1)
    decay_chunk = jnp.exp(segsum(jnp.pad(A_cs[..., -1], ((0, 0), (0, 0), (1, 0)))))
    prev_states = jnp.einsum('bhzc,bchpn->bzhpn', decay_chunk, states)[:, :-1]
    y_off = jnp.einsum('bclhn,bchpn,bhcl->bclhp', Cc, prev_states, jnp.exp(A_cs))
    return (y_diag + y_off).reshape(Bsz, S, H, P)


def mamba2_mixer(z, xbc, dt_raw, conv_w, conv_b, dt_bias, a_log, d_skip):
    B, S, _ = z.shape
    f32 = jnp.float32
    xbc = jax.nn.silu(causal_depthwise_conv(xbc, conv_w, conv_b))
    xs, bm, cm = jnp.split(xbc, [SSM_INNER, SSM_INNER + SSM_NGROUPS * SSM_STATE], axis=-1)
    xs = xs.reshape(B, S, SSM_HEADS, SSM_HEAD_DIM).astype(f32)
    rep = SSM_HEADS // SSM_NGROUPS
    bm = jnp.repeat(bm.reshape(B, S, SSM_NGROUPS, SSM_STATE).astype(f32), rep, axis=2)
    cm = jnp.repeat(cm.reshape(B, S, SSM_NGROUPS, SSM_STATE).astype(f32), rep, axis=2)
    dt = jax.nn.softplus(dt_raw.astype(f32) + dt_bias.astype(f32))
    A = -jnp.exp(a_log.astype(f32))
    y = ssd_chunked(xs, dt, A, bm, cm) + d_skip.astype(f32)[:, None] * xs
    return y.reshape(B, S, SSM_INNER).astype(z.dtype) * jax.nn.silu(z)


def moba_attention(q, k, v):
    B, S, H, Dh = q.shape
    pos = jnp.arange(S)
    q = rope(q, pos).transpose(0, 2, 1, 3)
    k = rope(k, pos).transpose(0, 2, 1, 3)
    v = v.transpose(0, 2, 1, 3)
    nb = -(-S // MOBA_BLOCK)
    pad = nb * MOBA_BLOCK - S
    kb = jnp.pad(k, ((0, 0), (0, 0), (0, pad), (0, 0))).reshape(B, H, nb, MOBA_BLOCK, Dh)
    vb = jnp.pad(v, ((0, 0), (0, 0), (0, pad), (0, 0))).reshape(B, H, nb, MOBA_BLOCK, Dh)
    k_mean = jnp.mean(kb, axis=3)
    gate = jnp.einsum('bhtd,bhnd->bhtn', q, k_mean).astype(jnp.float32)
    fully_past = jnp.arange(nb)[None, :] < (pos // MOBA_BLOCK)[:, None]
    gate = jnp.where(fully_past, gate, -jnp.inf)
    n_sel = max(1, min(MOBA_TOPK, nb - 1))
    _, sel = lax.top_k(gate, n_sel)
    n_chunks = S // MOBA_QCHUNK
    q_c = q.reshape(B, H, n_chunks, MOBA_QCHUNK, Dh).transpose(2, 0, 1, 3, 4)
    sel_c = sel.reshape(B, H, n_chunks, MOBA_QCHUNK, n_sel).transpose(2, 0, 1, 3, 4)
    b_ix = jnp.arange(B)[:, None, None, None]
    h_ix = jnp.arange(H)[None, :, None, None]
    scale = Dh ** -0.5

    def attend_chunk(args):
        ci, qc, sc = args
        t0 = ci * MOBA_QCHUNK
        own = t0 // MOBA_BLOCK
        k_sel = kb[b_ix, h_ix, sc]
        v_sel = vb[b_ix, h_ix, sc]
        s_past = jnp.einsum('bhtd,bhtnjd->bhtnj', qc, k_sel).astype(jnp.float32) * scale
        s_past = jnp.where((sc < own)[..., None], s_past, -jnp.inf).reshape(B, H, MOBA_QCHUNK, n_sel * MOBA_BLOCK)
        k_own = lax.dynamic_index_in_dim(kb, own, axis=2, keepdims=False)
        v_own = lax.dynamic_index_in_dim(vb, own, axis=2, keepdims=False)
        s_own = jnp.einsum('bhtd,bhjd->bhtj', qc, k_own).astype(jnp.float32) * scale
        causal = (own * MOBA_BLOCK + jnp.arange(MOBA_BLOCK))[None, :] <= (t0 + jnp.arange(MOBA_QCHUNK))[:, None]
        s_own = jnp.where(causal, s_own, -jnp.inf)
        p = jax.nn.softmax(jnp.concatenate([s_past, s_own], axis=-1), axis=-1).astype(v.dtype)
        p_past = p[..., :n_sel * MOBA_BLOCK].reshape(B, H, MOBA_QCHUNK, n_sel, MOBA_BLOCK)
        return (jnp.einsum('bhtnj,bhtnjd->bhtd', p_past, v_sel)
                + jnp.einsum('bhtj,bhjd->bhtd', p[..., n_sel * MOBA_BLOCK:], v_own))

    o = lax.map(attend_chunk, (jnp.arange(n_chunks), q_c, sel_c))
    return o.transpose(1, 0, 3, 2, 4).reshape(B, S, H * Dh)


def hybrid_mixer(h, w_in, lru_conv_w, lru_conv_b, lru_wr, lru_br, lru_wi, lru_bi, lru_lambda,
                 ssm_conv_w, ssm_conv_b, ssm_dt_bias, ssm_a_log, ssm_d, group_norm_g, w_out):
    B, S, _ = h.shape
    proj = h @ w_in
    offs = [int(o) for o in np.cumsum(IN_SPLITS)[:-1]]
    lru_x, lru_gate, sb_qkv, ssm_z, ssm_xbc, ssm_dt, mb_qkv = jnp.split(proj, offs, axis=-1)

    def heads(t):
        return t.reshape(B, S, N_HEADS_GROUP, HEAD_DIM)

    y_a = rglru_mixer(lru_x, lru_gate, lru_conv_w, lru_conv_b, lru_wr, lru_br, lru_wi, lru_bi, lru_lambda)
    sq, sk, sv = jnp.split(sb_qkv, 3, axis=-1)
    y_b = stick_breaking_attention(heads(sq), heads(sk), heads(sv))
    y_c = mamba2_mixer(ssm_z, ssm_xbc, ssm_dt, ssm_conv_w, ssm_conv_b, ssm_dt_bias, ssm_a_log, ssm_d)
    mq, mk, mv = jnp.split(mb_qkv, 3, axis=-1)
    y_d = moba_attention(heads(mq), heads(mk), heads(mv))
    y = jnp.concatenate([y_a, y_b, y_c, y_d], axis=-1).reshape(B, S, N_MIXERS, GROUP_WIDTH)
    y = rmsnorm(y, group_norm_g.reshape(N_MIXERS, GROUP_WIDTH)).reshape(B, S, MIX_WIDTH)
    return y @ w_out


def memory_cross_attention(h, m, wq, wkv, wo):
    B, S, D = h.shape
    M = m.shape[1]
    q = (h @ wq).reshape(B, S, XATTN_HEADS, XATTN_HEAD_DIM)
    k, v = jnp.split(m @ wkv, 2, axis=-1)
    k = k.reshape(B, M, XATTN_HEADS, XATTN_HEAD_DIM)
    v = v.reshape(B, M, XATTN_HEADS, XATTN_HEAD_DIM)
    s = jnp.einsum('bshd,bmhd->bhsm', q, k).astype(jnp.float32) * XATTN_HEAD_DIM ** -0.5
    p = jax.nn.softmax(s, axis=-1).astype(v.dtype)
    o = jnp.einsum('bhsm,bmhd->bshd', p, v).reshape(B, S, D)
    return o @ wo


def hierarchical_moe(h, wg, bg, we, be, w1, w3, w2):
    B, S, D = h.shape
    t = h.reshape(B * S, D)
    f32 = jnp.float32
    pg = jax.nn.softmax((t @ wg).astype(f32) + bg.astype(f32), axis=-1)
    pg_top, g_idx = lax.top_k(pg, 1)
    g_onehot = jax.nn.one_hot(g_idx[:, 0], MOE_GROUPS, dtype=f32)
    le = ((t @ we).astype(f32) + be.astype(f32)).reshape(-1, MOE_GROUPS, MOE_EXPERTS_PER_GROUP)
    pe = jax.nn.softmax(jnp.einsum('tg,tge->te', g_onehot, le), axis=-1)
    pe_top, e_idx = lax.top_k(pe, MOE_TOPK)
    pe_top = pe_top / jnp.sum(pe_top, axis=-1, keepdims=True)
    w_group = jnp.einsum('tk,tke->te', pe_top, jax.nn.one_hot(e_idx, MOE_EXPERTS_PER_GROUP, dtype=f32))
    comb = (pg_top[:, :, None] * g_onehot[:, :, None] * w_group[:, None, :]).reshape(-1, MOE_EXPERTS)
    hid = jax.nn.silu(jnp.einsum('td,ndf->tnf', t, w1)) * jnp.einsum('td,ndf->tnf', t, w3)
    y = jnp.einsum('tnf,nfd->td', hid * comb.astype(t.dtype)[:, :, None], w2)
    return y.reshape(B, S, D)


def setup_inputs(seed: int = 0) -> dict:
    key = jax.random.key(seed)
    kit = iter(jax.random.split(key, 40))
    f32 = jnp.float32
    L, D = DEPTH, D_MODEL

    def nrm(shape, scale):
        return jax.random.normal(next(kit), shape, f32) * scale

    def gain(shape):
        return 1.0 + 0.02 * jax.random.normal(next(kit), shape, f32)

    x = nrm((BATCH, SEQ, D), 1.0)
    mem = nrm((BATCH, MEM_LEN, D), 1.0)
    mix_norm_g = gain((L, D))
    w_in = nrm((L, D, IN_WIDTH), D ** -0.5)
    lru_conv_w = nrm((L, CONV_WIDTH, GROUP_WIDTH), CONV_WIDTH ** -0.5)
    lru_conv_b = nrm((L, GROUP_WIDTH), 0.01)
    blk = GROUP_WIDTH // LRU_BLOCKS
    lru_wr = nrm((L, LRU_BLOCKS, blk, blk), blk ** -0.5)
    lru_br = nrm((L, GROUP_WIDTH), 0.1)
    lru_wi = nrm((L, LRU_BLOCKS, blk, blk), blk ** -0.5)
    lru_bi = nrm((L, GROUP_WIDTH), 0.1)
    a0 = jax.random.uniform(next(kit), (L, GROUP_WIDTH), f32, 0.9, 0.999) ** (1.0 / LRU_C)
    lru_lambda = jnp.log(a0) - jnp.log1p(-a0)
    ssm_conv_w = nrm((L, CONV_WIDTH, SSM_CONV_CH), CONV_WIDTH ** -0.5)
    ssm_conv_b = nrm((L, SSM_CONV_CH), 0.01)
    dt0 = jnp.exp(jax.random.uniform(next(kit), (L, SSM_HEADS), f32, math.log(1e-3), math.log(1e-1)))
    ssm_dt_bias = dt0 + jnp.log(-jnp.expm1(-dt0))
    ssm_a_log = jnp.log(jax.random.uniform(next(kit), (L, SSM_HEADS), f32, 1.0, 16.0))
    ssm_d = 1.0 + nrm((L, SSM_HEADS), 0.1)
    group_norm_g = gain((L, MIX_WIDTH))
    w_out = nrm((L, MIX_WIDTH, D), 0.5 * MIX_WIDTH ** -0.5)
    xattn_norm_g = gain((L, D))
    mem_norm_g = gain((L, D))
    xattn_wq = nrm((L, D, D), D ** -0.5)
    xattn_wkv = nrm((L, D, 2 * D), D ** -0.5)
    xattn_wo = nrm((L, D, D), 0.5 * D ** -0.5)
    ffn_norm_g = gain((L, D))
    router_group_w = nrm((L, D, MOE_GROUPS), D ** -0.5)
    router_group_b = nrm((L, MOE_GROUPS), 0.01)
    router_expert_w = nrm((L, D, MOE_EXPERTS), D ** -0.5)
    router_expert_b = nrm((L, MOE_EXPERTS), 0.01)
    expert_w1 = nrm((L, MOE_EXPERTS, D, MOE_FF), D ** -0.5)
    expert_w3 = nrm((L, MOE_EXPERTS, D, MOE_FF), D ** -0.5)
    expert_w2 = nrm((L, MOE_EXPERTS, MOE_FF, D), 0.5 * MOE_FF ** -0.5)
    final_norm_g = gain((D,))
    return {'x': x, 'mem': mem, 'mix_norm_g': mix_norm_g, 'w_in': w_in,
            'lru_conv_w': lru_conv_w, 'lru_conv_b': lru_conv_b, 'lru_wr': lru_wr, 'lru_br': lru_br,
            'lru_wi': lru_wi, 'lru_bi': lru_bi, 'lru_lambda': lru_lambda,
            'ssm_conv_w': ssm_conv_w, 'ssm_conv_b': ssm_conv_b, 'ssm_dt_bias': ssm_dt_bias,
            'ssm_a_log': ssm_a_log, 'ssm_d': ssm_d, 'group_norm_g': group_norm_g, 'w_out': w_out,
            'xattn_norm_g': xattn_norm_g, 'mem_norm_g': mem_norm_g, 'xattn_wq': xattn_wq,
            'xattn_wkv': xattn_wkv, 'xattn_wo': xattn_wo, 'ffn_norm_g': ffn_norm_g,
            'router_group_w': router_group_w, 'router_group_b': router_group_b,
            'router_expert_w': router_expert_w, 'router_expert_b': router_expert_b,
            'expert_w1': expert_w1, 'expert_w3': expert_w3, 'expert_w2': expert_w2,
            'final_norm_g': final_norm_g}


def reference(x, mem, mix_norm_g, w_in, lru_conv_w, lru_conv_b, lru_wr, lru_br, lru_wi, lru_bi,
              lru_lambda, ssm_conv_w, ssm_conv_b, ssm_dt_bias, ssm_a_log, ssm_d, group_norm_g, w_out,
              xattn_norm_g, mem_norm_g, xattn_wq, xattn_wkv, xattn_wo, ffn_norm_g,
              router_group_w, router_group_b, router_expert_w, router_expert_b,
              expert_w1, expert_w3, expert_w2, final_norm_g):
    for l in range(DEPTH):
        h = rmsnorm(x, mix_norm_g[l])
        x = x + hybrid_mixer(h, w_in[l], lru_conv_w[l], lru_conv_b[l], lru_wr[l], lru_br[l],
                             lru_wi[l], lru_bi[l], lru_lambda[l], ssm_conv_w[l], ssm_conv_b[l],
                             ssm_dt_bias[l], ssm_a_log[l], ssm_d[l], group_norm_g[l], w_out[l])
        x = x + memory_cross_attention(rmsnorm(x, xattn_norm_g[l]), rmsnorm(mem, mem_norm_g[l]),
                                       xattn_wq[l], xattn_wkv[l], xattn_wo[l])
        x = x + hierarchical_moe(rmsnorm(x, ffn_norm_g[l]), router_group_w[l], router_group_b[l],
                                 router_expert_w[l], router_expert_b[l],
                                 expert_w1[l], expert_w3[l], expert_w2[l])
    return rmsnorm(x, final_norm_g)
```

```python
import functools
import math

import jax
import jax.numpy as jnp
from jax import lax
from jax.experimental import pallas as pl
from jax.experimental.pallas import tpu as pltpu

F32 = jnp.float32
BF16 = jnp.bfloat16

D_MODEL = 1024
GROUP_WIDTH = 256
HEAD_DIM = 64
N_HEADS = 4
NORM_EPS = 1e-6
CONV_WIDTH = 4
LRU_C = 8.0
SB_BLOCK = 128
SSM_CHUNK = 128
SSM_STATE = 128
MOBA_BLOCK = 256
MOBA_TOPK = 3
ROPE_THETA = 10000.0
XATTN_HEADS = 4
XATTN_HEAD_DIM = 256
MEM_LEN = 256
MOE_GROUPS = 4
MOE_EPG = 4
MOE_EXPERTS = 16
MOE_FF = 256
LANES = 128
NEG_BIG = -1e30
IN_OUT_WIDTHS = (512, 768, 256, 768, LANES, 768)
VMEM_LIMIT = 56 * 1024 * 1024


def _cparams(n_axes):
    return pltpu.CompilerParams(dimension_semantics=("arbitrary",) * n_axes,
                                vmem_limit_bytes=VMEM_LIMIT)


def _dot(a, b):
    return jnp.dot(a, b, preferred_element_type=F32)


def _dot_t(a, b):
    return lax.dot_general(a, b, (((1,), (1,)), ((), ())), preferred_element_type=F32)


def _dot_tl(a, b):
    return lax.dot_general(a, b, (((0,), (0,)), ((), ())), preferred_element_type=F32)


def _split2(x):
    hi = x.astype(BF16)
    lo = (x - hi.astype(F32)).astype(BF16)
    return hi, lo


def _split3(x):
    hi = x.astype(BF16)
    r = x - hi.astype(F32)
    mid = r.astype(BF16)
    lo = (r - mid.astype(F32)).astype(BF16)
    return hi, mid, lo


def _dot_wide_lhs(x, m_bf16, parts=3):
    pieces = _split3(x) if parts == 3 else _split2(x)
    out = _dot(pieces[0], m_bf16)
    for p in pieces[1:]:
        out = out + _dot(p, m_bf16)
    return out


def _rmsnorm(x, g):
    return x * lax.rsqrt(jnp.mean(x * x, axis=-1, keepdims=True) + NORM_EPS) * g


def _softplus(x):
    return jnp.maximum(x, 0.0) + jnp.log1p(jnp.exp(-jnp.abs(x)))


def _sigmoid(x):
    return 1.0 / (1.0 + jnp.exp(-x))


def _silu(x):
    return x * _sigmoid(x)


def _gelu_tanh(x):
    return 0.5 * x * (1.0 + jnp.tanh(math.sqrt(2.0 / math.pi) * (x + 0.044715 * (x * x * x))))


def _causal_conv(x, w_ref, b_ref):
    rows = lax.broadcasted_iota(jnp.int32, x.shape, 0)
    y = x * w_ref[CONV_WIDTH - 1:CONV_WIDTH, :] + b_ref[...]
    for s in range(1, CONV_WIDTH):
        xs = jnp.where(rows >= s, pltpu.roll(x, s, 0), 0.0)
        y = y + xs * w_ref[CONV_WIDTH - 1 - s:CONV_WIDTH - s, :]
    return y


def _inproj_kernel(x_ref, g_ref, w_ref, *o_refs):
    h = _rmsnorm(x_ref[...], g_ref[...]).astype(BF16)
    off = 0
    for o_ref, width in zip(o_refs, IN_OUT_WIDTHS):
        o_ref[...] = _dot(h, w_ref[:, off:off + width])
        off += width


def _inproj(x2d, g, w_cat, tm=512):
    T = x2d.shape[0]
    n_in = w_cat.shape[1]
    return pl.pallas_call(
        _inproj_kernel,
        grid=(T // tm,),
        in_specs=[pl.BlockSpec((tm, D_MODEL), lambda i: (i, 0)),
                  pl.BlockSpec((1, D_MODEL), lambda i: (0, 0)),
                  pl.BlockSpec((D_MODEL, n_in), lambda i: (0, 0))],
        out_specs=[pl.BlockSpec((tm, w), lambda i: (i, 0)) for w in IN_OUT_WIDTHS],
        out_shape=[jax.ShapeDtypeStruct((T, w), F32) for w in IN_OUT_WIDTHS],
        compiler_params=_cparams(1),
        name="inproj",
    )(x2d, g, w_cat)


def _lru_kernel(xg_ref, cw_ref, cb_ref, wbd_ref, bri_ref, lam_ref, o_ref):
    S = xg_ref.shape[0]
    W = GROUP_WIDTH
    xc = _causal_conv(xg_ref[:, 0:W], cw_ref, cb_ref)
    ri = _dot(xc.astype(BF16), wbd_ref[...]) + bri_ref[...]
    r = _sigmoid(ri[:, 0:W])
    i = _sigmoid(ri[:, W:2 * W])
    log_a = (LRU_C * r) * (-_softplus(-lam_ref[...]))
    a = jnp.exp(log_a)
    u = jnp.sqrt(1.0 - jnp.exp(2.0 * log_a)) * (i * xc)
    rows = lax.broadcasted_iota(jnp.int32, (S, W), 0)
    shift = 1
    while shift < S:
        keep = rows >= shift
        a_s = jnp.where(keep, pltpu.roll(a, shift, 0), 1.0)
        u_s = jnp.where(keep, pltpu.roll(u, shift, 0), 0.0)
        u = a * u_s + u
        a = a * a_s
        shift *= 2
    o_ref[...] = u * _gelu_tanh(xg_ref[:, W:2 * W])


def _lru(xg, conv_w, conv_b, w_bd, b_ri, lam):
    B, S, _ = xg.shape
    W = GROUP_WIDTH
    full = lambda shape: pl.BlockSpec(shape, lambda b: (0,) * len(shape))
    return pl.pallas_call(
        _lru_kernel,
        grid=(B,),
        in_specs=[pl.BlockSpec((None, S, 2 * W), lambda b: (b, 0, 0)),
                  full((CONV_WIDTH, W)), full((1, W)), full((W, 2 * W)), full((1, 2 * W)), full((1, W))],
        out_specs=pl.BlockSpec((None, S, W), lambda b: (b, 0, 0)),
        out_shape=jax.ShapeDtypeStruct((B, S, W), F32),
        compiler_params=_cparams(1),
        name="rglru",
    )(xg, conv_w, conv_b, w_bd, b_ri, lam)


def _sb_kernel(qkv_ref, o_ref, k_s, v_s):
    i = pl.program_id(1)
    W = GROUP_WIDTH
    TB = SB_BLOCK

    @pl.when(i == 0)
    def _():
        k_s[...] = qkv_ref[:, W:2 * W].astype(BF16)
        v_s[...] = qkv_ref[:, 2 * W:3 * W].astype(BF16)

    q = qkv_ref[pl.ds(pl.multiple_of(i * TB, TB), TB), 0:W] * (HEAD_DIM ** -0.5)
    lane = lax.broadcasted_iota(jnp.int32, (TB, W), 1)
    r_loc = lax.broadcasted_iota(jnp.int32, (TB, TB), 0)
    c_loc = lax.broadcasted_iota(jnp.int32, (TB, TB), 1)
    ur = lax.broadcasted_iota(jnp.int32, (TB, 2 * TB), 0)
    uc = lax.broadcasted_iota(jnp.int32, (TB, 2 * TB), 1)
    tri_ones = jnp.where((ur > uc) | (uc >= TB), 1.0, 0.0).astype(BF16)

    out = jnp.zeros((TB, W), F32)
    for h in range(N_HEADS):
        head = (lane >= h * HEAD_DIM) & (lane < (h + 1) * HEAD_DIM)
        qh = jnp.where(head, q, 0.0).astype(BF16)

        def body(n, carry):
            acc, later = carry
            j = i - n
            rows = pl.ds(pl.multiple_of(j * TB, TB), TB)
            z = _dot_t(qh, k_s[rows, :])
            past = (j * TB + c_loc) < (i * TB + r_loc)
            sp = _softplus(z)
            lf = jnp.where(past, -sp, 0.0)
            cs = _dot_wide_lhs(lf, tri_ones, parts=2)
            logw = (z - sp) + cs[:, 0:TB] + later
            w = jnp.where(past, jnp.exp(logw), 0.0)
            acc = acc + _dot(w.astype(BF16), v_s[rows, :])
            return acc, later + cs[:, TB:2 * TB]

        acc, _ = lax.fori_loop(0, i + 1, body,
                               (jnp.zeros((TB, W), F32), jnp.zeros((TB, TB), F32)))
        out = jnp.where(head, acc, out)
    o_ref[...] = out


def _sb_attention(qkv):
    B, S, _ = qkv.shape
    W = GROUP_WIDTH
    return pl.pallas_call(
        _sb_kernel,
        grid=(B, S // SB_BLOCK),
        in_specs=[pl.BlockSpec((None, S, 3 * W), lambda b, i: (b, 0, 0))],
        out_specs=pl.BlockSpec((None, SB_BLOCK, W), lambda b, i: (b, i, 0)),
        out_shape=jax.ShapeDtypeStruct((B, S, W), F32),
        scratch_shapes=[pltpu.VMEM((S, W), BF16), pltpu.VMEM((S, W), BF16)],
        compiler_params=_cparams(2),
        name="stickbreak",
    )(qkv)


def _ssd_kernel(z_ref, xbc_ref, dt_ref, cw_ref, cb_ref, dtb_ref, alog_ref, dskip_ref, o_ref,
                xbc_s, state_s):
    S = z_ref.shape[0]
    W = GROUP_WIDTH
    L = SSM_CHUNK
    xbc_s[...] = _silu(_causal_conv(xbc_ref[...], cw_ref, cb_ref))
    state_s[...] = jnp.zeros_like(state_s)
    a_row = -jnp.exp(alog_ref[...])

    r_i = lax.broadcasted_iota(jnp.int32, (L, L), 0)
    c_i = lax.broadcasted_iota(jnp.int32, (L, L), 1)
    tri_incl = jnp.where(c_i <= r_i, 1.0, 0.0).astype(BF16)
    lower = c_i <= r_i
    e_r = lax.broadcasted_iota(jnp.int32, (LANES, W), 0)
    e_c = lax.broadcasted_iota(jnp.int32, (LANES, W), 1)
    expand = jnp.where((e_c >= e_r * HEAD_DIM) & (e_c < (e_r + 1) * HEAD_DIM), 1.0, 0.0).astype(BF16)
    lane_l = lax.broadcasted_iota(jnp.int32, (L, LANES), 1)

    def chunk(c, _):
        rows = pl.ds(pl.multiple_of(c * L, L), L)
        xs = xbc_s[rows, 0:W]
        dt = _softplus(dt_ref[rows, :] + dtb_ref[...])
        a_dt = dt * a_row
        cs_col = _dot_wide_lhs_rhs(tri_incl, a_dt)
        cs_row = cs_col.T
        cs_full = _dot_wide_lhs(cs_col, expand)
        dt_full = _dot_wide_lhs(dt, expand)
        xd = xs * dt_full
        tot = cs_full[L - 1:L, :]
        xdec = (xd * jnp.exp(tot - cs_full)).astype(BF16)
        xd16 = xd.astype(BF16)
        ys = []
        for g in range(2):
            gl = slice(g * LANES, (g + 1) * LANES)
            bm = xbc_s[rows, W + g * SSM_STATE:W + (g + 1) * SSM_STATE].astype(BF16)
            cm = xbc_s[rows, 2 * W + g * SSM_STATE:2 * W + (g + 1) * SSM_STATE].astype(BF16)
            cb = _dot_t(cm, bm)
            prev = state_s[:, gl]
            y_off = _dot(cm, prev.astype(BF16)) * jnp.exp(cs_full[:, gl])
            y_g = y_off
            for hh in range(2):
                h = 2 * g + hh
                seg = jnp.where(lower, cs_col[:, h:h + 1] - cs_row[h:h + 1, :], -jnp.inf)
                y_h = _dot((cb * jnp.exp(seg)).astype(BF16), xd16[:, gl])
                in_head = (lane_l >= hh * HEAD_DIM) & (lane_l < (hh + 1) * HEAD_DIM)
                y_g = y_g + jnp.where(in_head, y_h, 0.0)
            state_s[:, gl] = prev * jnp.exp(tot[:, gl]) + _dot_tl(bm, xdec[:, gl])
            ys.append(y_g)
        y = jnp.concatenate(ys, axis=1) + dskip_ref[...] * xs
        o_ref[rows, :] = y * _silu(z_ref[rows, :])
        return 0

    lax.fori_loop(0, S // L, chunk, 0)


def _dot_wide_lhs_rhs(m_bf16, x):
    hi, mid, lo = _split3(x)
    return _dot(m_bf16, hi) + _dot(m_bf16, mid) + _dot(m_bf16, lo)


def _ssd(z, xbc, dt, conv_w, conv_b, dt_bias, a_log, d_skip):
    B, S, _ = z.shape
    W = GROUP_WIDTH
    full = lambda shape: pl.BlockSpec(shape, lambda b: (0,) * len(shape))
    return pl.pallas_call(
        _ssd_kernel,
        grid=(B,),
        in_specs=[pl.BlockSpec((None, S, W), lambda b: (b, 0, 0)),
                  pl.BlockSpec((None, S, 3 * W), lambda b: (b, 0, 0)),
                  pl.BlockSpec((None, S, LANES), lambda b: (b, 0, 0)),
                  full((CONV_WIDTH, 3 * W)), full((1, 3 * W)), full((1, LANES)), full((1, LANES)),
                  full((1, W))],
        out_specs=pl.BlockSpec((None, S, W), lambda b: (b, 0, 0)),
        out_shape=jax.ShapeDtypeStruct((B, S, W), F32),
        scratch_shapes=[pltpu.VMEM((S, 3 * W), F32), pltpu.VMEM((SSM_STATE, W), F32)],
        compiler_params=_cparams(1),
        name="ssd",
    )(z, xbc, dt, conv_w, conv_b, dt_bias, a_log, d_skip)


def _rope(x, cos, sin):
    lane = lax.broadcasted_iota(jnp.int32, (x.shape[0], LANES), 1)
    first_half = (lane % HEAD_DIM) < (HEAD_DIM // 2)
    halves = []
    for p in range(x.shape[1] // LANES):
        xp = x[:, p * LANES:(p + 1) * LANES]
        fwd = pltpu.roll(xp, HEAD_DIM // 2, 1)
        bwd = pltpu.roll(xp, LANES - HEAD_DIM // 2, 1)
        halves.append(jnp.where(first_half, -bwd, fwd))
    rot = jnp.concatenate(halves, axis=1)
    return x * cos + rot * sin


def _moba_kernel(qkv_ref, cos_ref, sin_ref, o_ref, k_s, v_s, kmean_s):
    i = pl.program_id(1)
    W = GROUP_WIDTH
    TB = MOBA_BLOCK
    S = qkv_ref.shape[0]
    NB = S // TB

    @pl.when(i == 0)
    def _():
        for blk in range(NB):
            rs = slice(blk * TB, (blk + 1) * TB)
            kb = _rope(qkv_ref[rs, W:2 * W], cos_ref[rs, :], sin_ref[rs, :])
            k_s[rs, :] = kb.astype(BF16)
            kmean_s[blk:blk + 1, :] = jnp.mean(kb, axis=0, keepdims=True)
        v_s[...] = qkv_ref[:, 2 * W:3 * W].astype(BF16)

    rows_i = pl.ds(pl.multiple_of(i * TB, TB), TB)
    q = _rope(qkv_ref[rows_i, 0:W], cos_ref[rows_i, :], sin_ref[rows_i, :])
    lane = lax.broadcasted_iota(jnp.int32, (TB, W), 1)
    lane8 = lax.broadcasted_iota(jnp.int32, (NB, W), 1)
    blk_id = lax.broadcasted_iota(jnp.int32, (NB, TB), 0)
    r_loc = lax.broadcasted_iota(jnp.int32, (TB, TB), 0)
    c_loc = lax.broadcasted_iota(jnp.int32, (TB, TB), 1)
    kmean = kmean_s[...]
    q_hi, q_lo = _split2(q)
    scale = HEAD_DIM ** -0.5

    out = jnp.zeros((TB, W), F32)
    for h in range(N_HEADS):
        head = (lane >= h * HEAD_DIM) & (lane < (h + 1) * HEAD_DIM)
        head8 = (lane8 >= h * HEAD_DIM) & (lane8 < (h + 1) * HEAD_DIM)
        km_hi, km_lo = _split2(jnp.where(head8, kmean, 0.0))
        gate = _dot_t(km_hi, q_hi) + _dot_t(km_hi, q_lo) + _dot_t(km_lo, q_hi)
        cnt = jnp.zeros((NB, TB), F32)
        for jp in range(NB):
            row = gate[jp:jp + 1, :]
            beats = (row > gate) | ((row == gate) & (blk_id > jp))
            cnt = cnt + jnp.where(beats, jnp.where(jp < i, 1.0, 0.0), 0.0)
        sel = jnp.where((cnt < float(MOBA_TOPK)) & (blk_id < i), 1.0, 0.0).astype(BF16)

        qh = (jnp.where(head, q, 0.0) * scale).astype(BF16)
        s = _dot_t(qh, k_s[rows_i, :])
        s = jnp.where(c_loc <= r_loc, s, NEG_BIG)
        m = jnp.max(s, axis=-1, keepdims=True)
        p = jnp.exp(s - m)
        l = jnp.sum(p, axis=-1, keepdims=True)
        acc = _dot(p.astype(BF16), v_s[rows_i, :])

        def body(j, carry):
            m, l, acc = carry
            rows = pl.ds(pl.multiple_of(j * TB, TB), TB)
            s = _dot_t(qh, k_s[rows, :])
            pick = jnp.where(blk_id == j, 1.0, 0.0).astype(BF16)
            chosen = _dot_tl(sel, pick)
            s = jnp.where(chosen > 0.5, s, NEG_BIG)
            m_new = jnp.maximum(m, jnp.max(s, axis=-1, keepdims=True))
            alpha = jnp.exp(m - m_new)
            p = jnp.exp(s - m_new)
            l = alpha * l + jnp.sum(p, axis=-1, keepdims=True)
            acc = alpha * acc + _dot(p.astype(BF16), v_s[rows, :])
            return m_new, l, acc

        m, l, acc = lax.fori_loop(0, i, body, (m, l, acc))
        out = jnp.where(head, acc / l, out)
    o_ref[...] = out


def _moba(qkv, cos, sin):
    B, S, _ = qkv.shape
    W = GROUP_WIDTH
    return pl.pallas_call(
        _moba_kernel,
        grid=(B, S // MOBA_BLOCK),
        in_specs=[pl.BlockSpec((None, S, 3 * W), lambda b, i: (b, 0, 0)),
                  pl.BlockSpec((S, W), lambda b, i: (0, 0)),
                  pl.BlockSpec((S, W), lambda b, i: (0, 0))],
        out_specs=pl.BlockSpec((None, MOBA_BLOCK, W), lambda b, i: (b, i, 0)),
        out_shape=jax.ShapeDtypeStruct((B, S, W), F32),
        scratch_shapes=[pltpu.VMEM((S, W), BF16), pltpu.VMEM((S, W), BF16),
                        pltpu.VMEM((S // MOBA_BLOCK, W), F32)],
        compiler_params=_cparams(2),
        name="moba",
    )(qkv, cos, sin)


def _outproj_kernel(ya_ref, yb_ref, yc_ref, yd_ref, gg_ref, w_ref, x_ref, o_ref):
    W = GROUP_WIDTH
    acc = x_ref[...]
    for g, y_ref in enumerate((ya_ref, yb_ref, yc_ref, yd_ref)):
        yn = _rmsnorm(y_ref[...], gg_ref[g:g + 1, :]).astype(BF16)
        acc = acc + _dot(yn, w_ref[g * W:(g + 1) * W, :])
    o_ref[...] = acc


def _outproj(ys, gg, w_out, x2d, tm=512):
    T = x2d.shape[0]
    W = GROUP_WIDTH
    return pl.pallas_call(
        _outproj_kernel,
        grid=(T // tm,),
        in_specs=[pl.BlockSpec((tm, W), lambda i: (i, 0))] * 4
                 + [pl.BlockSpec((4, W), lambda i: (0, 0)),
                    pl.BlockSpec((4 * W, D_MODEL), lambda i: (0, 0)),
                    pl.BlockSpec((tm, D_MODEL), lambda i: (i, 0))],
        out_specs=pl.BlockSpec((tm, D_MODEL), lambda i: (i, 0)),
        out_shape=jax.ShapeDtypeStruct((T, D_MODEL), F32),
        compiler_params=_cparams(1),
        name="outproj",
    )(*ys, gg, w_out, x2d)


def _memkv_kernel(m_ref, g_ref, w_ref, o_ref):
    mn = _rmsnorm(m_ref[...], g_ref[...]).astype(BF16)
    o_ref[...] = _dot(mn, w_ref[...]).astype(BF16)


def _memkv(mem, g, wkv):
    B, M, _ = mem.shape
    return pl.pallas_call(
        _memkv_kernel,
        grid=(B,),
        in_specs=[pl.BlockSpec((None, M, D_MODEL), lambda b: (b, 0, 0)),
                  pl.BlockSpec((1, D_MODEL), lambda b: (0, 0)),
                  pl.BlockSpec((D_MODEL, 2 * D_MODEL), lambda b: (0, 0))],
        out_specs=pl.BlockSpec((None, M, 2 * D_MODEL), lambda b: (b, 0, 0)),
        out_shape=jax.ShapeDtypeStruct((B, M, 2 * D_MODEL), BF16),
        compiler_params=_cparams(1),
        name="memkv",
    )(mem, g, wkv)


def _xattn_kernel(x_ref, g_ref, wq_ref, kv_ref, wo_ref, o_ref):
    x = x_ref[...]
    h = _rmsnorm(x, g_ref[...]).astype(BF16)
    q = (_dot(h, wq_ref[...]) * (XATTN_HEAD_DIM ** -0.5)).astype(BF16)
    acc = x
    for hd in range(XATTN_HEADS):
        cols = slice(hd * XATTN_HEAD_DIM, (hd + 1) * XATTN_HEAD_DIM)
        vcols = slice(D_MODEL + hd * XATTN_HEAD_DIM, D_MODEL + (hd + 1) * XATTN_HEAD_DIM)
        s = _dot_t(q[:, cols], kv_ref[:, cols])
        p = jnp.exp(s - jnp.max(s, axis=-1, keepdims=True))
        p = p / jnp.sum(p, axis=-1, keepdims=True)
        o = _dot(p.astype(BF16), kv_ref[:, vcols]).astype(BF16)
        acc = acc + _dot(o, wo_ref[cols, :])
    o_ref[...] = acc


def _xattn(x3d, g, wq, kv, wo, tm=512):
    B, S, _ = x3d.shape
    M = kv.shape[1]
    return pl.pallas_call(
        _xattn_kernel,
        grid=(B, S // tm),
        in_specs=[pl.BlockSpec((None, tm, D_MODEL), lambda b, i: (b, i, 0)),
                  pl.BlockSpec((1, D_MODEL), lambda b, i: (0, 0)),
                  pl.BlockSpec((D_MODEL, D_MODEL), lambda b, i: (0, 0)),
                  pl.BlockSpec((None, M, 2 * D_MODEL), lambda b, i: (b, 0, 0)),
                  pl.BlockSpec((D_MODEL, D_MODEL), lambda b, i: (0, 0))],
        out_specs=pl.BlockSpec((None, tm, D_MODEL), lambda b, i: (b, i, 0)),
        out_shape=jax.ShapeDtypeStruct((B, S, D_MODEL), F32),
        compiler_params=_cparams(2),
        name="xattn",
    )(x3d, g, wq, kv, wo)


def _moe_kernel(x_ref, g_ref, wr_ref, br_ref, w1_ref, w3_ref, w2_ref, fg_ref, o_ref,
                t_s, comb_s, acc_s, *, final_norm):
    grp = pl.program_id(1)
    tm = x_ref.shape[0]

    @pl.when(grp == 0)
    def _():
        t = _rmsnorm(x_ref[...], g_ref[...])
        t_s[...] = t.astype(BF16)
        acc_s[...] = jnp.zeros_like(acc_s)
        t_hi, t_lo = _split2(t)
        logits = (_dot(t_hi, wr_ref[0]) + _dot(t_hi, wr_ref[1]) + _dot(t_lo, wr_ref[0])) + br_ref[...]
        lane = lax.broadcasted_iota(jnp.int32, (tm, LANES), 1)
        lane_f = lane.astype(F32)
        is_g = (lane >= MOE_EXPERTS) & (lane < MOE_EXPERTS + MOE_GROUPS)
        lg = jnp.where(is_g, logits, NEG_BIG)
        gmax = jnp.max(lg, axis=-1, keepdims=True)
        pg_top = 1.0 / jnp.sum(jnp.exp(lg - gmax), axis=-1, keepdims=True)
        g_lane = jnp.min(jnp.where(is_g & (lg == gmax), lane_f, 1e9), axis=-1, keepdims=True)
        g_idx = g_lane - float(MOE_EXPERTS)
        e_grp = jnp.floor(lane_f * (1.0 / MOE_EPG))
        in_grp = (lane < MOE_EXPERTS) & (e_grp == g_idx)
        le = jnp.where(in_grp, logits, NEG_BIG)
        e1 = jnp.max(le, axis=-1, keepdims=True)
        i1 = jnp.min(jnp.where(in_grp & (le == e1), lane_f, 1e9), axis=-1, keepdims=True)
        le2 = jnp.where(lane_f == i1, NEG_BIG, le)
        e2 = jnp.max(le2, axis=-1, keepdims=True)
        i2 = jnp.min(jnp.where(in_grp & (lane_f != i1) & (le2 == e2), lane_f, 1e9), axis=-1, keepdims=True)
        r2 = jnp.exp(e2 - e1)
        w_first = 1.0 / (1.0 + r2)
        w_second = r2 / (1.0 + r2)
        comb_s[...] = pg_top * (jnp.where(lane_f == i1, w_first, 0.0) + jnp.where(lane_f == i2, w_second, 0.0))

    t = t_s[...]
    comb = comb_s[...]
    lane = lax.broadcasted_iota(jnp.int32, (tm, LANES), 1)
    upd = jnp.zeros((tm, D_MODEL), F32)
    for e in range(MOE_EPG):
        n = grp * MOE_EPG + e
        c = jnp.sum(jnp.where(lane == n, comb, 0.0), axis=-1, keepdims=True)
        hid = _silu(_dot(t, w1_ref[e])) * _dot(t, w3_ref[e])
        upd = upd + _dot((hid * c).astype(BF16), w2_ref[e])
    acc_s[...] += upd

    @pl.when(grp == MOE_GROUPS - 1)
    def _():
        y = x_ref[...] + acc_s[...]
        if final_norm:
            y = _rmsnorm(y, fg_ref[...])
        o_ref[...] = y


def _moe(x2d, g, w_router, b_router, w1, w3, w2, final_g, final_norm, tm=512):
    T = x2d.shape[0]
    return pl.pallas_call(
        functools.partial(_moe_kernel, final_norm=final_norm),
        grid=(T // tm, MOE_GROUPS),
        in_specs=[pl.BlockSpec((tm, D_MODEL), lambda i, e: (i, 0)),
                  pl.BlockSpec((1, D_MODEL), lambda i, e: (0, 0)),
                  pl.BlockSpec((2, D_MODEL, LANES), lambda i, e: (0, 0, 0)),
                  pl.BlockSpec((1, LANES), lambda i, e: (0, 0)),
                  pl.BlockSpec((MOE_EPG, D_MODEL, MOE_FF), lambda i, e: (e, 0, 0)),
                  pl.BlockSpec((MOE_EPG, D_MODEL, MOE_FF), lambda i, e: (e, 0, 0)),
                  pl.BlockSpec((MOE_EPG, MOE_FF, D_MODEL), lambda i, e: (e, 0, 0)),
                  pl.BlockSpec((1, D_MODEL), lambda i, e: (0, 0))],
        out_specs=pl.BlockSpec((tm, D_MODEL), lambda i, e: (i, 0)),
        out_shape=jax.ShapeDtypeStruct((T, D_MODEL), F32),
        scratch_shapes=[pltpu.VMEM((tm, D_MODEL), BF16), pltpu.VMEM((tm, LANES), F32),
                        pltpu.VMEM((tm, D_MODEL), F32)],
        compiler_params=_cparams(2),
        name="moe",
    )(x2d, g, w_router, b_router, w1, w3, w2, final_g)


def _pad_lanes(v, width=LANES):
    return jnp.pad(v, (0, width - v.shape[0]))[None, :]


def _block_diag(w):
    H, n, _ = w.shape
    eye = jnp.eye(H, dtype=w.dtype)
    return (eye[:, None, :, None] * w[:, :, None, :]).reshape(H * n, H * n)


def _rope_tables(S):
    half = HEAD_DIM // 2
    inv_freq = ROPE_THETA ** (-jnp.arange(half, dtype=F32) / half)
    ang = jnp.arange(S, dtype=F32)[:, None] * inv_freq[None, :]
    reps = GROUP_WIDTH // half
    return jnp.tile(jnp.cos(ang), (1, reps)), jnp.tile(jnp.sin(ang), (1, reps))


def kernel(x, mem, mix_norm_g, w_in, lru_conv_w, lru_conv_b, lru_wr, lru_br, lru_wi, lru_bi, lru_lambda, ssm_conv_w, ssm_conv_b, ssm_dt_bias, ssm_a_log, ssm_d, group_norm_g, w_out, xattn_norm_g, mem_norm_g, xattn_wq, xattn_wkv, xattn_wo, ffn_norm_g, router_group_w, router_group_b, router_expert_w, router_expert_b, expert_w1, expert_w3, expert_w2, final_norm_g):
    B, S, D = x.shape
    T = B * S
    depth = w_in.shape[0]
    W = GROUP_WIDTH
    cos, sin = _rope_tables(S)
    n_main = 2 * W + 3 * W + W + 3 * W
    x2d = x.reshape(T, D)
    for l in range(depth):
        w_dt = jnp.pad(w_in[l][:, n_main:n_main + N_HEADS], ((0, 0), (0, LANES - N_HEADS)))
        w_cat = jnp.concatenate([w_in[l][:, :n_main], w_dt, w_in[l][:, n_main + N_HEADS:]], axis=1).astype(BF16)
        lru_xg, sb_qkv, ssm_z, ssm_xbc, ssm_dt, mb_qkv = _inproj(x2d, mix_norm_g[l][None, :], w_cat)

        w_bd = jnp.concatenate([_block_diag(lru_wr[l]), _block_diag(lru_wi[l])], axis=1).astype(BF16)
        b_ri = jnp.concatenate([lru_br[l], lru_bi[l]])[None, :]
        y_a = _lru(lru_xg.reshape(B, S, 2 * W), lru_conv_w[l], lru_conv_b[l][None, :], w_bd, b_ri,
                   lru_lambda[l][None, :])
        y_b = _sb_attention(sb_qkv.reshape(B, S, 3 * W))
        y_c = _ssd(ssm_z.reshape(B, S, W), ssm_xbc.reshape(B, S, 3 * W), ssm_dt.reshape(B, S, LANES),
                   ssm_conv_w[l], ssm_conv_b[l][None, :], _pad_lanes(ssm_dt_bias[l]), _pad_lanes(ssm_a_log[l]),
                   jnp.repeat(ssm_d[l], HEAD_DIM)[None, :])
        y_d = _moba(mb_qkv.reshape(B, S, 3 * W), cos, sin)
        x2d = _outproj([y.reshape(T, W) for y in (y_a, y_b, y_c, y_d)], group_norm_g[l].reshape(4, W),
                       w_out[l].astype(BF16), x2d)

        kv = _memkv(mem, mem_norm_g[l][None, :], xattn_wkv[l].astype(BF16))
        x2d = _xattn(x2d.reshape(B, S, D), xattn_norm_g[l][None, :], xattn_wq[l].astype(BF16), kv,
                     xattn_wo[l].astype(BF16)).reshape(T, D)

        w_r = jnp.pad(jnp.concatenate([router_expert_w[l], router_group_w[l]], axis=1),
                      ((0, 0), (0, LANES - MOE_EXPERTS - MOE_GROUPS)))
        w_r_hi = w_r.astype(BF16)
        w_r_lo = (w_r - w_r_hi.astype(F32)).astype(BF16)
        b_r = _pad_lanes(jnp.concatenate([router_expert_b[l], router_group_b[l]]))
        x2d = _moe(x2d, ffn_norm_g[l][None, :], jnp.stack([w_r_hi, w_r_lo]), b_r,
                   expert_w1[l].astype(BF16), expert_w3[l].astype(BF16), expert_w2[l].astype(BF16),
                   final_norm_g[None, :], final_norm=(l == depth - 1))
    return x2d.reshape(B, S, D)
```

```python
import functools
import math

import jax
import jax.numpy as jnp
from jax import lax
from jax.experimental import pallas as pl
from jax.experimental.pallas import tpu as pltpu

F32 = jnp.float32
BF16 = jnp.bfloat16

D_MODEL = 1024
GROUP_WIDTH = 256
HEAD_DIM = 64
N_HEADS = 4
NORM_EPS = 1e-6
CONV_WIDTH = 4
LRU_C = 8.0
SB_BLOCK = 128
SSM_CHUNK = 128
SSM_STATE = 128
MOBA_BLOCK = 256
MOBA_TOPK = 3
ROPE_THETA = 10000.0
XATTN_HEADS = 4
XATTN_HEAD_DIM = 256
MEM_LEN = 256
MOE_GROUPS = 4
MOE_EPG = 4
MOE_EXPERTS = 16
MOE_FF = 256
LANES = 128
NEG_BIG = -1e30
SB_EXP_FLOOR = -104.0
IN_OUT_WIDTHS = (512, 768, 256, 768, LANES, 768)
VMEM_LIMIT = 56 * 1024 * 1024


def _cparams(n_axes):
    return pltpu.CompilerParams(dimension_semantics=("arbitrary",) * n_axes,
                                vmem_limit_bytes=VMEM_LIMIT)


def _dot(a, b):
    return jnp.dot(a, b, preferred_element_type=F32)


def _dot_t(a, b):
    return lax.dot_general(a, b, (((1,), (1,)), ((), ())), preferred_element_type=F32)


def _dot_tl(a, b):
    return lax.dot_general(a, b, (((0,), (0,)), ((), ())), preferred_element_type=F32)


def _split2(x):
    hi = x.astype(BF16)
    lo = (x - hi.astype(F32)).astype(BF16)
    return hi, lo


def _split3(x):
    hi = x.astype(BF16)
    r = x - hi.astype(F32)
    mid = r.astype(BF16)
    lo = (r - mid.astype(F32)).astype(BF16)
    return hi, mid, lo


def _dot_wide_lhs(x, m_bf16, parts=3):
    pieces = _split3(x) if parts == 3 else _split2(x)
    out = _dot(pieces[0], m_bf16)
    for p in pieces[1:]:
        out = out + _dot(p, m_bf16)
    return out


def _rmsnorm(x, g):
    return x * lax.rsqrt(jnp.mean(x * x, axis=-1, keepdims=True) + NORM_EPS) * g


def _softplus(x):
    return jnp.maximum(x, 0.0) + jnp.log1p(jnp.exp(-jnp.abs(x)))


def _sigmoid(x):
    return 1.0 / (1.0 + jnp.exp(-x))


def _silu(x):
    return x * _sigmoid(x)


def _gelu_tanh(x):
    return 0.5 * x * (1.0 + jnp.tanh(math.sqrt(2.0 / math.pi) * (x + 0.044715 * (x * x * x))))


def _causal_conv(x, w_ref, b_ref):
    rows = lax.broadcasted_iota(jnp.int32, x.shape, 0)
    y = x * w_ref[CONV_WIDTH - 1:CONV_WIDTH, :] + b_ref[...]
    for s in range(1, CONV_WIDTH):
        xs = jnp.where(rows >= s, pltpu.roll(x, s, 0), 0.0)
        y = y + xs * w_ref[CONV_WIDTH - 1 - s:CONV_WIDTH - s, :]
    return y


def _inproj_kernel(x_ref, g_ref, wa_ref, wdt_ref, wmb_ref, *o_refs):
    h = _rmsnorm(x_ref[...], g_ref[...]).astype(BF16)
    off = 0
    for o_ref, width in zip(o_refs[:4], IN_OUT_WIDTHS[:4]):
        o_ref[...] = _dot(h, wa_ref[:, off:off + width])
        off += width
    o_refs[4][...] = _dot(h, wdt_ref[...])
    o_refs[5][...] = _dot(h, wmb_ref[...])


def _inproj(x2d, g, w_main, w_dt, w_mb, tm=512):
    T = x2d.shape[0]
    const = lambda a: pl.BlockSpec(a.shape, lambda i: (0, 0))
    return pl.pallas_call(
        _inproj_kernel,
        grid=(T // tm,),
        in_specs=[pl.BlockSpec((tm, D_MODEL), lambda i: (i, 0)),
                  pl.BlockSpec((1, D_MODEL), lambda i: (0, 0)),
                  const(w_main), const(w_dt), const(w_mb)],
        out_specs=[pl.BlockSpec((tm, w), lambda i: (i, 0)) for w in IN_OUT_WIDTHS],
        out_shape=[jax.ShapeDtypeStruct((T, w), F32) for w in IN_OUT_WIDTHS],
        compiler_params=_cparams(1),
        name="inproj",
    )(x2d, g, w_main, w_dt, w_mb)


def _lru_kernel(xg_ref, cw_ref, cb_ref, wbd_ref, bri_ref, lam_ref, o_ref):
    S = xg_ref.shape[0]
    W = GROUP_WIDTH
    xc = _causal_conv(xg_ref[:, 0:W], cw_ref, cb_ref)
    ri = _dot(xc.astype(BF16), wbd_ref[...]) + bri_ref[...]
    r = _sigmoid(ri[:, 0:W])
    i = _sigmoid(ri[:, W:2 * W])
    log_a = (LRU_C * r) * (-_softplus(-lam_ref[...]))
    a = jnp.exp(log_a)
    u = jnp.sqrt(1.0 - jnp.exp(2.0 * log_a)) * (i * xc)
    rows = lax.broadcasted_iota(jnp.int32, (S, W), 0)
    shift = 1
    while shift < S:
        keep = rows >= shift
        a_s = jnp.where(keep, pltpu.roll(a, shift, 0), 1.0)
        u_s = jnp.where(keep, pltpu.roll(u, shift, 0), 0.0)
        u = a * u_s + u
        a = a * a_s
        shift *= 2
    o_ref[...] = u * _gelu_tanh(xg_ref[:, W:2 * W])


def _lru(xg, conv_w, conv_b, w_bd, b_ri, lam):
    B, S, _ = xg.shape
    W = GROUP_WIDTH
    full = lambda shape: pl.BlockSpec(shape, lambda b: (0,) * len(shape))
    return pl.pallas_call(
        _lru_kernel,
        grid=(B,),
        in_specs=[pl.BlockSpec((None, S, 2 * W), lambda b: (b, 0, 0)),
                  full((CONV_WIDTH, W)), full((1, W)), full((W, 2 * W)), full((1, 2 * W)), full((1, W))],
        out_specs=pl.BlockSpec((None, S, W), lambda b: (b, 0, 0)),
        out_shape=jax.ShapeDtypeStruct((B, S, W), F32),
        compiler_params=_cparams(1),
        name="rglru",
    )(xg, conv_w, conv_b, w_bd, b_ri, lam)


def _sb_kernel(qkv_ref, o_ref, k_s, v_s, acc_s, later_s):
    i = pl.program_id(1)
    W = GROUP_WIDTH
    TB = SB_BLOCK
    R = N_HEADS * TB

    @pl.when(i == 0)
    def _():
        k_s[...] = qkv_ref[:, W:2 * W].astype(BF16)
        v_s[...] = qkv_ref[:, 2 * W:3 * W].astype(BF16)

    q = qkv_ref[pl.ds(pl.multiple_of(i * TB, TB), TB), 0:W] * (HEAD_DIM ** -0.5)
    lane = lax.broadcasted_iota(jnp.int32, (TB, W), 1)
    heads = [(lane >= h * HEAD_DIM) & (lane < (h + 1) * HEAD_DIM) for h in range(N_HEADS)]
    qs = jnp.concatenate([jnp.where(hm, q, 0.0) for hm in heads], axis=0).astype(BF16)
    r_loc = lax.broadcasted_iota(jnp.int32, (R, TB), 0) & (TB - 1)
    c_loc = lax.broadcasted_iota(jnp.int32, (R, TB), 1)
    ur = lax.broadcasted_iota(jnp.int32, (TB, 2 * TB), 0)
    uc = lax.broadcasted_iota(jnp.int32, (TB, 2 * TB), 1)
    tri_ones = jnp.where((ur > uc) | (uc >= TB), 1.0, 0.0).astype(BF16)

    acc_s[...] = jnp.zeros_like(acc_s)
    later_s[...] = jnp.zeros_like(later_s)

    def cond(carry):
        n, later_max = carry
        return (n <= i) & (later_max > SB_EXP_FLOOR)

    def body(carry):
        n, _ = carry
        j = i - n
        rows = pl.ds(pl.multiple_of(j * TB, TB), TB)
        z = _dot_t(qs, k_s[rows, :])
        past = (j * TB + c_loc) < (i * TB + r_loc)
        sp = _softplus(z)
        lf = jnp.where(past, -sp, 0.0)
        lf_hi, lf_lo = _split2(lf)
        cs2 = _dot(jnp.concatenate([lf_hi, lf_lo], axis=0), tri_ones)
        cs = cs2[0:R, :] + cs2[R:2 * R, :]
        later = later_s[...]
        w = jnp.where(past, jnp.exp((z - sp) + cs[:, 0:TB] + later), 0.0)
        acc_s[...] += _dot(w.astype(BF16), v_s[rows, :])
        later = later + cs[:, TB:2 * TB]
        later_s[...] = later
        return n + 1, jnp.max(later)

    lax.while_loop(cond, body, (jnp.int32(0), jnp.float32(0.0)))
    out = acc_s[0:TB, :]
    for h in range(1, N_HEADS):
        out = jnp.where(heads[h], acc_s[h * TB:(h + 1) * TB, :], out)
    o_ref[...] = out


def _sb_attention(qkv):
    B, S, _ = qkv.shape
    W = GROUP_WIDTH
    return pl.pallas_call(
        _sb_kernel,
        grid=(B, S // SB_BLOCK),
        in_specs=[pl.BlockSpec((None, S, 3 * W), lambda b, i: (b, 0, 0))],
        out_specs=pl.BlockSpec((None, SB_BLOCK, W), lambda b, i: (b, i, 0)),
        out_shape=jax.ShapeDtypeStruct((B, S, W), F32),
        scratch_shapes=[pltpu.VMEM((S, W), BF16), pltpu.VMEM((S, W), BF16),
                        pltpu.VMEM((N_HEADS * SB_BLOCK, W), F32), pltpu.VMEM((N_HEADS * SB_BLOCK, SB_BLOCK), F32)],
        compiler_params=_cparams(2),
        name="stickbreak",
    )(qkv)


def _ssd_kernel(z_ref, xbc_ref, dt_ref, cw_ref, cb_ref, dtb_ref, alog_ref, dskip_ref, o_ref,
                xbc_s, state_s):
    S = z_ref.shape[0]
    W = GROUP_WIDTH
    L = SSM_CHUNK
    xbc_s[...] = _silu(_causal_conv(xbc_ref[...], cw_ref, cb_ref))
    state_s[...] = jnp.zeros_like(state_s)
    a_row = -jnp.exp(alog_ref[...])

    r_i = lax.broadcasted_iota(jnp.int32, (L, L), 0)
    c_i = lax.broadcasted_iota(jnp.int32, (L, L), 1)
    tri_incl = jnp.where(c_i <= r_i, 1.0, 0.0).astype(BF16)
    lower = c_i <= r_i
    e_r = lax.broadcasted_iota(jnp.int32, (LANES, W), 0)
    e_c = lax.broadcasted_iota(jnp.int32, (LANES, W), 1)
    expand = jnp.where((e_c >= e_r * HEAD_DIM) & (e_c < (e_r + 1) * HEAD_DIM), 1.0, 0.0).astype(BF16)
    lane_l = lax.broadcasted_iota(jnp.int32, (L, LANES), 1)

    def chunk(c, _):
        rows = pl.ds(pl.multiple_of(c * L, L), L)
        xs = xbc_s[rows, 0:W]
        dt = _softplus(dt_ref[rows, :] + dtb_ref[...])
        a_dt = dt * a_row
        cs_col = _dot_wide_lhs_rhs(tri_incl, a_dt)
        cs_row = cs_col.T
        cs_full = _dot_wide_lhs(cs_col, expand)
        dt_full = _dot_wide_lhs(dt, expand)
        xd = xs * dt_full
        tot = cs_full[L - 1:L, :]
        xdec = (xd * jnp.exp(tot - cs_full)).astype(BF16)
        xd16 = xd.astype(BF16)
        ys = []
        for g in range(2):
            gl = slice(g * LANES, (g + 1) * LANES)
            bm = xbc_s[rows, W + g * SSM_STATE:W + (g + 1) * SSM_STATE].astype(BF16)
            cm = xbc_s[rows, 2 * W + g * SSM_STATE:2 * W + (g + 1) * SSM_STATE].astype(BF16)
            cb = _dot_t(cm, bm)
            prev = state_s[:, gl]
            y_off = _dot(cm, prev.astype(BF16)) * jnp.exp(cs_full[:, gl])
            y_g = y_off
            for hh in range(2):
                h = 2 * g + hh
                seg = jnp.where(lower, cs_col[:, h:h + 1] - cs_row[h:h + 1, :], -jnp.inf)
                y_h = _dot((cb * jnp.exp(seg)).astype(BF16), xd16[:, gl])
                in_head = (lane_l >= hh * HEAD_DIM) & (lane_l < (hh + 1) * HEAD_DIM)
                y_g = y_g + jnp.where(in_head, y_h, 0.0)
            state_s[:, gl] = prev * jnp.exp(tot[:, gl]) + _dot_tl(bm, xdec[:, gl])
            ys.append(y_g)
        y = jnp.concatenate(ys, axis=1) + dskip_ref[...] * xs
        o_ref[rows, :] = y * _silu(z_ref[rows, :])
        return 0

    lax.fori_loop(0, S // L, chunk, 0)


def _dot_wide_lhs_rhs(m_bf16, x):
    hi, mid, lo = _split3(x)
    return _dot(m_bf16, hi) + _dot(m_bf16, mid) + _dot(m_bf16, lo)


def _ssd(z, xbc, dt, conv_w, conv_b, dt_bias, a_log, d_skip):
    B, S, _ = z.shape
    W = GROUP_WIDTH
    full = lambda shape: pl.BlockSpec(shape, lambda b: (0,) * len(shape))
    return pl.pallas_call(
        _ssd_kernel,
        grid=(B,),
        in_specs=[pl.BlockSpec((None, S, W), lambda b: (b, 0, 0)),
                  pl.BlockSpec((None, S, 3 * W), lambda b: (b, 0, 0)),
                  pl.BlockSpec((None, S, LANES), lambda b: (b, 0, 0)),
                  full((CONV_WIDTH, 3 * W)), full((1, 3 * W)), full((1, LANES)), full((1, LANES)),
                  full((1, W))],
        out_specs=pl.BlockSpec((None, S, W), lambda b: (b, 0, 0)),
        out_shape=jax.ShapeDtypeStruct((B, S, W), F32),
        scratch_shapes=[pltpu.VMEM((S, 3 * W), F32), pltpu.VMEM((SSM_STATE, W), F32)],
        compiler_params=_cparams(1),
        name="ssd",
    )(z, xbc, dt, conv_w, conv_b, dt_bias, a_log, d_skip)


def _rope(x, cos, sin):
    lane = lax.broadcasted_iota(jnp.int32, (x.shape[0], LANES), 1)
    first_half = (lane % HEAD_DIM) < (HEAD_DIM // 2)
    halves = []
    for p in range(x.shape[1] // LANES):
        xp = x[:, p * LANES:(p + 1) * LANES]
        fwd = pltpu.roll(xp, HEAD_DIM // 2, 1)
        bwd = pltpu.roll(xp, LANES - HEAD_DIM // 2, 1)
        halves.append(jnp.where(first_half, -bwd, fwd))
    rot = jnp.concatenate(halves, axis=1)
    return x * cos + rot * sin


def _moba_kernel(qkv_ref, cos_ref, sin_ref, o_ref, k_s, v_s, kmean_s, acc_s):
    i = pl.program_id(1)
    W = GROUP_WIDTH
    TB = MOBA_BLOCK
    S = qkv_ref.shape[0]
    NB = S // TB

    @pl.when(i == 0)
    def _():
        for blk in range(NB):
            rs = slice(blk * TB, (blk + 1) * TB)
            kb = _rope(qkv_ref[rs, W:2 * W], cos_ref[rs, :], sin_ref[rs, :])
            k_s[rs, :] = kb.astype(BF16)
            kmean_s[blk:blk + 1, :] = jnp.mean(kb, axis=0, keepdims=True)
        v_s[...] = qkv_ref[:, 2 * W:3 * W].astype(BF16)

    rows_i = pl.ds(pl.multiple_of(i * TB, TB), TB)
    q = _rope(qkv_ref[rows_i, 0:W], cos_ref[rows_i, :], sin_ref[rows_i, :])
    lane = lax.broadcasted_iota(jnp.int32, (TB, W), 1)
    lane8 = lax.broadcasted_iota(jnp.int32, (NB, W), 1)
    blk_id = lax.broadcasted_iota(jnp.int32, (NB, TB), 0)
    R = N_HEADS * TB
    r_loc = lax.broadcasted_iota(jnp.int32, (R, TB), 0) & (TB - 1)
    c_loc = lax.broadcasted_iota(jnp.int32, (R, TB), 1)
    kmean = kmean_s[...]
    q_hi, q_lo = _split2(q)
    scale = HEAD_DIM ** -0.5

    heads = [(lane >= h * HEAD_DIM) & (lane < (h + 1) * HEAD_DIM) for h in range(N_HEADS)]
    biases, qhs = [], []
    for h in range(N_HEADS):
        head8 = (lane8 >= h * HEAD_DIM) & (lane8 < (h + 1) * HEAD_DIM)
        km_hi, km_lo = _split2(jnp.where(head8, kmean, 0.0))
        gate = _dot_t(km_hi, q_hi) + _dot_t(km_hi, q_lo) + _dot_t(km_lo, q_hi)
        cnt = jnp.zeros((NB, TB), F32)
        for jp in range(NB):
            row = gate[jp:jp + 1, :]
            beats = (row > gate) | ((row == gate) & (blk_id > jp))
            cnt = cnt + jnp.where(beats, jnp.where(jp < i, 1.0, 0.0), 0.0)
        selected = (cnt < float(MOBA_TOPK)) & (blk_id < i)
        biases.append(jnp.where(selected, 0.0, NEG_BIG))
        qhs.append(jnp.where(heads[h], q, 0.0) * scale)
    bias_all = jnp.concatenate(biases, axis=1).astype(BF16)
    qs = jnp.concatenate(qhs, axis=0).astype(BF16)

    s = jnp.where(c_loc <= r_loc, _dot_t(qs, k_s[rows_i, :]), NEG_BIG)
    m = jnp.max(s, axis=-1, keepdims=True)
    p = jnp.exp(s - m)
    l = jnp.sum(p, axis=-1, keepdims=True)
    acc_s[...] = _dot(p.astype(BF16), v_s[rows_i, :])

    def body(j, carry):
        m, l = carry
        rows = pl.ds(pl.multiple_of(j * TB, TB), TB)
        pick = jnp.where(blk_id == j, 1.0, 0.0).astype(BF16)
        s = _dot_t(qs, k_s[rows, :]) + _dot_tl(bias_all, pick)
        m_new = jnp.maximum(m, jnp.max(s, axis=-1, keepdims=True))
        alpha = jnp.exp(m - m_new)
        p = jnp.exp(s - m_new)
        l = alpha * l + jnp.sum(p, axis=-1, keepdims=True)
        acc_s[...] = alpha * acc_s[...] + _dot(p.astype(BF16), v_s[rows, :])
        return m_new, l

    m, l = lax.fori_loop(0, i, body, (m, l))
    outs = acc_s[...] / l
    out = outs[0:TB, :]
    for h in range(1, N_HEADS):
        out = jnp.where(heads[h], outs[h * TB:(h + 1) * TB, :], out)
    o_ref[...] = out


def _moba(qkv, cos, sin):
    B, S, _ = qkv.shape
    W = GROUP_WIDTH
    return pl.pallas_call(
        _moba_kernel,
        grid=(B, S // MOBA_BLOCK),
        in_specs=[pl.BlockSpec((None, S, 3 * W), lambda b, i: (b, 0, 0)),
                  pl.BlockSpec((S, W), lambda b, i: (0, 0)),
                  pl.BlockSpec((S, W), lambda b, i: (0, 0))],
        out_specs=pl.BlockSpec((None, MOBA_BLOCK, W), lambda b, i: (b, i, 0)),
        out_shape=jax.ShapeDtypeStruct((B, S, W), F32),
        scratch_shapes=[pltpu.VMEM((S, W), BF16), pltpu.VMEM((S, W), BF16),
                        pltpu.VMEM((S // MOBA_BLOCK, W), F32), pltpu.VMEM((N_HEADS * MOBA_BLOCK, W), F32)],
        compiler_params=_cparams(2),
        name="moba",
    )(qkv, cos, sin)


def _outproj_kernel(ya_ref, yb_ref, yc_ref, yd_ref, gg_ref, w_ref, x_ref, o_ref):
    W = GROUP_WIDTH
    acc = x_ref[...]
    for g, y_ref in enumerate((ya_ref, yb_ref, yc_ref, yd_ref)):
        yn = _rmsnorm(y_ref[...], gg_ref[g:g + 1, :]).astype(BF16)
        acc = acc + _dot(yn, w_ref[g * W:(g + 1) * W, :])
    o_ref[...] = acc


def _outproj(ys, gg, w_out, x2d, tm=512):
    T = x2d.shape[0]
    W = GROUP_WIDTH
    return pl.pallas_call(
        _outproj_kernel,
        grid=(T // tm,),
        in_specs=[pl.BlockSpec((tm, W), lambda i: (i, 0))] * 4
                 + [pl.BlockSpec((4, W), lambda i: (0, 0)),
                    pl.BlockSpec((4 * W, D_MODEL), lambda i: (0, 0)),
                    pl.BlockSpec((tm, D_MODEL), lambda i: (i, 0))],
        out_specs=pl.BlockSpec((tm, D_MODEL), lambda i: (i, 0)),
        out_shape=jax.ShapeDtypeStruct((T, D_MODEL), F32),
        compiler_params=_cparams(1),
        name="outproj",
    )(*ys, gg, w_out, x2d)


def _memkv_kernel(m_ref, g_ref, w_ref, o_ref):
    mn = _rmsnorm(m_ref[...], g_ref[...]).astype(BF16)
    o_ref[...] = _dot(mn, w_ref[...]).astype(BF16)


def _memkv(mem, g, wkv):
    B, M, _ = mem.shape
    return pl.pallas_call(
        _memkv_kernel,
        grid=(B,),
        in_specs=[pl.BlockSpec((None, M, D_MODEL), lambda b: (b, 0, 0)),
                  pl.BlockSpec((1, D_MODEL), lambda b: (0, 0)),
                  pl.BlockSpec((D_MODEL, 2 * D_MODEL), lambda b: (0, 0))],
        out_specs=pl.BlockSpec((None, M, 2 * D_MODEL), lambda b: (b, 0, 0)),
        out_shape=jax.ShapeDtypeStruct((B, M, 2 * D_MODEL), BF16),
        compiler_params=_cparams(1),
        name="memkv",
    )(mem, g, wkv)


def _xattn_kernel(x_ref, g_ref, wq_ref, kv_ref, wo_ref, o_ref):
    x = x_ref[...]
    h = _rmsnorm(x, g_ref[...]).astype(BF16)
    q = (_dot(h, wq_ref[...]) * (XATTN_HEAD_DIM ** -0.5)).astype(BF16)
    acc = x
    for hd in range(XATTN_HEADS):
        cols = slice(hd * XATTN_HEAD_DIM, (hd + 1) * XATTN_HEAD_DIM)
        vcols = slice(D_MODEL + hd * XATTN_HEAD_DIM, D_MODEL + (hd + 1) * XATTN_HEAD_DIM)
        s = _dot_t(q[:, cols], kv_ref[:, cols])
        p = jnp.exp(s - jnp.max(s, axis=-1, keepdims=True))
        p = p / jnp.sum(p, axis=-1, keepdims=True)
        o = _dot(p.astype(BF16), kv_ref[:, vcols]).astype(BF16)
        acc = acc + _dot(o, wo_ref[cols, :])
    o_ref[...] = acc


def _xattn(x3d, g, wq, kv, wo, tm=512):
    B, S, _ = x3d.shape
    M = kv.shape[1]
    return pl.pallas_call(
        _xattn_kernel,
        grid=(B, S // tm),
        in_specs=[pl.BlockSpec((None, tm, D_MODEL), lambda b, i: (b, i, 0)),
                  pl.BlockSpec((1, D_MODEL), lambda b, i: (0, 0)),
                  pl.BlockSpec((D_MODEL, D_MODEL), lambda b, i: (0, 0)),
                  pl.BlockSpec((None, M, 2 * D_MODEL), lambda b, i: (b, 0, 0)),
                  pl.BlockSpec((D_MODEL, D_MODEL), lambda b, i: (0, 0))],
        out_specs=pl.BlockSpec((None, tm, D_MODEL), lambda b, i: (b, i, 0)),
        out_shape=jax.ShapeDtypeStruct((B, S, D_MODEL), F32),
        compiler_params=_cparams(2),
        name="xattn",
    )(x3d, g, wq, kv, wo)


def _moe_kernel(x_ref, g_ref, wr_ref, br_ref, w1_ref, w3_ref, w2_ref, fg_ref, o_ref,
                t_s, comb_s, acc_s, *, final_norm):
    grp = pl.program_id(1)
    tm = x_ref.shape[0]

    @pl.when(grp == 0)
    def _():
        t = _rmsnorm(x_ref[...], g_ref[...])
        t_s[...] = t.astype(BF16)
        acc_s[...] = jnp.zeros_like(acc_s)
        t_hi, t_lo = _split2(t)
        logits = (_dot(t_hi, wr_ref[0]) + _dot(t_hi, wr_ref[1]) + _dot(t_lo, wr_ref[0])) + br_ref[...]
        lane = lax.broadcasted_iota(jnp.int32, (tm, LANES), 1)
        lane_f = lane.astype(F32)
        is_g = (lane >= MOE_EXPERTS) & (lane < MOE_EXPERTS + MOE_GROUPS)
        lg = jnp.where(is_g, logits, NEG_BIG)
        gmax = jnp.max(lg, axis=-1, keepdims=True)
        pg_top = 1.0 / jnp.sum(jnp.exp(lg - gmax), axis=-1, keepdims=True)
        g_lane = jnp.min(jnp.where(is_g & (lg == gmax), lane_f, 1e9), axis=-1, keepdims=True)
        g_idx = g_lane - float(MOE_EXPERTS)
        e_grp = jnp.floor(lane_f * (1.0 / MOE_EPG))
        in_grp = (lane < MOE_EXPERTS) & (e_grp == g_idx)
        le = jnp.where(in_grp, logits, NEG_BIG)
        e1 = jnp.max(le, axis=-1, keepdims=True)
        i1 = jnp.min(jnp.where(in_grp & (le == e1), lane_f, 1e9), axis=-1, keepdims=True)
        le2 = jnp.where(lane_f == i1, NEG_BIG, le)
        e2 = jnp.max(le2, axis=-1, keepdims=True)
        i2 = jnp.min(jnp.where(in_grp & (lane_f != i1) & (le2 == e2), lane_f, 1e9), axis=-1, keepdims=True)
        r2 = jnp.exp(e2 - e1)
        w_first = 1.0 / (1.0 + r2)
        w_second = r2 / (1.0 + r2)
        comb_s[...] = pg_top * (jnp.where(lane_f == i1, w_first, 0.0) + jnp.where(lane_f == i2, w_second, 0.0))

    t = t_s[...]
    comb = comb_s[...]
    lane = lax.broadcasted_iota(jnp.int32, (tm, LANES), 1)
    upd = jnp.zeros((tm, D_MODEL), F32)
    for e in range(MOE_EPG):
        n = grp * MOE_EPG + e
        c = jnp.sum(jnp.where(lane == n, comb, 0.0), axis=-1, keepdims=True)
        hid = _silu(_dot(t, w1_ref[e])) * _dot(t, w3_ref[e])
        upd = upd + _dot((hid * c).astype(BF16), w2_ref[e])
    acc_s[...] += upd

    @pl.when(grp == MOE_GROUPS - 1)
    def _():
        y = x_ref[...] + acc_s[...]
        if final_norm:
            y = _rmsnorm(y, fg_ref[...])
        o_ref[...] = y


def _moe(x2d, g, w_router, b_router, w1, w3, w2, final_g, final_norm, tm=512):
    T = x2d.shape[0]
    return pl.pallas_call(
        functools.partial(_moe_kernel, final_norm=final_norm),
        grid=(T // tm, MOE_GROUPS),
        in_specs=[pl.BlockSpec((tm, D_MODEL), lambda i, e: (i, 0)),
                  pl.BlockSpec((1, D_MODEL), lambda i, e: (0, 0)),
                  pl.BlockSpec((2, D_MODEL, LANES), lambda i, e: (0, 0, 0)),
                  pl.BlockSpec((1, LANES), lambda i, e: (0, 0)),
                  pl.BlockSpec((MOE_EPG, D_MODEL, MOE_FF), lambda i, e: (e, 0, 0)),
                  pl.BlockSpec((MOE_EPG, D_MODEL, MOE_FF), lambda i, e: (e, 0, 0)),
                  pl.BlockSpec((MOE_EPG, MOE_FF, D_MODEL), lambda i, e: (e, 0, 0)),
                  pl.BlockSpec((1, D_MODEL), lambda i, e: (0, 0))],
        out_specs=pl.BlockSpec((tm, D_MODEL), lambda i, e: (i, 0)),
        out_shape=jax.ShapeDtypeStruct((T, D_MODEL), F32),
        scratch_shapes=[pltpu.VMEM((tm, D_MODEL), BF16), pltpu.VMEM((tm, LANES), F32),
                        pltpu.VMEM((tm, D_MODEL), F32)],
        compiler_params=_cparams(2),
        name="moe",
    )(x2d, g, w_router, b_router, w1, w3, w2, final_g)


def _pad_lanes(v, width=LANES):
    return jnp.pad(v, (0, width - v.shape[0]))[None, :]


def _block_diag(w):
    H, n, _ = w.shape
    eye = jnp.eye(H, dtype=w.dtype)
    return (eye[:, None, :, None] * w[:, :, None, :]).reshape(H * n, H * n)


def _rope_tables(S):
    half = HEAD_DIM // 2
    inv_freq = ROPE_THETA ** (-jnp.arange(half, dtype=F32) / half)
    ang = jnp.arange(S, dtype=F32)[:, None] * inv_freq[None, :]
    reps = GROUP_WIDTH // half
    return jnp.tile(jnp.cos(ang), (1, reps)), jnp.tile(jnp.sin(ang), (1, reps))


def kernel(x, mem, mix_norm_g, w_in, lru_conv_w, lru_conv_b, lru_wr, lru_br, lru_wi, lru_bi, lru_lambda, ssm_conv_w, ssm_conv_b, ssm_dt_bias, ssm_a_log, ssm_d, group_norm_g, w_out, xattn_norm_g, mem_norm_g, xattn_wq, xattn_wkv, xattn_wo, ffn_norm_g, router_group_w, router_group_b, router_expert_w, router_expert_b, expert_w1, expert_w3, expert_w2, final_norm_g):
    B, S, D = x.shape
    T = B * S
    depth = w_in.shape[0]
    W = GROUP_WIDTH
    cos, sin = _rope_tables(S)
    n_main = 2 * W + 3 * W + W + 3 * W
    x2d = x.reshape(T, D)
    for l in range(depth):
        w_main = w_in[l, :, :n_main].astype(BF16)
        w_dt = jnp.pad(w_in[l, :, n_main:n_main + N_HEADS], ((0, 0), (0, LANES - N_HEADS))).astype(BF16)
        w_mb = w_in[l, :, n_main + N_HEADS:].astype(BF16)
        lru_xg, sb_qkv, ssm_z, ssm_xbc, ssm_dt, mb_qkv = _inproj(x2d, mix_norm_g[l][None, :], w_main, w_dt, w_mb)

        w_bd = jnp.concatenate([_block_diag(lru_wr[l]), _block_diag(lru_wi[l])], axis=1).astype(BF16)
        b_ri = jnp.concatenate([lru_br[l], lru_bi[l]])[None, :]
        y_a = _lru(lru_xg.reshape(B, S, 2 * W), lru_conv_w[l], lru_conv_b[l][None, :], w_bd, b_ri,
                   lru_lambda[l][None, :])
        y_b = _sb_attention(sb_qkv.reshape(B, S, 3 * W))
        y_c = _ssd(ssm_z.reshape(B, S, W), ssm_xbc.reshape(B, S, 3 * W), ssm_dt.reshape(B, S, LANES),
                   ssm_conv_w[l], ssm_conv_b[l][None, :], _pad_lanes(ssm_dt_bias[l]), _pad_lanes(ssm_a_log[l]),
                   jnp.repeat(ssm_d[l], HEAD_DIM)[None, :])
        y_d = _moba(mb_qkv.reshape(B, S, 3 * W), cos, sin)
        x2d = _outproj([y.reshape(T, W) for y in (y_a, y_b, y_c, y_d)], group_norm_g[l].reshape(4, W),
                       w_out[l].astype(BF16), x2d)

        kv = _memkv(mem, mem_norm_g[l][None, :], xattn_wkv[l].astype(BF16))
        x2d = _xattn(x2d.reshape(B, S, D), xattn_norm_g[l][None, :], xattn_wq[l].astype(BF16), kv,
                     xattn_wo[l].astype(BF16)).reshape(T, D)

        w_r = jnp.pad(jnp.concatenate([router_expert_w[l], router_group_w[l]], axis=1),
                      ((0, 0), (0, LANES - MOE_EXPERTS - MOE_GROUPS)))
        w_r_hi = w_r.astype(BF16)
        w_r_lo = (w_r - w_r_hi.astype(F32)).astype(BF16)
        b_r = _pad_lanes(jnp.concatenate([router_expert_b[l], router_group_b[l]]))
        x2d = _moe(x2d, ffn_norm_g[l][None, :], jnp.stack([w_r_hi, w_r_lo]), b_r,
                   expert_w1[l].astype(BF16), expert_w3[l].astype(BF16), expert_w2[l].astype(BF16),
                   final_norm_g[None, :], final_norm=(l == depth - 1))
    return x2d.reshape(B, S, D)
```

```python
import functools
import math

import jax
import jax.numpy as jnp
from jax import lax
from jax.experimental import pallas as pl
from jax.experimental.pallas import tpu as pltpu

F32 = jnp.float32
BF16 = jnp.bfloat16

D_MODEL = 1024
GROUP_WIDTH = 256
HEAD_DIM = 64
N_HEADS = 4
NORM_EPS = 1e-6
CONV_WIDTH = 4
LRU_C = 8.0
SB_BLOCK = 128
SB_WINDOW_BLOCKS = 3
SB_CHAINS = 2
SSM_CHUNK = 128
SSM_STATE = 128
MOBA_BLOCK = 256
MOBA_TOPK = 3
ROPE_THETA = 10000.0
XATTN_HEADS = 4
XATTN_HEAD_DIM = 256
MEM_LEN = 256
MOE_GROUPS = 4
MOE_EPG = 4
MOE_EXPERTS = 16
MOE_FF = 256
MOE_CHUNK = 128
LANES = 128
NEG_BIG = -1e30
SB_EXP_FLOOR = -104.0
IN_OUT_WIDTHS = (512, 768, 256, 768, LANES, 768)
VMEM_LIMIT = 56 * 1024 * 1024


def _cparams(n_axes):
    return pltpu.CompilerParams(dimension_semantics=("arbitrary",) * n_axes,
                                vmem_limit_bytes=VMEM_LIMIT)


def _dot(a, b):
    return jnp.dot(a, b, preferred_element_type=F32)


def _dot_t(a, b):
    return lax.dot_general(a, b, (((1,), (1,)), ((), ())), preferred_element_type=F32)


def _dot_tl(a, b):
    return lax.dot_general(a, b, (((0,), (0,)), ((), ())), preferred_element_type=F32)


def _split2(x):
    hi = x.astype(BF16)
    lo = (x - hi.astype(F32)).astype(BF16)
    return hi, lo


def _split3(x):
    hi = x.astype(BF16)
    r = x - hi.astype(F32)
    mid = r.astype(BF16)
    lo = (r - mid.astype(F32)).astype(BF16)
    return hi, mid, lo


def _dot_wide_lhs(x, m_bf16, parts=3):
    pieces = _split3(x) if parts == 3 else _split2(x)
    out = _dot(pieces[0], m_bf16)
    for p in pieces[1:]:
        out = out + _dot(p, m_bf16)
    return out


def _rmsnorm(x, g):
    return x * lax.rsqrt(jnp.mean(x * x, axis=-1, keepdims=True) + NORM_EPS) * g


def _softplus(x):
    return jnp.maximum(x, 0.0) + jnp.log(1.0 + jnp.exp(-jnp.abs(x)))


def _sigmoid(x):
    return 1.0 / (1.0 + jnp.exp(-x))


def _silu(x):
    return x * _sigmoid(x)


def _gelu_tanh(x):
    return 0.5 * x * (1.0 + jnp.tanh(math.sqrt(2.0 / math.pi) * (x + 0.044715 * (x * x * x))))


def _causal_conv(x, w_ref, b_ref):
    rows = lax.broadcasted_iota(jnp.int32, x.shape, 0)
    y = x * w_ref[CONV_WIDTH - 1:CONV_WIDTH, :] + b_ref[...]
    for s in range(1, CONV_WIDTH):
        xs = jnp.where(rows >= s, pltpu.roll(x, s, 0), 0.0)
        y = y + xs * w_ref[CONV_WIDTH - 1 - s:CONV_WIDTH - s, :]
    return y


def _inproj_kernel(x_ref, g_ref, wa_ref, wdt_ref, wmb_ref, *o_refs):
    h = _rmsnorm(x_ref[...], g_ref[...]).astype(BF16)
    off = 0
    for o_ref, width in zip(o_refs[:4], IN_OUT_WIDTHS[:4]):
        o_ref[...] = _dot(h, wa_ref[:, off:off + width])
        off += width
    o_refs[4][...] = _dot(h, wdt_ref[...])
    o_refs[5][...] = _dot(h, wmb_ref[...])


def _inproj(x2d, g, w_main, w_dt, w_mb, tm=512):
    T = x2d.shape[0]
    const = lambda a: pl.BlockSpec(a.shape, lambda i: (0, 0))
    return pl.pallas_call(
        _inproj_kernel,
        grid=(T // tm,),
        in_specs=[pl.BlockSpec((tm, D_MODEL), lambda i: (i, 0)),
                  pl.BlockSpec((1, D_MODEL), lambda i: (0, 0)),
                  const(w_main), const(w_dt), const(w_mb)],
        out_specs=[pl.BlockSpec((tm, w), lambda i: (i, 0)) for w in IN_OUT_WIDTHS],
        out_shape=[jax.ShapeDtypeStruct((T, w), F32) for w in IN_OUT_WIDTHS],
        compiler_params=_cparams(1),
        name="inproj",
    )(x2d, g, w_main, w_dt, w_mb)


def _lru_kernel(xg_ref, cw_ref, cb_ref, wbd_ref, bri_ref, lam_ref, o_ref):
    S = xg_ref.shape[0]
    W = GROUP_WIDTH
    xc = _causal_conv(xg_ref[:, 0:W], cw_ref, cb_ref)
    ri = _dot(xc.astype(BF16), wbd_ref[...]) + bri_ref[...]
    r = _sigmoid(ri[:, 0:W])
    i = _sigmoid(ri[:, W:2 * W])
    log_a = (LRU_C * r) * (-_softplus(-lam_ref[...]))
    a = jnp.exp(log_a)
    u = jnp.sqrt(1.0 - jnp.exp(2.0 * log_a)) * (i * xc)
    rows = lax.broadcasted_iota(jnp.int32, (S, W), 0)
    shift = 1
    while shift < S:
        keep = rows >= shift
        a_s = jnp.where(keep, pltpu.roll(a, shift, 0), 1.0)
        u_s = jnp.where(keep, pltpu.roll(u, shift, 0), 0.0)
        u = a * u_s + u
        a = a * a_s
        shift *= 2
    o_ref[...] = u * _gelu_tanh(xg_ref[:, W:2 * W])


def _lru(xg, conv_w, conv_b, w_bd, b_ri, lam):
    B, S, _ = xg.shape
    W = GROUP_WIDTH
    full = lambda shape: pl.BlockSpec(shape, lambda b: (0,) * len(shape))
    return pl.pallas_call(
        _lru_kernel,
        grid=(B,),
        in_specs=[pl.BlockSpec((None, S, 2 * W), lambda b: (b, 0, 0)),
                  full((CONV_WIDTH, W)), full((1, W)), full((W, 2 * W)), full((1, 2 * W)), full((1, W))],
        out_specs=pl.BlockSpec((None, S, W), lambda b: (b, 0, 0)),
        out_shape=jax.ShapeDtypeStruct((B, S, W), F32),
        compiler_params=_cparams(1),
        name="rglru",
    )(xg, conv_w, conv_b, w_bd, b_ri, lam)


def _sb_kernel(qkv_ref, o_ref, k_s, v_s, acc_s, later_s):
    i = pl.program_id(1)
    W = GROUP_WIDTH
    TB = SB_BLOCK
    R = N_HEADS * TB

    NW = SB_WINDOW_BLOCKS
    KW = NW * TB
    PAD = (NW - 1) * TB
    S = qkv_ref.shape[0]

    @pl.when(i == 0)
    def _():
        k_s[0:PAD, :] = jnp.zeros((PAD, W), BF16)
        v_s[0:PAD, :] = jnp.zeros((PAD, W), BF16)
        k_s[PAD:PAD + S, :] = qkv_ref[:, W:2 * W].astype(BF16)
        v_s[PAD:PAD + S, :] = qkv_ref[:, 2 * W:3 * W].astype(BF16)

    lane = lax.broadcasted_iota(jnp.int32, (TB, W), 1)
    heads = [(lane >= h * HEAD_DIM) & (lane < (h + 1) * HEAD_DIM) for h in range(N_HEADS)]
    r_loc = lax.broadcasted_iota(jnp.int32, (R, KW), 0) & (TB - 1)
    c_loc = lax.broadcasted_iota(jnp.int32, (R, KW), 1)
    ur = lax.broadcasted_iota(jnp.int32, (TB, 2 * TB), 0)
    uc = lax.broadcasted_iota(jnp.int32, (TB, 2 * TB), 1)
    tri_ones = jnp.where((ur > uc) | (uc >= TB), 1.0, 0.0).astype(BF16)

    n_chains = o_ref.shape[0]
    blocks = [i + c * (S // TB // n_chains) for c in range(n_chains)]
    qss = []
    for blk in blocks:
        q = qkv_ref[pl.ds(pl.multiple_of(blk * TB, TB), TB), 0:W] * (HEAD_DIM ** -0.5)
        qss.append(jnp.concatenate([jnp.where(hm, q, 0.0) for hm in heads], axis=0).astype(BF16))

    acc_s[...] = jnp.zeros_like(acc_s)
    later_s[...] = jnp.zeros_like(later_s)

    def window(c, n):
        blk = blocks[c]
        first_key = (blk - n * NW - (NW - 1)) * TB
        rows = pl.ds(pl.multiple_of(jnp.maximum(first_key + PAD, 0), TB), KW)
        z = _dot_t(qss[c], k_s[rows, :])
        key_abs = first_key + c_loc
        live = (key_abs < blk * TB + r_loc) & (key_abs >= 0)
        sp = _softplus(z)
        lf = jnp.where(live, -sp, 0.0)
        lf_hi, lf_lo = _split2(lf)
        order = list(range(NW - 1, -1, -1))
        stacked = jnp.concatenate([piece[:, b * TB:(b + 1) * TB] for piece in (lf_hi, lf_lo) for b in order], axis=0)
        cs2 = _dot(stacked, tri_ones)
        offset = later_s[c]
        after = [None] * NW
        for pos, b in enumerate(order):
            cs = cs2[pos * R:(pos + 1) * R, :] + cs2[(NW + pos) * R:(NW + pos + 1) * R, :]
            after[b] = cs[:, 0:TB] + offset
            offset = offset + cs[:, TB:2 * TB]
        w = jnp.where(live, jnp.exp((z - sp) + jnp.concatenate(after, axis=1)), 0.0)
        acc_s[c] += _dot(w.astype(BF16), v_s[rows, :])
        later_s[c] = offset
        return jnp.where((n + 1) * NW <= blk, jnp.max(offset), SB_EXP_FLOOR)

    def cond(carry):
        return carry[1] > SB_EXP_FLOOR

    def body(carry):
        n = carry[0]
        later_max = window(0, n)
        for c in range(1, n_chains):
            later_max = jnp.maximum(later_max, window(c, n))
        return n + 1, later_max

    lax.while_loop(cond, body, (jnp.int32(0), jnp.float32(0.0)))
    for c in range(n_chains):
        out = acc_s[c, 0:TB, :]
        for h in range(1, N_HEADS):
            out = jnp.where(heads[h], acc_s[c, h * TB:(h + 1) * TB, :], out)
        o_ref[c] = out


def _sb_attention(qkv):
    B, S, _ = qkv.shape
    W = GROUP_WIDTH
    pad = (SB_WINDOW_BLOCKS - 1) * SB_BLOCK
    nc = SB_CHAINS
    rows = N_HEADS * SB_BLOCK
    out = pl.pallas_call(
        _sb_kernel,
        grid=(B, S // SB_BLOCK // nc),
        in_specs=[pl.BlockSpec((None, S, 3 * W), lambda b, i: (b, 0, 0))],
        out_specs=pl.BlockSpec((None, nc, SB_BLOCK, W), lambda b, i: (b, 0, i, 0)),
        out_shape=jax.ShapeDtypeStruct((B, nc, S // nc, W), F32),
        scratch_shapes=[pltpu.VMEM((S + pad, W), BF16), pltpu.VMEM((S + pad, W), BF16),
                        pltpu.VMEM((nc, rows, W), F32), pltpu.VMEM((nc, rows, SB_BLOCK), F32)],
        compiler_params=_cparams(2),
        name="stickbreak",
    )(qkv)
    return out.reshape(B, S, W)


def _ssd_kernel(z_ref, xbc_ref, dt_ref, cw_ref, cb_ref, dtb_ref, alog_ref, dskip_ref, o_ref,
                xbc_s, state_s):
    S = z_ref.shape[0]
    W = GROUP_WIDTH
    L = SSM_CHUNK
    xbc_s[...] = _silu(_causal_conv(xbc_ref[...], cw_ref, cb_ref))
    state_s[...] = jnp.zeros_like(state_s)
    a_row = -jnp.exp(alog_ref[...])

    r_i = lax.broadcasted_iota(jnp.int32, (L, L), 0)
    c_i = lax.broadcasted_iota(jnp.int32, (L, L), 1)
    tri_incl = jnp.where(c_i <= r_i, 1.0, 0.0).astype(BF16)
    lower = c_i <= r_i
    e_r = lax.broadcasted_iota(jnp.int32, (LANES, W), 0)
    e_c = lax.broadcasted_iota(jnp.int32, (LANES, W), 1)
    expand = jnp.where((e_c >= e_r * HEAD_DIM) & (e_c < (e_r + 1) * HEAD_DIM), 1.0, 0.0).astype(BF16)
    lane_l = lax.broadcasted_iota(jnp.int32, (L, LANES), 1)

    def chunk(c, _):
        rows = pl.ds(pl.multiple_of(c * L, L), L)
        xs = xbc_s[rows, 0:W]
        dt = _softplus(dt_ref[rows, :] + dtb_ref[...])
        a_dt = dt * a_row
        cs_col = _dot_wide_lhs_rhs(tri_incl, a_dt)
        cs_row = cs_col.T
        cs_full = _dot_wide_lhs(cs_col, expand)
        dt_full = _dot_wide_lhs(dt, expand)
        xd = xs * dt_full
        tot = cs_full[L - 1:L, :]
        xdec = (xd * jnp.exp(tot - cs_full)).astype(BF16)
        xd16 = xd.astype(BF16)
        ys = []
        for g in range(2):
            gl = slice(g * LANES, (g + 1) * LANES)
            bm = xbc_s[rows, W + g * SSM_STATE:W + (g + 1) * SSM_STATE].astype(BF16)
            cm = xbc_s[rows, 2 * W + g * SSM_STATE:2 * W + (g + 1) * SSM_STATE].astype(BF16)
            cb = _dot_t(cm, bm)
            prev = state_s[:, gl]
            y_off = _dot(cm, prev.astype(BF16)) * jnp.exp(cs_full[:, gl])
            y_g = y_off
            for hh in range(2):
                h = 2 * g + hh
                seg = jnp.where(lower, cs_col[:, h:h + 1] - cs_row[h:h + 1, :], -jnp.inf)
                y_h = _dot((cb * jnp.exp(seg)).astype(BF16), xd16[:, gl])
                in_head = (lane_l >= hh * HEAD_DIM) & (lane_l < (hh + 1) * HEAD_DIM)
                y_g = y_g + jnp.where(in_head, y_h, 0.0)
            state_s[:, gl] = prev * jnp.exp(tot[:, gl]) + _dot_tl(bm, xdec[:, gl])
            ys.append(y_g)
        y = jnp.concatenate(ys, axis=1) + dskip_ref[...] * xs
        o_ref[rows, :] = y * _silu(z_ref[rows, :])
        return 0

    lax.fori_loop(0, S // L, chunk, 0)


def _dot_wide_lhs_rhs(m_bf16, x):
    hi, mid, lo = _split3(x)
    return _dot(m_bf16, hi) + _dot(m_bf16, mid) + _dot(m_bf16, lo)


def _ssd(z, xbc, dt, conv_w, conv_b, dt_bias, a_log, d_skip):
    B, S, _ = z.shape
    W = GROUP_WIDTH
    full = lambda shape: pl.BlockSpec(shape, lambda b: (0,) * len(shape))
    return pl.pallas_call(
        _ssd_kernel,
        grid=(B,),
        in_specs=[pl.BlockSpec((None, S, W), lambda b: (b, 0, 0)),
                  pl.BlockSpec((None, S, 3 * W), lambda b: (b, 0, 0)),
                  pl.BlockSpec((None, S, LANES), lambda b: (b, 0, 0)),
                  full((CONV_WIDTH, 3 * W)), full((1, 3 * W)), full((1, LANES)), full((1, LANES)),
                  full((1, W))],
        out_specs=pl.BlockSpec((None, S, W), lambda b: (b, 0, 0)),
        out_shape=jax.ShapeDtypeStruct((B, S, W), F32),
        scratch_shapes=[pltpu.VMEM((S, 3 * W), F32), pltpu.VMEM((SSM_STATE, W), F32)],
        compiler_params=_cparams(1),
        name="ssd",
    )(z, xbc, dt, conv_w, conv_b, dt_bias, a_log, d_skip)


def _rope(x, cos, sin):
    lane = lax.broadcasted_iota(jnp.int32, (x.shape[0], LANES), 1)
    first_half = (lane % HEAD_DIM) < (HEAD_DIM // 2)
    halves = []
    for p in range(x.shape[1] // LANES):
        xp = x[:, p * LANES:(p + 1) * LANES]
        fwd = pltpu.roll(xp, HEAD_DIM // 2, 1)
        bwd = pltpu.roll(xp, LANES - HEAD_DIM // 2, 1)
        halves.append(jnp.where(first_half, -bwd, fwd))
    rot = jnp.concatenate(halves, axis=1)
    return x * cos + rot * sin


def _moba_kernel(qkv_ref, cos_ref, sin_ref, o_ref, k_s, v_s, kmean_s, acc_s):
    i = pl.program_id(1)
    W = GROUP_WIDTH
    TB = MOBA_BLOCK
    S = qkv_ref.shape[0]
    NB = S // TB

    @pl.when(i == 0)
    def _():
        for blk in range(NB):
            rs = slice(blk * TB, (blk + 1) * TB)
            kb = _rope(qkv_ref[rs, W:2 * W], cos_ref[rs, :], sin_ref[rs, :])
            k_s[rs, :] = kb.astype(BF16)
            kmean_s[blk:blk + 1, :] = jnp.mean(kb, axis=0, keepdims=True)
        v_s[...] = qkv_ref[:, 2 * W:3 * W].astype(BF16)

    rows_i = pl.ds(pl.multiple_of(i * TB, TB), TB)
    q = _rope(qkv_ref[rows_i, 0:W], cos_ref[rows_i, :], sin_ref[rows_i, :])
    lane = lax.broadcasted_iota(jnp.int32, (TB, W), 1)
    lane8 = lax.broadcasted_iota(jnp.int32, (NB, W), 1)
    blk_id = lax.broadcasted_iota(jnp.int32, (NB, TB), 0)
    R = N_HEADS * TB
    r_loc = lax.broadcasted_iota(jnp.int32, (R, TB), 0) & (TB - 1)
    c_loc = lax.broadcasted_iota(jnp.int32, (R, TB), 1)
    kmean = kmean_s[...]
    q_hi, q_lo = _split2(q)
    scale = HEAD_DIM ** -0.5

    heads = [(lane >= h * HEAD_DIM) & (lane < (h + 1) * HEAD_DIM) for h in range(N_HEADS)]
    biases, qhs = [], []
    for h in range(N_HEADS):
        head8 = (lane8 >= h * HEAD_DIM) & (lane8 < (h + 1) * HEAD_DIM)
        km_hi, km_lo = _split2(jnp.where(head8, kmean, 0.0))
        gate = _dot_t(km_hi, q_hi) + _dot_t(km_hi, q_lo) + _dot_t(km_lo, q_hi)
        cnt = jnp.zeros((NB, TB), F32)
        for jp in range(NB):
            row = gate[jp:jp + 1, :]
            beats = (row > gate) | ((row == gate) & (blk_id > jp))
            cnt = cnt + jnp.where(beats, jnp.where(jp < i, 1.0, 0.0), 0.0)
        selected = (cnt < float(MOBA_TOPK)) & (blk_id < i)
        biases.append(jnp.where(selected, 0.0, NEG_BIG))
        qhs.append(jnp.where(heads[h], q, 0.0) * scale)
    bias_all = jnp.concatenate(biases, axis=1).astype(BF16)
    qs = jnp.concatenate(qhs, axis=0).astype(BF16)

    s = jnp.where(c_loc <= r_loc, _dot_t(qs, k_s[rows_i, :]), NEG_BIG)
    m = jnp.max(s, axis=-1, keepdims=True)
    p = jnp.exp(s - m)
    l = jnp.sum(p, axis=-1, keepdims=True)
    acc_s[...] = _dot(p.astype(BF16), v_s[rows_i, :])

    def body(j, carry):
        m, l = carry
        rows = pl.ds(pl.multiple_of(j * TB, TB), TB)
        pick = jnp.where(blk_id == j, 1.0, 0.0).astype(BF16)
        s = _dot_t(qs, k_s[rows, :]) + _dot_tl(bias_all, pick)
        m_new = jnp.maximum(m, jnp.max(s, axis=-1, keepdims=True))
        alpha = jnp.exp(m - m_new)
        p = jnp.exp(s - m_new)
        l = alpha * l + jnp.sum(p, axis=-1, keepdims=True)
        acc_s[...] = alpha * acc_s[...] + _dot(p.astype(BF16), v_s[rows, :])
        return m_new, l

    m, l = lax.fori_loop(0, i, body, (m, l))
    outs = acc_s[...] / l
    out = outs[0:TB, :]
    for h in range(1, N_HEADS):
        out = jnp.where(heads[h], outs[h * TB:(h + 1) * TB, :], out)
    o_ref[...] = out


def _moba(qkv, cos, sin):
    B, S, _ = qkv.shape
    W = GROUP_WIDTH
    return pl.pallas_call(
        _moba_kernel,
        grid=(B, S // MOBA_BLOCK),
        in_specs=[pl.BlockSpec((None, S, 3 * W), lambda b, i: (b, 0, 0)),
                  pl.BlockSpec((S, W), lambda b, i: (0, 0)),
                  pl.BlockSpec((S, W), lambda b, i: (0, 0))],
        out_specs=pl.BlockSpec((None, MOBA_BLOCK, W), lambda b, i: (b, i, 0)),
        out_shape=jax.ShapeDtypeStruct((B, S, W), F32),
        scratch_shapes=[pltpu.VMEM((S, W), BF16), pltpu.VMEM((S, W), BF16),
                        pltpu.VMEM((S // MOBA_BLOCK, W), F32), pltpu.VMEM((N_HEADS * MOBA_BLOCK, W), F32)],
        compiler_params=_cparams(2),
        name="moba",
    )(qkv, cos, sin)


def _outproj_kernel(ya_ref, yb_ref, yc_ref, yd_ref, gg_ref, w_ref, x_ref, o_ref):
    W = GROUP_WIDTH
    acc = x_ref[...]
    for g, y_ref in enumerate((ya_ref, yb_ref, yc_ref, yd_ref)):
        yn = _rmsnorm(y_ref[...], gg_ref[g:g + 1, :]).astype(BF16)
        acc = acc + _dot(yn, w_ref[g * W:(g + 1) * W, :])
    o_ref[...] = acc


def _outproj(ys, gg, w_out, x2d, tm=512):
    T = x2d.shape[0]
    W = GROUP_WIDTH
    return pl.pallas_call(
        _outproj_kernel,
        grid=(T // tm,),
        in_specs=[pl.BlockSpec((tm, W), lambda i: (i, 0))] * 4
                 + [pl.BlockSpec((4, W), lambda i: (0, 0)),
                    pl.BlockSpec((4 * W, D_MODEL), lambda i: (0, 0)),
                    pl.BlockSpec((tm, D_MODEL), lambda i: (i, 0))],
        out_specs=pl.BlockSpec((tm, D_MODEL), lambda i: (i, 0)),
        out_shape=jax.ShapeDtypeStruct((T, D_MODEL), F32),
        compiler_params=_cparams(1),
        name="outproj",
    )(*ys, gg, w_out, x2d)


def _memkv_kernel(m_ref, g_ref, w_ref, o_ref):
    mn = _rmsnorm(m_ref[...], g_ref[...]).astype(BF16)
    o_ref[...] = _dot(mn, w_ref[...]).astype(BF16)


def _memkv(mem, g, wkv):
    B, M, _ = mem.shape
    return pl.pallas_call(
        _memkv_kernel,
        grid=(B,),
        in_specs=[pl.BlockSpec((None, M, D_MODEL), lambda b: (b, 0, 0)),
                  pl.BlockSpec((1, D_MODEL), lambda b: (0, 0)),
                  pl.BlockSpec((D_MODEL, 2 * D_MODEL), lambda b: (0, 0))],
        out_specs=pl.BlockSpec((None, M, 2 * D_MODEL), lambda b: (b, 0, 0)),
        out_shape=jax.ShapeDtypeStruct((B, M, 2 * D_MODEL), BF16),
        compiler_params=_cparams(1),
        name="memkv",
    )(mem, g, wkv)


def _xattn_kernel(x_ref, g_ref, wq_ref, kv_ref, wo_ref, o_ref):
    x = x_ref[...]
    h = _rmsnorm(x, g_ref[...]).astype(BF16)
    q = (_dot(h, wq_ref[...]) * (XATTN_HEAD_DIM ** -0.5)).astype(BF16)
    acc = x
    for hd in range(XATTN_HEADS):
        cols = slice(hd * XATTN_HEAD_DIM, (hd + 1) * XATTN_HEAD_DIM)
        vcols = slice(D_MODEL + hd * XATTN_HEAD_DIM, D_MODEL + (hd + 1) * XATTN_HEAD_DIM)
        s = _dot_t(q[:, cols], kv_ref[:, cols])
        p = jnp.exp(s - jnp.max(s, axis=-1, keepdims=True))
        p = p / jnp.sum(p, axis=-1, keepdims=True)
        o = _dot(p.astype(BF16), kv_ref[:, vcols]).astype(BF16)
        acc = acc + _dot(o, wo_ref[cols, :])
    o_ref[...] = acc


def _xattn(x3d, g, wq, kv, wo, tm=512):
    B, S, _ = x3d.shape
    M = kv.shape[1]
    return pl.pallas_call(
        _xattn_kernel,
        grid=(B, S // tm),
        in_specs=[pl.BlockSpec((None, tm, D_MODEL), lambda b, i: (b, i, 0)),
                  pl.BlockSpec((1, D_MODEL), lambda b, i: (0, 0)),
                  pl.BlockSpec((D_MODEL, D_MODEL), lambda b, i: (0, 0)),
                  pl.BlockSpec((None, M, 2 * D_MODEL), lambda b, i: (b, 0, 0)),
                  pl.BlockSpec((D_MODEL, D_MODEL), lambda b, i: (0, 0))],
        out_specs=pl.BlockSpec((None, tm, D_MODEL), lambda b, i: (b, i, 0)),
        out_shape=jax.ShapeDtypeStruct((B, S, D_MODEL), F32),
        compiler_params=_cparams(2),
        name="xattn",
    )(x3d, g, wq, kv, wo)


def _moe_kernel(x_ref, g_ref, wr_ref, br_ref, w1_ref, w3_ref, w2_ref, fg_ref, o_ref,
                ts_s, combs_s, ys_s, *, final_norm):
    tm = x_ref.shape[0]
    CH = MOE_CHUNK
    x = x_ref[...]
    t = _rmsnorm(x, g_ref[...])
    t_hi, t_lo = _split2(t)
    logits = (_dot(t_hi, wr_ref[0]) + _dot(t_hi, wr_ref[1]) + _dot(t_lo, wr_ref[0])) + br_ref[...]
    lane = lax.broadcasted_iota(jnp.int32, (tm, LANES), 1)
    lane_f = lane.astype(F32)
    is_g = (lane >= MOE_EXPERTS) & (lane < MOE_EXPERTS + MOE_GROUPS)
    lg = jnp.where(is_g, logits, NEG_BIG)
    gmax = jnp.max(lg, axis=-1, keepdims=True)
    pg_top = 1.0 / jnp.sum(jnp.exp(lg - gmax), axis=-1, keepdims=True)
    g_lane = jnp.min(jnp.where(is_g & (lg == gmax), lane_f, 1e9), axis=-1, keepdims=True)
    g_idx = g_lane - float(MOE_EXPERTS)
    e_grp = jnp.floor(lane_f * (1.0 / MOE_EPG))
    in_grp = (lane < MOE_EXPERTS) & (e_grp == g_idx)
    le = jnp.where(in_grp, logits, NEG_BIG)
    e1 = jnp.max(le, axis=-1, keepdims=True)
    i1 = jnp.min(jnp.where(in_grp & (le == e1), lane_f, 1e9), axis=-1, keepdims=True)
    le2 = jnp.where(lane_f == i1, NEG_BIG, le)
    e2 = jnp.max(le2, axis=-1, keepdims=True)
    i2 = jnp.min(jnp.where(in_grp & (lane_f != i1) & (le2 == e2), lane_f, 1e9), axis=-1, keepdims=True)
    r2 = jnp.exp(e2 - e1)
    w_first = 1.0 / (1.0 + r2)
    w_second = r2 / (1.0 + r2)
    comb = pg_top * (jnp.where(lane_f == i1, w_first, 0.0) + jnp.where(lane_f == i2, w_second, 0.0))

    onehot = jnp.where(lane_f == g_idx, 1.0, 0.0)
    r_tt = lax.broadcasted_iota(jnp.int32, (tm, tm), 0)
    c_tt = lax.broadcasted_iota(jnp.int32, (tm, tm), 1)
    tri_incl = jnp.where(c_tt <= r_tt, 1.0, 0.0).astype(BF16)
    seen = _dot(tri_incl, onehot.astype(BF16))
    counts = seen[tm - 1:tm, :]
    starts = [jnp.sum(jnp.where(lane[0:1, :] < g, counts, 0.0)) for g in range(MOE_GROUPS + 1)]
    start_row = jnp.zeros((1, LANES), F32)
    for g in range(1, MOE_GROUPS):
        start_row = jnp.where(lane[0:1, :] == g, starts[g], start_row)
    pos = jnp.sum(onehot * (seen - 1.0 + start_row), axis=-1, keepdims=True)
    pos_row = jnp.broadcast_to(pos, (tm, LANES)).T[0:1, :]
    perm = jnp.where(r_tt.astype(F32) == pos_row, 1.0, 0.0).astype(BF16)
    unperm = jnp.where(c_tt.astype(F32) == pos, 1.0, 0.0).astype(BF16)

    ts_s[...] = _dot(perm, t.astype(BF16)).astype(BF16)
    combs_s[...] = _dot_wide_lhs_rhs(perm, comb)
    ys_s[...] = jnp.zeros_like(ys_s)
    bounds = [s.astype(jnp.int32) for s in starts]
    lane_c = lax.broadcasted_iota(jnp.int32, (CH, LANES), 1)

    def item(it, _):
        c = lax.shift_right_logical(it, 2)
        g = lax.bitwise_and(it, MOE_GROUPS - 1)
        seg_lo = bounds[0]
        seg_hi = bounds[1]
        for gg in range(1, MOE_GROUPS):
            seg_lo = jnp.where(g == gg, bounds[gg], seg_lo)
            seg_hi = jnp.where(g == gg, bounds[gg + 1], seg_hi)

        @pl.when((seg_lo < (c + 1) * CH) & (seg_hi > c * CH))
        def _():
            rows = pl.ds(pl.multiple_of(c * CH, CH), CH)
            tc = ts_s[rows, :]
            cc = combs_s[rows, :]
            upd = jnp.zeros((CH, D_MODEL), F32)
            for e in range(MOE_EPG):
                n = g * MOE_EPG + e
                cw = jnp.sum(jnp.where(lane_c == n, cc, 0.0), axis=-1, keepdims=True)
                hid = _silu(_dot(tc, w1_ref[n])) * _dot(tc, w3_ref[n])
                upd = upd + _dot((hid * cw).astype(BF16), w2_ref[n])
            ys_s[rows, :] += upd
        return 0

    lax.fori_loop(0, (tm // CH) * MOE_GROUPS, item, 0)

    y_hi, y_lo = _split2(ys_s[...])
    y = x + (_dot(unperm, y_hi) + _dot(unperm, y_lo))
    if final_norm:
        y = _rmsnorm(y, fg_ref[...])
    o_ref[...] = y


def _moe(x2d, g, w_router, b_router, w1, w3, w2, final_g, final_norm, tm=512):
    T = x2d.shape[0]

    def const(a):
        return pl.BlockSpec(a.shape, lambda i: (0,) * a.ndim, pipeline_mode=pl.Buffered(1))

    return pl.pallas_call(
        functools.partial(_moe_kernel, final_norm=final_norm),
        grid=(T // tm,),
        in_specs=[pl.BlockSpec((tm, D_MODEL), lambda i: (i, 0)),
                  const(g), const(w_router), const(b_router), const(w1), const(w3), const(w2), const(final_g)],
        out_specs=pl.BlockSpec((tm, D_MODEL), lambda i: (i, 0)),
        out_shape=jax.ShapeDtypeStruct((T, D_MODEL), F32),
        scratch_shapes=[pltpu.VMEM((tm, D_MODEL), BF16), pltpu.VMEM((tm, LANES), F32),
                        pltpu.VMEM((tm, D_MODEL), F32)],
        compiler_params=_cparams(1),
        name="moe",
    )(x2d, g, w_router, b_router, w1, w3, w2, final_g)


def _pad_lanes(v, width=LANES):
    return jnp.pad(v, (0, width - v.shape[0]))[None, :]


def _block_diag(w):
    H, n, _ = w.shape
    eye = jnp.eye(H, dtype=w.dtype)
    return (eye[:, None, :, None] * w[:, :, None, :]).reshape(H * n, H * n)


def _rope_tables(S):
    half = HEAD_DIM // 2
    inv_freq = ROPE_THETA ** (-jnp.arange(half, dtype=F32) / half)
    ang = jnp.arange(S, dtype=F32)[:, None] * inv_freq[None, :]
    reps = GROUP_WIDTH // half
    return jnp.tile(jnp.cos(ang), (1, reps)), jnp.tile(jnp.sin(ang), (1, reps))


def kernel(x, mem, mix_norm_g, w_in, lru_conv_w, lru_conv_b, lru_wr, lru_br, lru_wi, lru_bi, lru_lambda, ssm_conv_w, ssm_conv_b, ssm_dt_bias, ssm_a_log, ssm_d, group_norm_g, w_out, xattn_norm_g, mem_norm_g, xattn_wq, xattn_wkv, xattn_wo, ffn_norm_g, router_group_w, router_group_b, router_expert_w, router_expert_b, expert_w1, expert_w3, expert_w2, final_norm_g):
    B, S, D = x.shape
    T = B * S
    depth = w_in.shape[0]
    W = GROUP_WIDTH
    cos, sin = _rope_tables(S)
    n_main = 2 * W + 3 * W + W + 3 * W
    x2d = x.reshape(T, D)
    for l in range(depth):
        w_main = w_in[l, :, :n_main].astype(BF16)
        w_dt = jnp.pad(w_in[l, :, n_main:n_main + N_HEADS], ((0, 0), (0, LANES - N_HEADS))).astype(BF16)
        w_mb = w_in[l, :, n_main + N_HEADS:].astype(BF16)
        lru_xg, sb_qkv, ssm_z, ssm_xbc, ssm_dt, mb_qkv = _inproj(x2d, mix_norm_g[l][None, :], w_main, w_dt, w_mb)

        w_bd = jnp.concatenate([_block_diag(lru_wr[l]), _block_diag(lru_wi[l])], axis=1).astype(BF16)
        b_ri = jnp.concatenate([lru_br[l], lru_bi[l]])[None, :]
        y_a = _lru(lru_xg.reshape(B, S, 2 * W), lru_conv_w[l], lru_conv_b[l][None, :], w_bd, b_ri,
                   lru_lambda[l][None, :])
        y_b = _sb_attention(sb_qkv.reshape(B, S, 3 * W))
        y_c = _ssd(ssm_z.reshape(B, S, W), ssm_xbc.reshape(B, S, 3 * W), ssm_dt.reshape(B, S, LANES),
                   ssm_conv_w[l], ssm_conv_b[l][None, :], _pad_lanes(ssm_dt_bias[l]), _pad_lanes(ssm_a_log[l]),
                   jnp.repeat(ssm_d[l], HEAD_DIM)[None, :])
        y_d = _moba(mb_qkv.reshape(B, S, 3 * W), cos, sin)
        x2d = _outproj([y.reshape(T, W) for y in (y_a, y_b, y_c, y_d)], group_norm_g[l].reshape(4, W),
                       w_out[l].astype(BF16), x2d)

        kv = _memkv(mem, mem_norm_g[l][None, :], xattn_wkv[l].astype(BF16))
        x2d = _xattn(x2d.reshape(B, S, D), xattn_norm_g[l][None, :], xattn_wq[l].astype(BF16), kv,
                     xattn_wo[l].astype(BF16)).reshape(T, D)

        w_r = jnp.pad(jnp.concatenate([router_expert_w[l], router_group_w[l]], axis=1),
                      ((0, 0), (0, LANES - MOE_EXPERTS - MOE_GROUPS)))
        w_r_hi = w_r.astype(BF16)
        w_r_lo = (w_r - w_r_hi.astype(F32)).astype(BF16)
        b_r = _pad_lanes(jnp.concatenate([router_expert_b[l], router_group_b[l]]))
        x2d = _moe(x2d, ffn_norm_g[l][None, :], jnp.stack([w_r_hi, w_r_lo]), b_r,
                   expert_w1[l].astype(BF16), expert_w3[l].astype(BF16), expert_w2[l].astype(BF16),
                   final_norm_g[None, :], final_norm=(l == depth - 1))
    return x2d.reshape(B, S, D)
```

```python
import functools
import math

import jax
import jax.numpy as jnp
from jax import lax
from jax.experimental import pallas as pl
from jax.experimental.pallas import tpu as pltpu

F32 = jnp.float32
BF16 = jnp.bfloat16

D_MODEL = 1024
GROUP_WIDTH = 256
HEAD_DIM = 64
N_HEADS = 4
NORM_EPS = 1e-6
CONV_WIDTH = 4
LRU_C = 8.0
SB_BLOCK = 128
SB_WINDOW_BLOCKS = 3
SB_CHAINS = 2
SSM_CHUNK = 128
SSM_STATE = 128
MOBA_BLOCK = 256
MOBA_TOPK = 3
ROPE_THETA = 10000.0
XATTN_HEADS = 4
XATTN_HEAD_DIM = 256
MEM_LEN = 256
MOE_GROUPS = 4
MOE_EPG = 4
MOE_EXPERTS = 16
MOE_FF = 256
LANES = 128
SUBLANES = 8
NEG_BIG = -1e30
SB_EXP_FLOOR = -104.0
IN_OUT_WIDTHS = (512, 768, 256, 768, LANES, 768)
IN_OUT_DTYPES = (F32, BF16, F32, F32, F32, F32)
VMEM_LIMIT = 56 * 1024 * 1024


def _cparams(n_axes):
    return pltpu.CompilerParams(dimension_semantics=("arbitrary",) * n_axes,
                                vmem_limit_bytes=VMEM_LIMIT)


def _dot(a, b):
    return jnp.dot(a, b, preferred_element_type=F32)


def _dot_t(a, b):
    return lax.dot_general(a, b, (((1,), (1,)), ((), ())), preferred_element_type=F32)


def _dot_tl(a, b):
    return lax.dot_general(a, b, (((0,), (0,)), ((), ())), preferred_element_type=F32)


def _split2(x):
    hi = x.astype(BF16)
    lo = (x - hi.astype(F32)).astype(BF16)
    return hi, lo


def _split3(x):
    hi = x.astype(BF16)
    r = x - hi.astype(F32)
    mid = r.astype(BF16)
    lo = (r - mid.astype(F32)).astype(BF16)
    return hi, mid, lo


def _dot_wide_lhs(x, m_bf16, parts=3):
    pieces = _split3(x) if parts == 3 else _split2(x)
    out = _dot(pieces[0], m_bf16)
    for p in pieces[1:]:
        out = out + _dot(p, m_bf16)
    return out


def _rmsnorm(x, g):
    return x * lax.rsqrt(jnp.mean(x * x, axis=-1, keepdims=True) + NORM_EPS) * g


def _softplus(x):
    return jnp.maximum(x, 0.0) + jnp.log(1.0 + jnp.exp(-jnp.abs(x)))


def _sigmoid(x):
    return 1.0 / (1.0 + jnp.exp(-x))


def _silu(x):
    return x * _sigmoid(x)


def _gelu_tanh(x):
    return 0.5 * x * (1.0 + jnp.tanh(math.sqrt(2.0 / math.pi) * (x + 0.044715 * (x * x * x))))


def _causal_conv(x, w_ref, b_ref):
    def taps(v, mask_rows):
        y = v * w_ref[CONV_WIDTH - 1:CONV_WIDTH, :] + b_ref[...]
        for s in range(1, CONV_WIDTH):
            vs = pltpu.roll(v, s, 0)
            if mask_rows is not None:
                vs = jnp.where(mask_rows >= s, vs, 0.0)
            y = y + vs * w_ref[CONV_WIDTH - 1 - s:CONV_WIDTH - s, :]
        return y

    head = x[0:SUBLANES, :]
    y_head = taps(head, lax.broadcasted_iota(jnp.int32, head.shape, 0))
    return jnp.concatenate([y_head, taps(x, None)[SUBLANES:, :]], axis=0)


def _inproj_kernel(x_ref, g_ref, wa_ref, wdt_ref, wmb_ref, *o_refs):
    h = _rmsnorm(x_ref[...], g_ref[...]).astype(BF16)
    off = 0
    for o_ref, width in zip(o_refs[:4], IN_OUT_WIDTHS[:4]):
        o_ref[...] = _dot(h, wa_ref[:, off:off + width]).astype(o_ref.dtype)
        off += width
    o_refs[4][...] = _dot(h, wdt_ref[...])
    o_refs[5][...] = _dot(h, wmb_ref[...])


def _inproj(x2d, g, w_main, w_dt, w_mb, tm=512):
    T = x2d.shape[0]
    const = lambda a: pl.BlockSpec(a.shape, lambda i: (0, 0))
    return pl.pallas_call(
        _inproj_kernel,
        grid=(T // tm,),
        in_specs=[pl.BlockSpec((tm, D_MODEL), lambda i: (i, 0)),
                  pl.BlockSpec((1, D_MODEL), lambda i: (0, 0)),
                  const(w_main), const(w_dt), const(w_mb)],
        out_specs=[pl.BlockSpec((tm, w), lambda i: (i, 0)) for w in IN_OUT_WIDTHS],
        out_shape=[jax.ShapeDtypeStruct((T, w), dt) for w, dt in zip(IN_OUT_WIDTHS, IN_OUT_DTYPES)],
        compiler_params=_cparams(1),
        name="inproj",
    )(x2d, g, w_main, w_dt, w_mb)


def _lru_kernel(xg_ref, cw_ref, cb_ref, wbd_ref, bri_ref, lam_ref, o_ref):
    S = xg_ref.shape[0]
    W = GROUP_WIDTH
    xc = _causal_conv(xg_ref[:, 0:W], cw_ref, cb_ref)
    ri = _dot(xc.astype(BF16), wbd_ref[...]) + bri_ref[...]
    r = _sigmoid(ri[:, 0:W])
    i = _sigmoid(ri[:, W:2 * W])
    log_a = (LRU_C * r) * (-_softplus(-lam_ref[...]))
    a = jnp.exp(log_a)
    u = jnp.sqrt(1.0 - jnp.exp(2.0 * log_a)) * (i * xc)
    rows = lax.broadcasted_iota(jnp.int32, (S, W), 0)
    shift = 1
    while shift < S:
        if shift < SUBLANES:
            keep = rows >= shift
            a_s = jnp.where(keep, pltpu.roll(a, shift, 0), 1.0)
            u_s = jnp.where(keep, pltpu.roll(u, shift, 0), 0.0)
            u = a * u_s + u
            a = a * a_s
        else:
            u = jnp.concatenate([u[:shift], a[shift:] * u[:S - shift] + u[shift:]], axis=0)
            a = jnp.concatenate([a[:shift], a[shift:] * a[:S - shift]], axis=0)
        shift *= 2
    o_ref[...] = u * _gelu_tanh(xg_ref[:, W:2 * W])


def _lru(xg, conv_w, conv_b, w_bd, b_ri, lam):
    B, S, _ = xg.shape
    W = GROUP_WIDTH
    full = lambda shape: pl.BlockSpec(shape, lambda b: (0,) * len(shape))
    return pl.pallas_call(
        _lru_kernel,
        grid=(B,),
        in_specs=[pl.BlockSpec((None, S, 2 * W), lambda b: (b, 0, 0)),
                  full((CONV_WIDTH, W)), full((1, W)), full((W, 2 * W)), full((1, 2 * W)), full((1, W))],
        out_specs=pl.BlockSpec((None, S, W), lambda b: (b, 0, 0)),
        out_shape=jax.ShapeDtypeStruct((B, S, W), F32),
        compiler_params=_cparams(1),
        name="rglru",
    )(xg, conv_w, conv_b, w_bd, b_ri, lam)


def _sb_kernel(qkv_ref, o_ref, k_s, v_s, acc_s, later_s):
    i = pl.program_id(1)
    W = GROUP_WIDTH
    TB = SB_BLOCK
    R = N_HEADS * TB

    NW = SB_WINDOW_BLOCKS
    KW = NW * TB
    PAD = (NW - 1) * TB
    S = qkv_ref.shape[0]

    @pl.when(i == 0)
    def _():
        k_s[0:PAD, :] = jnp.zeros((PAD, W), BF16)
        v_s[0:PAD, :] = jnp.zeros((PAD, W), BF16)
        k_s[PAD:PAD + S, :] = qkv_ref[:, W:2 * W].astype(BF16)
        v_s[PAD:PAD + S, :] = qkv_ref[:, 2 * W:3 * W].astype(BF16)

    lane = lax.broadcasted_iota(jnp.int32, (TB, W), 1)
    heads = [(lane >= h * HEAD_DIM) & (lane < (h + 1) * HEAD_DIM) for h in range(N_HEADS)]
    r_loc = lax.broadcasted_iota(jnp.int32, (R, KW), 0) & (TB - 1)
    c_loc = lax.broadcasted_iota(jnp.int32, (R, KW), 1)
    ur = lax.broadcasted_iota(jnp.int32, (TB, 2 * TB), 0)
    uc = lax.broadcasted_iota(jnp.int32, (TB, 2 * TB), 1)
    tri_ones = jnp.where((ur > uc) | (uc >= TB), 1.0, 0.0).astype(BF16)

    n_chains = o_ref.shape[0]
    blocks = [i + c * (S // TB // n_chains) for c in range(n_chains)]
    qss = []
    for blk in blocks:
        q = qkv_ref[pl.ds(pl.multiple_of(blk * TB, TB), TB), 0:W] * (HEAD_DIM ** -0.5)
        qss.append(jnp.concatenate([jnp.where(hm, q, 0.0) for hm in heads], axis=0).astype(BF16))

    acc_s[...] = jnp.zeros_like(acc_s)
    later_s[...] = jnp.zeros_like(later_s)

    def window(c, n):
        blk = blocks[c]
        first_key = (blk - n * NW - (NW - 1)) * TB
        rows = pl.ds(pl.multiple_of(jnp.maximum(first_key + PAD, 0), TB), KW)
        z = _dot_t(qss[c], k_s[rows, :])
        key_abs = first_key + c_loc
        live = (key_abs < blk * TB + r_loc) & (key_abs >= 0)
        sp = _softplus(z)
        lf = jnp.where(live, -sp, 0.0)
        lf_hi, lf_lo = _split2(lf)
        order = list(range(NW - 1, -1, -1))
        stacked = jnp.concatenate([piece[:, b * TB:(b + 1) * TB] for piece in (lf_hi, lf_lo) for b in order], axis=0)
        cs2 = _dot(stacked, tri_ones)
        offset = later_s[c]
        after = [None] * NW
        for pos, b in enumerate(order):
            cs = cs2[pos * R:(pos + 1) * R, :] + cs2[(NW + pos) * R:(NW + pos + 1) * R, :]
            after[b] = cs[:, 0:TB] + offset
            offset = offset + cs[:, TB:2 * TB]
        w = jnp.where(live, jnp.exp((z - sp) + jnp.concatenate(after, axis=1)), 0.0)
        acc_s[c] += _dot(w.astype(BF16), v_s[rows, :])
        later_s[c] = offset
        return jnp.where((n + 1) * NW <= blk, jnp.max(offset), SB_EXP_FLOOR)

    def cond(carry):
        return carry[1] > SB_EXP_FLOOR

    def body(carry):
        n = carry[0]
        later_max = window(0, n)
        for c in range(1, n_chains):
            later_max = jnp.maximum(later_max, window(c, n))
        return n + 1, later_max

    lax.while_loop(cond, body, (jnp.int32(0), jnp.float32(0.0)))
    for c in range(n_chains):
        out = acc_s[c, 0:TB, :]
        for h in range(1, N_HEADS):
            out = jnp.where(heads[h], acc_s[c, h * TB:(h + 1) * TB, :], out)
        o_ref[c] = out


def _sb_attention(qkv):
    B, S, _ = qkv.shape
    W = GROUP_WIDTH
    pad = (SB_WINDOW_BLOCKS - 1) * SB_BLOCK
    nc = SB_CHAINS
    rows = N_HEADS * SB_BLOCK
    out = pl.pallas_call(
        _sb_kernel,
        grid=(B, S // SB_BLOCK // nc),
        in_specs=[pl.BlockSpec((None, S, 3 * W), lambda b, i: (b, 0, 0))],
        out_specs=pl.BlockSpec((None, nc, SB_BLOCK, W), lambda b, i: (b, 0, i, 0)),
        out_shape=jax.ShapeDtypeStruct((B, nc, S // nc, W), F32),
        scratch_shapes=[pltpu.VMEM((S + pad, W), BF16), pltpu.VMEM((S + pad, W), BF16),
                        pltpu.VMEM((nc, rows, W), F32), pltpu.VMEM((nc, rows, SB_BLOCK), F32)],
        compiler_params=_cparams(2),
        name="stickbreak",
    )(qkv)
    return out.reshape(B, S, W)


def _ssd_kernel(z_ref, xbc_ref, dt_ref, cw_ref, cb_ref, dtb_ref, alog_ref, dskip_ref, o_ref, xbc_s):
    S = z_ref.shape[0]
    W = GROUP_WIDTH
    L = SSM_CHUNK
    xbc_s[...] = _silu(_causal_conv(xbc_ref[...], cw_ref, cb_ref))
    a_row = -jnp.exp(alog_ref[...])

    r_i = lax.broadcasted_iota(jnp.int32, (L, L), 0)
    c_i = lax.broadcasted_iota(jnp.int32, (L, L), 1)
    tri_incl = jnp.where(c_i <= r_i, 1.0, 0.0).astype(BF16)
    lower = c_i <= r_i
    e_r = lax.broadcasted_iota(jnp.int32, (LANES, W), 0)
    e_c = lax.broadcasted_iota(jnp.int32, (LANES, W), 1)
    expand = jnp.where((e_c >= e_r * HEAD_DIM) & (e_c < (e_r + 1) * HEAD_DIM), 1.0, 0.0).astype(BF16)
    lane_l = lax.broadcasted_iota(jnp.int32, (L, LANES), 1)

    def chunk(c, states):
        rows = slice(c * L, (c + 1) * L)
        xs = xbc_s[rows, 0:W]
        dt = _softplus(dt_ref[rows, :] + dtb_ref[...])
        a_dt = dt * a_row
        cs_col = _dot_wide_lhs_rhs(tri_incl, a_dt)
        cs_row = cs_col.T
        cs_full = _dot_wide_lhs(cs_col, expand)
        dt_full = _dot_wide_lhs(dt, expand)
        xd = xs * dt_full
        tot = cs_full[L - 1:L, :]
        xdec = (xd * jnp.exp(tot - cs_full)).astype(BF16)
        xd16 = xd.astype(BF16)
        ys = []
        new_states = []
        for g in range(2):
            gl = slice(g * LANES, (g + 1) * LANES)
            bm = xbc_s[rows, W + g * SSM_STATE:W + (g + 1) * SSM_STATE].astype(BF16)
            cm = xbc_s[rows, 2 * W + g * SSM_STATE:2 * W + (g + 1) * SSM_STATE].astype(BF16)
            cb = _dot_t(cm, bm)
            prev = states[g]
            y_off = _dot(cm, prev.astype(BF16)) * jnp.exp(cs_full[:, gl])
            y_g = y_off
            for hh in range(2):
                h = 2 * g + hh
                seg = jnp.where(lower, cs_col[:, h:h + 1] - cs_row[h:h + 1, :], -jnp.inf)
                y_h = _dot((cb * jnp.exp(seg)).astype(BF16), xd16[:, gl])
                in_head = (lane_l >= hh * HEAD_DIM) & (lane_l < (hh + 1) * HEAD_DIM)
                y_g = y_g + jnp.where(in_head, y_h, 0.0)
            new_states.append(prev * jnp.exp(tot[:, gl]) + _dot_tl(bm, xdec[:, gl]))
            ys.append(y_g)
        y = jnp.concatenate(ys, axis=1) + dskip_ref[...] * xs
        o_ref[rows, :] = y * _silu(z_ref[rows, :])
        return new_states

    states = [jnp.zeros((SSM_STATE, LANES), F32) for _ in range(2)]
    for c in range(S // L):
        states = chunk(c, states)


def _dot_wide_lhs_rhs(m_bf16, x):
    hi, mid, lo = _split3(x)
    return _dot(m_bf16, hi) + _dot(m_bf16, mid) + _dot(m_bf16, lo)


def _ssd(z, xbc, dt, conv_w, conv_b, dt_bias, a_log, d_skip):
    B, S, _ = z.shape
    W = GROUP_WIDTH
    full = lambda shape: pl.BlockSpec(shape, lambda b: (0,) * len(shape))
    return pl.pallas_call(
        _ssd_kernel,
        grid=(B,),
        in_specs=[pl.BlockSpec((None, S, W), lambda b: (b, 0, 0)),
                  pl.BlockSpec((None, S, 3 * W), lambda b: (b, 0, 0)),
                  pl.BlockSpec((None, S, LANES), lambda b: (b, 0, 0)),
                  full((CONV_WIDTH, 3 * W)), full((1, 3 * W)), full((1, LANES)), full((1, LANES)),
                  full((1, W))],
        out_specs=pl.BlockSpec((None, S, W), lambda b: (b, 0, 0)),
        out_shape=jax.ShapeDtypeStruct((B, S, W), F32),
        scratch_shapes=[pltpu.VMEM((S, 3 * W), F32)],
        compiler_params=_cparams(1),
        name="ssd",
    )(z, xbc, dt, conv_w, conv_b, dt_bias, a_log, d_skip)


def _rope(x, cos, sin):
    lane = lax.broadcasted_iota(jnp.int32, (x.shape[0], LANES), 1)
    first_half = (lane % HEAD_DIM) < (HEAD_DIM // 2)
    halves = []
    for p in range(x.shape[1] // LANES):
        xp = x[:, p * LANES:(p + 1) * LANES]
        fwd = pltpu.roll(xp, HEAD_DIM // 2, 1)
        bwd = pltpu.roll(xp, LANES - HEAD_DIM // 2, 1)
        halves.append(jnp.where(first_half, -bwd, fwd))
    rot = jnp.concatenate(halves, axis=1)
    return x * cos + rot * sin


def _moba_kernel(qkv_ref, cos_ref, sin_ref, o_ref, k_s, v_s, kmean_s, acc_s):
    i = pl.program_id(1)
    W = GROUP_WIDTH
    TB = MOBA_BLOCK
    S = qkv_ref.shape[0]
    NB = S // TB

    @pl.when(i == 0)
    def _():
        for blk in range(NB):
            rs = slice(blk * TB, (blk + 1) * TB)
            kb = _rope(qkv_ref[rs, W:2 * W], cos_ref[rs, :], sin_ref[rs, :])
            k_s[rs, :] = kb.astype(BF16)
            kmean_s[blk:blk + 1, :] = jnp.mean(kb, axis=0, keepdims=True)
        v_s[...] = qkv_ref[:, 2 * W:3 * W].astype(BF16)

    rows_i = pl.ds(pl.multiple_of(i * TB, TB), TB)
    q = _rope(qkv_ref[rows_i, 0:W], cos_ref[rows_i, :], sin_ref[rows_i, :])
    lane = lax.broadcasted_iota(jnp.int32, (TB, W), 1)
    lane8 = lax.broadcasted_iota(jnp.int32, (NB, W), 1)
    blk_id = lax.broadcasted_iota(jnp.int32, (NB, TB), 0)
    R = N_HEADS * TB
    r_loc = lax.broadcasted_iota(jnp.int32, (R, TB), 0) & (TB - 1)
    c_loc = lax.broadcasted_iota(jnp.int32, (R, TB), 1)
    kmean = kmean_s[...]
    q_hi, q_lo = _split2(q)
    scale = HEAD_DIM ** -0.5

    heads = [(lane >= h * HEAD_DIM) & (lane < (h + 1) * HEAD_DIM) for h in range(N_HEADS)]
    biases, qhs = [], []
    for h in range(N_HEADS):
        head8 = (lane8 >= h * HEAD_DIM) & (lane8 < (h + 1) * HEAD_DIM)
        km_hi, km_lo = _split2(jnp.where(head8, kmean, 0.0))
        gate = _dot_t(km_hi, q_hi) + _dot_t(km_hi, q_lo) + _dot_t(km_lo, q_hi)
        cnt = jnp.zeros((NB, TB), F32)
        for jp in range(NB):
            row = gate[jp:jp + 1, :]
            beats = (row > gate) | ((row == gate) & (blk_id > jp))
            cnt = cnt + jnp.where(beats, jnp.where(jp < i, 1.0, 0.0), 0.0)
        selected = (cnt < float(MOBA_TOPK)) & (blk_id < i)
        biases.append(jnp.where(selected, 0.0, NEG_BIG))
        qhs.append(jnp.where(heads[h], q, 0.0) * scale)
    bias_all = jnp.concatenate(biases, axis=1).astype(BF16)
    qs = jnp.concatenate(qhs, axis=0).astype(BF16)

    s = jnp.where(c_loc <= r_loc, _dot_t(qs, k_s[rows_i, :]), NEG_BIG)
    m = jnp.max(s, axis=-1, keepdims=True)
    p = jnp.exp(s - m)
    l = jnp.sum(p, axis=-1, keepdims=True)
    acc_s[...] = _dot(p.astype(BF16), v_s[rows_i, :])

    def body(j, carry):
        m, l = carry
        rows = pl.ds(pl.multiple_of(j * TB, TB), TB)
        pick = jnp.where(blk_id == j, 1.0, 0.0).astype(BF16)
        s = _dot_t(qs, k_s[rows, :]) + _dot_tl(bias_all, pick)
        m_new = jnp.maximum(m, jnp.max(s, axis=-1, keepdims=True))
        alpha = jnp.exp(m - m_new)
        p = jnp.exp(s - m_new)
        l = alpha * l + jnp.sum(p, axis=-1, keepdims=True)
        acc_s[...] = alpha * acc_s[...] + _dot(p.astype(BF16), v_s[rows, :])
        return m_new, l

    m, l = lax.fori_loop(0, i, body, (m, l))
    outs = acc_s[...] / l
    out = outs[0:TB, :]
    for h in range(1, N_HEADS):
        out = jnp.where(heads[h], outs[h * TB:(h + 1) * TB, :], out)
    o_ref[...] = out


def _moba(qkv, cos, sin):
    B, S, _ = qkv.shape
    W = GROUP_WIDTH
    return pl.pallas_call(
        _moba_kernel,
        grid=(B, S // MOBA_BLOCK),
        in_specs=[pl.BlockSpec((None, S, 3 * W), lambda b, i: (b, 0, 0)),
                  pl.BlockSpec((S, W), lambda b, i: (0, 0)),
                  pl.BlockSpec((S, W), lambda b, i: (0, 0))],
        out_specs=pl.BlockSpec((None, MOBA_BLOCK, W), lambda b, i: (b, i, 0)),
        out_shape=jax.ShapeDtypeStruct((B, S, W), F32),
        scratch_shapes=[pltpu.VMEM((S, W), BF16), pltpu.VMEM((S, W), BF16),
                        pltpu.VMEM((S // MOBA_BLOCK, W), F32), pltpu.VMEM((N_HEADS * MOBA_BLOCK, W), F32)],
        compiler_params=_cparams(2),
        name="moba",
    )(qkv, cos, sin)


def _outproj_kernel(ya_ref, yb_ref, yc_ref, yd_ref, gg_ref, w_ref, x_ref, o_ref):
    W = GROUP_WIDTH
    acc = x_ref[...]
    for g, y_ref in enumerate((ya_ref, yb_ref, yc_ref, yd_ref)):
        yn = _rmsnorm(y_ref[...], gg_ref[g:g + 1, :]).astype(BF16)
        acc = acc + _dot(yn, w_ref[g * W:(g + 1) * W, :])
    o_ref[...] = acc


def _outproj(ys, gg, w_out, x2d, tm=512):
    T = x2d.shape[0]
    W = GROUP_WIDTH
    return pl.pallas_call(
        _outproj_kernel,
        grid=(T // tm,),
        in_specs=[pl.BlockSpec((tm, W), lambda i: (i, 0))] * 4
                 + [pl.BlockSpec((4, W), lambda i: (0, 0)),
                    pl.BlockSpec((4 * W, D_MODEL), lambda i: (0, 0)),
                    pl.BlockSpec((tm, D_MODEL), lambda i: (i, 0))],
        out_specs=pl.BlockSpec((tm, D_MODEL), lambda i: (i, 0)),
        out_shape=jax.ShapeDtypeStruct((T, D_MODEL), F32),
        compiler_params=_cparams(1),
        name="outproj",
    )(*ys, gg, w_out, x2d)


def _memkv_kernel(m_ref, g_ref, w_ref, o_ref):
    mn = _rmsnorm(m_ref[...], g_ref[...]).astype(BF16)
    o_ref[...] = _dot(mn, w_ref[...]).astype(BF16)


def _memkv(mem, g, wkv):
    B, M, _ = mem.shape
    return pl.pallas_call(
        _memkv_kernel,
        grid=(B,),
        in_specs=[pl.BlockSpec((None, M, D_MODEL), lambda b: (b, 0, 0)),
                  pl.BlockSpec((1, D_MODEL), lambda b: (0, 0)),
                  pl.BlockSpec((D_MODEL, 2 * D_MODEL), lambda b: (0, 0))],
        out_specs=pl.BlockSpec((None, M, 2 * D_MODEL), lambda b: (b, 0, 0)),
        out_shape=jax.ShapeDtypeStruct((B, M, 2 * D_MODEL), BF16),
        compiler_params=_cparams(1),
        name="memkv",
    )(mem, g, wkv)


def _xattn_kernel(x_ref, g_ref, wq_ref, kv_ref, wo_ref, o_ref):
    x = x_ref[...]
    h = _rmsnorm(x, g_ref[...]).astype(BF16)
    q = (_dot(h, wq_ref[...]) * (XATTN_HEAD_DIM ** -0.5)).astype(BF16)
    acc = x
    for hd in range(XATTN_HEADS):
        cols = slice(hd * XATTN_HEAD_DIM, (hd + 1) * XATTN_HEAD_DIM)
        vcols = slice(D_MODEL + hd * XATTN_HEAD_DIM, D_MODEL + (hd + 1) * XATTN_HEAD_DIM)
        s = _dot_t(q[:, cols], kv_ref[:, cols])
        p = jnp.exp(s - jnp.max(s, axis=-1, keepdims=True))
        p = p / jnp.sum(p, axis=-1, keepdims=True)
        o = _dot(p.astype(BF16), kv_ref[:, vcols]).astype(BF16)
        acc = acc + _dot(o, wo_ref[cols, :])
    o_ref[...] = acc


def _xattn(x3d, g, wq, kv, wo, tm=1024):
    B, S, _ = x3d.shape
    M = kv.shape[1]
    return pl.pallas_call(
        _xattn_kernel,
        grid=(B, S // tm),
        in_specs=[pl.BlockSpec((None, tm, D_MODEL), lambda b, i: (b, i, 0)),
                  pl.BlockSpec((1, D_MODEL), lambda b, i: (0, 0)),
                  pl.BlockSpec((D_MODEL, D_MODEL), lambda b, i: (0, 0)),
                  pl.BlockSpec((None, M, 2 * D_MODEL), lambda b, i: (b, 0, 0)),
                  pl.BlockSpec((D_MODEL, D_MODEL), lambda b, i: (0, 0))],
        out_specs=pl.BlockSpec((None, tm, D_MODEL), lambda b, i: (b, i, 0)),
        out_shape=jax.ShapeDtypeStruct((B, S, D_MODEL), F32),
        compiler_params=_cparams(2),
        name="xattn",
    )(x3d, g, wq, kv, wo)


def _moe_kernel(x_ref, g_ref, wr_ref, br_ref, w1_ref, w3_ref, w2_ref, fg_ref, o_ref,
                t_s, comb_s, acc_s, *, final_norm):
    grp = pl.program_id(1)
    tm = x_ref.shape[0]

    @pl.when(grp == 0)
    def _():
        t = _rmsnorm(x_ref[...], g_ref[...])
        t_s[...] = t.astype(BF16)
        acc_s[...] = jnp.zeros_like(acc_s)
        t_hi, t_lo = _split2(t)
        logits = (_dot(t_hi, wr_ref[0]) + _dot(t_hi, wr_ref[1]) + _dot(t_lo, wr_ref[0])) + br_ref[...]
        lane = lax.broadcasted_iota(jnp.int32, (tm, LANES), 1)
        lane_f = lane.astype(F32)
        is_g = (lane >= MOE_EXPERTS) & (lane < MOE_EXPERTS + MOE_GROUPS)
        lg = jnp.where(is_g, logits, NEG_BIG)
        gmax = jnp.max(lg, axis=-1, keepdims=True)
        pg_top = 1.0 / jnp.sum(jnp.exp(lg - gmax), axis=-1, keepdims=True)
        g_lane = jnp.min(jnp.where(is_g & (lg == gmax), lane_f, 1e9), axis=-1, keepdims=True)
        g_idx = g_lane - float(MOE_EXPERTS)
        e_grp = jnp.floor(lane_f * (1.0 / MOE_EPG))
        in_grp = (lane < MOE_EXPERTS) & (e_grp == g_idx)
        le = jnp.where(in_grp, logits, NEG_BIG)
        e1 = jnp.max(le, axis=-1, keepdims=True)
        i1 = jnp.min(jnp.where(in_grp & (le == e1), lane_f, 1e9), axis=-1, keepdims=True)
        le2 = jnp.where(lane_f == i1, NEG_BIG, le)
        e2 = jnp.max(le2, axis=-1, keepdims=True)
        i2 = jnp.min(jnp.where(in_grp & (lane_f != i1) & (le2 == e2), lane_f, 1e9), axis=-1, keepdims=True)
        r2 = jnp.exp(e2 - e1)
        w_first = 1.0 / (1.0 + r2)
        w_second = r2 / (1.0 + r2)
        comb_s[...] = pg_top * (jnp.where(lane_f == i1, w_first, 0.0) + jnp.where(lane_f == i2, w_second, 0.0))

    t = t_s[...]
    comb = comb_s[...]
    lane = lax.broadcasted_iota(jnp.int32, (tm, LANES), 1)
    upd = jnp.zeros((tm, D_MODEL), F32)
    for e in range(MOE_EPG):
        n = grp * MOE_EPG + e
        c = jnp.sum(jnp.where(lane == n, comb, 0.0), axis=-1, keepdims=True)
        hid = _silu(_dot(t, w1_ref[e])) * _dot(t, w3_ref[e])
        upd = upd + _dot((hid * c).astype(BF16), w2_ref[e])
    acc_s[...] += upd

    @pl.when(grp == MOE_GROUPS - 1)
    def _():
        y = x_ref[...] + acc_s[...]
        if final_norm:
            y = _rmsnorm(y, fg_ref[...])
        o_ref[...] = y


def _moe(x2d, g, w_router, b_router, w1, w3, w2, final_g, final_norm, tm=1024):
    T = x2d.shape[0]
    return pl.pallas_call(
        functools.partial(_moe_kernel, final_norm=final_norm),
        grid=(T // tm, MOE_GROUPS),
        in_specs=[pl.BlockSpec((tm, D_MODEL), lambda i, e: (i, 0)),
                  pl.BlockSpec((1, D_MODEL), lambda i, e: (0, 0)),
                  pl.BlockSpec((2, D_MODEL, LANES), lambda i, e: (0, 0, 0)),
                  pl.BlockSpec((1, LANES), lambda i, e: (0, 0)),
                  pl.BlockSpec((MOE_EPG, D_MODEL, MOE_FF), lambda i, e: (e, 0, 0)),
                  pl.BlockSpec((MOE_EPG, D_MODEL, MOE_FF), lambda i, e: (e, 0, 0)),
                  pl.BlockSpec((MOE_EPG, MOE_FF, D_MODEL), lambda i, e: (e, 0, 0)),
                  pl.BlockSpec((1, D_MODEL), lambda i, e: (0, 0))],
        out_specs=pl.BlockSpec((tm, D_MODEL), lambda i, e: (i, 0)),
        out_shape=jax.ShapeDtypeStruct((T, D_MODEL), F32),
        scratch_shapes=[pltpu.VMEM((tm, D_MODEL), BF16), pltpu.VMEM((tm, LANES), F32),
                        pltpu.VMEM((tm, D_MODEL), F32)],
        compiler_params=_cparams(2),
        name="moe",
    )(x2d, g, w_router, b_router, w1, w3, w2, final_g)


def _pad_lanes(v, width=LANES):
    return jnp.pad(v, (0, width - v.shape[0]))[None, :]


def _block_diag(w):
    H, n, _ = w.shape
    eye = jnp.eye(H, dtype=w.dtype)
    return (eye[:, None, :, None] * w[:, :, None, :]).reshape(H * n, H * n)


def _rope_tables(S):
    half = HEAD_DIM // 2
    inv_freq = ROPE_THETA ** (-jnp.arange(half, dtype=F32) / half)
    ang = jnp.arange(S, dtype=F32)[:, None] * inv_freq[None, :]
    reps = GROUP_WIDTH // half
    return jnp.tile(jnp.cos(ang), (1, reps)), jnp.tile(jnp.sin(ang), (1, reps))


def kernel(x, mem, mix_norm_g, w_in, lru_conv_w, lru_conv_b, lru_wr, lru_br, lru_wi, lru_bi, lru_lambda, ssm_conv_w, ssm_conv_b, ssm_dt_bias, ssm_a_log, ssm_d, group_norm_g, w_out, xattn_norm_g, mem_norm_g, xattn_wq, xattn_wkv, xattn_wo, ffn_norm_g, router_group_w, router_group_b, router_expert_w, router_expert_b, expert_w1, expert_w3, expert_w2, final_norm_g):
    B, S, D = x.shape
    T = B * S
    depth = w_in.shape[0]
    W = GROUP_WIDTH
    cos, sin = _rope_tables(S)
    n_main = 2 * W + 3 * W + W + 3 * W
    x2d = x.reshape(T, D)
    for l in range(depth):
        w_main = w_in[l, :, :n_main].astype(BF16)
        w_dt = jnp.pad(w_in[l, :, n_main:n_main + N_HEADS], ((0, 0), (0, LANES - N_HEADS))).astype(BF16)
        w_mb = w_in[l, :, n_main + N_HEADS:].astype(BF16)
        lru_xg, sb_qkv, ssm_z, ssm_xbc, ssm_dt, mb_qkv = _inproj(x2d, mix_norm_g[l][None, :], w_main, w_dt, w_mb)

        w_bd = jnp.concatenate([_block_diag(lru_wr[l]), _block_diag(lru_wi[l])], axis=1).astype(BF16)
        b_ri = jnp.concatenate([lru_br[l], lru_bi[l]])[None, :]
        y_a = _lru(lru_xg.reshape(B, S, 2 * W), lru_conv_w[l], lru_conv_b[l][None, :], w_bd, b_ri,
                   lru_lambda[l][None, :])
        y_b = _sb_attention(sb_qkv.reshape(B, S, 3 * W))
        y_c = _ssd(ssm_z.reshape(B, S, W), ssm_xbc.reshape(B, S, 3 * W), ssm_dt.reshape(B, S, LANES),
                   ssm_conv_w[l], ssm_conv_b[l][None, :], _pad_lanes(ssm_dt_bias[l]), _pad_lanes(ssm_a_log[l]),
                   jnp.repeat(ssm_d[l], HEAD_DIM)[None, :])
        y_d = _moba(mb_qkv.reshape(B, S, 3 * W), cos, sin)
        x2d = _outproj([y.reshape(T, W) for y in (y_a, y_b, y_c, y_d)], group_norm_g[l].reshape(4, W),
                       w_out[l].astype(BF16), x2d)

        kv = _memkv(mem, mem_norm_g[l][None, :], xattn_wkv[l].astype(BF16))
        x2d = _xattn(x2d.reshape(B, S, D), xattn_norm_g[l][None, :], xattn_wq[l].astype(BF16), kv,
                     xattn_wo[l].astype(BF16)).reshape(T, D)

        w_r = jnp.pad(jnp.concatenate([router_expert_w[l], router_group_w[l]], axis=1),
                      ((0, 0), (0, LANES - MOE_EXPERTS - MOE_GROUPS)))
        w_r_hi = w_r.astype(BF16)
        w_r_lo = (w_r - w_r_hi.astype(F32)).astype(BF16)
        b_r = _pad_lanes(jnp.concatenate([router_expert_b[l], router_group_b[l]]))
        x2d = _moe(x2d, ffn_norm_g[l][None, :], jnp.stack([w_r_hi, w_r_lo]), b_r,
                   expert_w1[l].astype(BF16), expert_w3[l].astype(BF16), expert_w2[l].astype(BF16),
                   final_norm_g[None, :], final_norm=(l == depth - 1))
    return x2d.reshape(B, S, D)
```

```python
import functools
import math

import jax
import jax.numpy as jnp
from jax import lax
from jax.experimental import pallas as pl
from jax.experimental.pallas import tpu as pltpu

F32 = jnp.float32
BF16 = jnp.bfloat16

D_MODEL = 1024
GROUP_WIDTH = 256
HEAD_DIM = 64
N_HEADS = 4
NORM_EPS = 1e-6
CONV_WIDTH = 4
LRU_C = 8.0
SB_BLOCK = 128
SB_WINDOW_BLOCKS = 3
SB_CHAINS = 2
SSM_CHUNK = 128
SSM_STATE = 128
MOBA_BLOCK = 256
MOBA_TOPK = 3
ROPE_THETA = 10000.0
XATTN_HEADS = 4
XATTN_HEAD_DIM = 256
MEM_LEN = 256
MOE_GROUPS = 4
MOE_EPG = 4
MOE_EXPERTS = 16
MOE_FF = 256
LANES = 128
SUBLANES = 8
NEG_BIG = -1e30
SB_EXP_FLOOR = -104.0
IN_OUT_WIDTHS = (512, 768, 256, 768, LANES, 768)
IN_OUT_DTYPES = (F32, BF16, F32, F32, F32, F32)
VMEM_LIMIT = 56 * 1024 * 1024


def _cparams(n_axes):
    return pltpu.CompilerParams(dimension_semantics=("arbitrary",) * n_axes,
                                vmem_limit_bytes=VMEM_LIMIT)


def _dot(a, b):
    return jnp.dot(a, b, preferred_element_type=F32)


def _dot_t(a, b):
    return lax.dot_general(a, b, (((1,), (1,)), ((), ())), preferred_element_type=F32)


def _dot_tl(a, b):
    return lax.dot_general(a, b, (((0,), (0,)), ((), ())), preferred_element_type=F32)


def _split2(x):
    hi = x.astype(BF16)
    lo = (x - hi.astype(F32)).astype(BF16)
    return hi, lo


def _split3(x):
    hi = x.astype(BF16)
    r = x - hi.astype(F32)
    mid = r.astype(BF16)
    lo = (r - mid.astype(F32)).astype(BF16)
    return hi, mid, lo


def _dot_wide_lhs(x, m_bf16, parts=3):
    pieces = _split3(x) if parts == 3 else _split2(x)
    out = _dot(pieces[0], m_bf16)
    for p in pieces[1:]:
        out = out + _dot(p, m_bf16)
    return out


def _rmsnorm(x, g):
    return x * lax.rsqrt(jnp.mean(x * x, axis=-1, keepdims=True) + NORM_EPS) * g


def _softplus(x):
    return jnp.maximum(x, 0.0) + jnp.log(1.0 + jnp.exp(-jnp.abs(x)))


def _sigmoid(x):
    return 1.0 / (1.0 + jnp.exp(-x))


def _silu(x):
    return x * _sigmoid(x)


def _gelu_tanh(x):
    return 0.5 * x * (1.0 + jnp.tanh(math.sqrt(2.0 / math.pi) * (x + 0.044715 * (x * x * x))))


def _causal_conv(x, w_ref, b_ref):
    def taps(v, mask_rows):
        y = v * w_ref[CONV_WIDTH - 1:CONV_WIDTH, :] + b_ref[...]
        for s in range(1, CONV_WIDTH):
            vs = pltpu.roll(v, s, 0)
            if mask_rows is not None:
                vs = jnp.where(mask_rows >= s, vs, 0.0)
            y = y + vs * w_ref[CONV_WIDTH - 1 - s:CONV_WIDTH - s, :]
        return y

    head = x[0:SUBLANES, :]
    y_head = taps(head, lax.broadcasted_iota(jnp.int32, head.shape, 0))
    return jnp.concatenate([y_head, taps(x, None)[SUBLANES:, :]], axis=0)


def _inproj_kernel(x_ref, g_ref, wa_ref, wdt_ref, wmb_ref, *o_refs):
    h = _rmsnorm(x_ref[...], g_ref[...]).astype(BF16)
    off = 0
    for o_ref, width in zip(o_refs[:4], IN_OUT_WIDTHS[:4]):
        o_ref[...] = _dot(h, wa_ref[:, off:off + width]).astype(o_ref.dtype)
        off += width
    o_refs[4][...] = _dot(h, wdt_ref[...])
    o_refs[5][...] = _dot(h, wmb_ref[...])


def _inproj(x2d, g, w_main, w_dt, w_mb, tm=512):
    T = x2d.shape[0]
    const = lambda a: pl.BlockSpec(a.shape, lambda i: (0, 0))
    return pl.pallas_call(
        _inproj_kernel,
        grid=(T // tm,),
        in_specs=[pl.BlockSpec((tm, D_MODEL), lambda i: (i, 0)),
                  pl.BlockSpec((1, D_MODEL), lambda i: (0, 0)),
                  const(w_main), const(w_dt), const(w_mb)],
        out_specs=[pl.BlockSpec((tm, w), lambda i: (i, 0)) for w in IN_OUT_WIDTHS],
        out_shape=[jax.ShapeDtypeStruct((T, w), dt) for w, dt in zip(IN_OUT_WIDTHS, IN_OUT_DTYPES)],
        compiler_params=_cparams(1),
        name="inproj",
    )(x2d, g, w_main, w_dt, w_mb)


def _lru_kernel(xg_ref, cw_ref, cb_ref, wbd_ref, bri_ref, lam_ref, o_ref):
    S = xg_ref.shape[0]
    W = GROUP_WIDTH
    xc = _causal_conv(xg_ref[:, 0:W], cw_ref, cb_ref)
    ri = _dot(xc.astype(BF16), wbd_ref[...]) + bri_ref[...]
    r = _sigmoid(ri[:, 0:W])
    i = _sigmoid(ri[:, W:2 * W])
    log_a = (LRU_C * r) * (-_softplus(-lam_ref[...]))
    a = jnp.exp(log_a)
    u = jnp.sqrt(1.0 - jnp.exp(2.0 * log_a)) * (i * xc)
    rows = lax.broadcasted_iota(jnp.int32, (S, W), 0)
    shift = 1
    while shift < S:
        if shift < SUBLANES:
            keep = rows >= shift
            a_s = jnp.where(keep, pltpu.roll(a, shift, 0), 1.0)
            u_s = jnp.where(keep, pltpu.roll(u, shift, 0), 0.0)
            u = a * u_s + u
            a = a * a_s
        else:
            u = jnp.concatenate([u[:shift], a[shift:] * u[:S - shift] + u[shift:]], axis=0)
            a = jnp.concatenate([a[:shift], a[shift:] * a[:S - shift]], axis=0)
        shift *= 2
    o_ref[...] = u * _gelu_tanh(xg_ref[:, W:2 * W])


def _lru(xg, conv_w, conv_b, w_bd, b_ri, lam):
    B, S, _ = xg.shape
    W = GROUP_WIDTH
    full = lambda shape: pl.BlockSpec(shape, lambda b: (0,) * len(shape))
    return pl.pallas_call(
        _lru_kernel,
        grid=(B,),
        in_specs=[pl.BlockSpec((None, S, 2 * W), lambda b: (b, 0, 0)),
                  full((CONV_WIDTH, W)), full((1, W)), full((W, 2 * W)), full((1, 2 * W)), full((1, W))],
        out_specs=pl.BlockSpec((None, S, W), lambda b: (b, 0, 0)),
        out_shape=jax.ShapeDtypeStruct((B, S, W), F32),
        compiler_params=_cparams(1),
        name="rglru",
    )(xg, conv_w, conv_b, w_bd, b_ri, lam)


def _sb_kernel(qkv_ref, o_ref, k_s, v_s, acc_s, later_s):
    i = pl.program_id(1)
    W = GROUP_WIDTH
    TB = SB_BLOCK
    R = N_HEADS * TB

    NW = SB_WINDOW_BLOCKS
    KW = NW * TB
    PAD = (NW - 1) * TB
    S = qkv_ref.shape[0]

    @pl.when(i == 0)
    def _():
        k_s[0:PAD, :] = jnp.zeros((PAD, W), BF16)
        v_s[0:PAD, :] = jnp.zeros((PAD, W), BF16)
        k_s[PAD:PAD + S, :] = qkv_ref[:, W:2 * W].astype(BF16)
        v_s[PAD:PAD + S, :] = qkv_ref[:, 2 * W:3 * W].astype(BF16)

    lane = lax.broadcasted_iota(jnp.int32, (TB, W), 1)
    heads = [(lane >= h * HEAD_DIM) & (lane < (h + 1) * HEAD_DIM) for h in range(N_HEADS)]
    r_loc = lax.broadcasted_iota(jnp.int32, (R, KW), 0) & (TB - 1)
    c_loc = lax.broadcasted_iota(jnp.int32, (R, KW), 1)
    ur = lax.broadcasted_iota(jnp.int32, (TB, 2 * TB), 0)
    uc = lax.broadcasted_iota(jnp.int32, (TB, 2 * TB), 1)
    tri_ones = jnp.where((ur > uc) | (uc >= TB), 1.0, 0.0).astype(BF16)

    n_chains = o_ref.shape[0]
    blocks = [i + c * (S // TB // n_chains) for c in range(n_chains)]
    qss = []
    for blk in blocks:
        q = qkv_ref[pl.ds(pl.multiple_of(blk * TB, TB), TB), 0:W] * (HEAD_DIM ** -0.5)
        qss.append(jnp.concatenate([jnp.where(hm, q, 0.0) for hm in heads], axis=0).astype(BF16))

    acc_s[...] = jnp.zeros_like(acc_s)
    later_s[...] = jnp.zeros_like(later_s)

    def window(c, n):
        blk = blocks[c]
        first_key = (blk - n * NW - (NW - 1)) * TB
        rows = pl.ds(pl.multiple_of(jnp.maximum(first_key + PAD, 0), TB), KW)
        z = _dot_t(qss[c], k_s[rows, :])
        key_abs = first_key + c_loc
        live = (key_abs < blk * TB + r_loc) & (key_abs >= 0)
        sp = _softplus(z)
        lf = jnp.where(live, -sp, 0.0)
        lf_hi, lf_lo = _split2(lf)
        order = list(range(NW - 1, -1, -1))
        stacked = jnp.concatenate([piece[:, b * TB:(b + 1) * TB] for piece in (lf_hi, lf_lo) for b in order], axis=0)
        cs2 = _dot(stacked, tri_ones)
        offset = later_s[c]
        after = [None] * NW
        for pos, b in enumerate(order):
            cs = cs2[pos * R:(pos + 1) * R, :] + cs2[(NW + pos) * R:(NW + pos + 1) * R, :]
            after[b] = cs[:, 0:TB] + offset
            offset = offset + cs[:, TB:2 * TB]
        w = jnp.where(live, jnp.exp((z - sp) + jnp.concatenate(after, axis=1)), 0.0)
        acc_s[c] += _dot(w.astype(BF16), v_s[rows, :])
        later_s[c] = offset
        return jnp.where((n + 1) * NW <= blk, jnp.max(offset), SB_EXP_FLOOR)

    def cond(carry):
        return carry[1] > SB_EXP_FLOOR

    def body(carry):
        n = carry[0]
        later_max = window(0, n)
        for c in range(1, n_chains):
            later_max = jnp.maximum(later_max, window(c, n))
        return n + 1, later_max

    lax.while_loop(cond, body, (jnp.int32(0), jnp.float32(0.0)))
    for c in range(n_chains):
        out = acc_s[c, 0:TB, :]
        for h in range(1, N_HEADS):
            out = jnp.where(heads[h], acc_s[c, h * TB:(h + 1) * TB, :], out)
        o_ref[c] = out


def _sb_attention(qkv):
    B, S, _ = qkv.shape
    W = GROUP_WIDTH
    pad = (SB_WINDOW_BLOCKS - 1) * SB_BLOCK
    nc = SB_CHAINS
    rows = N_HEADS * SB_BLOCK
    out = pl.pallas_call(
        _sb_kernel,
        grid=(B, S // SB_BLOCK // nc),
        in_specs=[pl.BlockSpec((None, S, 3 * W), lambda b, i: (b, 0, 0))],
        out_specs=pl.BlockSpec((None, nc, SB_BLOCK, W), lambda b, i: (b, 0, i, 0)),
        out_shape=jax.ShapeDtypeStruct((B, nc, S // nc, W), F32),
        scratch_shapes=[pltpu.VMEM((S + pad, W), BF16), pltpu.VMEM((S + pad, W), BF16),
                        pltpu.VMEM((nc, rows, W), F32), pltpu.VMEM((nc, rows, SB_BLOCK), F32)],
        compiler_params=_cparams(2),
        name="stickbreak",
    )(qkv)
    return out.reshape(B, S, W)


def _ssd_kernel(z_ref, xbc_ref, dt_ref, cw_ref, cb_ref, dtb_ref, alog_ref, dskip_ref, o_ref, xbc_s):
    S = z_ref.shape[0]
    W = GROUP_WIDTH
    L = SSM_CHUNK
    xbc_s[...] = _silu(_causal_conv(xbc_ref[...], cw_ref, cb_ref))
    a_row = -jnp.exp(alog_ref[...])

    r_i = lax.broadcasted_iota(jnp.int32, (L, L), 0)
    c_i = lax.broadcasted_iota(jnp.int32, (L, L), 1)
    tri_incl = jnp.where(c_i <= r_i, 1.0, 0.0).astype(BF16)
    lower = c_i <= r_i
    e_r = lax.broadcasted_iota(jnp.int32, (LANES, W), 0)
    e_c = lax.broadcasted_iota(jnp.int32, (LANES, W), 1)
    expand = jnp.where((e_c >= e_r * HEAD_DIM) & (e_c < (e_r + 1) * HEAD_DIM), 1.0, 0.0).astype(BF16)
    lane_l = lax.broadcasted_iota(jnp.int32, (L, LANES), 1)

    def chunk(c, states):
        rows = slice(c * L, (c + 1) * L)
        xs = xbc_s[rows, 0:W]
        dt = _softplus(dt_ref[rows, :] + dtb_ref[...])
        a_dt = dt * a_row
        cs_col = _dot_wide_lhs_rhs(tri_incl, a_dt)
        cs_row = cs_col.T
        cs_full = _dot_wide_lhs(cs_col, expand)
        dt_full = _dot_wide_lhs(dt, expand)
        xd = xs * dt_full
        tot = cs_full[L - 1:L, :]
        xdec = (xd * jnp.exp(tot - cs_full)).astype(BF16)
        xd16 = xd.astype(BF16)
        ys = []
        new_states = []
        for g in range(2):
            gl = slice(g * LANES, (g + 1) * LANES)
            bm = xbc_s[rows, W + g * SSM_STATE:W + (g + 1) * SSM_STATE].astype(BF16)
            cm = xbc_s[rows, 2 * W + g * SSM_STATE:2 * W + (g + 1) * SSM_STATE].astype(BF16)
            cb = _dot_t(cm, bm)
            prev = states[g]
            y_off = _dot(cm, prev.astype(BF16)) * jnp.exp(cs_full[:, gl])
            y_g = y_off
            for hh in range(2):
                h = 2 * g + hh
                seg = jnp.where(lower, cs_col[:, h:h + 1] - cs_row[h:h + 1, :], -jnp.inf)
                y_h = _dot((cb * jnp.exp(seg)).astype(BF16), xd16[:, gl])
                in_head = (lane_l >= hh * HEAD_DIM) & (lane_l < (hh + 1) * HEAD_DIM)
                y_g = y_g + jnp.where(in_head, y_h, 0.0)
            new_states.append(prev * jnp.exp(tot[:, gl]) + _dot_tl(bm, xdec[:, gl]))
            ys.append(y_g)
        y = jnp.concatenate(ys, axis=1) + dskip_ref[...] * xs
        o_ref[rows, :] = y * _silu(z_ref[rows, :])
        return new_states

    states = [jnp.zeros((SSM_STATE, LANES), F32) for _ in range(2)]
    for c in range(S // L):
        states = chunk(c, states)


def _dot_wide_lhs_rhs(m_bf16, x):
    hi, mid, lo = _split3(x)
    return _dot(m_bf16, hi) + _dot(m_bf16, mid) + _dot(m_bf16, lo)


def _ssd(z, xbc, dt, conv_w, conv_b, dt_bias, a_log, d_skip):
    B, S, _ = z.shape
    W = GROUP_WIDTH
    full = lambda shape: pl.BlockSpec(shape, lambda b: (0,) * len(shape))
    return pl.pallas_call(
        _ssd_kernel,
        grid=(B,),
        in_specs=[pl.BlockSpec((None, S, W), lambda b: (b, 0, 0)),
                  pl.BlockSpec((None, S, 3 * W), lambda b: (b, 0, 0)),
                  pl.BlockSpec((None, S, LANES), lambda b: (b, 0, 0)),
                  full((CONV_WIDTH, 3 * W)), full((1, 3 * W)), full((1, LANES)), full((1, LANES)),
                  full((1, W))],
        out_specs=pl.BlockSpec((None, S, W), lambda b: (b, 0, 0)),
        out_shape=jax.ShapeDtypeStruct((B, S, W), F32),
        scratch_shapes=[pltpu.VMEM((S, 3 * W), F32)],
        compiler_params=_cparams(1),
        name="ssd",
    )(z, xbc, dt, conv_w, conv_b, dt_bias, a_log, d_skip)


def _rope(x, cos, sin):
    lane = lax.broadcasted_iota(jnp.int32, (x.shape[0], LANES), 1)
    first_half = (lane % HEAD_DIM) < (HEAD_DIM // 2)
    halves = []
    for p in range(x.shape[1] // LANES):
        xp = x[:, p * LANES:(p + 1) * LANES]
        fwd = pltpu.roll(xp, HEAD_DIM // 2, 1)
        bwd = pltpu.roll(xp, LANES - HEAD_DIM // 2, 1)
        halves.append(jnp.where(first_half, -bwd, fwd))
    rot = jnp.concatenate(halves, axis=1)
    return x * cos + rot * sin


def _moba_kernel(qkv_ref, cos_ref, sin_ref, o_ref, k_s, vt_s, kmean_s, acc_s, bias_s):
    i = pl.program_id(1)
    W = GROUP_WIDTH
    TB = MOBA_BLOCK
    S = qkv_ref.shape[0]
    NB = S // TB

    @pl.when(i == 0)
    def _():
        for blk in range(NB):
            rs = slice(blk * TB, (blk + 1) * TB)
            kb = _rope(qkv_ref[rs, W:2 * W], cos_ref[rs, :], sin_ref[rs, :])
            k_s[rs, :] = kb.astype(BF16)
            kmean_s[blk:blk + 1, :] = jnp.mean(kb, axis=0, keepdims=True)
            vt_s[blk] = qkv_ref[rs, 2 * W:3 * W].T.astype(BF16)

    rows_i = pl.ds(pl.multiple_of(i * TB, TB), TB)
    q = _rope(qkv_ref[rows_i, 0:W], cos_ref[rows_i, :], sin_ref[rows_i, :])
    lane = lax.broadcasted_iota(jnp.int32, (TB, W), 1)
    lane8 = lax.broadcasted_iota(jnp.int32, (NB, W), 1)
    blk_id = lax.broadcasted_iota(jnp.int32, (NB, TB), 0)
    R = N_HEADS * TB
    key_loc = lax.broadcasted_iota(jnp.int32, (TB, R), 0)
    q_loc = lax.broadcasted_iota(jnp.int32, (TB, R), 1) & (TB - 1)
    kmean = kmean_s[...]
    q_hi, q_lo = _split2(q)
    scale = HEAD_DIM ** -0.5

    heads = [(lane >= h * HEAD_DIM) & (lane < (h + 1) * HEAD_DIM) for h in range(N_HEADS)]
    qhs = []
    for h in range(N_HEADS):
        head8 = (lane8 >= h * HEAD_DIM) & (lane8 < (h + 1) * HEAD_DIM)
        km_hi, km_lo = _split2(jnp.where(head8, kmean, 0.0))
        gate = _dot_t(km_hi, q_hi) + _dot_t(km_hi, q_lo) + _dot_t(km_lo, q_hi)
        cnt = jnp.zeros((NB, TB), F32)
        for jp in range(NB):
            row = gate[jp:jp + 1, :]
            beats = (row > gate) | ((row == gate) & (blk_id > jp))
            cnt = cnt + jnp.where(beats, jnp.where(jp < i, 1.0, 0.0), 0.0)
        selected = (cnt < float(MOBA_TOPK)) & (blk_id < i)
        bias_s[:, h * TB:(h + 1) * TB] = jnp.where(selected, 0.0, NEG_BIG)
        qhs.append(jnp.where(heads[h], q, 0.0) * scale)
    qs = jnp.concatenate(qhs, axis=0).astype(BF16)

    s = jnp.where(key_loc <= q_loc, _dot_t(k_s[rows_i, :], qs), NEG_BIG)
    m = jnp.max(s, axis=0, keepdims=True)
    p = jnp.exp(s - m)
    l = jnp.sum(p, axis=0, keepdims=True)
    acc_s[...] = _dot(vt_s[i], p.astype(BF16))

    def body(j, carry):
        m, l = carry
        rows = pl.ds(pl.multiple_of(j * TB, TB), TB)
        s = _dot_t(k_s[rows, :], qs) + bias_s[pl.ds(j, 1), :]
        m_new = jnp.maximum(m, jnp.max(s, axis=0, keepdims=True))
        alpha = jnp.exp(m - m_new)
        p = jnp.exp(s - m_new)
        l = alpha * l + jnp.sum(p, axis=0, keepdims=True)
        acc_s[...] = alpha * acc_s[...] + _dot(vt_s[j], p.astype(BF16))
        return m_new, l

    m, l = lax.fori_loop(0, i, body, (m, l))
    outs = acc_s[...] / l
    out = outs[:, 0:TB].T
    for h in range(1, N_HEADS):
        out = jnp.where(heads[h], outs[:, h * TB:(h + 1) * TB].T, out)
    o_ref[...] = out


def _moba(qkv, cos, sin):
    B, S, _ = qkv.shape
    W = GROUP_WIDTH
    nb = S // MOBA_BLOCK
    return pl.pallas_call(
        _moba_kernel,
        grid=(B, S // MOBA_BLOCK),
        in_specs=[pl.BlockSpec((None, S, 3 * W), lambda b, i: (b, 0, 0)),
                  pl.BlockSpec((S, W), lambda b, i: (0, 0)),
                  pl.BlockSpec((S, W), lambda b, i: (0, 0))],
        out_specs=pl.BlockSpec((None, MOBA_BLOCK, W), lambda b, i: (b, i, 0)),
        out_shape=jax.ShapeDtypeStruct((B, S, W), F32),
        scratch_shapes=[pltpu.VMEM((S, W), BF16), pltpu.VMEM((nb, W, MOBA_BLOCK), BF16),
                        pltpu.VMEM((nb, W), F32), pltpu.VMEM((W, N_HEADS * MOBA_BLOCK), F32),
                        pltpu.VMEM((nb, N_HEADS * MOBA_BLOCK), F32)],
        compiler_params=_cparams(2),
        name="moba",
    )(qkv, cos, sin)


def _outproj_kernel(ya_ref, yb_ref, yc_ref, yd_ref, gg_ref, w_ref, x_ref, o_ref):
    W = GROUP_WIDTH
    acc = x_ref[...]
    for g, y_ref in enumerate((ya_ref, yb_ref, yc_ref, yd_ref)):
        yn = _rmsnorm(y_ref[...], gg_ref[g:g + 1, :]).astype(BF16)
        acc = acc + _dot(yn, w_ref[g * W:(g + 1) * W, :])
    o_ref[...] = acc


def _outproj(ys, gg, w_out, x2d, tm=512):
    T = x2d.shape[0]
    W = GROUP_WIDTH
    return pl.pallas_call(
        _outproj_kernel,
        grid=(T // tm,),
        in_specs=[pl.BlockSpec((tm, W), lambda i: (i, 0))] * 4
                 + [pl.BlockSpec((4, W), lambda i: (0, 0)),
                    pl.BlockSpec((4 * W, D_MODEL), lambda i: (0, 0)),
                    pl.BlockSpec((tm, D_MODEL), lambda i: (i, 0))],
        out_specs=pl.BlockSpec((tm, D_MODEL), lambda i: (i, 0)),
        out_shape=jax.ShapeDtypeStruct((T, D_MODEL), F32),
        compiler_params=_cparams(1),
        name="outproj",
    )(*ys, gg, w_out, x2d)


def _memkv_kernel(m_ref, g_ref, w_ref, o_ref):
    mn = _rmsnorm(m_ref[...], g_ref[...]).astype(BF16)
    o_ref[...] = _dot(mn, w_ref[...]).astype(BF16)


def _memkv(mem, g, wkv):
    B, M, _ = mem.shape
    return pl.pallas_call(
        _memkv_kernel,
        grid=(B,),
        in_specs=[pl.BlockSpec((None, M, D_MODEL), lambda b: (b, 0, 0)),
                  pl.BlockSpec((1, D_MODEL), lambda b: (0, 0)),
                  pl.BlockSpec((D_MODEL, 2 * D_MODEL), lambda b: (0, 0))],
        out_specs=pl.BlockSpec((None, M, 2 * D_MODEL), lambda b: (b, 0, 0)),
        out_shape=jax.ShapeDtypeStruct((B, M, 2 * D_MODEL), BF16),
        compiler_params=_cparams(1),
        name="memkv",
    )(mem, g, wkv)


def _xattn_kernel(x_ref, g_ref, wq_ref, kv_ref, wo_ref, o_ref):
    x = x_ref[...]
    h = _rmsnorm(x, g_ref[...]).astype(BF16)
    q = (_dot(h, wq_ref[...]) * (XATTN_HEAD_DIM ** -0.5)).astype(BF16)
    acc = x
    for hd in range(XATTN_HEADS):
        cols = slice(hd * XATTN_HEAD_DIM, (hd + 1) * XATTN_HEAD_DIM)
        vcols = slice(D_MODEL + hd * XATTN_HEAD_DIM, D_MODEL + (hd + 1) * XATTN_HEAD_DIM)
        s = _dot_t(q[:, cols], kv_ref[:, cols])
        p = jnp.exp(s - jnp.max(s, axis=-1, keepdims=True))
        p = p / jnp.sum(p, axis=-1, keepdims=True)
        o = _dot(p.astype(BF16), kv_ref[:, vcols]).astype(BF16)
        acc = acc + _dot(o, wo_ref[cols, :])
    o_ref[...] = acc


def _xattn(x3d, g, wq, kv, wo, tm=1024):
    B, S, _ = x3d.shape
    M = kv.shape[1]
    return pl.pallas_call(
        _xattn_kernel,
        grid=(B, S // tm),
        in_specs=[pl.BlockSpec((None, tm, D_MODEL), lambda b, i: (b, i, 0)),
                  pl.BlockSpec((1, D_MODEL), lambda b, i: (0, 0)),
                  pl.BlockSpec((D_MODEL, D_MODEL), lambda b, i: (0, 0)),
                  pl.BlockSpec((None, M, 2 * D_MODEL), lambda b, i: (b, 0, 0)),
                  pl.BlockSpec((D_MODEL, D_MODEL), lambda b, i: (0, 0))],
        out_specs=pl.BlockSpec((None, tm, D_MODEL), lambda b, i: (b, i, 0)),
        out_shape=jax.ShapeDtypeStruct((B, S, D_MODEL), F32),
        compiler_params=_cparams(2),
        name="xattn",
    )(x3d, g, wq, kv, wo)


def _moe_kernel(x_ref, g_ref, wr_ref, br_ref, w1_ref, w3_ref, w2_ref, fg_ref, o_ref,
                t_s, comb_s, acc_s, *, final_norm):
    grp = pl.program_id(1)
    tm = x_ref.shape[0]

    @pl.when(grp == 0)
    def _():
        t = _rmsnorm(x_ref[...], g_ref[...])
        t_s[...] = t.astype(BF16)
        acc_s[...] = jnp.zeros_like(acc_s)
        t_hi, t_lo = _split2(t)
        logits = (_dot(t_hi, wr_ref[0]) + _dot(t_hi, wr_ref[1]) + _dot(t_lo, wr_ref[0])) + br_ref[...]
        lane = lax.broadcasted_iota(jnp.int32, (tm, LANES), 1)
        lane_f = lane.astype(F32)
        is_g = (lane >= MOE_EXPERTS) & (lane < MOE_EXPERTS + MOE_GROUPS)
        lg = jnp.where(is_g, logits, NEG_BIG)
        gmax = jnp.max(lg, axis=-1, keepdims=True)
        pg_top = 1.0 / jnp.sum(jnp.exp(lg - gmax), axis=-1, keepdims=True)
        g_lane = jnp.min(jnp.where(is_g & (lg == gmax), lane_f, 1e9), axis=-1, keepdims=True)
        g_idx = g_lane - float(MOE_EXPERTS)
        e_grp = jnp.floor(lane_f * (1.0 / MOE_EPG))
        in_grp = (lane < MOE_EXPERTS) & (e_grp == g_idx)
        le = jnp.where(in_grp, logits, NEG_BIG)
        e1 = jnp.max(le, axis=-1, keepdims=True)
        i1 = jnp.min(jnp.where(in_grp & (le == e1), lane_f, 1e9), axis=-1, keepdims=True)
        le2 = jnp.where(lane_f == i1, NEG_BIG, le)
        e2 = jnp.max(le2, axis=-1, keepdims=True)
        i2 = jnp.min(jnp.where(in_grp & (lane_f != i1) & (le2 == e2), lane_f, 1e9), axis=-1, keepdims=True)
        r2 = jnp.exp(e2 - e1)
        w_first = 1.0 / (1.0 + r2)
        w_second = r2 / (1.0 + r2)
        comb_s[...] = pg_top * (jnp.where(lane_f == i1, w_first, 0.0) + jnp.where(lane_f == i2, w_second, 0.0))

    t = t_s[...]
    comb = comb_s[...]
    lane = lax.broadcasted_iota(jnp.int32, (tm, LANES), 1)
    upd = jnp.zeros((tm, D_MODEL), F32)
    for e in range(MOE_EPG):
        n = grp * MOE_EPG + e
        c = jnp.sum(jnp.where(lane == n, comb, 0.0), axis=-1, keepdims=True)
        hid = _silu(_dot(t, w1_ref[e])) * _dot(t, w3_ref[e])
        upd = upd + _dot((hid * c).astype(BF16), w2_ref[e])
    acc_s[...] += upd

    @pl.when(grp == MOE_GROUPS - 1)
    def _():
        y = x_ref[...] + acc_s[...]
        if final_norm:
            y = _rmsnorm(y, fg_ref[...])
        o_ref[...] = y


def _moe(x2d, g, w_router, b_router, w1, w3, w2, final_g, final_norm, tm=1024):
    T = x2d.shape[0]
    return pl.pallas_call(
        functools.partial(_moe_kernel, final_norm=final_norm),
        grid=(T // tm, MOE_GROUPS),
        in_specs=[pl.BlockSpec((tm, D_MODEL), lambda i, e: (i, 0)),
                  pl.BlockSpec((1, D_MODEL), lambda i, e: (0, 0)),
                  pl.BlockSpec((2, D_MODEL, LANES), lambda i, e: (0, 0, 0)),
                  pl.BlockSpec((1, LANES), lambda i, e: (0, 0)),
                  pl.BlockSpec((MOE_EPG, D_MODEL, MOE_FF), lambda i, e: (e, 0, 0)),
                  pl.BlockSpec((MOE_EPG, D_MODEL, MOE_FF), lambda i, e: (e, 0, 0)),
                  pl.BlockSpec((MOE_EPG, MOE_FF, D_MODEL), lambda i, e: (e, 0, 0)),
                  pl.BlockSpec((1, D_MODEL), lambda i, e: (0, 0))],
        out_specs=pl.BlockSpec((tm, D_MODEL), lambda i, e: (i, 0)),
        out_shape=jax.ShapeDtypeStruct((T, D_MODEL), F32),
        scratch_shapes=[pltpu.VMEM((tm, D_MODEL), BF16), pltpu.VMEM((tm, LANES), F32),
                        pltpu.VMEM((tm, D_MODEL), F32)],
        compiler_params=_cparams(2),
        name="moe",
    )(x2d, g, w_router, b_router, w1, w3, w2, final_g)


def _pad_lanes(v, width=LANES):
    return jnp.pad(v, (0, width - v.shape[0]))[None, :]


def _block_diag(w):
    H, n, _ = w.shape
    eye = jnp.eye(H, dtype=w.dtype)
    return (eye[:, None, :, None] * w[:, :, None, :]).reshape(H * n, H * n)


def _rope_tables(S):
    half = HEAD_DIM // 2
    inv_freq = ROPE_THETA ** (-jnp.arange(half, dtype=F32) / half)
    ang = jnp.arange(S, dtype=F32)[:, None] * inv_freq[None, :]
    reps = GROUP_WIDTH // half
    return jnp.tile(jnp.cos(ang), (1, reps)), jnp.tile(jnp.sin(ang), (1, reps))


def kernel(x, mem, mix_norm_g, w_in, lru_conv_w, lru_conv_b, lru_wr, lru_br, lru_wi, lru_bi, lru_lambda, ssm_conv_w, ssm_conv_b, ssm_dt_bias, ssm_a_log, ssm_d, group_norm_g, w_out, xattn_norm_g, mem_norm_g, xattn_wq, xattn_wkv, xattn_wo, ffn_norm_g, router_group_w, router_group_b, router_expert_w, router_expert_b, expert_w1, expert_w3, expert_w2, final_norm_g):
    B, S, D = x.shape
    T = B * S
    depth = w_in.shape[0]
    W = GROUP_WIDTH
    cos, sin = _rope_tables(S)
    n_main = 2 * W + 3 * W + W + 3 * W
    x2d = x.reshape(T, D)
    for l in range(depth):
        w_main = w_in[l, :, :n_main].astype(BF16)
        w_dt = jnp.pad(w_in[l, :, n_main:n_main + N_HEADS], ((0, 0), (0, LANES - N_HEADS))).astype(BF16)
        w_mb = w_in[l, :, n_main + N_HEADS:].astype(BF16)
        lru_xg, sb_qkv, ssm_z, ssm_xbc, ssm_dt, mb_qkv = _inproj(x2d, mix_norm_g[l][None, :], w_main, w_dt, w_mb)

        w_bd = jnp.concatenate([_block_diag(lru_wr[l]), _block_diag(lru_wi[l])], axis=1).astype(BF16)
        b_ri = jnp.concatenate([lru_br[l], lru_bi[l]])[None, :]
        y_a = _lru(lru_xg.reshape(B, S, 2 * W), lru_conv_w[l], lru_conv_b[l][None, :], w_bd, b_ri,
                   lru_lambda[l][None, :])
        y_b = _sb_attention(sb_qkv.reshape(B, S, 3 * W))
        y_c = _ssd(ssm_z.reshape(B, S, W), ssm_xbc.reshape(B, S, 3 * W), ssm_dt.reshape(B, S, LANES),
                   ssm_conv_w[l], ssm_conv_b[l][None, :], _pad_lanes(ssm_dt_bias[l]), _pad_lanes(ssm_a_log[l]),
                   jnp.repeat(ssm_d[l], HEAD_DIM)[None, :])
        y_d = _moba(mb_qkv.reshape(B, S, 3 * W), cos, sin)
        x2d = _outproj([y.reshape(T, W) for y in (y_a, y_b, y_c, y_d)], group_norm_g[l].reshape(4, W),
                       w_out[l].astype(BF16), x2d)

        kv = _memkv(mem, mem_norm_g[l][None, :], xattn_wkv[l].astype(BF16))
        x2d = _xattn(x2d.reshape(B, S, D), xattn_norm_g[l][None, :], xattn_wq[l].astype(BF16), kv,
                     xattn_wo[l].astype(BF16)).reshape(T, D)

        w_r = jnp.pad(jnp.concatenate([router_expert_w[l], router_group_w[l]], axis=1),
                      ((0, 0), (0, LANES - MOE_EXPERTS - MOE_GROUPS)))
        w_r_hi = w_r.astype(BF16)
        w_r_lo = (w_r - w_r_hi.astype(F32)).astype(BF16)
        b_r = _pad_lanes(jnp.concatenate([router_expert_b[l], router_group_b[l]]))
        x2d = _moe(x2d, ffn_norm_g[l][None, :], jnp.stack([w_r_hi, w_r_lo]), b_r,
                   expert_w1[l].astype(BF16), expert_w3[l].astype(BF16), expert_w2[l].astype(BF16),
                   final_norm_g[None, :], final_norm=(l == depth - 1))
    return x2d.reshape(B, S, D)
```

```python
import functools
import math

import jax
import jax.numpy as jnp
from jax import lax
from jax.experimental import pallas as pl
from jax.experimental.pallas import tpu as pltpu

F32 = jnp.float32
BF16 = jnp.bfloat16

D_MODEL = 1024
GROUP_WIDTH = 256
HEAD_DIM = 64
N_HEADS = 4
NORM_EPS = 1e-6
CONV_WIDTH = 4
LRU_C = 8.0
SB_BLOCK = 128
SB_WINDOW_BLOCKS = 3
SB_CHAINS = 2
SSM_CHUNK = 128
SSM_STATE = 128
MOBA_BLOCK = 256
MOBA_TOPK = 3
ROPE_THETA = 10000.0
XATTN_HEADS = 4
XATTN_HEAD_DIM = 256
MEM_LEN = 256
MOE_GROUPS = 4
MOE_EPG = 4
MOE_EXPERTS = 16
MOE_FF = 256
LANES = 128
SUBLANES = 8
NEG_BIG = -1e30
SB_EXP_FLOOR = -104.0
IN_OUT_WIDTHS = (512, 768, 256, 768, LANES, 768)
IN_MAIN = 512 + 768 + 256 + 768
IN_OUT_DTYPES = (F32, BF16, F32, F32, F32, F32)
VMEM_LIMIT = 56 * 1024 * 1024


def _cparams(n_axes):
    return pltpu.CompilerParams(dimension_semantics=("arbitrary",) * n_axes,
                                vmem_limit_bytes=VMEM_LIMIT)


def _dot(a, b):
    return jnp.dot(a, b, preferred_element_type=F32)


def _dot_t(a, b):
    return lax.dot_general(a, b, (((1,), (1,)), ((), ())), preferred_element_type=F32)


def _dot_tl(a, b):
    return lax.dot_general(a, b, (((0,), (0,)), ((), ())), preferred_element_type=F32)


def _split2(x):
    hi = x.astype(BF16)
    lo = (x - hi.astype(F32)).astype(BF16)
    return hi, lo


def _split3(x):
    hi = x.astype(BF16)
    r = x - hi.astype(F32)
    mid = r.astype(BF16)
    lo = (r - mid.astype(F32)).astype(BF16)
    return hi, mid, lo


def _dot_wide_lhs(x, m_bf16, parts=3):
    pieces = _split3(x) if parts == 3 else _split2(x)
    out = _dot(pieces[0], m_bf16)
    for p in pieces[1:]:
        out = out + _dot(p, m_bf16)
    return out


def _rmsnorm(x, g):
    return x * lax.rsqrt(jnp.mean(x * x, axis=-1, keepdims=True) + NORM_EPS) * g


def _softplus(x):
    return jnp.maximum(x, 0.0) + jnp.log(1.0 + jnp.exp(-jnp.abs(x)))


def _sigmoid(x):
    return 1.0 / (1.0 + jnp.exp(-x))


def _silu(x):
    return x * _sigmoid(x)


def _gelu_tanh(x):
    return 0.5 * x * (1.0 + jnp.tanh(math.sqrt(2.0 / math.pi) * (x + 0.044715 * (x * x * x))))


def _causal_conv(x, w_ref, b_ref):
    def taps(v, mask_rows):
        y = v * w_ref[CONV_WIDTH - 1:CONV_WIDTH, :] + b_ref[...]
        for s in range(1, CONV_WIDTH):
            vs = pltpu.roll(v, s, 0)
            if mask_rows is not None:
                vs = jnp.where(mask_rows >= s, vs, 0.0)
            y = y + vs * w_ref[CONV_WIDTH - 1 - s:CONV_WIDTH - s, :]
        return y

    head = x[0:SUBLANES, :]
    y_head = taps(head, lax.broadcasted_iota(jnp.int32, head.shape, 0))
    return jnp.concatenate([y_head, taps(x, None)[SUBLANES:, :]], axis=0)


def _inproj_kernel(x_ref, g_ref, w_ref, *refs):
    o_refs, w_s = refs[:-1], refs[-1]

    @pl.when(pl.program_id(0) == 0)
    def _():
        for c0 in range(0, IN_MAIN, 2 * LANES):
            w_s[:, c0:c0 + 2 * LANES] = w_ref[:, c0:c0 + 2 * LANES].astype(BF16)
        dt_tile = w_ref[:, IN_MAIN:IN_MAIN + LANES]
        lane = lax.broadcasted_iota(jnp.int32, dt_tile.shape, 1)
        w_s[:, IN_MAIN:IN_MAIN + LANES] = jnp.where(lane < N_HEADS, dt_tile, 0.0).astype(BF16)
        w_s[:, IN_MAIN + LANES:] = w_ref[:, IN_MAIN + N_HEADS:].astype(BF16)

    h = _rmsnorm(x_ref[...], g_ref[...]).astype(BF16)
    off = 0
    for o_ref, width in zip(o_refs, IN_OUT_WIDTHS):
        o_ref[...] = _dot(h, w_s[:, off:off + width]).astype(o_ref.dtype)
        off += width


def _inproj(x2d, g, w_in, layer, tm=512):
    T = x2d.shape[0]
    n_in = w_in.shape[-1]
    return pl.pallas_call(
        _inproj_kernel,
        grid=(T // tm,),
        in_specs=[pl.BlockSpec((tm, D_MODEL), lambda i: (i, 0)),
                  pl.BlockSpec((1, D_MODEL), lambda i: (0, 0)),
                  pl.BlockSpec((None, D_MODEL, n_in), lambda i: (layer, 0, 0), pipeline_mode=pl.Buffered(1))],
        out_specs=[pl.BlockSpec((tm, w), lambda i: (i, 0)) for w in IN_OUT_WIDTHS],
        out_shape=[jax.ShapeDtypeStruct((T, w), dt) for w, dt in zip(IN_OUT_WIDTHS, IN_OUT_DTYPES)],
        scratch_shapes=[pltpu.VMEM((D_MODEL, sum(IN_OUT_WIDTHS)), BF16)],
        compiler_params=_cparams(1),
        name="inproj",
    )(x2d, g, w_in)


def _lru_kernel(xg_ref, cw_ref, cb_ref, wbd_ref, bri_ref, lam_ref, o_ref):
    S = xg_ref.shape[0]
    W = GROUP_WIDTH
    xc = _causal_conv(xg_ref[:, 0:W], cw_ref, cb_ref)
    ri = _dot(xc.astype(BF16), wbd_ref[...]) + bri_ref[...]
    r = _sigmoid(ri[:, 0:W])
    i = _sigmoid(ri[:, W:2 * W])
    log_a = (LRU_C * r) * (-_softplus(-lam_ref[...]))
    a = jnp.exp(log_a)
    u = jnp.sqrt(1.0 - jnp.exp(2.0 * log_a)) * (i * xc)
    rows = lax.broadcasted_iota(jnp.int32, (S, W), 0)
    shift = 1
    while shift < S:
        if shift < SUBLANES:
            keep = rows >= shift
            a_s = jnp.where(keep, pltpu.roll(a, shift, 0), 1.0)
            u_s = jnp.where(keep, pltpu.roll(u, shift, 0), 0.0)
            u = a * u_s + u
            a = a * a_s
        else:
            u = jnp.concatenate([u[:shift], a[shift:] * u[:S - shift] + u[shift:]], axis=0)
            a = jnp.concatenate([a[:shift], a[shift:] * a[:S - shift]], axis=0)
        shift *= 2
    o_ref[...] = u * _gelu_tanh(xg_ref[:, W:2 * W])


def _lru(xg, conv_w, conv_b, w_bd, b_ri, lam):
    B, S, _ = xg.shape
    W = GROUP_WIDTH
    full = lambda shape: pl.BlockSpec(shape, lambda b: (0,) * len(shape))
    return pl.pallas_call(
        _lru_kernel,
        grid=(B,),
        in_specs=[pl.BlockSpec((None, S, 2 * W), lambda b: (b, 0, 0)),
                  full((CONV_WIDTH, W)), full((1, W)), full((W, 2 * W)), full((1, 2 * W)), full((1, W))],
        out_specs=pl.BlockSpec((None, S, W), lambda b: (b, 0, 0)),
        out_shape=jax.ShapeDtypeStruct((B, S, W), F32),
        compiler_params=_cparams(1),
        name="rglru",
    )(xg, conv_w, conv_b, w_bd, b_ri, lam)


def _sb_kernel(qkv_ref, o_ref, k_s, v_s, acc_s, later_s):
    i = pl.program_id(1)
    W = GROUP_WIDTH
    TB = SB_BLOCK
    R = N_HEADS * TB

    NW = SB_WINDOW_BLOCKS
    KW = NW * TB
    PAD = (NW - 1) * TB
    S = qkv_ref.shape[0]

    @pl.when(i == 0)
    def _():
        k_s[0:PAD, :] = jnp.zeros((PAD, W), BF16)
        v_s[0:PAD, :] = jnp.zeros((PAD, W), BF16)
        k_s[PAD:PAD + S, :] = qkv_ref[:, W:2 * W].astype(BF16)
        v_s[PAD:PAD + S, :] = qkv_ref[:, 2 * W:3 * W].astype(BF16)

    lane = lax.broadcasted_iota(jnp.int32, (TB, W), 1)
    heads = [(lane >= h * HEAD_DIM) & (lane < (h + 1) * HEAD_DIM) for h in range(N_HEADS)]
    r_loc = lax.broadcasted_iota(jnp.int32, (R, KW), 0) & (TB - 1)
    c_loc = lax.broadcasted_iota(jnp.int32, (R, KW), 1)
    ur = lax.broadcasted_iota(jnp.int32, (TB, 2 * TB), 0)
    uc = lax.broadcasted_iota(jnp.int32, (TB, 2 * TB), 1)
    tri_ones = jnp.where((ur > uc) | (uc >= TB), 1.0, 0.0).astype(BF16)

    n_chains = o_ref.shape[0]
    blocks = [i + c * (S // TB // n_chains) for c in range(n_chains)]
    qss = []
    for blk in blocks:
        q = qkv_ref[pl.ds(pl.multiple_of(blk * TB, TB), TB), 0:W] * (HEAD_DIM ** -0.5)
        qss.append(jnp.concatenate([jnp.where(hm, q, 0.0) for hm in heads], axis=0).astype(BF16))

    acc_s[...] = jnp.zeros_like(acc_s)
    later_s[...] = jnp.zeros_like(later_s)

    def window(c, n):
        blk = blocks[c]
        first_key = (blk - n * NW - (NW - 1)) * TB
        rows = pl.ds(pl.multiple_of(jnp.maximum(first_key + PAD, 0), TB), KW)
        z = _dot_t(qss[c], k_s[rows, :])
        key_abs = first_key + c_loc
        live = (key_abs < blk * TB + r_loc) & (key_abs >= 0)
        sp = _softplus(z)
        lf = jnp.where(live, -sp, 0.0)
        lf_hi, lf_lo = _split2(lf)
        order = list(range(NW - 1, -1, -1))
        stacked = jnp.concatenate([piece[:, b * TB:(b + 1) * TB] for piece in (lf_hi, lf_lo) for b in order], axis=0)
        cs2 = _dot(stacked, tri_ones)
        offset = later_s[c]
        after = [None] * NW
        for pos, b in enumerate(order):
            cs = cs2[pos * R:(pos + 1) * R, :] + cs2[(NW + pos) * R:(NW + pos + 1) * R, :]
            after[b] = cs[:, 0:TB] + offset
            offset = offset + cs[:, TB:2 * TB]
        w = jnp.where(live, jnp.exp((z - sp) + jnp.concatenate(after, axis=1)), 0.0)
        acc_s[c] += _dot(w.astype(BF16), v_s[rows, :])
        later_s[c] = offset
        return jnp.where((n + 1) * NW <= blk, jnp.max(offset), SB_EXP_FLOOR)

    def cond(carry):
        return carry[1] > SB_EXP_FLOOR

    def body(carry):
        n = carry[0]
        later_max = window(0, n)
        for c in range(1, n_chains):
            later_max = jnp.maximum(later_max, window(c, n))
        return n + 1, later_max

    lax.while_loop(cond, body, (jnp.int32(0), jnp.float32(0.0)))
    for c in range(n_chains):
        out = acc_s[c, 0:TB, :]
        for h in range(1, N_HEADS):
            out = jnp.where(heads[h], acc_s[c, h * TB:(h + 1) * TB, :], out)
        o_ref[c] = out


def _sb_attention(qkv):
    B, S, _ = qkv.shape
    W = GROUP_WIDTH
    pad = (SB_WINDOW_BLOCKS - 1) * SB_BLOCK
    nc = SB_CHAINS
    rows = N_HEADS * SB_BLOCK
    out = pl.pallas_call(
        _sb_kernel,
        grid=(B, S // SB_BLOCK // nc),
        in_specs=[pl.BlockSpec((None, S, 3 * W), lambda b, i: (b, 0, 0))],
        out_specs=pl.BlockSpec((None, nc, SB_BLOCK, W), lambda b, i: (b, 0, i, 0)),
        out_shape=jax.ShapeDtypeStruct((B, nc, S // nc, W), F32),
        scratch_shapes=[pltpu.VMEM((S + pad, W), BF16), pltpu.VMEM((S + pad, W), BF16),
                        pltpu.VMEM((nc, rows, W), F32), pltpu.VMEM((nc, rows, SB_BLOCK), F32)],
        compiler_params=_cparams(2),
        name="stickbreak",
    )(qkv)
    return out.reshape(B, S, W)


def _ssd_kernel(z_ref, xbc_ref, dt_ref, cw_ref, cb_ref, dtb_ref, alog_ref, dskip_ref, o_ref, xbc_s):
    S = z_ref.shape[0]
    W = GROUP_WIDTH
    L = SSM_CHUNK
    xbc_s[...] = _silu(_causal_conv(xbc_ref[...], cw_ref, cb_ref))
    a_row = -jnp.exp(alog_ref[...])

    r_i = lax.broadcasted_iota(jnp.int32, (L, L), 0)
    c_i = lax.broadcasted_iota(jnp.int32, (L, L), 1)
    tri_incl = jnp.where(c_i <= r_i, 1.0, 0.0).astype(BF16)
    lower = c_i <= r_i
    e_r = lax.broadcasted_iota(jnp.int32, (LANES, W), 0)
    e_c = lax.broadcasted_iota(jnp.int32, (LANES, W), 1)
    expand = jnp.where((e_c >= e_r * HEAD_DIM) & (e_c < (e_r + 1) * HEAD_DIM), 1.0, 0.0).astype(BF16)
    lane_l = lax.broadcasted_iota(jnp.int32, (L, LANES), 1)

    def chunk(c, states):
        rows = slice(c * L, (c + 1) * L)
        xs = xbc_s[rows, 0:W]
        dt = _softplus(dt_ref[rows, :] + dtb_ref[...])
        a_dt = dt * a_row
        cs_col = _dot_wide_lhs_rhs(tri_incl, a_dt)
        cs_row = cs_col.T
        cs_full = _dot_wide_lhs(cs_col, expand)
        dt_full = _dot_wide_lhs(dt, expand)
        xd = xs * dt_full
        tot = cs_full[L - 1:L, :]
        xdec = (xd * jnp.exp(tot - cs_full)).astype(BF16)
        xd16 = xd.astype(BF16)
        ys = []
        new_states = []
        for g in range(2):
            gl = slice(g * LANES, (g + 1) * LANES)
            bm = xbc_s[rows, W + g * SSM_STATE:W + (g + 1) * SSM_STATE].astype(BF16)
            cm = xbc_s[rows, 2 * W + g * SSM_STATE:2 * W + (g + 1) * SSM_STATE].astype(BF16)
            cb = _dot_t(cm, bm)
            prev = states[g]
            y_off = _dot(cm, prev.astype(BF16)) * jnp.exp(cs_full[:, gl])
            y_g = y_off
            for hh in range(2):
                h = 2 * g + hh
                seg = jnp.where(lower, cs_col[:, h:h + 1] - cs_row[h:h + 1, :], -jnp.inf)
                y_h = _dot((cb * jnp.exp(seg)).astype(BF16), xd16[:, gl])
                in_head = (lane_l >= hh * HEAD_DIM) & (lane_l < (hh + 1) * HEAD_DIM)
                y_g = y_g + jnp.where(in_head, y_h, 0.0)
            new_states.append(prev * jnp.exp(tot[:, gl]) + _dot_tl(bm, xdec[:, gl]))
            ys.append(y_g)
        y = jnp.concatenate(ys, axis=1) + dskip_ref[...] * xs
        o_ref[rows, :] = y * _silu(z_ref[rows, :])
        return new_states

    states = [jnp.zeros((SSM_STATE, LANES), F32) for _ in range(2)]
    for c in range(S // L):
        states = chunk(c, states)


def _dot_wide_lhs_rhs(m_bf16, x):
    hi, mid, lo = _split3(x)
    return _dot(m_bf16, hi) + _dot(m_bf16, mid) + _dot(m_bf16, lo)


def _ssd(z, xbc, dt, conv_w, conv_b, dt_bias, a_log, d_skip):
    B, S, _ = z.shape
    W = GROUP_WIDTH
    full = lambda shape: pl.BlockSpec(shape, lambda b: (0,) * len(shape))
    return pl.pallas_call(
        _ssd_kernel,
        grid=(B,),
        in_specs=[pl.BlockSpec((None, S, W), lambda b: (b, 0, 0)),
                  pl.BlockSpec((None, S, 3 * W), lambda b: (b, 0, 0)),
                  pl.BlockSpec((None, S, LANES), lambda b: (b, 0, 0)),
                  full((CONV_WIDTH, 3 * W)), full((1, 3 * W)), full((1, LANES)), full((1, LANES)),
                  full((1, W))],
        out_specs=pl.BlockSpec((None, S, W), lambda b: (b, 0, 0)),
        out_shape=jax.ShapeDtypeStruct((B, S, W), F32),
        scratch_shapes=[pltpu.VMEM((S, 3 * W), F32)],
        compiler_params=_cparams(1),
        name="ssd",
    )(z, xbc, dt, conv_w, conv_b, dt_bias, a_log, d_skip)


def _rope(x, cos, sin):
    lane = lax.broadcasted_iota(jnp.int32, (x.shape[0], LANES), 1)
    first_half = (lane % HEAD_DIM) < (HEAD_DIM // 2)
    halves = []
    for p in range(x.shape[1] // LANES):
        xp = x[:, p * LANES:(p + 1) * LANES]
        fwd = pltpu.roll(xp, HEAD_DIM // 2, 1)
        bwd = pltpu.roll(xp, LANES - HEAD_DIM // 2, 1)
        halves.append(jnp.where(first_half, -bwd, fwd))
    rot = jnp.concatenate(halves, axis=1)
    return x * cos + rot * sin


def _moba_kernel(qkv_ref, cos_ref, sin_ref, o_ref, k_s, vt_s, kmean_s, acc_s, bias_s):
    i = pl.program_id(1)
    W = GROUP_WIDTH
    TB = MOBA_BLOCK
    S = qkv_ref.shape[0]
    NB = S // TB

    @pl.when(i == 0)
    def _():
        for blk in range(NB):
            rs = slice(blk * TB, (blk + 1) * TB)
            kb = _rope(qkv_ref[rs, W:2 * W], cos_ref[rs, :], sin_ref[rs, :])
            k_s[rs, :] = kb.astype(BF16)
            kmean_s[blk:blk + 1, :] = jnp.mean(kb, axis=0, keepdims=True)
            vt_s[blk] = qkv_ref[rs, 2 * W:3 * W].T.astype(BF16)

    rows_i = pl.ds(pl.multiple_of(i * TB, TB), TB)
    q = _rope(qkv_ref[rows_i, 0:W], cos_ref[rows_i, :], sin_ref[rows_i, :])
    lane = lax.broadcasted_iota(jnp.int32, (TB, W), 1)
    lane8 = lax.broadcasted_iota(jnp.int32, (NB, W), 1)
    blk_id = lax.broadcasted_iota(jnp.int32, (NB, TB), 0)
    R = N_HEADS * TB
    key_loc = lax.broadcasted_iota(jnp.int32, (TB, R), 0)
    q_loc = lax.broadcasted_iota(jnp.int32, (TB, R), 1) & (TB - 1)
    kmean = kmean_s[...]
    q_hi, q_lo = _split2(q)
    scale = HEAD_DIM ** -0.5

    heads = [(lane >= h * HEAD_DIM) & (lane < (h + 1) * HEAD_DIM) for h in range(N_HEADS)]
    km_all = jnp.concatenate(
        [jnp.where((lane8 >= h * HEAD_DIM) & (lane8 < (h + 1) * HEAD_DIM), kmean, 0.0) for h in range(N_HEADS)],
        axis=0)
    km_hi, km_lo = _split2(km_all)
    gate_all = _dot_t(km_hi, q_hi) + _dot_t(km_hi, q_lo) + _dot_t(km_lo, q_hi)
    qhs = []
    for h in range(N_HEADS):
        gate = gate_all[h * NB:(h + 1) * NB, :]
        cnt = jnp.zeros((NB, TB), F32)
        for jp in range(NB):
            row = gate[jp:jp + 1, :]
            beats = (row > gate) | ((row == gate) & (blk_id > jp))
            cnt = cnt + jnp.where(beats, jnp.where(jp < i, 1.0, 0.0), 0.0)
        selected = (cnt < float(MOBA_TOPK)) & (blk_id < i)
        bias_s[:, h * TB:(h + 1) * TB] = jnp.where(selected, 0.0, NEG_BIG)
        qhs.append(jnp.where(heads[h], q, 0.0) * scale)
    qs = jnp.concatenate(qhs, axis=0).astype(BF16)

    s = jnp.where(key_loc <= q_loc, _dot_t(k_s[rows_i, :], qs), NEG_BIG)
    m = jnp.max(s, axis=0, keepdims=True)
    p = jnp.exp(s - m)
    l = jnp.sum(p, axis=0, keepdims=True)
    acc_s[...] = _dot(vt_s[i], p.astype(BF16))

    def body(j, carry):
        m, l = carry
        rows = pl.ds(pl.multiple_of(j * TB, TB), TB)
        s = _dot_t(k_s[rows, :], qs) + bias_s[pl.ds(j, 1), :]
        m_new = jnp.maximum(m, jnp.max(s, axis=0, keepdims=True))
        alpha = jnp.exp(m - m_new)
        p = jnp.exp(s - m_new)
        l = alpha * l + jnp.sum(p, axis=0, keepdims=True)
        acc_s[...] = alpha * acc_s[...] + _dot(vt_s[j], p.astype(BF16))
        return m_new, l

    m, l = lax.fori_loop(0, i, body, (m, l))
    outs = acc_s[...] / l
    out = outs[:, 0:TB].T
    for h in range(1, N_HEADS):
        out = jnp.where(heads[h], outs[:, h * TB:(h + 1) * TB].T, out)
    o_ref[...] = out


def _moba(qkv, cos, sin):
    B, S, _ = qkv.shape
    W = GROUP_WIDTH
    nb = S // MOBA_BLOCK
    return pl.pallas_call(
        _moba_kernel,
        grid=(B, S // MOBA_BLOCK),
        in_specs=[pl.BlockSpec((None, S, 3 * W), lambda b, i: (b, 0, 0)),
                  pl.BlockSpec((S, W), lambda b, i: (0, 0)),
                  pl.BlockSpec((S, W), lambda b, i: (0, 0))],
        out_specs=pl.BlockSpec((None, MOBA_BLOCK, W), lambda b, i: (b, i, 0)),
        out_shape=jax.ShapeDtypeStruct((B, S, W), F32),
        scratch_shapes=[pltpu.VMEM((S, W), BF16), pltpu.VMEM((nb, W, MOBA_BLOCK), BF16),
                        pltpu.VMEM((nb, W), F32), pltpu.VMEM((W, N_HEADS * MOBA_BLOCK), F32),
                        pltpu.VMEM((nb, N_HEADS * MOBA_BLOCK), F32)],
        compiler_params=_cparams(2),
        name="moba",
    )(qkv, cos, sin)


def _outproj_kernel(ya_ref, yb_ref, yc_ref, yd_ref, gg_ref, w_ref, x_ref, o_ref):
    W = GROUP_WIDTH
    acc = x_ref[...]
    for g, y_ref in enumerate((ya_ref, yb_ref, yc_ref, yd_ref)):
        yn = _rmsnorm(y_ref[...], gg_ref[g:g + 1, :]).astype(BF16)
        acc = acc + _dot(yn, w_ref[g * W:(g + 1) * W, :].astype(BF16))
    o_ref[...] = acc


def _outproj(ys, gg, w_out, layer, x2d, tm=1024):
    T = x2d.shape[0]
    W = GROUP_WIDTH
    return pl.pallas_call(
        _outproj_kernel,
        grid=(T // tm,),
        in_specs=[pl.BlockSpec((tm, W), lambda i: (i, 0))] * 4
                 + [pl.BlockSpec((4, W), lambda i: (0, 0)),
                    pl.BlockSpec((None, 4 * W, D_MODEL), lambda i: (layer, 0, 0), pipeline_mode=pl.Buffered(1)),
                    pl.BlockSpec((tm, D_MODEL), lambda i: (i, 0))],
        out_specs=pl.BlockSpec((tm, D_MODEL), lambda i: (i, 0)),
        out_shape=jax.ShapeDtypeStruct((T, D_MODEL), F32),
        compiler_params=_cparams(1),
        name="outproj",
    )(*ys, gg, w_out, x2d)


def _memkv_kernel(m_ref, g_ref, w_ref, o_ref):
    mn = _rmsnorm(m_ref[...], g_ref[...]).astype(BF16)
    o_ref[...] = _dot(mn, w_ref[...].astype(BF16)).astype(BF16)


def _memkv(mem, g, wkv, layer):
    B, M, _ = mem.shape
    return pl.pallas_call(
        _memkv_kernel,
        grid=(B,),
        in_specs=[pl.BlockSpec((None, M, D_MODEL), lambda b: (b, 0, 0)),
                  pl.BlockSpec((1, D_MODEL), lambda b: (0, 0)),
                  pl.BlockSpec((None, D_MODEL, 2 * D_MODEL), lambda b: (layer, 0, 0),
                               pipeline_mode=pl.Buffered(1))],
        out_specs=pl.BlockSpec((None, M, 2 * D_MODEL), lambda b: (b, 0, 0)),
        out_shape=jax.ShapeDtypeStruct((B, M, 2 * D_MODEL), BF16),
        compiler_params=_cparams(1),
        name="memkv",
    )(mem, g, wkv)


def _xattn_kernel(x_ref, g_ref, wq_ref, kv_ref, wo_ref, o_ref):
    x = x_ref[...]
    h = _rmsnorm(x, g_ref[...]).astype(BF16)
    q = (_dot(h, wq_ref[...].astype(BF16)) * (XATTN_HEAD_DIM ** -0.5)).astype(BF16)
    acc = x
    for hd in range(XATTN_HEADS):
        cols = slice(hd * XATTN_HEAD_DIM, (hd + 1) * XATTN_HEAD_DIM)
        vcols = slice(D_MODEL + hd * XATTN_HEAD_DIM, D_MODEL + (hd + 1) * XATTN_HEAD_DIM)
        s = _dot_t(q[:, cols], kv_ref[:, cols])
        p = jnp.exp(s - jnp.max(s, axis=-1, keepdims=True))
        p = p / jnp.sum(p, axis=-1, keepdims=True)
        o = _dot(p.astype(BF16), kv_ref[:, vcols]).astype(BF16)
        acc = acc + _dot(o, wo_ref[cols, :].astype(BF16))
    o_ref[...] = acc


def _xattn(x3d, g, wq, kv, wo, layer, tm=1024):
    B, S, _ = x3d.shape
    M = kv.shape[1]
    weight = pl.BlockSpec((None, D_MODEL, D_MODEL), lambda b, i: (layer, 0, 0), pipeline_mode=pl.Buffered(1))
    return pl.pallas_call(
        _xattn_kernel,
        grid=(B, S // tm),
        in_specs=[pl.BlockSpec((None, tm, D_MODEL), lambda b, i: (b, i, 0)),
                  pl.BlockSpec((1, D_MODEL), lambda b, i: (0, 0)),
                  weight,
                  pl.BlockSpec((None, M, 2 * D_MODEL), lambda b, i: (b, 0, 0)),
                  weight],
        out_specs=pl.BlockSpec((None, tm, D_MODEL), lambda b, i: (b, i, 0)),
        out_shape=jax.ShapeDtypeStruct((B, S, D_MODEL), F32),
        compiler_params=_cparams(2),
        name="xattn",
    )(x3d, g, wq, kv, wo)


def _moe_kernel(x_ref, g_ref, wr_ref, br_ref, w1_ref, w3_ref, w2_ref, fg_ref, o_ref,
                t_s, comb_s, acc_s, *, final_norm):
    grp = pl.program_id(1)
    tm = x_ref.shape[0]

    @pl.when(grp == 0)
    def _():
        t = _rmsnorm(x_ref[...], g_ref[...])
        t_s[...] = t.astype(BF16)
        acc_s[...] = jnp.zeros_like(acc_s)
        t_hi, t_lo = _split2(t)
        logits = (_dot(t_hi, wr_ref[0]) + _dot(t_hi, wr_ref[1]) + _dot(t_lo, wr_ref[0])) + br_ref[...]
        lane = lax.broadcasted_iota(jnp.int32, (tm, LANES), 1)
        lane_f = lane.astype(F32)
        is_g = (lane >= MOE_EXPERTS) & (lane < MOE_EXPERTS + MOE_GROUPS)
        lg = jnp.where(is_g, logits, NEG_BIG)
        gmax = jnp.max(lg, axis=-1, keepdims=True)
        pg_top = 1.0 / jnp.sum(jnp.exp(lg - gmax), axis=-1, keepdims=True)
        g_lane = jnp.min(jnp.where(is_g & (lg == gmax), lane_f, 1e9), axis=-1, keepdims=True)
        g_idx = g_lane - float(MOE_EXPERTS)
        e_grp = jnp.floor(lane_f * (1.0 / MOE_EPG))
        in_grp = (lane < MOE_EXPERTS) & (e_grp == g_idx)
        le = jnp.where(in_grp, logits, NEG_BIG)
        e1 = jnp.max(le, axis=-1, keepdims=True)
        i1 = jnp.min(jnp.where(in_grp & (le == e1), lane_f, 1e9), axis=-1, keepdims=True)
        le2 = jnp.where(lane_f == i1, NEG_BIG, le)
        e2 = jnp.max(le2, axis=-1, keepdims=True)
        i2 = jnp.min(jnp.where(in_grp & (lane_f != i1) & (le2 == e2), lane_f, 1e9), axis=-1, keepdims=True)
        r2 = jnp.exp(e2 - e1)
        w_first = 1.0 / (1.0 + r2)
        w_second = r2 / (1.0 + r2)
        comb_s[...] = pg_top * (jnp.where(lane_f == i1, w_first, 0.0) + jnp.where(lane_f == i2, w_second, 0.0))

    t = t_s[...]
    comb = comb_s[...]
    lane = lax.broadcasted_iota(jnp.int32, (tm, LANES), 1)
    upd = jnp.zeros((tm, D_MODEL), F32)
    for e in range(MOE_EPG):
        n = grp * MOE_EPG + e
        c = jnp.sum(jnp.where(lane == n, comb, 0.0), axis=-1, keepdims=True)
        hid = _silu(_dot(t, w1_ref[e])) * _dot(t, w3_ref[e])
        upd = upd + _dot((hid * c).astype(BF16), w2_ref[e])
    acc_s[...] += upd

    @pl.when(grp == MOE_GROUPS - 1)
    def _():
        y = x_ref[...] + acc_s[...]
        if final_norm:
            y = _rmsnorm(y, fg_ref[...])
        o_ref[...] = y


def _moe(x2d, g, w_router, b_router, w1, w3, w2, final_g, final_norm, tm=1024):
    T = x2d.shape[0]
    return pl.pallas_call(
        functools.partial(_moe_kernel, final_norm=final_norm),
        grid=(T // tm, MOE_GROUPS),
        in_specs=[pl.BlockSpec((tm, D_MODEL), lambda i, e: (i, 0)),
                  pl.BlockSpec((1, D_MODEL), lambda i, e: (0, 0)),
                  pl.BlockSpec((2, D_MODEL, LANES), lambda i, e: (0, 0, 0)),
                  pl.BlockSpec((1, LANES), lambda i, e: (0, 0)),
                  pl.BlockSpec((MOE_EPG, D_MODEL, MOE_FF), lambda i, e: (e, 0, 0)),
                  pl.BlockSpec((MOE_EPG, D_MODEL, MOE_FF), lambda i, e: (e, 0, 0)),
                  pl.BlockSpec((MOE_EPG, MOE_FF, D_MODEL), lambda i, e: (e, 0, 0)),
                  pl.BlockSpec((1, D_MODEL), lambda i, e: (0, 0))],
        out_specs=pl.BlockSpec((tm, D_MODEL), lambda i, e: (i, 0)),
        out_shape=jax.ShapeDtypeStruct((T, D_MODEL), F32),
        scratch_shapes=[pltpu.VMEM((tm, D_MODEL), BF16), pltpu.VMEM((tm, LANES), F32),
                        pltpu.VMEM((tm, D_MODEL), F32)],
        compiler_params=_cparams(2),
        name="moe",
    )(x2d, g, w_router, b_router, w1, w3, w2, final_g)


def _pad_lanes(v, width=LANES):
    return jnp.pad(v, (0, width - v.shape[0]))[None, :]


def _block_diag(w):
    H, n, _ = w.shape
    eye = jnp.eye(H, dtype=w.dtype)
    return (eye[:, None, :, None] * w[:, :, None, :]).reshape(H * n, H * n)


def _rope_tables(S):
    half = HEAD_DIM // 2
    inv_freq = ROPE_THETA ** (-jnp.arange(half, dtype=F32) / half)
    ang = jnp.arange(S, dtype=F32)[:, None] * inv_freq[None, :]
    reps = GROUP_WIDTH // half
    return jnp.tile(jnp.cos(ang), (1, reps)), jnp.tile(jnp.sin(ang), (1, reps))


def kernel(x, mem, mix_norm_g, w_in, lru_conv_w, lru_conv_b, lru_wr, lru_br, lru_wi, lru_bi, lru_lambda, ssm_conv_w, ssm_conv_b, ssm_dt_bias, ssm_a_log, ssm_d, group_norm_g, w_out, xattn_norm_g, mem_norm_g, xattn_wq, xattn_wkv, xattn_wo, ffn_norm_g, router_group_w, router_group_b, router_expert_w, router_expert_b, expert_w1, expert_w3, expert_w2, final_norm_g):
    B, S, D = x.shape
    T = B * S
    depth = w_in.shape[0]
    W = GROUP_WIDTH
    cos, sin = _rope_tables(S)
    x2d = x.reshape(T, D)
    for l in range(depth):
        lru_xg, sb_qkv, ssm_z, ssm_xbc, ssm_dt, mb_qkv = _inproj(x2d, mix_norm_g[l][None, :], w_in, l)

        w_bd = jnp.concatenate([_block_diag(lru_wr[l]), _block_diag(lru_wi[l])], axis=1).astype(BF16)
        b_ri = jnp.concatenate([lru_br[l], lru_bi[l]])[None, :]
        y_a = _lru(lru_xg.reshape(B, S, 2 * W), lru_conv_w[l], lru_conv_b[l][None, :], w_bd, b_ri,
                   lru_lambda[l][None, :])
        y_b = _sb_attention(sb_qkv.reshape(B, S, 3 * W))
        y_c = _ssd(ssm_z.reshape(B, S, W), ssm_xbc.reshape(B, S, 3 * W), ssm_dt.reshape(B, S, LANES),
                   ssm_conv_w[l], ssm_conv_b[l][None, :], _pad_lanes(ssm_dt_bias[l]), _pad_lanes(ssm_a_log[l]),
                   jnp.repeat(ssm_d[l], HEAD_DIM)[None, :])
        y_d = _moba(mb_qkv.reshape(B, S, 3 * W), cos, sin)
        x2d = _outproj([y.reshape(T, W) for y in (y_a, y_b, y_c, y_d)], group_norm_g[l].reshape(4, W),
                       w_out, l, x2d)

        kv = _memkv(mem, mem_norm_g[l][None, :], xattn_wkv, l)
        x2d = _xattn(x2d.reshape(B, S, D), xattn_norm_g[l][None, :], xattn_wq, kv, xattn_wo, l).reshape(T, D)

        w_r = jnp.pad(jnp.concatenate([router_expert_w[l], router_group_w[l]], axis=1),
                      ((0, 0), (0, LANES - MOE_EXPERTS - MOE_GROUPS)))
        w_r_hi = w_r.astype(BF16)
        w_r_lo = (w_r - w_r_hi.astype(F32)).astype(BF16)
        b_r = _pad_lanes(jnp.concatenate([router_expert_b[l], router_group_b[l]]))
        x2d = _moe(x2d, ffn_norm_g[l][None, :], jnp.stack([w_r_hi, w_r_lo]), b_r,
                   expert_w1[l].astype(BF16), expert_w3[l].astype(BF16), expert_w2[l].astype(BF16),
                   final_norm_g[None, :], final_norm=(l == depth - 1))
    return x2d.reshape(B, S, D)
```

```python
import functools
import math

import jax
import jax.numpy as jnp
from jax import lax
from jax.experimental import pallas as pl
from jax.experimental.pallas import tpu as pltpu

F32 = jnp.float32
BF16 = jnp.bfloat16

D_MODEL = 1024
GROUP_WIDTH = 256
HEAD_DIM = 64
N_HEADS = 4
NORM_EPS = 1e-6
CONV_WIDTH = 4
LRU_C = 8.0
SB_BLOCK = 128
SB_WINDOW_BLOCKS = 3
SB_CHAINS = 2
SSM_CHUNK = 128
SSM_STATE = 128
MOBA_BLOCK = 256
MOBA_TOPK = 3
ROPE_THETA = 10000.0
XATTN_HEADS = 4
XATTN_HEAD_DIM = 256
MEM_LEN = 256
MOE_GROUPS = 4
MOE_EPG = 4
MOE_EXPERTS = 16
MOE_FF = 256
LANES = 128
SUBLANES = 8
NEG_BIG = -1e30
SB_EXP_FLOOR = -104.0
IN_OUT_WIDTHS = (512, 768, 256, 768, LANES, 768)
IN_MAIN = 512 + 768 + 256 + 768
IN_OUT_DTYPES = (F32, BF16, F32, F32, F32, F32)
VMEM_LIMIT = 56 * 1024 * 1024


def _cparams(n_axes):
    return pltpu.CompilerParams(dimension_semantics=("arbitrary",) * n_axes,
                                vmem_limit_bytes=VMEM_LIMIT)


def _dot(a, b):
    return jnp.dot(a, b, preferred_element_type=F32)


def _dot_t(a, b):
    return lax.dot_general(a, b, (((1,), (1,)), ((), ())), preferred_element_type=F32)


def _dot_tl(a, b):
    return lax.dot_general(a, b, (((0,), (0,)), ((), ())), preferred_element_type=F32)


def _split2(x):
    hi = x.astype(BF16)
    lo = (x - hi.astype(F32)).astype(BF16)
    return hi, lo


def _split3(x):
    hi = x.astype(BF16)
    r = x - hi.astype(F32)
    mid = r.astype(BF16)
    lo = (r - mid.astype(F32)).astype(BF16)
    return hi, mid, lo


def _dot_wide_lhs(x, m_bf16, parts=3):
    pieces = _split3(x) if parts == 3 else _split2(x)
    out = _dot(pieces[0], m_bf16)
    for p in pieces[1:]:
        out = out + _dot(p, m_bf16)
    return out


def _rmsnorm(x, g):
    return x * lax.rsqrt(jnp.mean(x * x, axis=-1, keepdims=True) + NORM_EPS) * g


def _softplus(x):
    return jnp.maximum(x, 0.0) + jnp.log(1.0 + jnp.exp(-jnp.abs(x)))


def _sigmoid(x):
    return 1.0 / (1.0 + jnp.exp(-x))


def _silu(x):
    return x * _sigmoid(x)


def _gelu_tanh(x):
    return 0.5 * x * (1.0 + jnp.tanh(math.sqrt(2.0 / math.pi) * (x + 0.044715 * (x * x * x))))


def _causal_conv(x, w_ref, b_ref):
    def taps(v, mask_rows):
        y = v * w_ref[CONV_WIDTH - 1:CONV_WIDTH, :] + b_ref[...]
        for s in range(1, CONV_WIDTH):
            vs = pltpu.roll(v, s, 0)
            if mask_rows is not None:
                vs = jnp.where(mask_rows >= s, vs, 0.0)
            y = y + vs * w_ref[CONV_WIDTH - 1 - s:CONV_WIDTH - s, :]
        return y

    head = x[0:SUBLANES, :]
    y_head = taps(head, lax.broadcasted_iota(jnp.int32, head.shape, 0))
    return jnp.concatenate([y_head, taps(x, None)[SUBLANES:, :]], axis=0)


def _inproj_kernel(x_ref, g_ref, w_ref, *refs):
    o_refs, w_s = refs[:-1], refs[-1]

    @pl.when(pl.program_id(0) == 0)
    def _():
        for c0 in range(0, IN_MAIN, 2 * LANES):
            w_s[:, c0:c0 + 2 * LANES] = w_ref[:, c0:c0 + 2 * LANES].astype(BF16)
        dt_tile = w_ref[:, IN_MAIN:IN_MAIN + LANES]
        lane = lax.broadcasted_iota(jnp.int32, dt_tile.shape, 1)
        w_s[:, IN_MAIN:IN_MAIN + LANES] = jnp.where(lane < N_HEADS, dt_tile, 0.0).astype(BF16)
        w_s[:, IN_MAIN + LANES:] = w_ref[:, IN_MAIN + N_HEADS:].astype(BF16)

    h = _rmsnorm(x_ref[...], g_ref[...]).astype(BF16)
    off = 0
    for o_ref, width in zip(o_refs, IN_OUT_WIDTHS):
        o_ref[...] = _dot(h, w_s[:, off:off + width]).astype(o_ref.dtype)
        off += width


def _inproj(x2d, g, w_in, layer, tm=512):
    T = x2d.shape[0]
    n_in = w_in.shape[-1]
    return pl.pallas_call(
        _inproj_kernel,
        grid=(T // tm,),
        in_specs=[pl.BlockSpec((tm, D_MODEL), lambda i: (i, 0)),
                  pl.BlockSpec((1, D_MODEL), lambda i: (0, 0)),
                  pl.BlockSpec((None, D_MODEL, n_in), lambda i: (layer, 0, 0), pipeline_mode=pl.Buffered(1))],
        out_specs=[pl.BlockSpec((tm, w), lambda i: (i, 0)) for w in IN_OUT_WIDTHS],
        out_shape=[jax.ShapeDtypeStruct((T, w), dt) for w, dt in zip(IN_OUT_WIDTHS, IN_OUT_DTYPES)],
        scratch_shapes=[pltpu.VMEM((D_MODEL, sum(IN_OUT_WIDTHS)), BF16)],
        compiler_params=_cparams(1),
        name="inproj",
    )(x2d, g, w_in)


def _lru_kernel(xg_ref, cw_ref, cb_ref, wbd_ref, bri_ref, lam_ref, o_ref):
    S = xg_ref.shape[0]
    W = GROUP_WIDTH
    xc = _causal_conv(xg_ref[:, 0:W], cw_ref, cb_ref)
    ri = _dot(xc.astype(BF16), wbd_ref[...]) + bri_ref[...]
    r = _sigmoid(ri[:, 0:W])
    i = _sigmoid(ri[:, W:2 * W])
    log_a = (LRU_C * r) * (-_softplus(-lam_ref[...]))
    a = jnp.exp(log_a)
    u = jnp.sqrt(1.0 - jnp.exp(2.0 * log_a)) * (i * xc)
    rows = lax.broadcasted_iota(jnp.int32, (S, W), 0)
    shift = 1
    while shift < S:
        if shift < SUBLANES:
            keep = rows >= shift
            a_s = jnp.where(keep, pltpu.roll(a, shift, 0), 1.0)
            u_s = jnp.where(keep, pltpu.roll(u, shift, 0), 0.0)
            u = a * u_s + u
            a = a * a_s
        else:
            u = jnp.concatenate([u[:shift], a[shift:] * u[:S - shift] + u[shift:]], axis=0)
            a = jnp.concatenate([a[:shift], a[shift:] * a[:S - shift]], axis=0)
        shift *= 2
    o_ref[...] = u * _gelu_tanh(xg_ref[:, W:2 * W])


def _lru(xg, conv_w, conv_b, w_bd, b_ri, lam):
    B, S, _ = xg.shape
    W = GROUP_WIDTH
    full = lambda shape: pl.BlockSpec(shape, lambda b: (0,) * len(shape))
    return pl.pallas_call(
        _lru_kernel,
        grid=(B,),
        in_specs=[pl.BlockSpec((None, S, 2 * W), lambda b: (b, 0, 0)),
                  full((CONV_WIDTH, W)), full((1, W)), full((W, 2 * W)), full((1, 2 * W)), full((1, W))],
        out_specs=pl.BlockSpec((None, S, W), lambda b: (b, 0, 0)),
        out_shape=jax.ShapeDtypeStruct((B, S, W), F32),
        compiler_params=_cparams(1),
        name="rglru",
    )(xg, conv_w, conv_b, w_bd, b_ri, lam)


def _sb_kernel(qkv_ref, o_ref, k_s, v_s, acc_s, later_s):
    i = pl.program_id(1)
    W = GROUP_WIDTH
    TB = SB_BLOCK
    R = N_HEADS * TB

    NW = SB_WINDOW_BLOCKS
    KW = NW * TB
    PAD = (NW - 1) * TB
    S = qkv_ref.shape[0]

    @pl.when(i == 0)
    def _():
        k_s[0:PAD, :] = jnp.zeros((PAD, W), BF16)
        v_s[0:PAD, :] = jnp.zeros((PAD, W), BF16)
        k_s[PAD:PAD + S, :] = qkv_ref[:, W:2 * W].astype(BF16)
        v_s[PAD:PAD + S, :] = qkv_ref[:, 2 * W:3 * W].astype(BF16)

    lane = lax.broadcasted_iota(jnp.int32, (TB, W), 1)
    heads = [(lane >= h * HEAD_DIM) & (lane < (h + 1) * HEAD_DIM) for h in range(N_HEADS)]
    r_loc = lax.broadcasted_iota(jnp.int32, (R, KW), 0) & (TB - 1)
    c_loc = lax.broadcasted_iota(jnp.int32, (R, KW), 1)
    ur = lax.broadcasted_iota(jnp.int32, (TB, 2 * TB), 0)
    uc = lax.broadcasted_iota(jnp.int32, (TB, 2 * TB), 1)
    tri_ones = jnp.where((ur > uc) | (uc >= TB), 1.0, 0.0).astype(BF16)

    n_chains = o_ref.shape[0]
    blocks = [i + c * (S // TB // n_chains) for c in range(n_chains)]
    qss = []
    for blk in blocks:
        q = qkv_ref[pl.ds(pl.multiple_of(blk * TB, TB), TB), 0:W] * (HEAD_DIM ** -0.5)
        qss.append(jnp.concatenate([jnp.where(hm, q, 0.0) for hm in heads], axis=0).astype(BF16))

    acc_s[...] = jnp.zeros_like(acc_s)
    later_s[...] = jnp.zeros_like(later_s)

    def window(c, n):
        blk = blocks[c]
        first_key = (blk - n * NW - (NW - 1)) * TB
        rows = pl.ds(pl.multiple_of(jnp.maximum(first_key + PAD, 0), TB), KW)
        z = _dot_t(qss[c], k_s[rows, :])
        key_abs = first_key + c_loc
        live = (key_abs < blk * TB + r_loc) & (key_abs >= 0)
        sp = _softplus(z)
        lf = jnp.where(live, -sp, 0.0)
        lf16 = lf.astype(BF16)
        order = list(range(NW - 1, -1, -1))
        stacked = jnp.concatenate([lf16[:, b * TB:(b + 1) * TB] for b in order], axis=0)
        cs_all = _dot(stacked, tri_ones)
        offset = later_s[c]
        after = [None] * NW
        for pos, b in enumerate(order):
            cs = cs_all[pos * R:(pos + 1) * R, :]
            after[b] = cs[:, 0:TB] + offset
            offset = offset + cs[:, TB:2 * TB]
        w = jnp.where(live, jnp.exp((z - sp) + jnp.concatenate(after, axis=1)), 0.0)
        acc_s[c] += _dot(w.astype(BF16), v_s[rows, :])
        later_s[c] = offset
        return jnp.where((n + 1) * NW <= blk, jnp.max(offset), SB_EXP_FLOOR)

    def cond(carry):
        return carry[1] > SB_EXP_FLOOR

    def body(carry):
        n = carry[0]
        later_max = window(0, n)
        for c in range(1, n_chains):
            later_max = jnp.maximum(later_max, window(c, n))
        return n + 1, later_max

    lax.while_loop(cond, body, (jnp.int32(0), jnp.float32(0.0)))
    for c in range(n_chains):
        out = acc_s[c, 0:TB, :]
        for h in range(1, N_HEADS):
            out = jnp.where(heads[h], acc_s[c, h * TB:(h + 1) * TB, :], out)
        o_ref[c] = out


def _sb_attention(qkv):
    B, S, _ = qkv.shape
    W = GROUP_WIDTH
    pad = (SB_WINDOW_BLOCKS - 1) * SB_BLOCK
    nc = SB_CHAINS
    rows = N_HEADS * SB_BLOCK
    out = pl.pallas_call(
        _sb_kernel,
        grid=(B, S // SB_BLOCK // nc),
        in_specs=[pl.BlockSpec((None, S, 3 * W), lambda b, i: (b, 0, 0))],
        out_specs=pl.BlockSpec((None, nc, SB_BLOCK, W), lambda b, i: (b, 0, i, 0)),
        out_shape=jax.ShapeDtypeStruct((B, nc, S // nc, W), F32),
        scratch_shapes=[pltpu.VMEM((S + pad, W), BF16), pltpu.VMEM((S + pad, W), BF16),
                        pltpu.VMEM((nc, rows, W), F32), pltpu.VMEM((nc, rows, SB_BLOCK), F32)],
        compiler_params=_cparams(2),
        name="stickbreak",
    )(qkv)
    return out.reshape(B, S, W)


def _ssd_kernel(z_ref, xbc_ref, dt_ref, cw_ref, cb_ref, dtb_ref, alog_ref, dskip_ref, o_ref, xbc_s):
    S = z_ref.shape[0]
    W = GROUP_WIDTH
    L = SSM_CHUNK
    xbc_s[...] = _silu(_causal_conv(xbc_ref[...], cw_ref, cb_ref))
    a_row = -jnp.exp(alog_ref[...])

    r_i = lax.broadcasted_iota(jnp.int32, (L, L), 0)
    c_i = lax.broadcasted_iota(jnp.int32, (L, L), 1)
    tri_incl = jnp.where(c_i <= r_i, 1.0, 0.0).astype(BF16)
    lower = c_i <= r_i
    e_r = lax.broadcasted_iota(jnp.int32, (LANES, W), 0)
    e_c = lax.broadcasted_iota(jnp.int32, (LANES, W), 1)
    expand = jnp.where((e_c >= e_r * HEAD_DIM) & (e_c < (e_r + 1) * HEAD_DIM), 1.0, 0.0).astype(BF16)
    lane_l = lax.broadcasted_iota(jnp.int32, (L, LANES), 1)

    def chunk(c, states):
        rows = slice(c * L, (c + 1) * L)
        xs = xbc_s[rows, 0:W]
        dt = _softplus(dt_ref[rows, :] + dtb_ref[...])
        a_dt = dt * a_row
        cs_col = _dot_wide_lhs_rhs(tri_incl, a_dt)
        cs_row = cs_col.T
        cs_full = _dot_wide_lhs(cs_col, expand)
        dt_full = _dot_wide_lhs(dt, expand)
        xd = xs * dt_full
        tot = cs_full[L - 1:L, :]
        xdec = (xd * jnp.exp(tot - cs_full)).astype(BF16)
        xd16 = xd.astype(BF16)
        ys = []
        new_states = []
        for g in range(2):
            gl = slice(g * LANES, (g + 1) * LANES)
            bm = xbc_s[rows, W + g * SSM_STATE:W + (g + 1) * SSM_STATE].astype(BF16)
            cm = xbc_s[rows, 2 * W + g * SSM_STATE:2 * W + (g + 1) * SSM_STATE].astype(BF16)
            cb = _dot_t(cm, bm)
            prev = states[g]
            y_off = _dot(cm, prev.astype(BF16)) * jnp.exp(cs_full[:, gl])
            y_g = y_off
            for hh in range(2):
                h = 2 * g + hh
                seg = jnp.where(lower, cs_col[:, h:h + 1] - cs_row[h:h + 1, :], -jnp.inf)
                y_h = _dot((cb * jnp.exp(seg)).astype(BF16), xd16[:, gl])
                in_head = (lane_l >= hh * HEAD_DIM) & (lane_l < (hh + 1) * HEAD_DIM)
                y_g = y_g + jnp.where(in_head, y_h, 0.0)
            new_states.append(prev * jnp.exp(tot[:, gl]) + _dot_tl(bm, xdec[:, gl]))
            ys.append(y_g)
        y = jnp.concatenate(ys, axis=1) + dskip_ref[...] * xs
        o_ref[rows, :] = y * _silu(z_ref[rows, :])
        return new_states

    states = [jnp.zeros((SSM_STATE, LANES), F32) for _ in range(2)]
    for c in range(S // L):
        states = chunk(c, states)


def _dot_wide_lhs_rhs(m_bf16, x):
    hi, mid, lo = _split3(x)
    return _dot(m_bf16, hi) + _dot(m_bf16, mid) + _dot(m_bf16, lo)


def _ssd(z, xbc, dt, conv_w, conv_b, dt_bias, a_log, d_skip):
    B, S, _ = z.shape
    W = GROUP_WIDTH
    full = lambda shape: pl.BlockSpec(shape, lambda b: (0,) * len(shape))
    return pl.pallas_call(
        _ssd_kernel,
        grid=(B,),
        in_specs=[pl.BlockSpec((None, S, W), lambda b: (b, 0, 0)),
                  pl.BlockSpec((None, S, 3 * W), lambda b: (b, 0, 0)),
                  pl.BlockSpec((None, S, LANES), lambda b: (b, 0, 0)),
                  full((CONV_WIDTH, 3 * W)), full((1, 3 * W)), full((1, LANES)), full((1, LANES)),
                  full((1, W))],
        out_specs=pl.BlockSpec((None, S, W), lambda b: (b, 0, 0)),
        out_shape=jax.ShapeDtypeStruct((B, S, W), F32),
        scratch_shapes=[pltpu.VMEM((S, 3 * W), F32)],
        compiler_params=_cparams(1),
        name="ssd",
    )(z, xbc, dt, conv_w, conv_b, dt_bias, a_log, d_skip)


def _rope(x, cos, sin):
    lane = lax.broadcasted_iota(jnp.int32, (x.shape[0], LANES), 1)
    first_half = (lane % HEAD_DIM) < (HEAD_DIM // 2)
    halves = []
    for p in range(x.shape[1] // LANES):
        xp = x[:, p * LANES:(p + 1) * LANES]
        fwd = pltpu.roll(xp, HEAD_DIM // 2, 1)
        bwd = pltpu.roll(xp, LANES - HEAD_DIM // 2, 1)
        halves.append(jnp.where(first_half, -bwd, fwd))
    rot = jnp.concatenate(halves, axis=1)
    return x * cos + rot * sin


def _moba_kernel(qkv_ref, cos_ref, sin_ref, o_ref, k_s, vt_s, kmean_s, acc_s, bias_s):
    i = pl.program_id(1)
    W = GROUP_WIDTH
    TB = MOBA_BLOCK
    S = qkv_ref.shape[0]
    NB = S // TB

    @pl.when(i == 0)
    def _():
        for blk in range(NB):
            rs = slice(blk * TB, (blk + 1) * TB)
            kb = _rope(qkv_ref[rs, W:2 * W], cos_ref[rs, :], sin_ref[rs, :])
            k_s[rs, :] = kb.astype(BF16)
            kmean_s[blk:blk + 1, :] = jnp.mean(kb, axis=0, keepdims=True)
            vt_s[blk] = qkv_ref[rs, 2 * W:3 * W].T.astype(BF16)

    rows_i = pl.ds(pl.multiple_of(i * TB, TB), TB)
    q = _rope(qkv_ref[rows_i, 0:W], cos_ref[rows_i, :], sin_ref[rows_i, :])
    lane = lax.broadcasted_iota(jnp.int32, (TB, W), 1)
    lane8 = lax.broadcasted_iota(jnp.int32, (NB, W), 1)
    blk_id = lax.broadcasted_iota(jnp.int32, (NB, TB), 0)
    R = N_HEADS * TB
    key_loc = lax.broadcasted_iota(jnp.int32, (TB, R), 0)
    q_loc = lax.broadcasted_iota(jnp.int32, (TB, R), 1) & (TB - 1)
    kmean = kmean_s[...]
    q_hi, q_lo = _split2(q)
    scale = HEAD_DIM ** -0.5

    heads = [(lane >= h * HEAD_DIM) & (lane < (h + 1) * HEAD_DIM) for h in range(N_HEADS)]
    km_all = jnp.concatenate(
        [jnp.where((lane8 >= h * HEAD_DIM) & (lane8 < (h + 1) * HEAD_DIM), kmean, 0.0) for h in range(N_HEADS)],
        axis=0)
    km_hi, km_lo = _split2(km_all)
    gate_all = _dot_t(km_hi, q_hi) + _dot_t(km_hi, q_lo) + _dot_t(km_lo, q_hi)
    qhs = []
    for h in range(N_HEADS):
        gate = gate_all[h * NB:(h + 1) * NB, :]
        cnt = jnp.zeros((NB, TB), F32)
        for jp in range(NB):
            row = gate[jp:jp + 1, :]
            beats = (row > gate) | ((row == gate) & (blk_id > jp))
            cnt = cnt + jnp.where(beats, jnp.where(jp < i, 1.0, 0.0), 0.0)
        selected = (cnt < float(MOBA_TOPK)) & (blk_id < i)
        bias_s[:, h * TB:(h + 1) * TB] = jnp.where(selected, 0.0, NEG_BIG)
        qhs.append(jnp.where(heads[h], q, 0.0) * scale)
    qs = jnp.concatenate(qhs, axis=0).astype(BF16)

    s = jnp.where(key_loc <= q_loc, _dot_t(k_s[rows_i, :], qs), NEG_BIG)
    m = jnp.max(s, axis=0, keepdims=True)
    p = jnp.exp(s - m)
    l = jnp.sum(p, axis=0, keepdims=True)
    acc_s[...] = _dot(vt_s[i], p.astype(BF16))

    def body(j, carry):
        m, l = carry
        rows = pl.ds(pl.multiple_of(j * TB, TB), TB)
        s = _dot_t(k_s[rows, :], qs) + bias_s[pl.ds(j, 1), :]
        m_new = jnp.maximum(m, jnp.max(s, axis=0, keepdims=True))
        alpha = jnp.exp(m - m_new)
        p = jnp.exp(s - m_new)
        l = alpha * l + jnp.sum(p, axis=0, keepdims=True)
        acc_s[...] = alpha * acc_s[...] + _dot(vt_s[j], p.astype(BF16))
        return m_new, l

    m, l = lax.fori_loop(0, i, body, (m, l))
    outs = acc_s[...] / l
    out = outs[:, 0:TB].T
    for h in range(1, N_HEADS):
        out = jnp.where(heads[h], outs[:, h * TB:(h + 1) * TB].T, out)
    o_ref[...] = out


def _moba(qkv, cos, sin):
    B, S, _ = qkv.shape
    W = GROUP_WIDTH
    nb = S // MOBA_BLOCK
    return pl.pallas_call(
        _moba_kernel,
        grid=(B, S // MOBA_BLOCK),
        in_specs=[pl.BlockSpec((None, S, 3 * W), lambda b, i: (b, 0, 0)),
                  pl.BlockSpec((S, W), lambda b, i: (0, 0)),
                  pl.BlockSpec((S, W), lambda b, i: (0, 0))],
        out_specs=pl.BlockSpec((None, MOBA_BLOCK, W), lambda b, i: (b, i, 0)),
        out_shape=jax.ShapeDtypeStruct((B, S, W), F32),
        scratch_shapes=[pltpu.VMEM((S, W), BF16), pltpu.VMEM((nb, W, MOBA_BLOCK), BF16),
                        pltpu.VMEM((nb, W), F32), pltpu.VMEM((W, N_HEADS * MOBA_BLOCK), F32),
                        pltpu.VMEM((nb, N_HEADS * MOBA_BLOCK), F32)],
        compiler_params=_cparams(2),
        name="moba",
    )(qkv, cos, sin)


def _outproj_kernel(ya_ref, yb_ref, yc_ref, yd_ref, gg_ref, w_ref, x_ref, o_ref):
    W = GROUP_WIDTH
    acc = x_ref[...]
    for g, y_ref in enumerate((ya_ref, yb_ref, yc_ref, yd_ref)):
        yn = _rmsnorm(y_ref[...], gg_ref[g:g + 1, :]).astype(BF16)
        acc = acc + _dot(yn, w_ref[g * W:(g + 1) * W, :].astype(BF16))
    o_ref[...] = acc


def _outproj(ys, gg, w_out, layer, x2d, tm=1024):
    T = x2d.shape[0]
    W = GROUP_WIDTH
    return pl.pallas_call(
        _outproj_kernel,
        grid=(T // tm,),
        in_specs=[pl.BlockSpec((tm, W), lambda i: (i, 0))] * 4
                 + [pl.BlockSpec((4, W), lambda i: (0, 0)),
                    pl.BlockSpec((None, 4 * W, D_MODEL), lambda i: (layer, 0, 0), pipeline_mode=pl.Buffered(1)),
                    pl.BlockSpec((tm, D_MODEL), lambda i: (i, 0))],
        out_specs=pl.BlockSpec((tm, D_MODEL), lambda i: (i, 0)),
        out_shape=jax.ShapeDtypeStruct((T, D_MODEL), F32),
        compiler_params=_cparams(1),
        name="outproj",
    )(*ys, gg, w_out, x2d)


def _memkv_kernel(m_ref, g_ref, w_ref, o_ref):
    mn = _rmsnorm(m_ref[...], g_ref[...]).astype(BF16)
    o_ref[...] = _dot(mn, w_ref[...].astype(BF16)).astype(BF16)


def _memkv(mem, g, wkv, layer):
    B, M, _ = mem.shape
    return pl.pallas_call(
        _memkv_kernel,
        grid=(B,),
        in_specs=[pl.BlockSpec((None, M, D_MODEL), lambda b: (b, 0, 0)),
                  pl.BlockSpec((1, D_MODEL), lambda b: (0, 0)),
                  pl.BlockSpec((None, D_MODEL, 2 * D_MODEL), lambda b: (layer, 0, 0),
                               pipeline_mode=pl.Buffered(1))],
        out_specs=pl.BlockSpec((None, M, 2 * D_MODEL), lambda b: (b, 0, 0)),
        out_shape=jax.ShapeDtypeStruct((B, M, 2 * D_MODEL), BF16),
        compiler_params=_cparams(1),
        name="memkv",
    )(mem, g, wkv)


def _xattn_kernel(x_ref, g_ref, wq_ref, kv_ref, wo_ref, o_ref):
    x = x_ref[...]
    h = _rmsnorm(x, g_ref[...]).astype(BF16)
    q = (_dot(h, wq_ref[...].astype(BF16)) * (XATTN_HEAD_DIM ** -0.5)).astype(BF16)
    acc = x
    for hd in range(XATTN_HEADS):
        cols = slice(hd * XATTN_HEAD_DIM, (hd + 1) * XATTN_HEAD_DIM)
        vcols = slice(D_MODEL + hd * XATTN_HEAD_DIM, D_MODEL + (hd + 1) * XATTN_HEAD_DIM)
        s = _dot_t(q[:, cols], kv_ref[:, cols])
        p = jnp.exp(s - jnp.max(s, axis=-1, keepdims=True))
        p = p / jnp.sum(p, axis=-1, keepdims=True)
        o = _dot(p.astype(BF16), kv_ref[:, vcols]).astype(BF16)
        acc = acc + _dot(o, wo_ref[cols, :].astype(BF16))
    o_ref[...] = acc


def _xattn(x3d, g, wq, kv, wo, layer, tm=1024):
    B, S, _ = x3d.shape
    M = kv.shape[1]
    weight = pl.BlockSpec((None, D_MODEL, D_MODEL), lambda b, i: (layer, 0, 0), pipeline_mode=pl.Buffered(1))
    return pl.pallas_call(
        _xattn_kernel,
        grid=(B, S // tm),
        in_specs=[pl.BlockSpec((None, tm, D_MODEL), lambda b, i: (b, i, 0)),
                  pl.BlockSpec((1, D_MODEL), lambda b, i: (0, 0)),
                  weight,
                  pl.BlockSpec((None, M, 2 * D_MODEL), lambda b, i: (b, 0, 0)),
                  weight],
        out_specs=pl.BlockSpec((None, tm, D_MODEL), lambda b, i: (b, i, 0)),
        out_shape=jax.ShapeDtypeStruct((B, S, D_MODEL), F32),
        compiler_params=_cparams(2),
        name="xattn",
    )(x3d, g, wq, kv, wo)


def _moe_kernel(x_ref, g_ref, wr_ref, br_ref, w1_ref, w3_ref, w2_ref, fg_ref, o_ref,
                t_s, comb_s, acc_s, *, final_norm):
    grp = pl.program_id(1)
    tm = x_ref.shape[0]

    @pl.when(grp == 0)
    def _():
        t = _rmsnorm(x_ref[...], g_ref[...])
        t_s[...] = t.astype(BF16)
        acc_s[...] = jnp.zeros_like(acc_s)
        t_hi, t_lo = _split2(t)
        logits = (_dot(t_hi, wr_ref[0]) + _dot(t_hi, wr_ref[1]) + _dot(t_lo, wr_ref[0])) + br_ref[...]
        lane = lax.broadcasted_iota(jnp.int32, (tm, LANES), 1)
        lane_f = lane.astype(F32)
        is_g = (lane >= MOE_EXPERTS) & (lane < MOE_EXPERTS + MOE_GROUPS)
        lg = jnp.where(is_g, logits, NEG_BIG)
        gmax = jnp.max(lg, axis=-1, keepdims=True)
        pg_top = 1.0 / jnp.sum(jnp.exp(lg - gmax), axis=-1, keepdims=True)
        g_lane = jnp.min(jnp.where(is_g & (lg == gmax), lane_f, 1e9), axis=-1, keepdims=True)
        g_idx = g_lane - float(MOE_EXPERTS)
        e_grp = jnp.floor(lane_f * (1.0 / MOE_EPG))
        in_grp = (lane < MOE_EXPERTS) & (e_grp == g_idx)
        le = jnp.where(in_grp, logits, NEG_BIG)
        e1 = jnp.max(le, axis=-1, keepdims=True)
        i1 = jnp.min(jnp.where(in_grp & (le == e1), lane_f, 1e9), axis=-1, keepdims=True)
        le2 = jnp.where(lane_f == i1, NEG_BIG, le)
        e2 = jnp.max(le2, axis=-1, keepdims=True)
        i2 = jnp.min(jnp.where(in_grp & (lane_f != i1) & (le2 == e2), lane_f, 1e9), axis=-1, keepdims=True)
        r2 = jnp.exp(e2 - e1)
        w_first = 1.0 / (1.0 + r2)
        w_second = r2 / (1.0 + r2)
        comb_s[...] = pg_top * (jnp.where(lane_f == i1, w_first, 0.0) + jnp.where(lane_f == i2, w_second, 0.0))

    t = t_s[...]
    comb = comb_s[...]
    lane = lax.broadcasted_iota(jnp.int32, (tm, LANES), 1)
    upd = jnp.zeros((tm, D_MODEL), F32)
    for e in range(MOE_EPG):
        n = grp * MOE_EPG + e
        c = jnp.sum(jnp.where(lane == n, comb, 0.0), axis=-1, keepdims=True)
        hid = _silu(_dot(t, w1_ref[e])) * _dot(t, w3_ref[e])
        upd = upd + _dot((hid * c).astype(BF16), w2_ref[e])
    acc_s[...] += upd

    @pl.when(grp == MOE_GROUPS - 1)
    def _():
        y = x_ref[...] + acc_s[...]
        if final_norm:
            y = _rmsnorm(y, fg_ref[...])
        o_ref[...] = y


def _moe(x2d, g, w_router, b_router, w1, w3, w2, layer, final_g, final_norm, tm=1024):
    T = x2d.shape[0]
    return pl.pallas_call(
        functools.partial(_moe_kernel, final_norm=final_norm),
        grid=(T // tm, MOE_GROUPS),
        in_specs=[pl.BlockSpec((tm, D_MODEL), lambda i, e: (i, 0)),
                  pl.BlockSpec((1, D_MODEL), lambda i, e: (0, 0)),
                  pl.BlockSpec((2, D_MODEL, LANES), lambda i, e: (0, 0, 0)),
                  pl.BlockSpec((1, LANES), lambda i, e: (0, 0)),
                  pl.BlockSpec((None, MOE_EPG, D_MODEL, MOE_FF), lambda i, e: (layer, e, 0, 0)),
                  pl.BlockSpec((None, MOE_EPG, D_MODEL, MOE_FF), lambda i, e: (layer, e, 0, 0)),
                  pl.BlockSpec((None, MOE_EPG, MOE_FF, D_MODEL), lambda i, e: (layer, e, 0, 0)),
                  pl.BlockSpec((1, D_MODEL), lambda i, e: (0, 0))],
        out_specs=pl.BlockSpec((tm, D_MODEL), lambda i, e: (i, 0)),
        out_shape=jax.ShapeDtypeStruct((T, D_MODEL), F32),
        scratch_shapes=[pltpu.VMEM((tm, D_MODEL), BF16), pltpu.VMEM((tm, LANES), F32),
                        pltpu.VMEM((tm, D_MODEL), F32)],
        compiler_params=_cparams(2),
        name="moe",
    )(x2d, g, w_router, b_router, w1, w3, w2, final_g)


def _pad_lanes(v, width=LANES):
    return jnp.pad(v, (0, width - v.shape[0]))[None, :]


def _block_diag(w):
    H, n, _ = w.shape
    eye = jnp.eye(H, dtype=w.dtype)
    return (eye[:, None, :, None] * w[:, :, None, :]).reshape(H * n, H * n)


def _rope_tables(S):
    half = HEAD_DIM // 2
    inv_freq = ROPE_THETA ** (-jnp.arange(half, dtype=F32) / half)
    ang = jnp.arange(S, dtype=F32)[:, None] * inv_freq[None, :]
    reps = GROUP_WIDTH // half
    return jnp.tile(jnp.cos(ang), (1, reps)), jnp.tile(jnp.sin(ang), (1, reps))


def kernel(x, mem, mix_norm_g, w_in, lru_conv_w, lru_conv_b, lru_wr, lru_br, lru_wi, lru_bi, lru_lambda, ssm_conv_w, ssm_conv_b, ssm_dt_bias, ssm_a_log, ssm_d, group_norm_g, w_out, xattn_norm_g, mem_norm_g, xattn_wq, xattn_wkv, xattn_wo, ffn_norm_g, router_group_w, router_group_b, router_expert_w, router_expert_b, expert_w1, expert_w3, expert_w2, final_norm_g):
    B, S, D = x.shape
    T = B * S
    depth = w_in.shape[0]
    W = GROUP_WIDTH
    cos, sin = _rope_tables(S)
    x2d = x.reshape(T, D)
    w1_bf, w3_bf, w2_bf = expert_w1.astype(BF16), expert_w3.astype(BF16), expert_w2.astype(BF16)
    for l in range(depth):
        lru_xg, sb_qkv, ssm_z, ssm_xbc, ssm_dt, mb_qkv = _inproj(x2d, mix_norm_g[l][None, :], w_in, l)

        w_bd = jnp.concatenate([_block_diag(lru_wr[l]), _block_diag(lru_wi[l])], axis=1).astype(BF16)
        b_ri = jnp.concatenate([lru_br[l], lru_bi[l]])[None, :]
        y_a = _lru(lru_xg.reshape(B, S, 2 * W), lru_conv_w[l], lru_conv_b[l][None, :], w_bd, b_ri,
                   lru_lambda[l][None, :])
        y_b = _sb_attention(sb_qkv.reshape(B, S, 3 * W))
        y_c = _ssd(ssm_z.reshape(B, S, W), ssm_xbc.reshape(B, S, 3 * W), ssm_dt.reshape(B, S, LANES),
                   ssm_conv_w[l], ssm_conv_b[l][None, :], _pad_lanes(ssm_dt_bias[l]), _pad_lanes(ssm_a_log[l]),
                   jnp.repeat(ssm_d[l], HEAD_DIM)[None, :])
        y_d = _moba(mb_qkv.reshape(B, S, 3 * W), cos, sin)
        x2d = _outproj([y.reshape(T, W) for y in (y_a, y_b, y_c, y_d)], group_norm_g[l].reshape(4, W),
                       w_out, l, x2d)

        kv = _memkv(mem, mem_norm_g[l][None, :], xattn_wkv, l)
        x2d = _xattn(x2d.reshape(B, S, D), xattn_norm_g[l][None, :], xattn_wq, kv, xattn_wo, l).reshape(T, D)

        w_r = jnp.pad(jnp.concatenate([router_expert_w[l], router_group_w[l]], axis=1),
                      ((0, 0), (0, LANES - MOE_EXPERTS - MOE_GROUPS)))
        w_r_hi = w_r.astype(BF16)
        w_r_lo = (w_r - w_r_hi.astype(F32)).astype(BF16)
        b_r = _pad_lanes(jnp.concatenate([router_expert_b[l], router_group_b[l]]))
        x2d = _moe(x2d, ffn_norm_g[l][None, :], jnp.stack([w_r_hi, w_r_lo]), b_r, w1_bf, w3_bf, w2_bf, l,
                   final_norm_g[None, :], final_norm=(l == depth - 1))
    return x2d.reshape(B, S, D)
```

```python
import functools
import math

import jax
import jax.numpy as jnp
from jax import lax
from jax.experimental import pallas as pl
from jax.experimental.pallas import tpu as pltpu

F32 = jnp.float32
BF16 = jnp.bfloat16

D_MODEL = 1024
GROUP_WIDTH = 256
HEAD_DIM = 64
N_HEADS = 4
NORM_EPS = 1e-6
CONV_WIDTH = 4
LRU_C = 8.0
SB_BLOCK = 128
SB_WINDOW_BLOCKS = 3
SB_CHAINS = 4
SSM_CHUNK = 128
SSM_STATE = 128
MOBA_BLOCK = 256
MOBA_TOPK = 3
ROPE_THETA = 10000.0
XATTN_HEADS = 4
XATTN_HEAD_DIM = 256
MEM_LEN = 256
MOE_GROUPS = 4
MOE_EPG = 4
MOE_EXPERTS = 16
MOE_FF = 256
LANES = 128
SUBLANES = 8
NEG_BIG = -1e30
SB_EXP_FLOOR = -104.0
IN_OUT_WIDTHS = (512, 768, 256, 768, LANES, 768)
IN_MAIN = 512 + 768 + 256 + 768
IN_OUT_DTYPES = (F32, BF16, F32, F32, F32, F32)
VMEM_LIMIT = 56 * 1024 * 1024


def _cparams(n_axes):
    return pltpu.CompilerParams(dimension_semantics=("arbitrary",) * n_axes,
                                vmem_limit_bytes=VMEM_LIMIT)


def _dot(a, b):
    return jnp.dot(a, b, preferred_element_type=F32)


def _dot_t(a, b):
    return lax.dot_general(a, b, (((1,), (1,)), ((), ())), preferred_element_type=F32)


def _dot_tl(a, b):
    return lax.dot_general(a, b, (((0,), (0,)), ((), ())), preferred_element_type=F32)


def _split2(x):
    hi = x.astype(BF16)
    lo = (x - hi.astype(F32)).astype(BF16)
    return hi, lo


def _split3(x):
    hi = x.astype(BF16)
    r = x - hi.astype(F32)
    mid = r.astype(BF16)
    lo = (r - mid.astype(F32)).astype(BF16)
    return hi, mid, lo


def _dot_wide_lhs(x, m_bf16, parts=3):
    pieces = _split3(x) if parts == 3 else _split2(x)
    out = _dot(pieces[0], m_bf16)
    for p in pieces[1:]:
        out = out + _dot(p, m_bf16)
    return out


def _rmsnorm(x, g):
    return x * lax.rsqrt(jnp.mean(x * x, axis=-1, keepdims=True) + NORM_EPS) * g


def _softplus(x):
    return jnp.maximum(x, 0.0) + jnp.log(1.0 + jnp.exp(-jnp.abs(x)))


def _sigmoid(x):
    return 1.0 / (1.0 + jnp.exp(-x))


def _silu(x):
    return x * _sigmoid(x)


def _gelu_tanh(x):
    return 0.5 * x * (1.0 + jnp.tanh(math.sqrt(2.0 / math.pi) * (x + 0.044715 * (x * x * x))))


def _causal_conv(x, w_ref, b_ref):
    def taps(v, mask_rows):
        y = v * w_ref[CONV_WIDTH - 1:CONV_WIDTH, :] + b_ref[...]
        for s in range(1, CONV_WIDTH):
            vs = pltpu.roll(v, s, 0)
            if mask_rows is not None:
                vs = jnp.where(mask_rows >= s, vs, 0.0)
            y = y + vs * w_ref[CONV_WIDTH - 1 - s:CONV_WIDTH - s, :]
        return y

    head = x[0:SUBLANES, :]
    y_head = taps(head, lax.broadcasted_iota(jnp.int32, head.shape, 0))
    return jnp.concatenate([y_head, taps(x, None)[SUBLANES:, :]], axis=0)


def _inproj_kernel(x_ref, g_ref, w_ref, *refs):
    o_refs, w_s = refs[:-1], refs[-1]

    @pl.when(pl.program_id(0) == 0)
    def _():
        for c0 in range(0, IN_MAIN, 2 * LANES):
            w_s[:, c0:c0 + 2 * LANES] = w_ref[:, c0:c0 + 2 * LANES].astype(BF16)
        dt_tile = w_ref[:, IN_MAIN:IN_MAIN + LANES]
        lane = lax.broadcasted_iota(jnp.int32, dt_tile.shape, 1)
        w_s[:, IN_MAIN:IN_MAIN + LANES] = jnp.where(lane < N_HEADS, dt_tile, 0.0).astype(BF16)
        w_s[:, IN_MAIN + LANES:] = w_ref[:, IN_MAIN + N_HEADS:].astype(BF16)

    h = _rmsnorm(x_ref[...], g_ref[...]).astype(BF16)
    off = 0
    for o_ref, width in zip(o_refs, IN_OUT_WIDTHS):
        o_ref[...] = _dot(h, w_s[:, off:off + width]).astype(o_ref.dtype)
        off += width


def _inproj(x2d, g, w_in, layer, tm=512):
    T = x2d.shape[0]
    n_in = w_in.shape[-1]
    return pl.pallas_call(
        _inproj_kernel,
        grid=(T // tm,),
        in_specs=[pl.BlockSpec((tm, D_MODEL), lambda i: (i, 0)),
                  pl.BlockSpec((1, D_MODEL), lambda i: (0, 0)),
                  pl.BlockSpec((None, D_MODEL, n_in), lambda i: (layer, 0, 0), pipeline_mode=pl.Buffered(1))],
        out_specs=[pl.BlockSpec((tm, w), lambda i: (i, 0)) for w in IN_OUT_WIDTHS],
        out_shape=[jax.ShapeDtypeStruct((T, w), dt) for w, dt in zip(IN_OUT_WIDTHS, IN_OUT_DTYPES)],
        scratch_shapes=[pltpu.VMEM((D_MODEL, sum(IN_OUT_WIDTHS)), BF16)],
        compiler_params=_cparams(1),
        name="inproj",
    )(x2d, g, w_in)


def _lru_kernel(xg_ref, cw_ref, cb_ref, wbd_ref, bri_ref, lam_ref, o_ref):
    S = xg_ref.shape[0]
    W = GROUP_WIDTH
    xc = _causal_conv(xg_ref[:, 0:W], cw_ref, cb_ref)
    ri = _dot(xc.astype(BF16), wbd_ref[...]) + bri_ref[...]
    r = _sigmoid(ri[:, 0:W])
    i = _sigmoid(ri[:, W:2 * W])
    log_a = (LRU_C * r) * (-_softplus(-lam_ref[...]))
    a = jnp.exp(log_a)
    u = jnp.sqrt(1.0 - jnp.exp(2.0 * log_a)) * (i * xc)
    rows = lax.broadcasted_iota(jnp.int32, (S, W), 0)
    shift = 1
    while shift < S:
        if shift < SUBLANES:
            keep = rows >= shift
            a_s = jnp.where(keep, pltpu.roll(a, shift, 0), 1.0)
            u_s = jnp.where(keep, pltpu.roll(u, shift, 0), 0.0)
            u = a * u_s + u
            a = a * a_s
        else:
            u = jnp.concatenate([u[:shift], a[shift:] * u[:S - shift] + u[shift:]], axis=0)
            a = jnp.concatenate([a[:shift], a[shift:] * a[:S - shift]], axis=0)
        shift *= 2
    o_ref[...] = u * _gelu_tanh(xg_ref[:, W:2 * W])


def _lru(xg, conv_w, conv_b, w_bd, b_ri, lam):
    B, S, _ = xg.shape
    W = GROUP_WIDTH
    full = lambda shape: pl.BlockSpec(shape, lambda b: (0,) * len(shape))
    return pl.pallas_call(
        _lru_kernel,
        grid=(B,),
        in_specs=[pl.BlockSpec((None, S, 2 * W), lambda b: (b, 0, 0)),
                  full((CONV_WIDTH, W)), full((1, W)), full((W, 2 * W)), full((1, 2 * W)), full((1, W))],
        out_specs=pl.BlockSpec((None, S, W), lambda b: (b, 0, 0)),
        out_shape=jax.ShapeDtypeStruct((B, S, W), F32),
        compiler_params=_cparams(1),
        name="rglru",
    )(xg, conv_w, conv_b, w_bd, b_ri, lam)


def _sb_kernel(qkv_ref, o_ref, kt_s, v_s, acc_s, later_s):
    i = pl.program_id(1)
    W = GROUP_WIDTH
    TB = SB_BLOCK
    R = N_HEADS * TB

    NW = SB_WINDOW_BLOCKS
    KW = NW * TB
    PAD = (NW - 1) * TB
    S = qkv_ref.shape[0]

    @pl.when(i == 0)
    def _():
        kt_s[:, 0:PAD] = jnp.zeros((W, PAD), BF16)
        for r0 in range(0, S, W):
            kt_s[:, PAD + r0:PAD + r0 + W] = qkv_ref[r0:r0 + W, W:2 * W].astype(F32).T.astype(BF16)
        v_s[0:PAD, :] = jnp.zeros((PAD, W), BF16)
        v_s[PAD:PAD + S, :] = qkv_ref[:, 2 * W:3 * W].astype(BF16)

    lane = lax.broadcasted_iota(jnp.int32, (TB, W), 1)
    heads = [(lane >= h * HEAD_DIM) & (lane < (h + 1) * HEAD_DIM) for h in range(N_HEADS)]
    r_loc = lax.broadcasted_iota(jnp.int32, (R, KW), 0) & (TB - 1)
    c_loc = lax.broadcasted_iota(jnp.int32, (R, KW), 1)
    ur = lax.broadcasted_iota(jnp.int32, (TB, 2 * TB), 0)
    uc = lax.broadcasted_iota(jnp.int32, (TB, 2 * TB), 1)
    tri_ones = jnp.where((ur > uc) | (uc >= TB), 1.0, 0.0).astype(BF16)

    n_chains = o_ref.shape[0]
    blocks = [i + c * (S // TB // n_chains) for c in range(n_chains)]
    qss = []
    for blk in blocks:
        q = qkv_ref[pl.ds(pl.multiple_of(blk * TB, TB), TB), 0:W] * (HEAD_DIM ** -0.5)
        qss.append(jnp.concatenate([jnp.where(hm, q, 0.0) for hm in heads], axis=0).astype(BF16))

    acc_s[...] = jnp.zeros_like(acc_s)
    later_s[...] = jnp.zeros_like(later_s)

    def window(c, n):
        blk = blocks[c]
        first_key = (blk - n * NW - (NW - 1)) * TB
        rows = pl.ds(pl.multiple_of(jnp.maximum(first_key + PAD, 0), TB), KW)
        z = _dot(qss[c], kt_s[:, rows])
        key_abs = first_key + c_loc
        live = (key_abs < blk * TB + r_loc) & (key_abs >= 0)
        sp = _softplus(z)
        lf = jnp.where(live, -sp, 0.0)
        lf16 = lf.astype(BF16)
        order = list(range(NW - 1, -1, -1))
        stacked = jnp.concatenate([lf16[:, b * TB:(b + 1) * TB] for b in order], axis=0)
        cs_all = _dot(stacked, tri_ones)
        offset = later_s[c]
        after = [None] * NW
        for pos, b in enumerate(order):
            cs = cs_all[pos * R:(pos + 1) * R, :]
            after[b] = cs[:, 0:TB] + offset
            offset = offset + cs[:, TB:2 * TB]
        w = jnp.where(live, jnp.exp((z - sp) + jnp.concatenate(after, axis=1)), 0.0)
        acc_s[c] += _dot(w.astype(BF16), v_s[rows, :])
        later_s[c] = offset
        return jnp.where((n + 1) * NW <= blk, jnp.max(offset), SB_EXP_FLOOR)

    def cond(carry):
        return carry[1] > SB_EXP_FLOOR

    def body(carry):
        n = carry[0]
        later_max = window(0, n)
        for c in range(1, n_chains):
            later_max = jnp.maximum(later_max, window(c, n))
        return n + 1, later_max

    lax.while_loop(cond, body, (jnp.int32(0), jnp.float32(0.0)))
    for c in range(n_chains):
        out = acc_s[c, 0:TB, :]
        for h in range(1, N_HEADS):
            out = jnp.where(heads[h], acc_s[c, h * TB:(h + 1) * TB, :], out)
        o_ref[c] = out


def _sb_attention(qkv):
    B, S, _ = qkv.shape
    W = GROUP_WIDTH
    pad = (SB_WINDOW_BLOCKS - 1) * SB_BLOCK
    nc = SB_CHAINS
    rows = N_HEADS * SB_BLOCK
    out = pl.pallas_call(
        _sb_kernel,
        grid=(B, S // SB_BLOCK // nc),
        in_specs=[pl.BlockSpec((None, S, 3 * W), lambda b, i: (b, 0, 0))],
        out_specs=pl.BlockSpec((None, nc, SB_BLOCK, W), lambda b, i: (b, 0, i, 0)),
        out_shape=jax.ShapeDtypeStruct((B, nc, S // nc, W), F32),
        scratch_shapes=[pltpu.VMEM((W, S + pad), BF16), pltpu.VMEM((S + pad, W), BF16),
                        pltpu.VMEM((nc, rows, W), F32), pltpu.VMEM((nc, rows, SB_BLOCK), F32)],
        compiler_params=_cparams(2),
        name="stickbreak",
    )(qkv)
    return out.reshape(B, S, W)


def _ssd_kernel(z_ref, xbc_ref, dt_ref, cw_ref, cb_ref, dtb_ref, alog_ref, dskip_ref, o_ref, xbc_s):
    S = z_ref.shape[0]
    W = GROUP_WIDTH
    L = SSM_CHUNK
    xbc_s[...] = _silu(_causal_conv(xbc_ref[...], cw_ref, cb_ref))
    a_row = -jnp.exp(alog_ref[...])

    r_i = lax.broadcasted_iota(jnp.int32, (L, L), 0)
    c_i = lax.broadcasted_iota(jnp.int32, (L, L), 1)
    tri_incl = jnp.where(c_i <= r_i, 1.0, 0.0).astype(BF16)
    lower = c_i <= r_i
    e_r = lax.broadcasted_iota(jnp.int32, (LANES, W), 0)
    e_c = lax.broadcasted_iota(jnp.int32, (LANES, W), 1)
    expand = jnp.where((e_c >= e_r * HEAD_DIM) & (e_c < (e_r + 1) * HEAD_DIM), 1.0, 0.0).astype(BF16)
    lane_l = lax.broadcasted_iota(jnp.int32, (L, LANES), 1)

    def chunk(c, states):
        rows = slice(c * L, (c + 1) * L)
        xs = xbc_s[rows, 0:W]
        dt = _softplus(dt_ref[rows, :] + dtb_ref[...])
        a_dt = dt * a_row
        cs_col = _dot_wide_lhs_rhs(tri_incl, a_dt)
        cs_row = cs_col.T
        cs_full = _dot_wide_lhs(cs_col, expand)
        dt_full = _dot_wide_lhs(dt, expand)
        xd = xs * dt_full
        tot = cs_full[L - 1:L, :]
        xdec = (xd * jnp.exp(tot - cs_full)).astype(BF16)
        xd16 = xd.astype(BF16)
        ys = []
        new_states = []
        for g in range(2):
            gl = slice(g * LANES, (g + 1) * LANES)
            bm = xbc_s[rows, W + g * SSM_STATE:W + (g + 1) * SSM_STATE].astype(BF16)
            cm = xbc_s[rows, 2 * W + g * SSM_STATE:2 * W + (g + 1) * SSM_STATE].astype(BF16)
            cb = _dot_t(cm, bm)
            prev = states[g]
            y_off = _dot(cm, prev.astype(BF16)) * jnp.exp(cs_full[:, gl])
            y_g = y_off
            for hh in range(2):
                h = 2 * g + hh
                seg = jnp.where(lower, cs_col[:, h:h + 1] - cs_row[h:h + 1, :], -jnp.inf)
                y_h = _dot((cb * jnp.exp(seg)).astype(BF16), xd16[:, gl])
                in_head = (lane_l >= hh * HEAD_DIM) & (lane_l < (hh + 1) * HEAD_DIM)
                y_g = y_g + jnp.where(in_head, y_h, 0.0)
            new_states.append(prev * jnp.exp(tot[:, gl]) + _dot_tl(bm, xdec[:, gl]))
            ys.append(y_g)
        y = jnp.concatenate(ys, axis=1) + dskip_ref[...] * xs
        o_ref[rows, :] = y * _silu(z_ref[rows, :])
        return new_states

    states = [jnp.zeros((SSM_STATE, LANES), F32) for _ in range(2)]
    for c in range(S // L):
        states = chunk(c, states)


def _dot_wide_lhs_rhs(m_bf16, x):
    hi, mid, lo = _split3(x)
    return _dot(m_bf16, hi) + _dot(m_bf16, mid) + _dot(m_bf16, lo)


def _ssd(z, xbc, dt, conv_w, conv_b, dt_bias, a_log, d_skip):
    B, S, _ = z.shape
    W = GROUP_WIDTH
    full = lambda shape: pl.BlockSpec(shape, lambda b: (0,) * len(shape))
    return pl.pallas_call(
        _ssd_kernel,
        grid=(B,),
        in_specs=[pl.BlockSpec((None, S, W), lambda b: (b, 0, 0)),
                  pl.BlockSpec((None, S, 3 * W), lambda b: (b, 0, 0)),
                  pl.BlockSpec((None, S, LANES), lambda b: (b, 0, 0)),
                  full((CONV_WIDTH, 3 * W)), full((1, 3 * W)), full((1, LANES)), full((1, LANES)),
                  full((1, W))],
        out_specs=pl.BlockSpec((None, S, W), lambda b: (b, 0, 0)),
        out_shape=jax.ShapeDtypeStruct((B, S, W), F32),
        scratch_shapes=[pltpu.VMEM((S, 3 * W), F32)],
        compiler_params=_cparams(1),
        name="ssd",
    )(z, xbc, dt, conv_w, conv_b, dt_bias, a_log, d_skip)


def _rope(x, cos, sin):
    lane = lax.broadcasted_iota(jnp.int32, (x.shape[0], LANES), 1)
    first_half = (lane % HEAD_DIM) < (HEAD_DIM // 2)
    halves = []
    for p in range(x.shape[1] // LANES):
        xp = x[:, p * LANES:(p + 1) * LANES]
        fwd = pltpu.roll(xp, HEAD_DIM // 2, 1)
        bwd = pltpu.roll(xp, LANES - HEAD_DIM // 2, 1)
        halves.append(jnp.where(first_half, -bwd, fwd))
    rot = jnp.concatenate(halves, axis=1)
    return x * cos + rot * sin


def _moba_kernel(qkv_ref, cos_ref, sin_ref, o_ref, k_s, vt_s, kmean_s, acc_s, bias_s):
    i = pl.program_id(1)
    W = GROUP_WIDTH
    TB = MOBA_BLOCK
    S = qkv_ref.shape[0]
    NB = S // TB

    @pl.when(i == 0)
    def _():
        for blk in range(NB):
            rs = slice(blk * TB, (blk + 1) * TB)
            kb = _rope(qkv_ref[rs, W:2 * W], cos_ref[rs, :], sin_ref[rs, :])
            k_s[rs, :] = kb.astype(BF16)
            kmean_s[blk:blk + 1, :] = jnp.mean(kb, axis=0, keepdims=True)
            vt_s[blk] = qkv_ref[rs, 2 * W:3 * W].T.astype(BF16)

    rows_i = pl.ds(pl.multiple_of(i * TB, TB), TB)
    q = _rope(qkv_ref[rows_i, 0:W], cos_ref[rows_i, :], sin_ref[rows_i, :])
    lane = lax.broadcasted_iota(jnp.int32, (TB, W), 1)
    lane8 = lax.broadcasted_iota(jnp.int32, (NB, W), 1)
    blk_id = lax.broadcasted_iota(jnp.int32, (NB, TB), 0)
    R = N_HEADS * TB
    key_loc = lax.broadcasted_iota(jnp.int32, (TB, R), 0)
    q_loc = lax.broadcasted_iota(jnp.int32, (TB, R), 1) & (TB - 1)
    kmean = kmean_s[...]
    q_t = q.T
    qt_hi, qt_lo = _split2(q_t)
    dim_id = lax.broadcasted_iota(jnp.int32, (W, TB), 0)
    scale = HEAD_DIM ** -0.5

    heads = [(lane >= h * HEAD_DIM) & (lane < (h + 1) * HEAD_DIM) for h in range(N_HEADS)]
    km_all = jnp.concatenate(
        [jnp.where((lane8 >= h * HEAD_DIM) & (lane8 < (h + 1) * HEAD_DIM), kmean, 0.0) for h in range(N_HEADS)],
        axis=0)
    km_hi, km_lo = _split2(km_all)
    gate_all = _dot(km_hi, qt_hi) + _dot(km_hi, qt_lo) + _dot(km_lo, qt_hi)
    qhs = []
    for h in range(N_HEADS):
        gate = gate_all[h * NB:(h + 1) * NB, :]
        cnt = jnp.zeros((NB, TB), F32)
        for jp in range(NB):
            row = gate[jp:jp + 1, :]
            beats = (row > gate) | ((row == gate) & (blk_id > jp))
            cnt = cnt + jnp.where(beats, jnp.where(jp < i, 1.0, 0.0), 0.0)
        selected = (cnt < float(MOBA_TOPK)) & (blk_id < i)
        bias_s[:, h * TB:(h + 1) * TB] = jnp.where(selected, 0.0, NEG_BIG)
        in_head = (dim_id >= h * HEAD_DIM) & (dim_id < (h + 1) * HEAD_DIM)
        qhs.append(jnp.where(in_head, q_t, 0.0) * scale)
    qs_t = jnp.concatenate(qhs, axis=1).astype(BF16)

    s = jnp.where(key_loc <= q_loc, _dot(k_s[rows_i, :], qs_t), NEG_BIG)
    m = jnp.max(s, axis=0, keepdims=True)
    p = jnp.exp(s - m)
    l = jnp.sum(p, axis=0, keepdims=True)
    acc_s[...] = _dot(vt_s[i], p.astype(BF16))

    def body(j, carry):
        m, l = carry
        rows = pl.ds(pl.multiple_of(j * TB, TB), TB)
        s = _dot(k_s[rows, :], qs_t) + bias_s[pl.ds(j, 1), :]
        m_new = jnp.maximum(m, jnp.max(s, axis=0, keepdims=True))
        alpha = jnp.exp(m - m_new)
        p = jnp.exp(s - m_new)
        l = alpha * l + jnp.sum(p, axis=0, keepdims=True)
        acc_s[...] = alpha * acc_s[...] + _dot(vt_s[j], p.astype(BF16))
        return m_new, l

    m, l = lax.fori_loop(0, i, body, (m, l))
    outs = acc_s[...] / l
    out = outs[:, 0:TB].T
    for h in range(1, N_HEADS):
        out = jnp.where(heads[h], outs[:, h * TB:(h + 1) * TB].T, out)
    o_ref[...] = out


def _moba(qkv, cos, sin):
    B, S, _ = qkv.shape
    W = GROUP_WIDTH
    nb = S // MOBA_BLOCK
    return pl.pallas_call(
        _moba_kernel,
        grid=(B, S // MOBA_BLOCK),
        in_specs=[pl.BlockSpec((None, S, 3 * W), lambda b, i: (b, 0, 0)),
                  pl.BlockSpec((S, W), lambda b, i: (0, 0)),
                  pl.BlockSpec((S, W), lambda b, i: (0, 0))],
        out_specs=pl.BlockSpec((None, MOBA_BLOCK, W), lambda b, i: (b, i, 0)),
        out_shape=jax.ShapeDtypeStruct((B, S, W), F32),
        scratch_shapes=[pltpu.VMEM((S, W), BF16), pltpu.VMEM((nb, W, MOBA_BLOCK), BF16),
                        pltpu.VMEM((nb, W), F32), pltpu.VMEM((W, N_HEADS * MOBA_BLOCK), F32),
                        pltpu.VMEM((nb, N_HEADS * MOBA_BLOCK), F32)],
        compiler_params=_cparams(2),
        name="moba",
    )(qkv, cos, sin)


def _outproj_kernel(ya_ref, yb_ref, yc_ref, yd_ref, gg_ref, w_ref, x_ref, o_ref):
    W = GROUP_WIDTH
    acc = x_ref[...]
    for g, y_ref in enumerate((ya_ref, yb_ref, yc_ref, yd_ref)):
        yn = _rmsnorm(y_ref[...], gg_ref[g:g + 1, :]).astype(BF16)
        acc = acc + _dot(yn, w_ref[g * W:(g + 1) * W, :].astype(BF16))
    o_ref[...] = acc


def _outproj(ys, gg, w_out, layer, x2d, tm=1024):
    T = x2d.shape[0]
    W = GROUP_WIDTH
    return pl.pallas_call(
        _outproj_kernel,
        grid=(T // tm,),
        in_specs=[pl.BlockSpec((tm, W), lambda i: (i, 0))] * 4
                 + [pl.BlockSpec((4, W), lambda i: (0, 0)),
                    pl.BlockSpec((None, 4 * W, D_MODEL), lambda i: (layer, 0, 0), pipeline_mode=pl.Buffered(1)),
                    pl.BlockSpec((tm, D_MODEL), lambda i: (i, 0))],
        out_specs=pl.BlockSpec((tm, D_MODEL), lambda i: (i, 0)),
        out_shape=jax.ShapeDtypeStruct((T, D_MODEL), F32),
        compiler_params=_cparams(1),
        name="outproj",
    )(*ys, gg, w_out, x2d)


def _memkv_kernel(m_ref, g_ref, w_ref, o_ref):
    mn = _rmsnorm(m_ref[...], g_ref[...]).astype(BF16)
    o_ref[...] = _dot(mn, w_ref[...].astype(BF16)).astype(BF16)


def _memkv(mem, g, wkv, layer):
    B, M, _ = mem.shape
    return pl.pallas_call(
        _memkv_kernel,
        grid=(B,),
        in_specs=[pl.BlockSpec((None, M, D_MODEL), lambda b: (b, 0, 0)),
                  pl.BlockSpec((1, D_MODEL), lambda b: (0, 0)),
                  pl.BlockSpec((None, D_MODEL, 2 * D_MODEL), lambda b: (layer, 0, 0),
                               pipeline_mode=pl.Buffered(1))],
        out_specs=pl.BlockSpec((None, M, 2 * D_MODEL), lambda b: (b, 0, 0)),
        out_shape=jax.ShapeDtypeStruct((B, M, 2 * D_MODEL), BF16),
        compiler_params=_cparams(1),
        name="memkv",
    )(mem, g, wkv)


def _xattn_kernel(x_ref, g_ref, wq_ref, kv_ref, wo_ref, o_ref):
    x = x_ref[...]
    h = _rmsnorm(x, g_ref[...]).astype(BF16)
    q = (_dot(h, wq_ref[...].astype(BF16)) * (XATTN_HEAD_DIM ** -0.5)).astype(BF16)
    acc = x
    for hd in range(XATTN_HEADS):
        cols = slice(hd * XATTN_HEAD_DIM, (hd + 1) * XATTN_HEAD_DIM)
        vcols = slice(D_MODEL + hd * XATTN_HEAD_DIM, D_MODEL + (hd + 1) * XATTN_HEAD_DIM)
        s = _dot_t(q[:, cols], kv_ref[:, cols])
        p = jnp.exp(s - jnp.max(s, axis=-1, keepdims=True))
        p = p / jnp.sum(p, axis=-1, keepdims=True)
        o = _dot(p.astype(BF16), kv_ref[:, vcols]).astype(BF16)
        acc = acc + _dot(o, wo_ref[cols, :].astype(BF16))
    o_ref[...] = acc


def _xattn(x3d, g, wq, kv, wo, layer, tm=1024):
    B, S, _ = x3d.shape
    M = kv.shape[1]
    weight = pl.BlockSpec((None, D_MODEL, D_MODEL), lambda b, i: (layer, 0, 0), pipeline_mode=pl.Buffered(1))
    return pl.pallas_call(
        _xattn_kernel,
        grid=(B, S // tm),
        in_specs=[pl.BlockSpec((None, tm, D_MODEL), lambda b, i: (b, i, 0)),
                  pl.BlockSpec((1, D_MODEL), lambda b, i: (0, 0)),
                  weight,
                  pl.BlockSpec((None, M, 2 * D_MODEL), lambda b, i: (b, 0, 0)),
                  weight],
        out_specs=pl.BlockSpec((None, tm, D_MODEL), lambda b, i: (b, i, 0)),
        out_shape=jax.ShapeDtypeStruct((B, S, D_MODEL), F32),
        compiler_params=_cparams(2),
        name="xattn",
    )(x3d, g, wq, kv, wo)


def _moe_kernel(x_ref, g_ref, wr_ref, br_ref, w1_ref, w3_ref, w2_ref, fg_ref, o_ref,
                t_s, comb_s, acc_s, *, final_norm):
    grp = pl.program_id(1)
    tm = x_ref.shape[0]

    @pl.when(grp == 0)
    def _():
        t = _rmsnorm(x_ref[...], g_ref[...])
        t_s[...] = t.astype(BF16)
        acc_s[...] = jnp.zeros_like(acc_s)
        t_hi, t_lo = _split2(t)
        logits = (_dot(t_hi, wr_ref[0]) + _dot(t_hi, wr_ref[1]) + _dot(t_lo, wr_ref[0])) + br_ref[...]
        lane = lax.broadcasted_iota(jnp.int32, (tm, LANES), 1)
        lane_f = lane.astype(F32)
        is_g = (lane >= MOE_EXPERTS) & (lane < MOE_EXPERTS + MOE_GROUPS)
        lg = jnp.where(is_g, logits, NEG_BIG)
        gmax = jnp.max(lg, axis=-1, keepdims=True)
        pg_top = 1.0 / jnp.sum(jnp.exp(lg - gmax), axis=-1, keepdims=True)
        g_lane = jnp.min(jnp.where(is_g & (lg == gmax), lane_f, 1e9), axis=-1, keepdims=True)
        g_idx = g_lane - float(MOE_EXPERTS)
        e_grp = jnp.floor(lane_f * (1.0 / MOE_EPG))
        in_grp = (lane < MOE_EXPERTS) & (e_grp == g_idx)
        le = jnp.where(in_grp, logits, NEG_BIG)
        e1 = jnp.max(le, axis=-1, keepdims=True)
        i1 = jnp.min(jnp.where(in_grp & (le == e1), lane_f, 1e9), axis=-1, keepdims=True)
        le2 = jnp.where(lane_f == i1, NEG_BIG, le)
        e2 = jnp.max(le2, axis=-1, keepdims=True)
        i2 = jnp.min(jnp.where(in_grp & (lane_f != i1) & (le2 == e2), lane_f, 1e9), axis=-1, keepdims=True)
        r2 = jnp.exp(e2 - e1)
        w_first = 1.0 / (1.0 + r2)
        w_second = r2 / (1.0 + r2)
        comb_s[...] = pg_top * (jnp.where(lane_f == i1, w_first, 0.0) + jnp.where(lane_f == i2, w_second, 0.0))

    t = t_s[...]
    comb = comb_s[...]
    lane = lax.broadcasted_iota(jnp.int32, (tm, LANES), 1)
    upd = jnp.zeros((tm, D_MODEL), F32)
    for e in range(MOE_EPG):
        n = grp * MOE_EPG + e
        c = jnp.sum(jnp.where(lane == n, comb, 0.0), axis=-1, keepdims=True)
        hid = _silu(_dot(t, w1_ref[e])) * _dot(t, w3_ref[e])
        upd = upd + _dot((hid * c).astype(BF16), w2_ref[e])
    acc_s[...] += upd

    @pl.when(grp == MOE_GROUPS - 1)
    def _():
        y = x_ref[...] + acc_s[...]
        if final_norm:
            y = _rmsnorm(y, fg_ref[...])
        o_ref[...] = y


def _moe(x2d, g, w_router, b_router, w1, w3, w2, layer, final_g, final_norm, tm=1024):
    T = x2d.shape[0]
    return pl.pallas_call(
        functools.partial(_moe_kernel, final_norm=final_norm),
        grid=(T // tm, MOE_GROUPS),
        in_specs=[pl.BlockSpec((tm, D_MODEL), lambda i, e: (i, 0)),
                  pl.BlockSpec((1, D_MODEL), lambda i, e: (0, 0)),
                  pl.BlockSpec((2, D_MODEL, LANES), lambda i, e: (0, 0, 0)),
                  pl.BlockSpec((1, LANES), lambda i, e: (0, 0)),
                  pl.BlockSpec((None, MOE_EPG, D_MODEL, MOE_FF), lambda i, e: (layer, e, 0, 0)),
                  pl.BlockSpec((None, MOE_EPG, D_MODEL, MOE_FF), lambda i, e: (layer, e, 0, 0)),
                  pl.BlockSpec((None, MOE_EPG, MOE_FF, D_MODEL), lambda i, e: (layer, e, 0, 0)),
                  pl.BlockSpec((1, D_MODEL), lambda i, e: (0, 0))],
        out_specs=pl.BlockSpec((tm, D_MODEL), lambda i, e: (i, 0)),
        out_shape=jax.ShapeDtypeStruct((T, D_MODEL), F32),
        scratch_shapes=[pltpu.VMEM((tm, D_MODEL), BF16), pltpu.VMEM((tm, LANES), F32),
                        pltpu.VMEM((tm, D_MODEL), F32)],
        compiler_params=_cparams(2),
        name="moe",
    )(x2d, g, w_router, b_router, w1, w3, w2, final_g)


def _pad_lanes(v, width=LANES):
    return jnp.pad(v, (0, width - v.shape[0]))[None, :]


def _block_diag(w):
    H, n, _ = w.shape
    eye = jnp.eye(H, dtype=w.dtype)
    return (eye[:, None, :, None] * w[:, :, None, :]).reshape(H * n, H * n)


def _rope_tables(S):
    half = HEAD_DIM // 2
    inv_freq = ROPE_THETA ** (-jnp.arange(half, dtype=F32) / half)
    ang = jnp.arange(S, dtype=F32)[:, None] * inv_freq[None, :]
    reps = GROUP_WIDTH // half
    return jnp.tile(jnp.cos(ang), (1, reps)), jnp.tile(jnp.sin(ang), (1, reps))


def kernel(x, mem, mix_norm_g, w_in, lru_conv_w, lru_conv_b, lru_wr, lru_br, lru_wi, lru_bi, lru_lambda, ssm_conv_w, ssm_conv_b, ssm_dt_bias, ssm_a_log, ssm_d, group_norm_g, w_out, xattn_norm_g, mem_norm_g, xattn_wq, xattn_wkv, xattn_wo, ffn_norm_g, router_group_w, router_group_b, router_expert_w, router_expert_b, expert_w1, expert_w3, expert_w2, final_norm_g):
    B, S, D = x.shape
    T = B * S
    depth = w_in.shape[0]
    W = GROUP_WIDTH
    cos, sin = _rope_tables(S)
    x2d = x.reshape(T, D)
    w1_bf, w3_bf, w2_bf = expert_w1.astype(BF16), expert_w3.astype(BF16), expert_w2.astype(BF16)
    for l in range(depth):
        lru_xg, sb_qkv, ssm_z, ssm_xbc, ssm_dt, mb_qkv = _inproj(x2d, mix_norm_g[l][None, :], w_in, l)

        w_bd = jnp.concatenate([_block_diag(lru_wr[l]), _block_diag(lru_wi[l])], axis=1).astype(BF16)
        b_ri = jnp.concatenate([lru_br[l], lru_bi[l]])[None, :]
        y_a = _lru(lru_xg.reshape(B, S, 2 * W), lru_conv_w[l], lru_conv_b[l][None, :], w_bd, b_ri,
                   lru_lambda[l][None, :])
        y_b = _sb_attention(sb_qkv.reshape(B, S, 3 * W))
        y_c = _ssd(ssm_z.reshape(B, S, W), ssm_xbc.reshape(B, S, 3 * W), ssm_dt.reshape(B, S, LANES),
                   ssm_conv_w[l], ssm_conv_b[l][None, :], _pad_lanes(ssm_dt_bias[l]), _pad_lanes(ssm_a_log[l]),
                   jnp.repeat(ssm_d[l], HEAD_DIM)[None, :])
        y_d = _moba(mb_qkv.reshape(B, S, 3 * W), cos, sin)
        x2d = _outproj([y.reshape(T, W) for y in (y_a, y_b, y_c, y_d)], group_norm_g[l].reshape(4, W),
                       w_out, l, x2d)

        kv = _memkv(mem, mem_norm_g[l][None, :], xattn_wkv, l)
        x2d = _xattn(x2d.reshape(B, S, D), xattn_norm_g[l][None, :], xattn_wq, kv, xattn_wo, l).reshape(T, D)

        w_r = jnp.pad(jnp.concatenate([router_expert_w[l], router_group_w[l]], axis=1),
                      ((0, 0), (0, LANES - MOE_EXPERTS - MOE_GROUPS)))
        w_r_hi = w_r.astype(BF16)
        w_r_lo = (w_r - w_r_hi.astype(F32)).astype(BF16)
        b_r = _pad_lanes(jnp.concatenate([router_expert_b[l], router_group_b[l]]))
        x2d = _moe(x2d, ffn_norm_g[l][None, :], jnp.stack([w_r_hi, w_r_lo]), b_r, w1_bf, w3_bf, w2_bf, l,
                   final_norm_g[None, :], final_norm=(l == depth - 1))
    return x2d.reshape(B, S, D)
```

```python
import functools
import math

import jax
import jax.numpy as jnp
from jax import lax
from jax.experimental import pallas as pl
from jax.experimental.pallas import tpu as pltpu

F32 = jnp.float32
BF16 = jnp.bfloat16

D_MODEL = 1024
GROUP_WIDTH = 256
HEAD_DIM = 64
N_HEADS = 4
NORM_EPS = 1e-6
CONV_WIDTH = 4
LRU_C = 8.0
SB_BLOCK = 128
SB_WINDOW_BLOCKS = 3
SB_CHAINS = 4
SSM_CHUNK = 128
SSM_STATE = 128
MOBA_BLOCK = 256
MOBA_TOPK = 3
ROPE_THETA = 10000.0
XATTN_HEADS = 4
XATTN_HEAD_DIM = 256
MEM_LEN = 256
MOE_GROUPS = 4
MOE_EPG = 4
MOE_EXPERTS = 16
MOE_FF = 256
LANES = 128
SUBLANES = 8
NEG_BIG = -1e30
SB_EXP_FLOOR = -104.0
IN_OUT_WIDTHS = (512, 768, 256, 768, LANES, 768)
IN_MAIN = 512 + 768 + 256 + 768
IN_OUT_DTYPES = (F32, BF16, F32, F32, F32, F32)
VMEM_LIMIT = 56 * 1024 * 1024


def _cparams(n_axes):
    return pltpu.CompilerParams(dimension_semantics=("arbitrary",) * n_axes,
                                vmem_limit_bytes=VMEM_LIMIT)


def _dot(a, b):
    return jnp.dot(a, b, preferred_element_type=F32)


def _dot_t(a, b):
    return lax.dot_general(a, b, (((1,), (1,)), ((), ())), preferred_element_type=F32)


def _dot_tl(a, b):
    return lax.dot_general(a, b, (((0,), (0,)), ((), ())), preferred_element_type=F32)


def _split2(x):
    hi = x.astype(BF16)
    lo = (x - hi.astype(F32)).astype(BF16)
    return hi, lo


def _split3(x):
    hi = x.astype(BF16)
    r = x - hi.astype(F32)
    mid = r.astype(BF16)
    lo = (r - mid.astype(F32)).astype(BF16)
    return hi, mid, lo


def _dot_wide_lhs(x, m_bf16, parts=3):
    pieces = _split3(x) if parts == 3 else _split2(x)
    out = _dot(pieces[0], m_bf16)
    for p in pieces[1:]:
        out = out + _dot(p, m_bf16)
    return out


def _rmsnorm(x, g):
    return x * lax.rsqrt(jnp.mean(x * x, axis=-1, keepdims=True) + NORM_EPS) * g


def _softplus(x):
    return jnp.maximum(x, 0.0) + jnp.log(1.0 + jnp.exp(-jnp.abs(x)))


def _sigmoid(x):
    return 1.0 / (1.0 + jnp.exp(-x))


def _silu(x):
    return x * _sigmoid(x)


def _gelu_tanh(x):
    return 0.5 * x * (1.0 + jnp.tanh(math.sqrt(2.0 / math.pi) * (x + 0.044715 * (x * x * x))))


def _shift_rows_down(x):
    rows = lax.broadcasted_iota(jnp.int32, x.shape, 0)
    return jnp.where(rows >= 1, pltpu.roll(x, 1, 0), 0.0)


def _phase_conv(slab_ref, w_ref, b_ref, cols):
    P = SUBLANES
    nt = slab_ref.shape[0] // P
    x = [slab_ref[pl.ds(p, nt, stride=P), :] for p in range(P)]
    prev = {p: _shift_rows_down(x[p]) for p in range(P - CONV_WIDTH + 1, P)}
    out = []
    for p in range(P):
        y = x[p] * w_ref[CONV_WIDTH - 1:CONV_WIDTH, cols] + b_ref[:, cols]
        for k in range(1, CONV_WIDTH):
            src = x[p - k] if p - k >= 0 else prev[p - k + P]
            y = y + src * w_ref[CONV_WIDTH - 1 - k:CONV_WIDTH - k, cols]
        out.append(y)
    return out


def _inproj_kernel(x_ref, g_ref, w_ref, *refs):
    o_refs, w_s = refs[:-1], refs[-1]

    @pl.when(pl.program_id(0) == 0)
    def _():
        for c0 in range(0, IN_MAIN, 2 * LANES):
            w_s[:, c0:c0 + 2 * LANES] = w_ref[:, c0:c0 + 2 * LANES].astype(BF16)
        dt_tile = w_ref[:, IN_MAIN:IN_MAIN + LANES]
        lane = lax.broadcasted_iota(jnp.int32, dt_tile.shape, 1)
        w_s[:, IN_MAIN:IN_MAIN + LANES] = jnp.where(lane < N_HEADS, dt_tile, 0.0).astype(BF16)
        w_s[:, IN_MAIN + LANES:] = w_ref[:, IN_MAIN + N_HEADS:].astype(BF16)

    h = _rmsnorm(x_ref[...], g_ref[...]).astype(BF16)
    off = 0
    for o_ref, width in zip(o_refs, IN_OUT_WIDTHS):
        o_ref[...] = _dot(h, w_s[:, off:off + width]).astype(o_ref.dtype)
        off += width


def _inproj(x2d, g, w_in, layer, tm=512):
    T = x2d.shape[0]
    n_in = w_in.shape[-1]
    return pl.pallas_call(
        _inproj_kernel,
        grid=(T // tm,),
        in_specs=[pl.BlockSpec((tm, D_MODEL), lambda i: (i, 0)),
                  pl.BlockSpec((1, D_MODEL), lambda i: (0, 0)),
                  pl.BlockSpec((None, D_MODEL, n_in), lambda i: (layer, 0, 0), pipeline_mode=pl.Buffered(1))],
        out_specs=[pl.BlockSpec((tm, w), lambda i: (i, 0)) for w in IN_OUT_WIDTHS],
        out_shape=[jax.ShapeDtypeStruct((T, w), dt) for w, dt in zip(IN_OUT_WIDTHS, IN_OUT_DTYPES)],
        scratch_shapes=[pltpu.VMEM((D_MODEL, sum(IN_OUT_WIDTHS)), BF16)],
        compiler_params=_cparams(1),
        name="inproj",
    )(x2d, g, w_in)


def _scan_rows(a, u):
    n = a.shape[0]
    rows = lax.broadcasted_iota(jnp.int32, a.shape, 0)
    shift = 1
    while shift < n:
        if shift < SUBLANES:
            keep = rows >= shift
            a_s = jnp.where(keep, pltpu.roll(a, shift, 0), 1.0)
            u_s = jnp.where(keep, pltpu.roll(u, shift, 0), 0.0)
            u = a * u_s + u
            a = a * a_s
        else:
            u = jnp.concatenate([u[:shift], a[shift:] * u[:n - shift] + u[shift:]], axis=0)
            a = jnp.concatenate([a[:shift], a[shift:] * a[:n - shift]], axis=0)
        shift *= 2
    return u


def _lru_kernel(xg_ref, cw_ref, cb_ref, wbd_ref, bri_ref, lam_ref, o_ref, in_s, out_s):
    S = xg_ref.shape[0]
    W = GROUP_WIDTH
    P = SUBLANES
    NT = S // P
    for s in range(2 * W // LANES):
        in_s[s] = xg_ref[:, s * LANES:(s + 1) * LANES]
    log_sig_lam = -_softplus(-lam_ref[...])

    for s in range(W // LANES):
        cols = slice(s * LANES, (s + 1) * LANES)
        xc_all = jnp.concatenate(_phase_conv(in_s.at[s], cw_ref, cb_ref, cols), axis=0)
        w_slab = jnp.concatenate([wbd_ref[cols, cols], wbd_ref[cols, W + s * LANES:W + (s + 1) * LANES]], axis=1)
        ri = _dot(xc_all.astype(BF16), w_slab)
        r = _sigmoid(ri[:, 0:LANES] + bri_ref[:, cols])
        i = _sigmoid(ri[:, LANES:2 * LANES] + bri_ref[:, W + s * LANES:W + (s + 1) * LANES])
        log_a = (LRU_C * r) * log_sig_lam[:, cols]
        a = jnp.exp(log_a)
        u = jnp.sqrt(1.0 - jnp.exp(2.0 * log_a)) * (i * xc_all)
        loc = [u[0:NT]]
        dec = [a[0:NT]]
        for p in range(1, P):
            ap = a[p * NT:(p + 1) * NT]
            loc.append(ap * loc[-1] + u[p * NT:(p + 1) * NT])
            dec.append(ap * dec[-1])
        carry = _shift_rows_down(_scan_rows(dec[-1], loc[-1]))
        for p in range(P):
            h = loc[p] + dec[p] * carry
            out_s[s, pl.ds(p, NT, stride=P), :] = h * _gelu_tanh(in_s[W // LANES + s, pl.ds(p, NT, stride=P), :])
    for s in range(W // LANES):
        o_ref[:, s * LANES:(s + 1) * LANES] = out_s[s]


def _lru(xg, conv_w, conv_b, w_bd, b_ri, lam):
    B, S, _ = xg.shape
    W = GROUP_WIDTH
    full = lambda shape: pl.BlockSpec(shape, lambda b: (0,) * len(shape))
    return pl.pallas_call(
        _lru_kernel,
        grid=(B,),
        in_specs=[pl.BlockSpec((None, S, 2 * W), lambda b: (b, 0, 0)),
                  full((CONV_WIDTH, W)), full((1, W)), full((W, 2 * W)), full((1, 2 * W)), full((1, W))],
        out_specs=pl.BlockSpec((None, S, W), lambda b: (b, 0, 0)),
        out_shape=jax.ShapeDtypeStruct((B, S, W), F32),
        scratch_shapes=[pltpu.VMEM((2 * W // LANES, S, LANES), F32), pltpu.VMEM((W // LANES, S, LANES), F32)],
        compiler_params=_cparams(1),
        name="rglru",
    )(xg, conv_w, conv_b, w_bd, b_ri, lam)


def _sb_kernel(qkv_ref, o_ref, kt_s, v_s, acc_s, later_s):
    i = pl.program_id(1)
    W = GROUP_WIDTH
    TB = SB_BLOCK
    R = N_HEADS * TB

    NW = SB_WINDOW_BLOCKS
    KW = NW * TB
    PAD = (NW - 1) * TB
    S = qkv_ref.shape[0]

    @pl.when(i == 0)
    def _():
        kt_s[:, 0:PAD] = jnp.zeros((W, PAD), BF16)
        for r0 in range(0, S, W):
            kt_s[:, PAD + r0:PAD + r0 + W] = qkv_ref[r0:r0 + W, W:2 * W].astype(F32).T.astype(BF16)
        v_s[0:PAD, :] = jnp.zeros((PAD, W), BF16)
        v_s[PAD:PAD + S, :] = qkv_ref[:, 2 * W:3 * W].astype(BF16)

    lane = lax.broadcasted_iota(jnp.int32, (TB, W), 1)
    heads = [(lane >= h * HEAD_DIM) & (lane < (h + 1) * HEAD_DIM) for h in range(N_HEADS)]
    r_loc = lax.broadcasted_iota(jnp.int32, (R, KW), 0) & (TB - 1)
    c_loc = lax.broadcasted_iota(jnp.int32, (R, KW), 1)
    ur = lax.broadcasted_iota(jnp.int32, (TB, 2 * TB), 0)
    uc = lax.broadcasted_iota(jnp.int32, (TB, 2 * TB), 1)
    tri_ones = jnp.where((ur > uc) | (uc >= TB), 1.0, 0.0).astype(BF16)

    n_chains = o_ref.shape[0]
    blocks = [i + c * (S // TB // n_chains) for c in range(n_chains)]
    qss = []
    for blk in blocks:
        q = qkv_ref[pl.ds(pl.multiple_of(blk * TB, TB), TB), 0:W] * (HEAD_DIM ** -0.5)
        qss.append(jnp.concatenate([jnp.where(hm, q, 0.0) for hm in heads], axis=0).astype(BF16))

    acc_s[...] = jnp.zeros_like(acc_s)
    later_s[...] = jnp.zeros_like(later_s)

    def window(c, n):
        blk = blocks[c]
        first_key = (blk - n * NW - (NW - 1)) * TB
        rows = pl.ds(pl.multiple_of(jnp.maximum(first_key + PAD, 0), TB), KW)
        z = _dot(qss[c], kt_s[:, rows])
        key_abs = first_key + c_loc
        live = (key_abs < blk * TB + r_loc) & (key_abs >= 0)
        sp = _softplus(z)
        lf = jnp.where(live, -sp, 0.0)
        lf16 = lf.astype(BF16)
        order = list(range(NW - 1, -1, -1))
        stacked = jnp.concatenate([lf16[:, b * TB:(b + 1) * TB] for b in order], axis=0)
        cs_all = _dot(stacked, tri_ones)
        offset = later_s[c]
        after = [None] * NW
        for pos, b in enumerate(order):
            cs = cs_all[pos * R:(pos + 1) * R, :]
            after[b] = cs[:, 0:TB] + offset
            offset = offset + cs[:, TB:2 * TB]
        w = jnp.where(live, jnp.exp((z - sp) + jnp.concatenate(after, axis=1)), 0.0)
        acc_s[c] += _dot(w.astype(BF16), v_s[rows, :])
        later_s[c] = offset
        return jnp.where((n + 1) * NW <= blk, jnp.max(offset), SB_EXP_FLOOR)

    def cond(carry):
        return carry[1] > SB_EXP_FLOOR

    def body(carry):
        n = carry[0]
        later_max = window(0, n)
        for c in range(1, n_chains):
            later_max = jnp.maximum(later_max, window(c, n))
        return n + 1, later_max

    lax.while_loop(cond, body, (jnp.int32(0), jnp.float32(0.0)))
    for c in range(n_chains):
        out = acc_s[c, 0:TB, :]
        for h in range(1, N_HEADS):
            out = jnp.where(heads[h], acc_s[c, h * TB:(h + 1) * TB, :], out)
        o_ref[c] = out


def _sb_attention(qkv):
    B, S, _ = qkv.shape
    W = GROUP_WIDTH
    pad = (SB_WINDOW_BLOCKS - 1) * SB_BLOCK
    nc = SB_CHAINS
    rows = N_HEADS * SB_BLOCK
    out = pl.pallas_call(
        _sb_kernel,
        grid=(B, S // SB_BLOCK // nc),
        in_specs=[pl.BlockSpec((None, S, 3 * W), lambda b, i: (b, 0, 0))],
        out_specs=pl.BlockSpec((None, nc, SB_BLOCK, W), lambda b, i: (b, 0, i, 0)),
        out_shape=jax.ShapeDtypeStruct((B, nc, S // nc, W), F32),
        scratch_shapes=[pltpu.VMEM((W, S + pad), BF16), pltpu.VMEM((S + pad, W), BF16),
                        pltpu.VMEM((nc, rows, W), F32), pltpu.VMEM((nc, rows, SB_BLOCK), F32)],
        compiler_params=_cparams(2),
        name="stickbreak",
    )(qkv)
    return out.reshape(B, S, W)


def _ssd_kernel(z_ref, xbc_ref, dt_ref, cw_ref, cb_ref, dtb_ref, alog_ref, dskip_ref, o_ref, in_s, xbc_s):
    S = z_ref.shape[0]
    W = GROUP_WIDTH
    L = SSM_CHUNK
    for s in range(3 * W // LANES):
        cols = slice(s * LANES, (s + 1) * LANES)
        in_s[...] = xbc_ref[:, cols]
        for p, y in enumerate(_phase_conv(in_s, cw_ref, cb_ref, cols)):
            xbc_s[s, pl.ds(p, S // SUBLANES, stride=SUBLANES), :] = _silu(y)
    a_row = -jnp.exp(alog_ref[...])

    r_i = lax.broadcasted_iota(jnp.int32, (L, L), 0)
    c_i = lax.broadcasted_iota(jnp.int32, (L, L), 1)
    tri_incl = jnp.where(c_i <= r_i, 1.0, 0.0).astype(BF16)
    lower = c_i <= r_i
    e_r = lax.broadcasted_iota(jnp.int32, (LANES, W), 0)
    e_c = lax.broadcasted_iota(jnp.int32, (LANES, W), 1)
    expand = jnp.where((e_c >= e_r * HEAD_DIM) & (e_c < (e_r + 1) * HEAD_DIM), 1.0, 0.0).astype(BF16)
    lane_l = lax.broadcasted_iota(jnp.int32, (L, LANES), 1)

    def chunk(c, states):
        rows = slice(c * L, (c + 1) * L)
        xs = jnp.concatenate([xbc_s[0, rows, :], xbc_s[1, rows, :]], axis=1)
        dt = _softplus(dt_ref[rows, :] + dtb_ref[...])
        a_dt = dt * a_row
        cs_col = _dot_wide_lhs_rhs(tri_incl, a_dt)
        cs_row = cs_col.T
        cs_full = _dot_wide_lhs(cs_col, expand)
        dt_full = _dot_wide_lhs(dt, expand)
        xd = xs * dt_full
        tot = cs_full[L - 1:L, :]
        xdec = (xd * jnp.exp(tot - cs_full)).astype(BF16)
        xd16 = xd.astype(BF16)
        ys = []
        new_states = []
        for g in range(2):
            gl = slice(g * LANES, (g + 1) * LANES)
            bm = xbc_s[2 + g, rows, :].astype(BF16)
            cm = xbc_s[4 + g, rows, :].astype(BF16)
            cb = _dot_t(cm, bm)
            prev = states[g]
            y_off = _dot(cm, prev.astype(BF16)) * jnp.exp(cs_full[:, gl])
            y_g = y_off
            for hh in range(2):
                h = 2 * g + hh
                seg = jnp.where(lower, cs_col[:, h:h + 1] - cs_row[h:h + 1, :], -jnp.inf)
                y_h = _dot((cb * jnp.exp(seg)).astype(BF16), xd16[:, gl])
                in_head = (lane_l >= hh * HEAD_DIM) & (lane_l < (hh + 1) * HEAD_DIM)
                y_g = y_g + jnp.where(in_head, y_h, 0.0)
            new_states.append(prev * jnp.exp(tot[:, gl]) + _dot_tl(bm, xdec[:, gl]))
            ys.append(y_g)
        y = jnp.concatenate(ys, axis=1) + dskip_ref[...] * xs
        o_ref[rows, :] = y * _silu(z_ref[rows, :])
        return new_states

    states = [jnp.zeros((SSM_STATE, LANES), F32) for _ in range(2)]
    for c in range(S // L):
        states = chunk(c, states)


def _dot_wide_lhs_rhs(m_bf16, x):
    hi, mid, lo = _split3(x)
    return _dot(m_bf16, hi) + _dot(m_bf16, mid) + _dot(m_bf16, lo)


def _ssd(z, xbc, dt, conv_w, conv_b, dt_bias, a_log, d_skip):
    B, S, _ = z.shape
    W = GROUP_WIDTH
    full = lambda shape: pl.BlockSpec(shape, lambda b: (0,) * len(shape))
    return pl.pallas_call(
        _ssd_kernel,
        grid=(B,),
        in_specs=[pl.BlockSpec((None, S, W), lambda b: (b, 0, 0)),
                  pl.BlockSpec((None, S, 3 * W), lambda b: (b, 0, 0)),
                  pl.BlockSpec((None, S, LANES), lambda b: (b, 0, 0)),
                  full((CONV_WIDTH, 3 * W)), full((1, 3 * W)), full((1, LANES)), full((1, LANES)),
                  full((1, W))],
        out_specs=pl.BlockSpec((None, S, W), lambda b: (b, 0, 0)),
        out_shape=jax.ShapeDtypeStruct((B, S, W), F32),
        scratch_shapes=[pltpu.VMEM((S, LANES), F32), pltpu.VMEM((3 * W // LANES, S, LANES), F32)],
        compiler_params=_cparams(1),
        name="ssd",
    )(z, xbc, dt, conv_w, conv_b, dt_bias, a_log, d_skip)


def _rope(x, cos, sin):
    lane = lax.broadcasted_iota(jnp.int32, (x.shape[0], LANES), 1)
    first_half = (lane % HEAD_DIM) < (HEAD_DIM // 2)
    halves = []
    for p in range(x.shape[1] // LANES):
        xp = x[:, p * LANES:(p + 1) * LANES]
        fwd = pltpu.roll(xp, HEAD_DIM // 2, 1)
        bwd = pltpu.roll(xp, LANES - HEAD_DIM // 2, 1)
        halves.append(jnp.where(first_half, -bwd, fwd))
    rot = jnp.concatenate(halves, axis=1)
    return x * cos + rot * sin


def _moba_kernel(qkv_ref, cos_ref, sin_ref, o_ref, k_s, vt_s, kmean_s, acc_s, bias_s):
    i = pl.program_id(1)
    W = GROUP_WIDTH
    TB = MOBA_BLOCK
    S = qkv_ref.shape[0]
    NB = S // TB

    @pl.when(i == 0)
    def _():
        for blk in range(NB):
            rs = slice(blk * TB, (blk + 1) * TB)
            kb = _rope(qkv_ref[rs, W:2 * W], cos_ref[rs, :], sin_ref[rs, :])
            k_s[rs, :] = kb.astype(BF16)
            kmean_s[blk:blk + 1, :] = jnp.mean(kb, axis=0, keepdims=True)
            vt_s[blk] = qkv_ref[rs, 2 * W:3 * W].T.astype(BF16)

    rows_i = pl.ds(pl.multiple_of(i * TB, TB), TB)
    q = _rope(qkv_ref[rows_i, 0:W], cos_ref[rows_i, :], sin_ref[rows_i, :])
    lane = lax.broadcasted_iota(jnp.int32, (TB, W), 1)
    lane8 = lax.broadcasted_iota(jnp.int32, (NB, W), 1)
    blk_id = lax.broadcasted_iota(jnp.int32, (NB, TB), 0)
    R = N_HEADS * TB
    key_loc = lax.broadcasted_iota(jnp.int32, (TB, R), 0)
    q_loc = lax.broadcasted_iota(jnp.int32, (TB, R), 1) & (TB - 1)
    kmean = kmean_s[...]
    q_t = q.T
    qt_hi, qt_lo = _split2(q_t)
    dim_id = lax.broadcasted_iota(jnp.int32, (W, TB), 0)
    scale = HEAD_DIM ** -0.5

    heads = [(lane >= h * HEAD_DIM) & (lane < (h + 1) * HEAD_DIM) for h in range(N_HEADS)]
    km_all = jnp.concatenate(
        [jnp.where((lane8 >= h * HEAD_DIM) & (lane8 < (h + 1) * HEAD_DIM), kmean, 0.0) for h in range(N_HEADS)],
        axis=0)
    km_hi, km_lo = _split2(km_all)
    gate_all = _dot(km_hi, qt_hi) + _dot(km_hi, qt_lo) + _dot(km_lo, qt_hi)
    qhs = []
    for h in range(N_HEADS):
        gate = gate_all[h * NB:(h + 1) * NB, :]
        cnt = jnp.zeros((NB, TB), F32)
        for jp in range(NB):
            row = gate[jp:jp + 1, :]
            beats = (row > gate) | ((row == gate) & (blk_id > jp))
            cnt = cnt + jnp.where(beats, jnp.where(jp < i, 1.0, 0.0), 0.0)
        selected = (cnt < float(MOBA_TOPK)) & (blk_id < i)
        bias_s[:, h * TB:(h + 1) * TB] = jnp.where(selected, 0.0, NEG_BIG)
        in_head = (dim_id >= h * HEAD_DIM) & (dim_id < (h + 1) * HEAD_DIM)
        qhs.append(jnp.where(in_head, q_t, 0.0) * scale)
    qs_t = jnp.concatenate(qhs, axis=1).astype(BF16)

    s = jnp.where(key_loc <= q_loc, _dot(k_s[rows_i, :], qs_t), NEG_BIG)
    m = jnp.max(s, axis=0, keepdims=True)
    p = jnp.exp(s - m)
    l = jnp.sum(p, axis=0, keepdims=True)
    acc_s[...] = _dot(vt_s[i], p.astype(BF16))

    def body(j, carry):
        m, l = carry
        rows = pl.ds(pl.multiple_of(j * TB, TB), TB)
        s = _dot(k_s[rows, :], qs_t) + bias_s[pl.ds(j, 1), :]
        m_new = jnp.maximum(m, jnp.max(s, axis=0, keepdims=True))
        alpha = jnp.exp(m - m_new)
        p = jnp.exp(s - m_new)
        l = alpha * l + jnp.sum(p, axis=0, keepdims=True)
        acc_s[...] = alpha * acc_s[...] + _dot(vt_s[j], p.astype(BF16))
        return m_new, l

    m, l = lax.fori_loop(0, i, body, (m, l))
    outs = acc_s[...] / l
    out = outs[:, 0:TB].T
    for h in range(1, N_HEADS):
        out = jnp.where(heads[h], outs[:, h * TB:(h + 1) * TB].T, out)
    o_ref[...] = out


def _moba(qkv, cos, sin):
    B, S, _ = qkv.shape
    W = GROUP_WIDTH
    nb = S // MOBA_BLOCK
    return pl.pallas_call(
        _moba_kernel,
        grid=(B, S // MOBA_BLOCK),
        in_specs=[pl.BlockSpec((None, S, 3 * W), lambda b, i: (b, 0, 0)),
                  pl.BlockSpec((S, W), lambda b, i: (0, 0)),
                  pl.BlockSpec((S, W), lambda b, i: (0, 0))],
        out_specs=pl.BlockSpec((None, MOBA_BLOCK, W), lambda b, i: (b, i, 0)),
        out_shape=jax.ShapeDtypeStruct((B, S, W), F32),
        scratch_shapes=[pltpu.VMEM((S, W), BF16), pltpu.VMEM((nb, W, MOBA_BLOCK), BF16),
                        pltpu.VMEM((nb, W), F32), pltpu.VMEM((W, N_HEADS * MOBA_BLOCK), F32),
                        pltpu.VMEM((nb, N_HEADS * MOBA_BLOCK), F32)],
        compiler_params=_cparams(2),
        name="moba",
    )(qkv, cos, sin)


def _outproj_kernel(ya_ref, yb_ref, yc_ref, yd_ref, gg_ref, w_ref, x_ref, o_ref):
    W = GROUP_WIDTH
    acc = x_ref[...]
    for g, y_ref in enumerate((ya_ref, yb_ref, yc_ref, yd_ref)):
        yn = _rmsnorm(y_ref[...], gg_ref[g:g + 1, :]).astype(BF16)
        acc = acc + _dot(yn, w_ref[g * W:(g + 1) * W, :].astype(BF16))
    o_ref[...] = acc


def _outproj(ys, gg, w_out, layer, x2d, tm=1024):
    T = x2d.shape[0]
    W = GROUP_WIDTH
    return pl.pallas_call(
        _outproj_kernel,
        grid=(T // tm,),
        in_specs=[pl.BlockSpec((tm, W), lambda i: (i, 0))] * 4
                 + [pl.BlockSpec((4, W), lambda i: (0, 0)),
                    pl.BlockSpec((None, 4 * W, D_MODEL), lambda i: (layer, 0, 0), pipeline_mode=pl.Buffered(1)),
                    pl.BlockSpec((tm, D_MODEL), lambda i: (i, 0))],
        out_specs=pl.BlockSpec((tm, D_MODEL), lambda i: (i, 0)),
        out_shape=jax.ShapeDtypeStruct((T, D_MODEL), F32),
        compiler_params=_cparams(1),
        name="outproj",
    )(*ys, gg, w_out, x2d)


def _memkv_kernel(m_ref, g_ref, w_ref, o_ref):
    mn = _rmsnorm(m_ref[...], g_ref[...]).astype(BF16)
    o_ref[...] = _dot(mn, w_ref[...].astype(BF16)).astype(BF16)


def _memkv(mem, g, wkv, layer):
    B, M, _ = mem.shape
    return pl.pallas_call(
        _memkv_kernel,
        grid=(B,),
        in_specs=[pl.BlockSpec((None, M, D_MODEL), lambda b: (b, 0, 0)),
                  pl.BlockSpec((1, D_MODEL), lambda b: (0, 0)),
                  pl.BlockSpec((None, D_MODEL, 2 * D_MODEL), lambda b: (layer, 0, 0),
                               pipeline_mode=pl.Buffered(1))],
        out_specs=pl.BlockSpec((None, M, 2 * D_MODEL), lambda b: (b, 0, 0)),
        out_shape=jax.ShapeDtypeStruct((B, M, 2 * D_MODEL), BF16),
        compiler_params=_cparams(1),
        name="memkv",
    )(mem, g, wkv)


def _xattn_kernel(x_ref, g_ref, wq_ref, kv_ref, wo_ref, o_ref):
    x = x_ref[...]
    h = _rmsnorm(x, g_ref[...]).astype(BF16)
    q = (_dot(h, wq_ref[...].astype(BF16)) * (XATTN_HEAD_DIM ** -0.5)).astype(BF16)
    acc = x
    for hd in range(XATTN_HEADS):
        cols = slice(hd * XATTN_HEAD_DIM, (hd + 1) * XATTN_HEAD_DIM)
        vcols = slice(D_MODEL + hd * XATTN_HEAD_DIM, D_MODEL + (hd + 1) * XATTN_HEAD_DIM)
        s = _dot_t(q[:, cols], kv_ref[:, cols])
        p = jnp.exp(s - jnp.max(s, axis=-1, keepdims=True))
        p = p / jnp.sum(p, axis=-1, keepdims=True)
        o = _dot(p.astype(BF16), kv_ref[:, vcols]).astype(BF16)
        acc = acc + _dot(o, wo_ref[cols, :].astype(BF16))
    o_ref[...] = acc


def _xattn(x3d, g, wq, kv, wo, layer, tm=1024):
    B, S, _ = x3d.shape
    M = kv.shape[1]
    weight = pl.BlockSpec((None, D_MODEL, D_MODEL), lambda b, i: (layer, 0, 0), pipeline_mode=pl.Buffered(1))
    return pl.pallas_call(
        _xattn_kernel,
        grid=(B, S // tm),
        in_specs=[pl.BlockSpec((None, tm, D_MODEL), lambda b, i: (b, i, 0)),
                  pl.BlockSpec((1, D_MODEL), lambda b, i: (0, 0)),
                  weight,
                  pl.BlockSpec((None, M, 2 * D_MODEL), lambda b, i: (b, 0, 0)),
                  weight],
        out_specs=pl.BlockSpec((None, tm, D_MODEL), lambda b, i: (b, i, 0)),
        out_shape=jax.ShapeDtypeStruct((B, S, D_MODEL), F32),
        compiler_params=_cparams(2),
        name="xattn",
    )(x3d, g, wq, kv, wo)


def _moe_kernel(x_ref, g_ref, wr_ref, br_ref, w1_ref, w3_ref, w2_ref, fg_ref, o_ref,
                t_s, comb_s, acc_s, *, final_norm):
    grp = pl.program_id(1)
    tm = x_ref.shape[0]

    @pl.when(grp == 0)
    def _():
        t = _rmsnorm(x_ref[...], g_ref[...])
        t_s[...] = t.astype(BF16)
        acc_s[...] = jnp.zeros_like(acc_s)
        t_hi, t_lo = _split2(t)
        logits = (_dot(t_hi, wr_ref[0]) + _dot(t_hi, wr_ref[1]) + _dot(t_lo, wr_ref[0])) + br_ref[...]
        lane = lax.broadcasted_iota(jnp.int32, (tm, LANES), 1)
        lane_f = lane.astype(F32)
        is_g = (lane >= MOE_EXPERTS) & (lane < MOE_EXPERTS + MOE_GROUPS)
        lg = jnp.where(is_g, logits, NEG_BIG)
        gmax = jnp.max(lg, axis=-1, keepdims=True)
        pg_top = 1.0 / jnp.sum(jnp.exp(lg - gmax), axis=-1, keepdims=True)
        g_lane = jnp.min(jnp.where(is_g & (lg == gmax), lane_f, 1e9), axis=-1, keepdims=True)
        g_idx = g_lane - float(MOE_EXPERTS)
        e_grp = jnp.floor(lane_f * (1.0 / MOE_EPG))
        in_grp = (lane < MOE_EXPERTS) & (e_grp == g_idx)
        le = jnp.where(in_grp, logits, NEG_BIG)
        e1 = jnp.max(le, axis=-1, keepdims=True)
        i1 = jnp.min(jnp.where(in_grp & (le == e1), lane_f, 1e9), axis=-1, keepdims=True)
        le2 = jnp.where(lane_f == i1, NEG_BIG, le)
        e2 = jnp.max(le2, axis=-1, keepdims=True)
        i2 = jnp.min(jnp.where(in_grp & (lane_f != i1) & (le2 == e2), lane_f, 1e9), axis=-1, keepdims=True)
        r2 = jnp.exp(e2 - e1)
        w_first = 1.0 / (1.0 + r2)
        w_second = r2 / (1.0 + r2)
        comb_s[...] = pg_top * (jnp.where(lane_f == i1, w_first, 0.0) + jnp.where(lane_f == i2, w_second, 0.0))

    t = t_s[...]
    comb = comb_s[...]
    lane = lax.broadcasted_iota(jnp.int32, (tm, LANES), 1)
    upd = jnp.zeros((tm, D_MODEL), F32)
    for e in range(MOE_EPG):
        n = grp * MOE_EPG + e
        c = jnp.sum(jnp.where(lane == n, comb, 0.0), axis=-1, keepdims=True)
        hid = _silu(_dot(t, w1_ref[e])) * _dot(t, w3_ref[e])
        upd = upd + _dot((hid * c).astype(BF16), w2_ref[e])
    acc_s[...] += upd

    @pl.when(grp == MOE_GROUPS - 1)
    def _():
        y = x_ref[...] + acc_s[...]
        if final_norm:
            y = _rmsnorm(y, fg_ref[...])
        o_ref[...] = y


def _moe(x2d, g, w_router, b_router, w1, w3, w2, layer, final_g, final_norm, tm=1024):
    T = x2d.shape[0]
    return pl.pallas_call(
        functools.partial(_moe_kernel, final_norm=final_norm),
        grid=(T // tm, MOE_GROUPS),
        in_specs=[pl.BlockSpec((tm, D_MODEL), lambda i, e: (i, 0)),
                  pl.BlockSpec((1, D_MODEL), lambda i, e: (0, 0)),
                  pl.BlockSpec((2, D_MODEL, LANES), lambda i, e: (0, 0, 0)),
                  pl.BlockSpec((1, LANES), lambda i, e: (0, 0)),
                  pl.BlockSpec((None, MOE_EPG, D_MODEL, MOE_FF), lambda i, e: (layer, e, 0, 0)),
                  pl.BlockSpec((None, MOE_EPG, D_MODEL, MOE_FF), lambda i, e: (layer, e, 0, 0)),
                  pl.BlockSpec((None, MOE_EPG, MOE_FF, D_MODEL), lambda i, e: (layer, e, 0, 0)),
                  pl.BlockSpec((1, D_MODEL), lambda i, e: (0, 0))],
        out_specs=pl.BlockSpec((tm, D_MODEL), lambda i, e: (i, 0)),
        out_shape=jax.ShapeDtypeStruct((T, D_MODEL), F32),
        scratch_shapes=[pltpu.VMEM((tm, D_MODEL), BF16), pltpu.VMEM((tm, LANES), F32),
                        pltpu.VMEM((tm, D_MODEL), F32)],
        compiler_params=_cparams(2),
        name="moe",
    )(x2d, g, w_router, b_router, w1, w3, w2, final_g)


def _pad_lanes(v, width=LANES):
    return jnp.pad(v, (0, width - v.shape[0]))[None, :]


def _block_diag(w):
    H, n, _ = w.shape
    eye = jnp.eye(H, dtype=w.dtype)
    return (eye[:, None, :, None] * w[:, :, None, :]).reshape(H * n, H * n)


def _rope_tables(S):
    half = HEAD_DIM // 2
    inv_freq = ROPE_THETA ** (-jnp.arange(half, dtype=F32) / half)
    ang = jnp.arange(S, dtype=F32)[:, None] * inv_freq[None, :]
    reps = GROUP_WIDTH // half
    return jnp.tile(jnp.cos(ang), (1, reps)), jnp.tile(jnp.sin(ang), (1, reps))


def kernel(x, mem, mix_norm_g, w_in, lru_conv_w, lru_conv_b, lru_wr, lru_br, lru_wi, lru_bi, lru_lambda, ssm_conv_w, ssm_conv_b, ssm_dt_bias, ssm_a_log, ssm_d, group_norm_g, w_out, xattn_norm_g, mem_norm_g, xattn_wq, xattn_wkv, xattn_wo, ffn_norm_g, router_group_w, router_group_b, router_expert_w, router_expert_b, expert_w1, expert_w3, expert_w2, final_norm_g):
    B, S, D = x.shape
    T = B * S
    depth = w_in.shape[0]
    W = GROUP_WIDTH
    cos, sin = _rope_tables(S)
    x2d = x.reshape(T, D)
    w1_bf, w3_bf, w2_bf = expert_w1.astype(BF16), expert_w3.astype(BF16), expert_w2.astype(BF16)
    for l in range(depth):
        lru_xg, sb_qkv, ssm_z, ssm_xbc, ssm_dt, mb_qkv = _inproj(x2d, mix_norm_g[l][None, :], w_in, l)

        w_bd = jnp.concatenate([_block_diag(lru_wr[l]), _block_diag(lru_wi[l])], axis=1).astype(BF16)
        b_ri = jnp.concatenate([lru_br[l], lru_bi[l]])[None, :]
        y_a = _lru(lru_xg.reshape(B, S, 2 * W), lru_conv_w[l], lru_conv_b[l][None, :], w_bd, b_ri,
                   lru_lambda[l][None, :])
        y_b = _sb_attention(sb_qkv.reshape(B, S, 3 * W))
        y_c = _ssd(ssm_z.reshape(B, S, W), ssm_xbc.reshape(B, S, 3 * W), ssm_dt.reshape(B, S, LANES),
                   ssm_conv_w[l], ssm_conv_b[l][None, :], _pad_lanes(ssm_dt_bias[l]), _pad_lanes(ssm_a_log[l]),
                   jnp.repeat(ssm_d[l], HEAD_DIM)[None, :])
        y_d = _moba(mb_qkv.reshape(B, S, 3 * W), cos, sin)
        x2d = _outproj([y.reshape(T, W) for y in (y_a, y_b, y_c, y_d)], group_norm_g[l].reshape(4, W),
                       w_out, l, x2d)

        kv = _memkv(mem, mem_norm_g[l][None, :], xattn_wkv, l)
        x2d = _xattn(x2d.reshape(B, S, D), xattn_norm_g[l][None, :], xattn_wq, kv, xattn_wo, l).reshape(T, D)

        w_r = jnp.pad(jnp.concatenate([router_expert_w[l], router_group_w[l]], axis=1),
                      ((0, 0), (0, LANES - MOE_EXPERTS - MOE_GROUPS)))
        w_r_hi = w_r.astype(BF16)
        w_r_lo = (w_r - w_r_hi.astype(F32)).astype(BF16)
        b_r = _pad_lanes(jnp.concatenate([router_expert_b[l], router_group_b[l]]))
        x2d = _moe(x2d, ffn_norm_g[l][None, :], jnp.stack([w_r_hi, w_r_lo]), b_r, w1_bf, w3_bf, w2_bf, l,
                   final_norm_g[None, :], final_norm=(l == depth - 1))
    return x2d.reshape(B, S, D)
```

```python
import functools
import math

import jax
import jax.numpy as jnp
from jax import lax
from jax.experimental import pallas as pl
from jax.experimental.pallas import tpu as pltpu

F32 = jnp.float32
BF16 = jnp.bfloat16

D_MODEL = 1024
GROUP_WIDTH = 256
HEAD_DIM = 64
N_HEADS = 4
NORM_EPS = 1e-6
CONV_WIDTH = 4
LRU_C = 8.0
SB_BLOCK = 128
SB_WINDOW_BLOCKS = 3
SB_CHAINS = 4
SSM_CHUNK = 128
SSM_STATE = 128
MOBA_BLOCK = 256
MOBA_TOPK = 3
ROPE_THETA = 10000.0
XATTN_HEADS = 4
XATTN_HEAD_DIM = 256
MEM_LEN = 256
MOE_GROUPS = 4
MOE_EPG = 4
MOE_EXPERTS = 16
MOE_FF = 256
LANES = 128
SUBLANES = 8
NEG_BIG = -1e30
SB_EXP_FLOOR = -104.0
IN_OUT_WIDTHS = (512, 768, 256, 768, LANES, 768)
IN_MAIN = 512 + 768 + 256 + 768
IN_OUT_DTYPES = (F32, BF16, F32, F32, F32, F32)
VMEM_LIMIT = 56 * 1024 * 1024


def _cparams(n_axes):
    return pltpu.CompilerParams(dimension_semantics=("arbitrary",) * n_axes,
                                vmem_limit_bytes=VMEM_LIMIT)


def _dot(a, b):
    return jnp.dot(a, b, preferred_element_type=F32)


def _dot_t(a, b):
    return lax.dot_general(a, b, (((1,), (1,)), ((), ())), preferred_element_type=F32)


def _dot_tl(a, b):
    return lax.dot_general(a, b, (((0,), (0,)), ((), ())), preferred_element_type=F32)


def _split2(x):
    hi = x.astype(BF16)
    lo = (x - hi.astype(F32)).astype(BF16)
    return hi, lo


def _split3(x):
    hi = x.astype(BF16)
    r = x - hi.astype(F32)
    mid = r.astype(BF16)
    lo = (r - mid.astype(F32)).astype(BF16)
    return hi, mid, lo


def _dot_wide_lhs(x, m_bf16, parts=3):
    pieces = _split3(x) if parts == 3 else _split2(x)
    out = _dot(pieces[0], m_bf16)
    for p in pieces[1:]:
        out = out + _dot(p, m_bf16)
    return out


def _rmsnorm(x, g):
    return x * lax.rsqrt(jnp.mean(x * x, axis=-1, keepdims=True) + NORM_EPS) * g


def _softplus(x):
    return jnp.maximum(x, 0.0) + jnp.log(1.0 + jnp.exp(-jnp.abs(x)))


def _sigmoid(x):
    return 1.0 / (1.0 + jnp.exp(-x))


def _silu(x):
    return x * _sigmoid(x)


def _gelu_tanh(x):
    return 0.5 * x * (1.0 + jnp.tanh(math.sqrt(2.0 / math.pi) * (x + 0.044715 * (x * x * x))))


def _shift_rows_down(x):
    rows = lax.broadcasted_iota(jnp.int32, x.shape, 0)
    return jnp.where(rows >= 1, pltpu.roll(x, 1, 0), 0.0)


def _causal_conv(x, w_ref, b_ref, cols):
    def taps(v, mask_rows):
        y = v * w_ref[CONV_WIDTH - 1:CONV_WIDTH, cols] + b_ref[:, cols]
        for s in range(1, CONV_WIDTH):
            vs = pltpu.roll(v, s, 0)
            if mask_rows is not None:
                vs = jnp.where(mask_rows >= s, vs, 0.0)
            y = y + vs * w_ref[CONV_WIDTH - 1 - s:CONV_WIDTH - s, cols]
        return y

    head = x[0:SUBLANES, :]
    y_head = taps(head, lax.broadcasted_iota(jnp.int32, head.shape, 0))
    return jnp.concatenate([y_head, taps(x, None)[SUBLANES:, :]], axis=0)


def _phase_conv(slab_ref, w_ref, b_ref, cols):
    P = SUBLANES
    nt = slab_ref.shape[0] // P
    x = [slab_ref[pl.ds(p, nt, stride=P), :] for p in range(P)]
    prev = {p: _shift_rows_down(x[p]) for p in range(P - CONV_WIDTH + 1, P)}
    out = []
    for p in range(P):
        y = x[p] * w_ref[CONV_WIDTH - 1:CONV_WIDTH, cols] + b_ref[:, cols]
        for k in range(1, CONV_WIDTH):
            src = x[p - k] if p - k >= 0 else prev[p - k + P]
            y = y + src * w_ref[CONV_WIDTH - 1 - k:CONV_WIDTH - k, cols]
        out.append(y)
    return out


def _inproj_kernel(x_ref, g_ref, w_ref, *refs):
    o_refs, w_s = refs[:-1], refs[-1]

    @pl.when(pl.program_id(0) == 0)
    def _():
        for c0 in range(0, IN_MAIN, 2 * LANES):
            w_s[:, c0:c0 + 2 * LANES] = w_ref[:, c0:c0 + 2 * LANES].astype(BF16)
        dt_tile = w_ref[:, IN_MAIN:IN_MAIN + LANES]
        lane = lax.broadcasted_iota(jnp.int32, dt_tile.shape, 1)
        w_s[:, IN_MAIN:IN_MAIN + LANES] = jnp.where(lane < N_HEADS, dt_tile, 0.0).astype(BF16)
        w_s[:, IN_MAIN + LANES:] = w_ref[:, IN_MAIN + N_HEADS:].astype(BF16)

    h = _rmsnorm(x_ref[...], g_ref[...]).astype(BF16)
    off = 0
    for o_ref, width in zip(o_refs, IN_OUT_WIDTHS):
        o_ref[...] = _dot(h, w_s[:, off:off + width]).astype(o_ref.dtype)
        off += width


def _inproj(x2d, g, w_in, layer, tm=512):
    T = x2d.shape[0]
    n_in = w_in.shape[-1]
    return pl.pallas_call(
        _inproj_kernel,
        grid=(T // tm,),
        in_specs=[pl.BlockSpec((tm, D_MODEL), lambda i: (i, 0)),
                  pl.BlockSpec((1, D_MODEL), lambda i: (0, 0)),
                  pl.BlockSpec((None, D_MODEL, n_in), lambda i: (layer, 0, 0), pipeline_mode=pl.Buffered(1))],
        out_specs=[pl.BlockSpec((tm, w), lambda i: (i, 0)) for w in IN_OUT_WIDTHS],
        out_shape=[jax.ShapeDtypeStruct((T, w), dt) for w, dt in zip(IN_OUT_WIDTHS, IN_OUT_DTYPES)],
        scratch_shapes=[pltpu.VMEM((D_MODEL, sum(IN_OUT_WIDTHS)), BF16)],
        compiler_params=_cparams(1),
        name="inproj",
    )(x2d, g, w_in)


def _scan_rows(a, u):
    n = a.shape[0]
    rows = lax.broadcasted_iota(jnp.int32, a.shape, 0)
    shift = 1
    while shift < n:
        if shift < SUBLANES:
            keep = rows >= shift
            a_s = jnp.where(keep, pltpu.roll(a, shift, 0), 1.0)
            u_s = jnp.where(keep, pltpu.roll(u, shift, 0), 0.0)
            u = a * u_s + u
            a = a * a_s
        else:
            u = jnp.concatenate([u[:shift], a[shift:] * u[:n - shift] + u[shift:]], axis=0)
            a = jnp.concatenate([a[:shift], a[shift:] * a[:n - shift]], axis=0)
        shift *= 2
    return u


def _lru_kernel(xg_ref, cw_ref, cb_ref, wbd_ref, bri_ref, lam_ref, o_ref, in_s, out_s):
    S = xg_ref.shape[0]
    W = GROUP_WIDTH
    P = SUBLANES
    NT = S // P
    for s in range(2 * W // LANES):
        in_s[s] = xg_ref[:, s * LANES:(s + 1) * LANES]
    log_sig_lam = -_softplus(-lam_ref[...])

    for s in range(W // LANES):
        cols = slice(s * LANES, (s + 1) * LANES)
        xc_all = jnp.concatenate(_phase_conv(in_s.at[s], cw_ref, cb_ref, cols), axis=0)
        w_slab = jnp.concatenate([wbd_ref[cols, cols], wbd_ref[cols, W + s * LANES:W + (s + 1) * LANES]], axis=1)
        ri = _dot(xc_all.astype(BF16), w_slab)
        r = _sigmoid(ri[:, 0:LANES] + bri_ref[:, cols])
        i = _sigmoid(ri[:, LANES:2 * LANES] + bri_ref[:, W + s * LANES:W + (s + 1) * LANES])
        log_a = (LRU_C * r) * log_sig_lam[:, cols]
        a = jnp.exp(log_a)
        u = jnp.sqrt(1.0 - jnp.exp(2.0 * log_a)) * (i * xc_all)
        loc = [u[0:NT]]
        dec = [a[0:NT]]
        for p in range(1, P):
            ap = a[p * NT:(p + 1) * NT]
            loc.append(ap * loc[-1] + u[p * NT:(p + 1) * NT])
            dec.append(ap * dec[-1])
        carry = _shift_rows_down(_scan_rows(dec[-1], loc[-1]))
        for p in range(P):
            h = loc[p] + dec[p] * carry
            out_s[s, pl.ds(p, NT, stride=P), :] = h * _gelu_tanh(in_s[W // LANES + s, pl.ds(p, NT, stride=P), :])
    for s in range(W // LANES):
        o_ref[:, s * LANES:(s + 1) * LANES] = out_s[s]


def _lru(xg, conv_w, conv_b, w_bd, b_ri, lam):
    B, S, _ = xg.shape
    W = GROUP_WIDTH
    full = lambda shape: pl.BlockSpec(shape, lambda b: (0,) * len(shape))
    return pl.pallas_call(
        _lru_kernel,
        grid=(B,),
        in_specs=[pl.BlockSpec((None, S, 2 * W), lambda b: (b, 0, 0)),
                  full((CONV_WIDTH, W)), full((1, W)), full((W, 2 * W)), full((1, 2 * W)), full((1, W))],
        out_specs=pl.BlockSpec((None, S, W), lambda b: (b, 0, 0)),
        out_shape=jax.ShapeDtypeStruct((B, S, W), F32),
        scratch_shapes=[pltpu.VMEM((2 * W // LANES, S, LANES), F32), pltpu.VMEM((W // LANES, S, LANES), F32)],
        compiler_params=_cparams(1),
        name="rglru",
    )(xg, conv_w, conv_b, w_bd, b_ri, lam)


def _sb_kernel(qkv_ref, o_ref, kt_s, v_s, acc_s, later_s):
    i = pl.program_id(1)
    W = GROUP_WIDTH
    TB = SB_BLOCK
    R = N_HEADS * TB

    NW = SB_WINDOW_BLOCKS
    KW = NW * TB
    PAD = (NW - 1) * TB
    S = qkv_ref.shape[0]

    @pl.when(i == 0)
    def _():
        kt_s[:, 0:PAD] = jnp.zeros((W, PAD), BF16)
        for r0 in range(0, S, W):
            kt_s[:, PAD + r0:PAD + r0 + W] = qkv_ref[r0:r0 + W, W:2 * W].astype(F32).T.astype(BF16)
        v_s[0:PAD, :] = jnp.zeros((PAD, W), BF16)
        v_s[PAD:PAD + S, :] = qkv_ref[:, 2 * W:3 * W].astype(BF16)

    lane = lax.broadcasted_iota(jnp.int32, (TB, W), 1)
    heads = [(lane >= h * HEAD_DIM) & (lane < (h + 1) * HEAD_DIM) for h in range(N_HEADS)]
    r_loc = lax.broadcasted_iota(jnp.int32, (R, KW), 0) & (TB - 1)
    c_loc = lax.broadcasted_iota(jnp.int32, (R, KW), 1)
    ur = lax.broadcasted_iota(jnp.int32, (TB, 2 * TB), 0)
    uc = lax.broadcasted_iota(jnp.int32, (TB, 2 * TB), 1)
    tri_ones = jnp.where((ur > uc) | (uc >= TB), 1.0, 0.0).astype(BF16)

    n_chains = o_ref.shape[0]
    blocks = [i + c * (S // TB // n_chains) for c in range(n_chains)]
    qss = []
    for blk in blocks:
        q = qkv_ref[pl.ds(pl.multiple_of(blk * TB, TB), TB), 0:W] * (HEAD_DIM ** -0.5)
        qss.append(jnp.concatenate([jnp.where(hm, q, 0.0) for hm in heads], axis=0).astype(BF16))

    acc_s[...] = jnp.zeros_like(acc_s)
    later_s[...] = jnp.zeros_like(later_s)

    def window(c, n):
        blk = blocks[c]
        first_key = (blk - n * NW - (NW - 1)) * TB
        rows = pl.ds(pl.multiple_of(jnp.maximum(first_key + PAD, 0), TB), KW)
        z = _dot(qss[c], kt_s[:, rows])
        key_abs = first_key + c_loc
        live = (key_abs < blk * TB + r_loc) & (key_abs >= 0)
        sp = _softplus(z)
        lf = jnp.where(live, -sp, 0.0)
        lf16 = lf.astype(BF16)
        order = list(range(NW - 1, -1, -1))
        stacked = jnp.concatenate([lf16[:, b * TB:(b + 1) * TB] for b in order], axis=0)
        cs_all = _dot(stacked, tri_ones)
        offset = later_s[c]
        after = [None] * NW
        for pos, b in enumerate(order):
            cs = cs_all[pos * R:(pos + 1) * R, :]
            after[b] = cs[:, 0:TB] + offset
            offset = offset + cs[:, TB:2 * TB]
        w = jnp.where(live, jnp.exp((z - sp) + jnp.concatenate(after, axis=1)), 0.0)
        acc_s[c] += _dot(w.astype(BF16), v_s[rows, :])
        later_s[c] = offset
        return jnp.where((n + 1) * NW <= blk, jnp.max(offset), SB_EXP_FLOOR)

    def cond(carry):
        return carry[1] > SB_EXP_FLOOR

    def body(carry):
        n = carry[0]
        later_max = window(0, n)
        for c in range(1, n_chains):
            later_max = jnp.maximum(later_max, window(c, n))
        return n + 1, later_max

    lax.while_loop(cond, body, (jnp.int32(0), jnp.float32(0.0)))
    for c in range(n_chains):
        out = acc_s[c, 0:TB, :]
        for h in range(1, N_HEADS):
            out = jnp.where(heads[h], acc_s[c, h * TB:(h + 1) * TB, :], out)
        o_ref[c] = out


def _sb_attention(qkv):
    B, S, _ = qkv.shape
    W = GROUP_WIDTH
    pad = (SB_WINDOW_BLOCKS - 1) * SB_BLOCK
    nc = SB_CHAINS
    rows = N_HEADS * SB_BLOCK
    out = pl.pallas_call(
        _sb_kernel,
        grid=(B, S // SB_BLOCK // nc),
        in_specs=[pl.BlockSpec((None, S, 3 * W), lambda b, i: (b, 0, 0))],
        out_specs=pl.BlockSpec((None, nc, SB_BLOCK, W), lambda b, i: (b, 0, i, 0)),
        out_shape=jax.ShapeDtypeStruct((B, nc, S // nc, W), F32),
        scratch_shapes=[pltpu.VMEM((W, S + pad), BF16), pltpu.VMEM((S + pad, W), BF16),
                        pltpu.VMEM((nc, rows, W), F32), pltpu.VMEM((nc, rows, SB_BLOCK), F32)],
        compiler_params=_cparams(2),
        name="stickbreak",
    )(qkv)
    return out.reshape(B, S, W)


def _ssd_kernel(z_ref, xbc_ref, dt_ref, cw_ref, cb_ref, dtb_ref, alog_ref, dskip_ref, o_ref, xbc_s):
    S = z_ref.shape[0]
    W = GROUP_WIDTH
    L = SSM_CHUNK
    for s in range(3 * W // LANES):
        cols = slice(s * LANES, (s + 1) * LANES)
        xbc_s[s] = _silu(_causal_conv(xbc_ref[:, cols], cw_ref, cb_ref, cols))
    a_row = -jnp.exp(alog_ref[...])

    r_i = lax.broadcasted_iota(jnp.int32, (L, L), 0)
    c_i = lax.broadcasted_iota(jnp.int32, (L, L), 1)
    tri_incl = jnp.where(c_i <= r_i, 1.0, 0.0).astype(BF16)
    lower = c_i <= r_i
    e_r = lax.broadcasted_iota(jnp.int32, (LANES, W), 0)
    e_c = lax.broadcasted_iota(jnp.int32, (LANES, W), 1)
    expand = jnp.where((e_c >= e_r * HEAD_DIM) & (e_c < (e_r + 1) * HEAD_DIM), 1.0, 0.0).astype(BF16)
    lane_l = lax.broadcasted_iota(jnp.int32, (L, LANES), 1)

    def chunk(c, states):
        rows = slice(c * L, (c + 1) * L)
        xs = jnp.concatenate([xbc_s[0, rows, :], xbc_s[1, rows, :]], axis=1)
        dt = _softplus(dt_ref[rows, :] + dtb_ref[...])
        a_dt = dt * a_row
        cs_col = _dot_wide_lhs_rhs(tri_incl, a_dt)
        cs_row = cs_col.T
        cs_full = _dot_wide_lhs(cs_col, expand)
        dt_full = _dot_wide_lhs(dt, expand)
        xd = xs * dt_full
        tot = cs_full[L - 1:L, :]
        xdec = (xd * jnp.exp(tot - cs_full)).astype(BF16)
        xd16 = xd.astype(BF16)
        ys = []
        new_states = []
        for g in range(2):
            gl = slice(g * LANES, (g + 1) * LANES)
            bm = xbc_s[2 + g, rows, :].astype(BF16)
            cm = xbc_s[4 + g, rows, :].astype(BF16)
            cb = _dot_t(cm, bm)
            prev = states[g]
            y_off = _dot(cm, prev.astype(BF16)) * jnp.exp(cs_full[:, gl])
            y_g = y_off
            for hh in range(2):
                h = 2 * g + hh
                seg = jnp.where(lower, cs_col[:, h:h + 1] - cs_row[h:h + 1, :], -jnp.inf)
                y_h = _dot((cb * jnp.exp(seg)).astype(BF16), xd16[:, gl])
                in_head = (lane_l >= hh * HEAD_DIM) & (lane_l < (hh + 1) * HEAD_DIM)
                y_g = y_g + jnp.where(in_head, y_h, 0.0)
            new_states.append(prev * jnp.exp(tot[:, gl]) + _dot_tl(bm, xdec[:, gl]))
            ys.append(y_g)
        y = jnp.concatenate(ys, axis=1) + dskip_ref[...] * xs
        o_ref[rows, :] = y * _silu(z_ref[rows, :])
        return new_states

    states = [jnp.zeros((SSM_STATE, LANES), F32) for _ in range(2)]
    for c in range(S // L):
        states = chunk(c, states)


def _dot_wide_lhs_rhs(m_bf16, x):
    hi, mid, lo = _split3(x)
    return _dot(m_bf16, hi) + _dot(m_bf16, mid) + _dot(m_bf16, lo)


def _ssd(z, xbc, dt, conv_w, conv_b, dt_bias, a_log, d_skip):
    B, S, _ = z.shape
    W = GROUP_WIDTH
    full = lambda shape: pl.BlockSpec(shape, lambda b: (0,) * len(shape))
    return pl.pallas_call(
        _ssd_kernel,
        grid=(B,),
        in_specs=[pl.BlockSpec((None, S, W), lambda b: (b, 0, 0)),
                  pl.BlockSpec((None, S, 3 * W), lambda b: (b, 0, 0)),
                  pl.BlockSpec((None, S, LANES), lambda b: (b, 0, 0)),
                  full((CONV_WIDTH, 3 * W)), full((1, 3 * W)), full((1, LANES)), full((1, LANES)),
                  full((1, W))],
        out_specs=pl.BlockSpec((None, S, W), lambda b: (b, 0, 0)),
        out_shape=jax.ShapeDtypeStruct((B, S, W), F32),
        scratch_shapes=[pltpu.VMEM((3 * W // LANES, S, LANES), F32)],
        compiler_params=_cparams(1),
        name="ssd",
    )(z, xbc, dt, conv_w, conv_b, dt_bias, a_log, d_skip)


def _rope(x, cos, sin):
    lane = lax.broadcasted_iota(jnp.int32, (x.shape[0], LANES), 1)
    first_half = (lane % HEAD_DIM) < (HEAD_DIM // 2)
    halves = []
    for p in range(x.shape[1] // LANES):
        xp = x[:, p * LANES:(p + 1) * LANES]
        fwd = pltpu.roll(xp, HEAD_DIM // 2, 1)
        bwd = pltpu.roll(xp, LANES - HEAD_DIM // 2, 1)
        halves.append(jnp.where(first_half, -bwd, fwd))
    rot = jnp.concatenate(halves, axis=1)
    return x * cos + rot * sin


def _moba_kernel(qkv_ref, cos_ref, sin_ref, o_ref, k_s, vt_s, kmean_s, acc_s, bias_s):
    i = pl.program_id(1)
    W = GROUP_WIDTH
    TB = MOBA_BLOCK
    S = qkv_ref.shape[0]
    NB = S // TB

    @pl.when(i == 0)
    def _():
        for blk in range(NB):
            rs = slice(blk * TB, (blk + 1) * TB)
            kb = _rope(qkv_ref[rs, W:2 * W], cos_ref[rs, :], sin_ref[rs, :])
            k_s[rs, :] = kb.astype(BF16)
            kmean_s[blk:blk + 1, :] = jnp.mean(kb, axis=0, keepdims=True)
            vt_s[blk] = qkv_ref[rs, 2 * W:3 * W].T.astype(BF16)

    rows_i = pl.ds(pl.multiple_of(i * TB, TB), TB)
    q = _rope(qkv_ref[rows_i, 0:W], cos_ref[rows_i, :], sin_ref[rows_i, :])
    lane = lax.broadcasted_iota(jnp.int32, (TB, W), 1)
    lane8 = lax.broadcasted_iota(jnp.int32, (NB, W), 1)
    blk_id = lax.broadcasted_iota(jnp.int32, (NB, TB), 0)
    R = N_HEADS * TB
    key_loc = lax.broadcasted_iota(jnp.int32, (TB, R), 0)
    q_loc = lax.broadcasted_iota(jnp.int32, (TB, R), 1) & (TB - 1)
    kmean = kmean_s[...]
    q_t = q.T
    qt_hi, qt_lo = _split2(q_t)
    dim_id = lax.broadcasted_iota(jnp.int32, (W, TB), 0)
    scale = HEAD_DIM ** -0.5

    heads = [(lane >= h * HEAD_DIM) & (lane < (h + 1) * HEAD_DIM) for h in range(N_HEADS)]
    km_all = jnp.concatenate(
        [jnp.where((lane8 >= h * HEAD_DIM) & (lane8 < (h + 1) * HEAD_DIM), kmean, 0.0) for h in range(N_HEADS)],
        axis=0)
    km_hi, km_lo = _split2(km_all)
    gate_all = _dot(km_hi, qt_hi) + _dot(km_hi, qt_lo) + _dot(km_lo, qt_hi)
    qhs = []
    for h in range(N_HEADS):
        gate = gate_all[h * NB:(h + 1) * NB, :]
        cnt = jnp.zeros((NB, TB), F32)
        for jp in range(NB):
            row = gate[jp:jp + 1, :]
            beats = (row > gate) | ((row == gate) & (blk_id > jp))
            cnt = cnt + jnp.where(beats, jnp.where(jp < i, 1.0, 0.0), 0.0)
        selected = (cnt < float(MOBA_TOPK)) & (blk_id < i)
        bias_s[:, h * TB:(h + 1) * TB] = jnp.where(selected, 0.0, NEG_BIG)
        in_head = (dim_id >= h * HEAD_DIM) & (dim_id < (h + 1) * HEAD_DIM)
        qhs.append(jnp.where(in_head, q_t, 0.0) * scale)
    qs_t = jnp.concatenate(qhs, axis=1).astype(BF16)

    s = jnp.where(key_loc <= q_loc, _dot(k_s[rows_i, :], qs_t), NEG_BIG)
    m = jnp.max(s, axis=0, keepdims=True)
    p = jnp.exp(s - m)
    l = jnp.sum(p, axis=0, keepdims=True)
    acc_s[...] = _dot(vt_s[i], p.astype(BF16))

    def body(j, carry):
        m, l = carry
        rows = pl.ds(pl.multiple_of(j * TB, TB), TB)
        s = _dot(k_s[rows, :], qs_t) + bias_s[pl.ds(j, 1), :]
        m_new = jnp.maximum(m, jnp.max(s, axis=0, keepdims=True))
        alpha = jnp.exp(m - m_new)
        p = jnp.exp(s - m_new)
        l = alpha * l + jnp.sum(p, axis=0, keepdims=True)
        acc_s[...] = alpha * acc_s[...] + _dot(vt_s[j], p.astype(BF16))
        return m_new, l

    m, l = lax.fori_loop(0, i, body, (m, l))
    outs = acc_s[...] / l
    out = outs[:, 0:TB].T
    for h in range(1, N_HEADS):
        out = jnp.where(heads[h], outs[:, h * TB:(h + 1) * TB].T, out)
    o_ref[...] = out


def _moba(qkv, cos, sin):
    B, S, _ = qkv.shape
    W = GROUP_WIDTH
    nb = S // MOBA_BLOCK
    return pl.pallas_call(
        _moba_kernel,
        grid=(B, S // MOBA_BLOCK),
        in_specs=[pl.BlockSpec((None, S, 3 * W), lambda b, i: (b, 0, 0)),
                  pl.BlockSpec((S, W), lambda b, i: (0, 0)),
                  pl.BlockSpec((S, W), lambda b, i: (0, 0))],
        out_specs=pl.BlockSpec((None, MOBA_BLOCK, W), lambda b, i: (b, i, 0)),
        out_shape=jax.ShapeDtypeStruct((B, S, W), F32),
        scratch_shapes=[pltpu.VMEM((S, W), BF16), pltpu.VMEM((nb, W, MOBA_BLOCK), BF16),
                        pltpu.VMEM((nb, W), F32), pltpu.VMEM((W, N_HEADS * MOBA_BLOCK), F32),
                        pltpu.VMEM((nb, N_HEADS * MOBA_BLOCK), F32)],
        compiler_params=_cparams(2),
        name="moba",
    )(qkv, cos, sin)


def _outproj_kernel(ya_ref, yb_ref, yc_ref, yd_ref, gg_ref, w_ref, x_ref, o_ref):
    W = GROUP_WIDTH
    acc = x_ref[...]
    for g, y_ref in enumerate((ya_ref, yb_ref, yc_ref, yd_ref)):
        yn = _rmsnorm(y_ref[...], gg_ref[g:g + 1, :]).astype(BF16)
        acc = acc + _dot(yn, w_ref[g * W:(g + 1) * W, :].astype(BF16))
    o_ref[...] = acc


def _outproj(ys, gg, w_out, layer, x2d, tm=1024):
    T = x2d.shape[0]
    W = GROUP_WIDTH
    return pl.pallas_call(
        _outproj_kernel,
        grid=(T // tm,),
        in_specs=[pl.BlockSpec((tm, W), lambda i: (i, 0))] * 4
                 + [pl.BlockSpec((4, W), lambda i: (0, 0)),
                    pl.BlockSpec((None, 4 * W, D_MODEL), lambda i: (layer, 0, 0), pipeline_mode=pl.Buffered(1)),
                    pl.BlockSpec((tm, D_MODEL), lambda i: (i, 0))],
        out_specs=pl.BlockSpec((tm, D_MODEL), lambda i: (i, 0)),
        out_shape=jax.ShapeDtypeStruct((T, D_MODEL), F32),
        compiler_params=_cparams(1),
        name="outproj",
    )(*ys, gg, w_out, x2d)


def _memkv_kernel(m_ref, g_ref, w_ref, o_ref):
    mn = _rmsnorm(m_ref[...], g_ref[...]).astype(BF16)
    o_ref[...] = _dot(mn, w_ref[...].astype(BF16)).astype(BF16)


def _memkv(mem, g, wkv, layer):
    B, M, _ = mem.shape
    return pl.pallas_call(
        _memkv_kernel,
        grid=(B,),
        in_specs=[pl.BlockSpec((None, M, D_MODEL), lambda b: (b, 0, 0)),
                  pl.BlockSpec((1, D_MODEL), lambda b: (0, 0)),
                  pl.BlockSpec((None, D_MODEL, 2 * D_MODEL), lambda b: (layer, 0, 0),
                               pipeline_mode=pl.Buffered(1))],
        out_specs=pl.BlockSpec((None, M, 2 * D_MODEL), lambda b: (b, 0, 0)),
        out_shape=jax.ShapeDtypeStruct((B, M, 2 * D_MODEL), BF16),
        compiler_params=_cparams(1),
        name="memkv",
    )(mem, g, wkv)


def _xattn_kernel(x_ref, g_ref, wq_ref, kv_ref, wo_ref, o_ref):
    x = x_ref[...]
    h = _rmsnorm(x, g_ref[...]).astype(BF16)
    q = (_dot(h, wq_ref[...].astype(BF16)) * (XATTN_HEAD_DIM ** -0.5)).astype(BF16)
    acc = x
    for hd in range(XATTN_HEADS):
        cols = slice(hd * XATTN_HEAD_DIM, (hd + 1) * XATTN_HEAD_DIM)
        vcols = slice(D_MODEL + hd * XATTN_HEAD_DIM, D_MODEL + (hd + 1) * XATTN_HEAD_DIM)
        s = _dot_t(q[:, cols], kv_ref[:, cols])
        p = jnp.exp(s - jnp.max(s, axis=-1, keepdims=True))
        p = p / jnp.sum(p, axis=-1, keepdims=True)
        o = _dot(p.astype(BF16), kv_ref[:, vcols]).astype(BF16)
        acc = acc + _dot(o, wo_ref[cols, :].astype(BF16))
    o_ref[...] = acc


def _xattn(x3d, g, wq, kv, wo, layer, tm=1024):
    B, S, _ = x3d.shape
    M = kv.shape[1]
    weight = pl.BlockSpec((None, D_MODEL, D_MODEL), lambda b, i: (layer, 0, 0), pipeline_mode=pl.Buffered(1))
    return pl.pallas_call(
        _xattn_kernel,
        grid=(B, S // tm),
        in_specs=[pl.BlockSpec((None, tm, D_MODEL), lambda b, i: (b, i, 0)),
                  pl.BlockSpec((1, D_MODEL), lambda b, i: (0, 0)),
                  weight,
                  pl.BlockSpec((None, M, 2 * D_MODEL), lambda b, i: (b, 0, 0)),
                  weight],
        out_specs=pl.BlockSpec((None, tm, D_MODEL), lambda b, i: (b, i, 0)),
        out_shape=jax.ShapeDtypeStruct((B, S, D_MODEL), F32),
        compiler_params=_cparams(2),
        name="xattn",
    )(x3d, g, wq, kv, wo)


def _moe_kernel(x_ref, g_ref, wr_ref, br_ref, w1_ref, w3_ref, w2_ref, fg_ref, o_ref,
                t_s, comb_s, acc_s, xkeep_s, *, final_norm):
    tile = pl.program_id(0)
    grp = pl.program_id(1)
    tm = x_ref.shape[0]
    slot = lax.bitwise_and(tile, 1)

    def route(dst):
        t = _rmsnorm(x_ref[...], g_ref[...])
        t_s[dst] = t.astype(BF16)
        t_hi, t_lo = _split2(t)
        logits = (_dot(t_hi, wr_ref[0]) + _dot(t_hi, wr_ref[1]) + _dot(t_lo, wr_ref[0])) + br_ref[...]
        lt = logits.T
        gsl = SUBLANES * (MOE_EXPERTS // SUBLANES)
        g_row = lax.broadcasted_iota(jnp.int32, (SUBLANES, tm), 0).astype(F32)
        lg = jnp.where(g_row < float(MOE_GROUPS), lt[gsl:gsl + SUBLANES, :], NEG_BIG)
        gmax = jnp.max(lg, axis=0, keepdims=True)
        pg_top = 1.0 / jnp.sum(jnp.exp(lg - gmax), axis=0, keepdims=True)
        g_idx = jnp.min(jnp.where(lg == gmax, g_row, 1e9), axis=0, keepdims=True)
        e_row = lax.broadcasted_iota(jnp.int32, (MOE_EXPERTS, tm), 0).astype(F32)
        in_grp = jnp.floor(e_row * (1.0 / MOE_EPG)) == g_idx
        le = jnp.where(in_grp, lt[0:MOE_EXPERTS, :], NEG_BIG)
        e1 = jnp.max(le, axis=0, keepdims=True)
        i1 = jnp.min(jnp.where(in_grp & (le == e1), e_row, 1e9), axis=0, keepdims=True)
        le2 = jnp.where(e_row == i1, NEG_BIG, le)
        e2 = jnp.max(le2, axis=0, keepdims=True)
        i2 = jnp.min(jnp.where(in_grp & (e_row != i1) & (le2 == e2), e_row, 1e9), axis=0, keepdims=True)
        r2 = jnp.exp(e2 - e1)
        w_first = 1.0 / (1.0 + r2)
        w_second = r2 / (1.0 + r2)
        comb_t = pg_top * (jnp.where(e_row == i1, w_first, 0.0) + jnp.where(e_row == i2, w_second, 0.0))
        comb_s[dst] = jnp.concatenate([comb_t, jnp.zeros((LANES - MOE_EXPERTS, tm), F32)], axis=0).T

    def experts():
        t = t_s[slot]
        comb = comb_s[slot]
        lane = lax.broadcasted_iota(jnp.int32, (tm, LANES), 1)
        upd = jnp.zeros((tm, D_MODEL), F32)
        for e in range(MOE_EPG):
            n = grp * MOE_EPG + e
            c = jnp.sum(jnp.where(lane == n, comb, 0.0), axis=-1, keepdims=True)
            hid = _silu(_dot(t, w1_ref[e])) * _dot(t, w3_ref[e])
            upd = upd + _dot((hid * c).astype(BF16), w2_ref[e])
        return upd

    @pl.when((tile == 0) & (grp == 0))
    def _():
        route(0)

    @pl.when(grp == 0)
    def _():
        xkeep_s[...] = x_ref[...]
        acc_s[...] = experts()

    @pl.when((grp > 0) & (grp < MOE_GROUPS - 1))
    def _():
        acc_s[...] += experts()

    @pl.when(grp == MOE_GROUPS - 1)
    def _():
        y = xkeep_s[...] + (acc_s[...] + experts())
        route(1 - slot)
        if final_norm:
            y = _rmsnorm(y, fg_ref[...])
        o_ref[...] = y


def _moe(x2d, g, w_router, b_router, w1, w3, w2, layer, final_g, final_norm, tm=1024):
    T = x2d.shape[0]
    n_tiles = T // tm

    def x_window(i, e):
        return jnp.minimum(i + e // (MOE_GROUPS - 1), n_tiles - 1), 0

    return pl.pallas_call(
        functools.partial(_moe_kernel, final_norm=final_norm),
        grid=(n_tiles, MOE_GROUPS),
        in_specs=[pl.BlockSpec((tm, D_MODEL), x_window),
                  pl.BlockSpec((1, D_MODEL), lambda i, e: (0, 0)),
                  pl.BlockSpec((2, D_MODEL, LANES), lambda i, e: (0, 0, 0)),
                  pl.BlockSpec((1, LANES), lambda i, e: (0, 0)),
                  pl.BlockSpec((None, MOE_EPG, D_MODEL, MOE_FF), lambda i, e: (layer, e, 0, 0)),
                  pl.BlockSpec((None, MOE_EPG, D_MODEL, MOE_FF), lambda i, e: (layer, e, 0, 0)),
                  pl.BlockSpec((None, MOE_EPG, MOE_FF, D_MODEL), lambda i, e: (layer, e, 0, 0)),
                  pl.BlockSpec((1, D_MODEL), lambda i, e: (0, 0))],
        out_specs=pl.BlockSpec((tm, D_MODEL), lambda i, e: (i, 0)),
        out_shape=jax.ShapeDtypeStruct((T, D_MODEL), F32),
        scratch_shapes=[pltpu.VMEM((2, tm, D_MODEL), BF16), pltpu.VMEM((2, tm, LANES), F32),
                        pltpu.VMEM((tm, D_MODEL), F32), pltpu.VMEM((tm, D_MODEL), F32)],
        compiler_params=_cparams(2),
        name="moe",
    )(x2d, g, w_router, b_router, w1, w3, w2, final_g)


def _pad_lanes(v, width=LANES):
    return jnp.pad(v, (0, width - v.shape[0]))[None, :]


def _block_diag(w):
    H, n, _ = w.shape
    eye = jnp.eye(H, dtype=w.dtype)
    return (eye[:, None, :, None] * w[:, :, None, :]).reshape(H * n, H * n)


def _rope_tables(S):
    half = HEAD_DIM // 2
    inv_freq = ROPE_THETA ** (-jnp.arange(half, dtype=F32) / half)
    ang = jnp.arange(S, dtype=F32)[:, None] * inv_freq[None, :]
    reps = GROUP_WIDTH // half
    return jnp.tile(jnp.cos(ang), (1, reps)), jnp.tile(jnp.sin(ang), (1, reps))


def kernel(x, mem, mix_norm_g, w_in, lru_conv_w, lru_conv_b, lru_wr, lru_br, lru_wi, lru_bi, lru_lambda, ssm_conv_w, ssm_conv_b, ssm_dt_bias, ssm_a_log, ssm_d, group_norm_g, w_out, xattn_norm_g, mem_norm_g, xattn_wq, xattn_wkv, xattn_wo, ffn_norm_g, router_group_w, router_group_b, router_expert_w, router_expert_b, expert_w1, expert_w3, expert_w2, final_norm_g):
    B, S, D = x.shape
    T = B * S
    depth = w_in.shape[0]
    W = GROUP_WIDTH
    cos, sin = _rope_tables(S)
    x2d = x.reshape(T, D)
    w1_bf, w3_bf, w2_bf = expert_w1.astype(BF16), expert_w3.astype(BF16), expert_w2.astype(BF16)
    for l in range(depth):
        lru_xg, sb_qkv, ssm_z, ssm_xbc, ssm_dt, mb_qkv = _inproj(x2d, mix_norm_g[l][None, :], w_in, l)

        w_bd = jnp.concatenate([_block_diag(lru_wr[l]), _block_diag(lru_wi[l])], axis=1).astype(BF16)
        b_ri = jnp.concatenate([lru_br[l], lru_bi[l]])[None, :]
        y_a = _lru(lru_xg.reshape(B, S, 2 * W), lru_conv_w[l], lru_conv_b[l][None, :], w_bd, b_ri,
                   lru_lambda[l][None, :])
        y_b = _sb_attention(sb_qkv.reshape(B, S, 3 * W))
        y_c = _ssd(ssm_z.reshape(B, S, W), ssm_xbc.reshape(B, S, 3 * W), ssm_dt.reshape(B, S, LANES),
                   ssm_conv_w[l], ssm_conv_b[l][None, :], _pad_lanes(ssm_dt_bias[l]), _pad_lanes(ssm_a_log[l]),
                   jnp.repeat(ssm_d[l], HEAD_DIM)[None, :])
        y_d = _moba(mb_qkv.reshape(B, S, 3 * W), cos, sin)
        x2d = _outproj([y.reshape(T, W) for y in (y_a, y_b, y_c, y_d)], group_norm_g[l].reshape(4, W),
                       w_out, l, x2d)

        kv = _memkv(mem, mem_norm_g[l][None, :], xattn_wkv, l)
        x2d = _xattn(x2d.reshape(B, S, D), xattn_norm_g[l][None, :], xattn_wq, kv, xattn_wo, l).reshape(T, D)

        w_r = jnp.pad(jnp.concatenate([router_expert_w[l], router_group_w[l]], axis=1),
                      ((0, 0), (0, LANES - MOE_EXPERTS - MOE_GROUPS)))
        w_r_hi = w_r.astype(BF16)
        w_r_lo = (w_r - w_r_hi.astype(F32)).astype(BF16)
        b_r = _pad_lanes(jnp.concatenate([router_expert_b[l], router_group_b[l]]))
        x2d = _moe(x2d, ffn_norm_g[l][None, :], jnp.stack([w_r_hi, w_r_lo]), b_r, w1_bf, w3_bf, w2_bf, l,
                   final_norm_g[None, :], final_norm=(l == depth - 1))
    return x2d.reshape(B, S, D)
```

```python
import functools
import math

import jax
import jax.numpy as jnp
from jax import lax
from jax.experimental import pallas as pl
from jax.experimental.pallas import tpu as pltpu

F32 = jnp.float32
BF16 = jnp.bfloat16

D_MODEL = 1024
GROUP_WIDTH = 256
HEAD_DIM = 64
N_HEADS = 4
NORM_EPS = 1e-6
CONV_WIDTH = 4
LRU_C = 8.0
SB_BLOCK = 128
SB_WINDOW_BLOCKS = 3
SB_CHAINS = 4
SSM_CHUNK = 128
SSM_STATE = 128
MOBA_BLOCK = 256
MOBA_TOPK = 3
ROPE_THETA = 10000.0
XATTN_HEADS = 4
XATTN_HEAD_DIM = 256
MEM_LEN = 256
MOE_GROUPS = 4
MOE_EPG = 4
MOE_EXPERTS = 16
MOE_FF = 256
LANES = 128
SUBLANES = 8
NEG_BIG = -1e30
SB_EXP_FLOOR = -104.0
IN_OUT_WIDTHS = (512, 768, 256, 768, LANES, 768)
IN_MAIN = 512 + 768 + 256 + 768
IN_OUT_DTYPES = (F32, BF16, F32, F32, F32, F32)
VMEM_LIMIT = 56 * 1024 * 1024


def _cparams(n_axes):
    return pltpu.CompilerParams(dimension_semantics=("arbitrary",) * n_axes,
                                vmem_limit_bytes=VMEM_LIMIT)


def _dot(a, b):
    return jnp.dot(a, b, preferred_element_type=F32)


def _dot_t(a, b):
    return lax.dot_general(a, b, (((1,), (1,)), ((), ())), preferred_element_type=F32)


def _dot_tl(a, b):
    return lax.dot_general(a, b, (((0,), (0,)), ((), ())), preferred_element_type=F32)


def _split2(x):
    hi = x.astype(BF16)
    lo = (x - hi.astype(F32)).astype(BF16)
    return hi, lo


def _split3(x):
    hi = x.astype(BF16)
    r = x - hi.astype(F32)
    mid = r.astype(BF16)
    lo = (r - mid.astype(F32)).astype(BF16)
    return hi, mid, lo


def _dot_wide_lhs(x, m_bf16, parts=3):
    pieces = _split3(x) if parts == 3 else _split2(x)
    out = _dot(pieces[0], m_bf16)
    for p in pieces[1:]:
        out = out + _dot(p, m_bf16)
    return out


def _rmsnorm(x, g):
    return x * lax.rsqrt(jnp.mean(x * x, axis=-1, keepdims=True) + NORM_EPS) * g


def _softplus(x):
    return jnp.maximum(x, 0.0) + jnp.log(1.0 + jnp.exp(-jnp.abs(x)))


def _sigmoid(x):
    return 1.0 / (1.0 + jnp.exp(-x))


def _silu(x):
    return x * _sigmoid(x)


def _gelu_tanh(x):
    return 0.5 * x * (1.0 + jnp.tanh(math.sqrt(2.0 / math.pi) * (x + 0.044715 * (x * x * x))))


def _shift_rows_down(x):
    rows = lax.broadcasted_iota(jnp.int32, x.shape, 0)
    return jnp.where(rows >= 1, pltpu.roll(x, 1, 0), 0.0)


def _causal_conv(x, w_ref, b_ref, cols):
    def taps(v, mask_rows):
        y = v * w_ref[CONV_WIDTH - 1:CONV_WIDTH, cols] + b_ref[:, cols]
        for s in range(1, CONV_WIDTH):
            vs = pltpu.roll(v, s, 0)
            if mask_rows is not None:
                vs = jnp.where(mask_rows >= s, vs, 0.0)
            y = y + vs * w_ref[CONV_WIDTH - 1 - s:CONV_WIDTH - s, cols]
        return y

    head = x[0:SUBLANES, :]
    y_head = taps(head, lax.broadcasted_iota(jnp.int32, head.shape, 0))
    return jnp.concatenate([y_head, taps(x, None)[SUBLANES:, :]], axis=0)


def _phase_conv(slab_ref, w_ref, b_ref, cols):
    P = SUBLANES
    nt = slab_ref.shape[0] // P
    x = [slab_ref[pl.ds(p, nt, stride=P), :] for p in range(P)]
    prev = {p: _shift_rows_down(x[p]) for p in range(P - CONV_WIDTH + 1, P)}
    out = []
    for p in range(P):
        y = x[p] * w_ref[CONV_WIDTH - 1:CONV_WIDTH, cols] + b_ref[:, cols]
        for k in range(1, CONV_WIDTH):
            src = x[p - k] if p - k >= 0 else prev[p - k + P]
            y = y + src * w_ref[CONV_WIDTH - 1 - k:CONV_WIDTH - k, cols]
        out.append(y)
    return out


def _inproj_kernel(x_ref, g_ref, w_ref, *refs):
    o_refs, w_s = refs[:-1], refs[-1]

    @pl.when(pl.program_id(0) == 0)
    def _():
        for c0 in range(0, IN_MAIN, 2 * LANES):
            w_s[:, c0:c0 + 2 * LANES] = w_ref[:, c0:c0 + 2 * LANES].astype(BF16)
        dt_tile = w_ref[:, IN_MAIN:IN_MAIN + LANES]
        lane = lax.broadcasted_iota(jnp.int32, dt_tile.shape, 1)
        w_s[:, IN_MAIN:IN_MAIN + LANES] = jnp.where(lane < N_HEADS, dt_tile, 0.0).astype(BF16)
        w_s[:, IN_MAIN + LANES:] = w_ref[:, IN_MAIN + N_HEADS:].astype(BF16)

    h = _rmsnorm(x_ref[...], g_ref[...]).astype(BF16)
    off = 0
    for o_ref, width in zip(o_refs, IN_OUT_WIDTHS):
        o_ref[...] = _dot(h, w_s[:, off:off + width]).astype(o_ref.dtype)
        off += width


def _inproj(x2d, g, w_in, layer, tm=512):
    T = x2d.shape[0]
    n_in = w_in.shape[-1]
    return pl.pallas_call(
        _inproj_kernel,
        grid=(T // tm,),
        in_specs=[pl.BlockSpec((tm, D_MODEL), lambda i: (i, 0)),
                  pl.BlockSpec((1, D_MODEL), lambda i: (0, 0)),
                  pl.BlockSpec((None, D_MODEL, n_in), lambda i: (layer, 0, 0), pipeline_mode=pl.Buffered(1))],
        out_specs=[pl.BlockSpec((tm, w), lambda i: (i, 0)) for w in IN_OUT_WIDTHS],
        out_shape=[jax.ShapeDtypeStruct((T, w), dt) for w, dt in zip(IN_OUT_WIDTHS, IN_OUT_DTYPES)],
        scratch_shapes=[pltpu.VMEM((D_MODEL, sum(IN_OUT_WIDTHS)), BF16)],
        compiler_params=_cparams(1),
        name="inproj",
    )(x2d, g, w_in)


def _scan_rows(a, u):
    n = a.shape[0]
    rows = lax.broadcasted_iota(jnp.int32, a.shape, 0)
    shift = 1
    while shift < n:
        if shift < SUBLANES:
            keep = rows >= shift
            a_s = jnp.where(keep, pltpu.roll(a, shift, 0), 1.0)
            u_s = jnp.where(keep, pltpu.roll(u, shift, 0), 0.0)
            u = a * u_s + u
            a = a * a_s
        else:
            u = jnp.concatenate([u[:shift], a[shift:] * u[:n - shift] + u[shift:]], axis=0)
            a = jnp.concatenate([a[:shift], a[shift:] * a[:n - shift]], axis=0)
        shift *= 2
    return u


def _lru_kernel(xg_ref, cw_ref, cb_ref, wbd_ref, bri_ref, lam_ref, o_ref, in_s, out_s):
    S = xg_ref.shape[0]
    W = GROUP_WIDTH
    P = SUBLANES
    NT = S // P
    for s in range(2 * W // LANES):
        in_s[s] = xg_ref[:, s * LANES:(s + 1) * LANES]
    log_sig_lam = -_softplus(-lam_ref[...])

    for s in range(W // LANES):
        cols = slice(s * LANES, (s + 1) * LANES)
        xc_all = jnp.concatenate(_phase_conv(in_s.at[s], cw_ref, cb_ref, cols), axis=0)
        w_slab = jnp.concatenate([wbd_ref[cols, cols], wbd_ref[cols, W + s * LANES:W + (s + 1) * LANES]], axis=1)
        ri = _dot(xc_all.astype(BF16), w_slab)
        r = _sigmoid(ri[:, 0:LANES] + bri_ref[:, cols])
        i = _sigmoid(ri[:, LANES:2 * LANES] + bri_ref[:, W + s * LANES:W + (s + 1) * LANES])
        log_a = (LRU_C * r) * log_sig_lam[:, cols]
        a = jnp.exp(log_a)
        u = jnp.sqrt(1.0 - jnp.exp(2.0 * log_a)) * (i * xc_all)
        loc = [u[0:NT]]
        dec = [a[0:NT]]
        for p in range(1, P):
            ap = a[p * NT:(p + 1) * NT]
            loc.append(ap * loc[-1] + u[p * NT:(p + 1) * NT])
            dec.append(ap * dec[-1])
        carry = _shift_rows_down(_scan_rows(dec[-1], loc[-1]))
        for p in range(P):
            h = loc[p] + dec[p] * carry
            out_s[s, pl.ds(p, NT, stride=P), :] = h * _gelu_tanh(in_s[W // LANES + s, pl.ds(p, NT, stride=P), :])
    for s in range(W // LANES):
        o_ref[:, s * LANES:(s + 1) * LANES] = out_s[s]


def _lru(xg, conv_w, conv_b, w_bd, b_ri, lam):
    B, S, _ = xg.shape
    W = GROUP_WIDTH
    full = lambda shape: pl.BlockSpec(shape, lambda b: (0,) * len(shape))
    return pl.pallas_call(
        _lru_kernel,
        grid=(B,),
        in_specs=[pl.BlockSpec((None, S, 2 * W), lambda b: (b, 0, 0)),
                  full((CONV_WIDTH, W)), full((1, W)), full((W, 2 * W)), full((1, 2 * W)), full((1, W))],
        out_specs=pl.BlockSpec((None, S, W), lambda b: (b, 0, 0)),
        out_shape=jax.ShapeDtypeStruct((B, S, W), F32),
        scratch_shapes=[pltpu.VMEM((2 * W // LANES, S, LANES), F32), pltpu.VMEM((W // LANES, S, LANES), F32)],
        compiler_params=_cparams(1),
        name="rglru",
    )(xg, conv_w, conv_b, w_bd, b_ri, lam)


def _sb_kernel(qkv_ref, o_ref, kt_s, v_s, acc_s, later_s):
    i = pl.program_id(1)
    W = GROUP_WIDTH
    TB = SB_BLOCK
    R = N_HEADS * TB

    NW = SB_WINDOW_BLOCKS
    KW = NW * TB
    PAD = (NW - 1) * TB
    S = qkv_ref.shape[0]

    @pl.when(i == 0)
    def _():
        kt_s[:, 0:PAD] = jnp.zeros((W, PAD), BF16)
        for r0 in range(0, S, W):
            kt_s[:, PAD + r0:PAD + r0 + W] = qkv_ref[r0:r0 + W, W:2 * W].astype(F32).T.astype(BF16)
        v_s[0:PAD, :] = jnp.zeros((PAD, W), BF16)
        v_s[PAD:PAD + S, :] = qkv_ref[:, 2 * W:3 * W].astype(BF16)

    lane = lax.broadcasted_iota(jnp.int32, (TB, W), 1)
    heads = [(lane >= h * HEAD_DIM) & (lane < (h + 1) * HEAD_DIM) for h in range(N_HEADS)]
    r_loc = lax.broadcasted_iota(jnp.int32, (R, KW), 0) & (TB - 1)
    c_loc = lax.broadcasted_iota(jnp.int32, (R, KW), 1)
    ur = lax.broadcasted_iota(jnp.int32, (TB, 2 * TB), 0)
    uc = lax.broadcasted_iota(jnp.int32, (TB, 2 * TB), 1)
    tri_ones = jnp.where((ur > uc) | (uc >= TB), 1.0, 0.0).astype(BF16)

    n_chains = o_ref.shape[0]
    blocks = [i + c * (S // TB // n_chains) for c in range(n_chains)]
    qss = []
    for blk in blocks:
        q = qkv_ref[pl.ds(pl.multiple_of(blk * TB, TB), TB), 0:W] * (HEAD_DIM ** -0.5)
        qss.append(jnp.concatenate([jnp.where(hm, q, 0.0) for hm in heads], axis=0).astype(BF16))

    acc_s[...] = jnp.zeros_like(acc_s)
    later_s[...] = jnp.zeros_like(later_s)

    def window(c, n):
        blk = blocks[c]
        first_key = (blk - n * NW - (NW - 1)) * TB
        rows = pl.ds(pl.multiple_of(jnp.maximum(first_key + PAD, 0), TB), KW)
        z = _dot(qss[c], kt_s[:, rows])
        key_abs = first_key + c_loc
        live = (key_abs < blk * TB + r_loc) & (key_abs >= 0)
        sp = _softplus(z)
        lf = jnp.where(live, -sp, 0.0)
        lf16 = lf.astype(BF16)
        order = list(range(NW - 1, -1, -1))
        stacked = jnp.concatenate([lf16[:, b * TB:(b + 1) * TB] for b in order], axis=0)
        cs_all = _dot(stacked, tri_ones)
        offset = later_s[c]
        after = [None] * NW
        for pos, b in enumerate(order):
            cs = cs_all[pos * R:(pos + 1) * R, :]
            after[b] = cs[:, 0:TB] + offset
            offset = offset + cs[:, TB:2 * TB]
        w = jnp.where(live, jnp.exp((z - sp) + jnp.concatenate(after, axis=1)), 0.0)
        acc_s[c] += _dot(w.astype(BF16), v_s[rows, :])
        later_s[c] = offset
        return jnp.where((n + 1) * NW <= blk, jnp.max(offset), SB_EXP_FLOOR)

    def cond(carry):
        return carry[1] > SB_EXP_FLOOR

    def body(carry):
        n = carry[0]
        later_max = window(0, n)
        for c in range(1, n_chains):
            later_max = jnp.maximum(later_max, window(c, n))
        return n + 1, later_max

    lax.while_loop(cond, body, (jnp.int32(0), jnp.float32(0.0)))
    for c in range(n_chains):
        out = acc_s[c, 0:TB, :]
        for h in range(1, N_HEADS):
            out = jnp.where(heads[h], acc_s[c, h * TB:(h + 1) * TB, :], out)
        o_ref[c] = out


def _sb_attention(qkv):
    B, S, _ = qkv.shape
    W = GROUP_WIDTH
    pad = (SB_WINDOW_BLOCKS - 1) * SB_BLOCK
    nc = SB_CHAINS
    rows = N_HEADS * SB_BLOCK
    out = pl.pallas_call(
        _sb_kernel,
        grid=(B, S // SB_BLOCK // nc),
        in_specs=[pl.BlockSpec((None, S, 3 * W), lambda b, i: (b, 0, 0))],
        out_specs=pl.BlockSpec((None, nc, SB_BLOCK, W), lambda b, i: (b, 0, i, 0)),
        out_shape=jax.ShapeDtypeStruct((B, nc, S // nc, W), F32),
        scratch_shapes=[pltpu.VMEM((W, S + pad), BF16), pltpu.VMEM((S + pad, W), BF16),
                        pltpu.VMEM((nc, rows, W), F32), pltpu.VMEM((nc, rows, SB_BLOCK), F32)],
        compiler_params=_cparams(2),
        name="stickbreak",
    )(qkv)
    return out.reshape(B, S, W)


def _ssd_kernel(z_ref, xbc_ref, dt_ref, cw_ref, cb_ref, dtb_ref, alog_ref, dskip_ref, o_ref, xbc_s):
    S = z_ref.shape[0]
    W = GROUP_WIDTH
    L = SSM_CHUNK
    for s in range(3 * W // LANES):
        cols = slice(s * LANES, (s + 1) * LANES)
        xbc_s[s] = _silu(_causal_conv(xbc_ref[:, cols], cw_ref, cb_ref, cols))
    a_row = -jnp.exp(alog_ref[...])

    r_i = lax.broadcasted_iota(jnp.int32, (L, L), 0)
    c_i = lax.broadcasted_iota(jnp.int32, (L, L), 1)
    tri_incl = jnp.where(c_i <= r_i, 1.0, 0.0).astype(BF16)
    lower = c_i <= r_i
    e_r = lax.broadcasted_iota(jnp.int32, (LANES, W), 0)
    e_c = lax.broadcasted_iota(jnp.int32, (LANES, W), 1)
    expand = jnp.where((e_c >= e_r * HEAD_DIM) & (e_c < (e_r + 1) * HEAD_DIM), 1.0, 0.0).astype(BF16)
    lane_l = lax.broadcasted_iota(jnp.int32, (L, LANES), 1)

    def chunk(c, states):
        rows = slice(c * L, (c + 1) * L)
        xs = jnp.concatenate([xbc_s[0, rows, :], xbc_s[1, rows, :]], axis=1)
        dt = _softplus(dt_ref[rows, :] + dtb_ref[...])
        a_dt = dt * a_row
        cs_col = _dot_wide_lhs_rhs(tri_incl, a_dt)
        cs_row = cs_col.T
        cs_full = _dot_wide_lhs(cs_col, expand)
        dt_full = _dot_wide_lhs(dt, expand)
        xd = xs * dt_full
        tot = cs_full[L - 1:L, :]
        xdec = (xd * jnp.exp(tot - cs_full)).astype(BF16)
        xd16 = xd.astype(BF16)
        ys = []
        new_states = []
        for g in range(2):
            gl = slice(g * LANES, (g + 1) * LANES)
            bm = xbc_s[2 + g, rows, :].astype(BF16)
            cm = xbc_s[4 + g, rows, :].astype(BF16)
            cb = _dot_t(cm, bm)
            prev = states[g]
            y_off = _dot(cm, prev.astype(BF16)) * jnp.exp(cs_full[:, gl])
            y_g = y_off
            for hh in range(2):
                h = 2 * g + hh
                seg = jnp.where(lower, cs_col[:, h:h + 1] - cs_row[h:h + 1, :], -jnp.inf)
                y_h = _dot((cb * jnp.exp(seg)).astype(BF16), xd16[:, gl])
                in_head = (lane_l >= hh * HEAD_DIM) & (lane_l < (hh + 1) * HEAD_DIM)
                y_g = y_g + jnp.where(in_head, y_h, 0.0)
            new_states.append(prev * jnp.exp(tot[:, gl]) + _dot_tl(bm, xdec[:, gl]))
            ys.append(y_g)
        y = jnp.concatenate(ys, axis=1) + dskip_ref[...] * xs
        o_ref[rows, :] = y * _silu(z_ref[rows, :])
        return new_states

    states = [jnp.zeros((SSM_STATE, LANES), F32) for _ in range(2)]
    for c in range(S // L):
        states = chunk(c, states)


def _dot_wide_lhs_rhs(m_bf16, x):
    hi, mid, lo = _split3(x)
    return _dot(m_bf16, hi) + _dot(m_bf16, mid) + _dot(m_bf16, lo)


def _ssd(z, xbc, dt, conv_w, conv_b, dt_bias, a_log, d_skip):
    B, S, _ = z.shape
    W = GROUP_WIDTH
    full = lambda shape: pl.BlockSpec(shape, lambda b: (0,) * len(shape))
    return pl.pallas_call(
        _ssd_kernel,
        grid=(B,),
        in_specs=[pl.BlockSpec((None, S, W), lambda b: (b, 0, 0)),
                  pl.BlockSpec((None, S, 3 * W), lambda b: (b, 0, 0)),
                  pl.BlockSpec((None, S, LANES), lambda b: (b, 0, 0)),
                  full((CONV_WIDTH, 3 * W)), full((1, 3 * W)), full((1, LANES)), full((1, LANES)),
                  full((1, W))],
        out_specs=pl.BlockSpec((None, S, W), lambda b: (b, 0, 0)),
        out_shape=jax.ShapeDtypeStruct((B, S, W), F32),
        scratch_shapes=[pltpu.VMEM((3 * W // LANES, S, LANES), F32)],
        compiler_params=_cparams(1),
        name="ssd",
    )(z, xbc, dt, conv_w, conv_b, dt_bias, a_log, d_skip)


def _rope(x, cos, sin):
    lane = lax.broadcasted_iota(jnp.int32, (x.shape[0], LANES), 1)
    first_half = (lane % HEAD_DIM) < (HEAD_DIM // 2)
    halves = []
    for p in range(x.shape[1] // LANES):
        xp = x[:, p * LANES:(p + 1) * LANES]
        fwd = pltpu.roll(xp, HEAD_DIM // 2, 1)
        bwd = pltpu.roll(xp, LANES - HEAD_DIM // 2, 1)
        halves.append(jnp.where(first_half, -bwd, fwd))
    rot = jnp.concatenate(halves, axis=1)
    return x * cos + rot * sin


def _moba_kernel(qkv_ref, cos_ref, sin_ref, o_ref, k_s, vt_s, kmean_s, acc_s, bias_s):
    i = pl.program_id(1)
    W = GROUP_WIDTH
    TB = MOBA_BLOCK
    S = qkv_ref.shape[0]
    NB = S // TB

    @pl.when(i == 0)
    def _():
        for blk in range(NB):
            rs = slice(blk * TB, (blk + 1) * TB)
            kb = _rope(qkv_ref[rs, W:2 * W], cos_ref[rs, :], sin_ref[rs, :])
            for h in range(N_HEADS):
                k_s[h, rs, :] = kb[:, h * HEAD_DIM:(h + 1) * HEAD_DIM].astype(BF16)
            kmean_s[blk:blk + 1, :] = jnp.mean(kb, axis=0, keepdims=True)
            vt_s[blk] =qkv_ref[rs, 2 * W:3 * W].T.astype(BF16)

    rows_i = pl.ds(pl.multiple_of(i * TB, TB), TB)
    q = _rope(qkv_ref[rows_i, 0:W], cos_ref[rows_i, :], sin_ref[rows_i, :])
    lane = lax.broadcasted_iota(jnp.int32, (TB, W), 1)
    lane8 = lax.broadcasted_iota(jnp.int32, (NB, W), 1)
    blk_id = lax.broadcasted_iota(jnp.int32, (NB, TB), 0)
    R = N_HEADS * TB
    key_loc = lax.broadcasted_iota(jnp.int32, (TB, R), 0)
    q_loc = lax.broadcasted_iota(jnp.int32, (TB, R), 1) & (TB - 1)
    kmean = kmean_s[...]
    q_t = q.T
    qt_hi, qt_lo = _split2(q_t)
    scale = HEAD_DIM ** -0.5

    heads = [(lane >= h * HEAD_DIM) & (lane < (h + 1) * HEAD_DIM) for h in range(N_HEADS)]
    km_all = jnp.concatenate(
        [jnp.where((lane8 >= h * HEAD_DIM) & (lane8 < (h + 1) * HEAD_DIM), kmean, 0.0) for h in range(N_HEADS)],
        axis=0)
    km_hi, km_lo = _split2(km_all)
    gate_all = _dot(km_hi, qt_hi) + _dot(km_hi, qt_lo) + _dot(km_lo, qt_hi)
    qhs = []
    for h in range(N_HEADS):
        gate = gate_all[h * NB:(h + 1) * NB, :]
        cnt = jnp.zeros((NB, TB), F32)
        for jp in range(NB):
            row = gate[jp:jp + 1, :]
            beats = (row > gate) | ((row == gate) & (blk_id > jp))
            cnt = cnt + jnp.where(beats, jnp.where(jp < i, 1.0, 0.0), 0.0)
        selected = (cnt < float(MOBA_TOPK)) & (blk_id < i)
        bias_s[:, h * TB:(h + 1) * TB] = jnp.where(selected, 0.0, NEG_BIG)
        qhs.append((q_t[h * HEAD_DIM:(h + 1) * HEAD_DIM, :] * scale).astype(BF16))

    def scores(rows):
        return jnp.concatenate([_dot(k_s[h, rows, :], qhs[h]) for h in range(N_HEADS)], axis=1)

    s = jnp.where(key_loc <= q_loc, scores(rows_i), NEG_BIG)
    m = jnp.max(s, axis=0, keepdims=True)
    p = jnp.exp(s - m)
    l = jnp.sum(p, axis=0, keepdims=True)
    acc_s[...] = _dot(vt_s[i], p.astype(BF16))

    def body(j, carry):
        m, l = carry
        rows = pl.ds(pl.multiple_of(j * TB, TB), TB)
        s = scores(rows) + bias_s[pl.ds(j, 1), :]
        m_new = jnp.maximum(m, jnp.max(s, axis=0, keepdims=True))
        alpha = jnp.exp(m - m_new)
        p = jnp.exp(s - m_new)
        l = alpha * l + jnp.sum(p, axis=0, keepdims=True)
        acc_s[...] = alpha * acc_s[...] + _dot(vt_s[j], p.astype(BF16))
        return m_new, l

    m, l = lax.fori_loop(0, i, body, (m, l))
    outs = acc_s[...] / l
    out = outs[:, 0:TB].T
    for h in range(1, N_HEADS):
        out = jnp.where(heads[h], outs[:, h * TB:(h + 1) * TB].T, out)
    o_ref[...] = out


def _moba(qkv, cos, sin):
    B, S, _ = qkv.shape
    W = GROUP_WIDTH
    nb = S // MOBA_BLOCK
    return pl.pallas_call(
        _moba_kernel,
        grid=(B, S // MOBA_BLOCK),
        in_specs=[pl.BlockSpec((None, S, 3 * W), lambda b, i: (b, 0, 0)),
                  pl.BlockSpec((S, W), lambda b, i: (0, 0)),
                  pl.BlockSpec((S, W), lambda b, i: (0, 0))],
        out_specs=pl.BlockSpec((None, MOBA_BLOCK, W), lambda b, i: (b, i, 0)),
        out_shape=jax.ShapeDtypeStruct((B, S, W), F32),
        scratch_shapes=[pltpu.VMEM((N_HEADS, S, HEAD_DIM), BF16), pltpu.VMEM((nb, W, MOBA_BLOCK), BF16),
                        pltpu.VMEM((nb, W), F32), pltpu.VMEM((W, N_HEADS * MOBA_BLOCK), F32),
                        pltpu.VMEM((nb, N_HEADS * MOBA_BLOCK), F32)],
        compiler_params=_cparams(2),
        name="moba",
    )(qkv, cos, sin)


def _outproj_kernel(ya_ref, yb_ref, yc_ref, yd_ref, gg_ref, w_ref, x_ref, o_ref):
    W = GROUP_WIDTH
    acc = x_ref[...]
    for g, y_ref in enumerate((ya_ref, yb_ref, yc_ref, yd_ref)):
        yn = _rmsnorm(y_ref[...], gg_ref[g:g + 1, :]).astype(BF16)
        acc = acc + _dot(yn, w_ref[g * W:(g + 1) * W, :].astype(BF16))
    o_ref[...] = acc


def _outproj(ys, gg, w_out, layer, x2d, tm=1024):
    T = x2d.shape[0]
    W = GROUP_WIDTH
    return pl.pallas_call(
        _outproj_kernel,
        grid=(T // tm,),
        in_specs=[pl.BlockSpec((tm, W), lambda i: (i, 0))] * 4
                 + [pl.BlockSpec((4, W), lambda i: (0, 0)),
                    pl.BlockSpec((None, 4 * W, D_MODEL), lambda i: (layer, 0, 0), pipeline_mode=pl.Buffered(1)),
                    pl.BlockSpec((tm, D_MODEL), lambda i: (i, 0))],
        out_specs=pl.BlockSpec((tm, D_MODEL), lambda i: (i, 0)),
        out_shape=jax.ShapeDtypeStruct((T, D_MODEL), F32),
        compiler_params=_cparams(1),
        name="outproj",
    )(*ys, gg, w_out, x2d)


def _memkv_kernel(m_ref, g_ref, w_ref, kt_ref, v_ref):
    mn = _rmsnorm(m_ref[...], g_ref[...]).astype(BF16)
    kv = _dot(mn, w_ref[...].astype(BF16))
    kt_ref[...] = kv[:, 0:D_MODEL].T.astype(BF16)
    v_ref[...] = kv[:, D_MODEL:2 * D_MODEL].astype(BF16)


def _memkv(mem, g, wkv, layer):
    B, M, _ = mem.shape
    return pl.pallas_call(
        _memkv_kernel,
        grid=(B,),
        in_specs=[pl.BlockSpec((None, M, D_MODEL), lambda b: (b, 0, 0)),
                  pl.BlockSpec((1, D_MODEL), lambda b: (0, 0)),
                  pl.BlockSpec((None, D_MODEL, 2 * D_MODEL), lambda b: (layer, 0, 0),
                               pipeline_mode=pl.Buffered(1))],
        out_specs=[pl.BlockSpec((None, D_MODEL, M), lambda b: (b, 0, 0)),
                   pl.BlockSpec((None, M, D_MODEL), lambda b: (b, 0, 0))],
        out_shape=[jax.ShapeDtypeStruct((B, D_MODEL, M), BF16), jax.ShapeDtypeStruct((B, M, D_MODEL), BF16)],
        compiler_params=_cparams(1),
        name="memkv",
    )(mem, g, wkv)


def _xattn_kernel(x_ref, g_ref, wq_ref, kt_ref, v_ref, wo_ref, o_ref):
    x = x_ref[...]
    h = _rmsnorm(x, g_ref[...]).astype(BF16)
    q = (_dot(h, wq_ref[...].astype(BF16)) * (XATTN_HEAD_DIM ** -0.5)).astype(BF16)
    acc = x
    for hd in range(XATTN_HEADS):
        cols = slice(hd * XATTN_HEAD_DIM, (hd + 1) * XATTN_HEAD_DIM)
        s = _dot(q[:, cols], kt_ref[cols, :])
        p = jnp.exp(s - jnp.max(s, axis=-1, keepdims=True))
        p = p / jnp.sum(p, axis=-1, keepdims=True)
        o = _dot(p.astype(BF16), v_ref[:, cols]).astype(BF16)
        acc = acc + _dot(o, wo_ref[cols, :].astype(BF16))
    o_ref[...] = acc


def _xattn(x3d, g, wq, kt, v, wo, layer, tm=1024):
    B, S, _ = x3d.shape
    M = v.shape[1]
    weight = pl.BlockSpec((None, D_MODEL, D_MODEL), lambda b, i: (layer, 0, 0), pipeline_mode=pl.Buffered(1))
    return pl.pallas_call(
        _xattn_kernel,
        grid=(B, S // tm),
        in_specs=[pl.BlockSpec((None, tm, D_MODEL), lambda b, i: (b, i, 0)),
                  pl.BlockSpec((1, D_MODEL), lambda b, i: (0, 0)),
                  weight,
                  pl.BlockSpec((None, D_MODEL, M), lambda b, i: (b, 0, 0)),
                  pl.BlockSpec((None, M, D_MODEL), lambda b, i: (b, 0, 0)),
                  weight],
        out_specs=pl.BlockSpec((None, tm, D_MODEL), lambda b, i: (b, i, 0)),
        out_shape=jax.ShapeDtypeStruct((B, S, D_MODEL), F32),
        compiler_params=_cparams(2),
        name="xattn",
    )(x3d, g, wq, kt, v, wo)


def _moe_kernel(x_ref, g_ref, wr_ref, br_ref, w1_ref, w3_ref, w2_ref, fg_ref, o_ref,
                t_s, comb_s, acc_s, xkeep_s, *, final_norm):
    tile = pl.program_id(0)
    grp = pl.program_id(1)
    tm = x_ref.shape[0]
    slot = lax.bitwise_and(tile, 1)

    def route(dst):
        t = _rmsnorm(x_ref[...], g_ref[...])
        t_s[dst] = t.astype(BF16)
        t_hi, t_lo = _split2(t)
        logits = (_dot(t_hi, wr_ref[0]) + _dot(t_hi, wr_ref[1]) + _dot(t_lo, wr_ref[0])) + br_ref[...]
        lt = logits.T
        gsl = SUBLANES * (MOE_EXPERTS // SUBLANES)
        g_row = lax.broadcasted_iota(jnp.int32, (SUBLANES, tm), 0).astype(F32)
        lg = jnp.where(g_row < float(MOE_GROUPS), lt[gsl:gsl + SUBLANES, :], NEG_BIG)
        gmax = jnp.max(lg, axis=0, keepdims=True)
        pg_top = 1.0 / jnp.sum(jnp.exp(lg - gmax), axis=0, keepdims=True)
        g_idx = jnp.min(jnp.where(lg == gmax, g_row, 1e9), axis=0, keepdims=True)
        e_row = lax.broadcasted_iota(jnp.int32, (MOE_EXPERTS, tm), 0).astype(F32)
        in_grp = jnp.floor(e_row * (1.0 / MOE_EPG)) == g_idx
        le = jnp.where(in_grp, lt[0:MOE_EXPERTS, :], NEG_BIG)
        e1 = jnp.max(le, axis=0, keepdims=True)
        i1 = jnp.min(jnp.where(in_grp & (le == e1), e_row, 1e9), axis=0, keepdims=True)
        le2 = jnp.where(e_row == i1, NEG_BIG, le)
        e2 = jnp.max(le2, axis=0, keepdims=True)
        i2 = jnp.min(jnp.where(in_grp & (e_row != i1) & (le2 == e2), e_row, 1e9), axis=0, keepdims=True)
        r2 = jnp.exp(e2 - e1)
        w_first = 1.0 / (1.0 + r2)
        w_second = r2 / (1.0 + r2)
        comb_t = pg_top * (jnp.where(e_row == i1, w_first, 0.0) + jnp.where(e_row == i2, w_second, 0.0))
        comb_s[dst] = jnp.concatenate([comb_t, jnp.zeros((LANES - MOE_EXPERTS, tm), F32)], axis=0).T

    def experts():
        t = t_s[slot]
        comb = comb_s[slot]
        lane = lax.broadcasted_iota(jnp.int32, (tm, LANES), 1)
        upd = jnp.zeros((tm, D_MODEL), F32)
        for e in range(MOE_EPG):
            n = grp * MOE_EPG + e
            c = jnp.sum(jnp.where(lane == n, comb, 0.0), axis=-1, keepdims=True)
            hid = _silu(_dot(t, w1_ref[e])) * _dot(t, w3_ref[e])
            upd = upd + _dot((hid * c).astype(BF16), w2_ref[e])
        return upd

    @pl.when((tile == 0) & (grp == 0))
    def _():
        route(0)

    @pl.when(grp == 0)
    def _():
        xkeep_s[...] = x_ref[...]
        acc_s[...] = experts()

    @pl.when((grp > 0) & (grp < MOE_GROUPS - 1))
    def _():
        acc_s[...] += experts()

    @pl.when(grp == MOE_GROUPS - 1)
    def _():
        y = xkeep_s[...] + (acc_s[...] + experts())
        route(1 - slot)
        if final_norm:
            y = _rmsnorm(y, fg_ref[...])
        o_ref[...] = y


def _moe(x2d, g, w_router, b_router, w1, w3, w2, layer, final_g, final_norm, tm=1024):
    T = x2d.shape[0]
    n_tiles = T // tm

    def x_window(i, e):
        return jnp.minimum(i + e // (MOE_GROUPS - 1), n_tiles - 1), 0

    return pl.pallas_call(
        functools.partial(_moe_kernel, final_norm=final_norm),
        grid=(n_tiles, MOE_GROUPS),
        in_specs=[pl.BlockSpec((tm, D_MODEL), x_window),
                  pl.BlockSpec((1, D_MODEL), lambda i, e: (0, 0)),
                  pl.BlockSpec((2, D_MODEL, LANES), lambda i, e: (0, 0, 0)),
                  pl.BlockSpec((1, LANES), lambda i, e: (0, 0)),
                  pl.BlockSpec((None, MOE_EPG, D_MODEL, MOE_FF), lambda i, e: (layer, e, 0, 0)),
                  pl.BlockSpec((None, MOE_EPG, D_MODEL, MOE_FF), lambda i, e: (layer, e, 0, 0)),
                  pl.BlockSpec((None, MOE_EPG, MOE_FF, D_MODEL), lambda i, e: (layer, e, 0, 0)),
                  pl.BlockSpec((1, D_MODEL), lambda i, e: (0, 0))],
        out_specs=pl.BlockSpec((tm, D_MODEL), lambda i, e: (i, 0)),
        out_shape=jax.ShapeDtypeStruct((T, D_MODEL), F32),
        scratch_shapes=[pltpu.VMEM((2, tm, D_MODEL), BF16), pltpu.VMEM((2, tm, LANES), F32),
                        pltpu.VMEM((tm, D_MODEL), F32), pltpu.VMEM((tm, D_MODEL), F32)],
        compiler_params=_cparams(2),
        name="moe",
    )(x2d, g, w_router, b_router, w1, w3, w2, final_g)


def _pad_lanes(v, width=LANES):
    return jnp.pad(v, (0, width - v.shape[0]))[None, :]


def _block_diag(w):
    H, n, _ = w.shape
    eye = jnp.eye(H, dtype=w.dtype)
    return (eye[:, None, :, None] * w[:, :, None, :]).reshape(H * n, H * n)


def _rope_tables(S):
    half = HEAD_DIM // 2
    inv_freq = ROPE_THETA ** (-jnp.arange(half, dtype=F32) / half)
    ang = jnp.arange(S, dtype=F32)[:, None] * inv_freq[None, :]
    reps = GROUP_WIDTH // half
    return jnp.tile(jnp.cos(ang), (1, reps)), jnp.tile(jnp.sin(ang), (1, reps))


def kernel(x, mem, mix_norm_g, w_in, lru_conv_w, lru_conv_b, lru_wr, lru_br, lru_wi, lru_bi, lru_lambda, ssm_conv_w, ssm_conv_b, ssm_dt_bias, ssm_a_log, ssm_d, group_norm_g, w_out, xattn_norm_g, mem_norm_g, xattn_wq, xattn_wkv, xattn_wo, ffn_norm_g, router_group_w, router_group_b, router_expert_w, router_expert_b, expert_w1, expert_w3, expert_w2, final_norm_g):
    B, S, D = x.shape
    T = B * S
    depth = w_in.shape[0]
    W = GROUP_WIDTH
    cos, sin = _rope_tables(S)
    x2d = x.reshape(T, D)
    w1_bf, w3_bf, w2_bf = expert_w1.astype(BF16), expert_w3.astype(BF16), expert_w2.astype(BF16)
    for l in range(depth):
        lru_xg, sb_qkv, ssm_z, ssm_xbc, ssm_dt, mb_qkv = _inproj(x2d, mix_norm_g[l][None, :], w_in, l)

        w_bd = jnp.concatenate([_block_diag(lru_wr[l]), _block_diag(lru_wi[l])], axis=1).astype(BF16)
        b_ri = jnp.concatenate([lru_br[l], lru_bi[l]])[None, :]
        y_a = _lru(lru_xg.reshape(B, S, 2 * W), lru_conv_w[l], lru_conv_b[l][None, :], w_bd, b_ri,
                   lru_lambda[l][None, :])
        y_b = _sb_attention(sb_qkv.reshape(B, S, 3 * W))
        y_c = _ssd(ssm_z.reshape(B, S, W), ssm_xbc.reshape(B, S, 3 * W), ssm_dt.reshape(B, S, LANES),
                   ssm_conv_w[l], ssm_conv_b[l][None, :], _pad_lanes(ssm_dt_bias[l]), _pad_lanes(ssm_a_log[l]),
                   jnp.repeat(ssm_d[l], HEAD_DIM)[None, :])
        y_d = _moba(mb_qkv.reshape(B, S, 3 * W), cos, sin)
        x2d = _outproj([y.reshape(T, W) for y in (y_a, y_b, y_c, y_d)], group_norm_g[l].reshape(4, W),
                       w_out, l, x2d)

        mem_kt, mem_v = _memkv(mem, mem_norm_g[l][None, :], xattn_wkv, l)
        x2d = _xattn(x2d.reshape(B, S, D), xattn_norm_g[l][None, :], xattn_wq, mem_kt, mem_v, xattn_wo,
                     l).reshape(T, D)

        w_r = jnp.pad(jnp.concatenate([router_expert_w[l], router_group_w[l]], axis=1),
                      ((0, 0), (0, LANES - MOE_EXPERTS - MOE_GROUPS)))
        w_r_hi = w_r.astype(BF16)
        w_r_lo = (w_r - w_r_hi.astype(F32)).astype(BF16)
        b_r = _pad_lanes(jnp.concatenate([router_expert_b[l], router_group_b[l]]))
        x2d = _moe(x2d, ffn_norm_g[l][None, :], jnp.stack([w_r_hi, w_r_lo]), b_r, w1_bf, w3_bf, w2_bf, l,
                   final_norm_g[None, :], final_norm=(l == depth - 1))
    return x2d.reshape(B, S, D)
```

```python
import functools
import math

import jax
import jax.numpy as jnp
from jax import lax
from jax.experimental import pallas as pl
from jax.experimental.pallas import tpu as pltpu

F32 = jnp.float32
BF16 = jnp.bfloat16

D_MODEL = 1024
GROUP_WIDTH = 256
HEAD_DIM = 64
N_HEADS = 4
NORM_EPS = 1e-6
CONV_WIDTH = 4
LRU_C = 8.0
SB_BLOCK = 128
SB_WINDOW_BLOCKS = 3
SB_CHAINS = 4
SSM_CHUNK = 128
SSM_STATE = 128
MOBA_BLOCK = 256
MOBA_TOPK = 3
ROPE_THETA = 10000.0
XATTN_HEADS = 4
XATTN_HEAD_DIM = 256
MEM_LEN = 256
MOE_GROUPS = 4
MOE_EPG = 4
MOE_EXPERTS = 16
MOE_FF = 256
LANES = 128
SUBLANES = 8
NEG_BIG = -1e30
SB_EXP_FLOOR = -104.0
IN_OUT_WIDTHS = (512, 768, 256, 768, LANES, 768)
IN_MAIN = 512 + 768 + 256 + 768
IN_OUT_DTYPES = (F32, BF16, F32, F32, F32, F32)
VMEM_LIMIT = 56 * 1024 * 1024


def _cparams(n_axes):
    return pltpu.CompilerParams(dimension_semantics=("arbitrary",) * n_axes,
                                vmem_limit_bytes=VMEM_LIMIT)


def _dot(a, b):
    return jnp.dot(a, b, preferred_element_type=F32)


def _dot_t(a, b):
    return lax.dot_general(a, b, (((1,), (1,)), ((), ())), preferred_element_type=F32)


def _dot_tl(a, b):
    return lax.dot_general(a, b, (((0,), (0,)), ((), ())), preferred_element_type=F32)


def _split2(x):
    hi = x.astype(BF16)
    lo = (x - hi.astype(F32)).astype(BF16)
    return hi, lo


def _split3(x):
    hi = x.astype(BF16)
    r = x - hi.astype(F32)
    mid = r.astype(BF16)
    lo = (r - mid.astype(F32)).astype(BF16)
    return hi, mid, lo


def _dot_wide_lhs(x, m_bf16, parts=3):
    pieces = _split3(x) if parts == 3 else _split2(x)
    out = _dot(pieces[0], m_bf16)
    for p in pieces[1:]:
        out = out + _dot(p, m_bf16)
    return out


def _rmsnorm(x, g):
    return x * lax.rsqrt(jnp.mean(x * x, axis=-1, keepdims=True) + NORM_EPS) * g


def _softplus(x):
    return jnp.maximum(x, 0.0) + jnp.log(1.0 + jnp.exp(-jnp.abs(x)))


def _sigmoid(x):
    return 1.0 / (1.0 + jnp.exp(-x))


def _silu(x):
    return x * _sigmoid(x)


def _gelu_tanh(x):
    return 0.5 * x * (1.0 + jnp.tanh(math.sqrt(2.0 / math.pi) * (x + 0.044715 * (x * x * x))))


def _shift_rows_down(x):
    rows = lax.broadcasted_iota(jnp.int32, x.shape, 0)
    return jnp.where(rows >= 1, pltpu.roll(x, 1, 0), 0.0)


def _causal_conv(x, w_ref, b_ref, cols):
    def taps(v, mask_rows):
        y = v * w_ref[CONV_WIDTH - 1:CONV_WIDTH, cols] + b_ref[:, cols]
        for s in range(1, CONV_WIDTH):
            vs = pltpu.roll(v, s, 0)
            if mask_rows is not None:
                vs = jnp.where(mask_rows >= s, vs, 0.0)
            y = y + vs * w_ref[CONV_WIDTH - 1 - s:CONV_WIDTH - s, cols]
        return y

    head = x[0:SUBLANES, :]
    y_head = taps(head, lax.broadcasted_iota(jnp.int32, head.shape, 0))
    return jnp.concatenate([y_head, taps(x, None)[SUBLANES:, :]], axis=0)


def _phase_conv(slab_ref, w_ref, b_ref, cols):
    P = SUBLANES
    nt = slab_ref.shape[0] // P
    x = [slab_ref[pl.ds(p, nt, stride=P), :] for p in range(P)]
    prev = {p: _shift_rows_down(x[p]) for p in range(P - CONV_WIDTH + 1, P)}
    out = []
    for p in range(P):
        y = x[p] * w_ref[CONV_WIDTH - 1:CONV_WIDTH, cols] + b_ref[:, cols]
        for k in range(1, CONV_WIDTH):
            src = x[p - k] if p - k >= 0 else prev[p - k + P]
            y = y + src * w_ref[CONV_WIDTH - 1 - k:CONV_WIDTH - k, cols]
        out.append(y)
    return out


def _inproj_kernel(x_ref, g_ref, w_ref, *refs):
    o_refs, w_s = refs[:-1], refs[-1]

    @pl.when(pl.program_id(0) == 0)
    def _():
        for c0 in range(0, IN_MAIN, 2 * LANES):
            w_s[:, c0:c0 + 2 * LANES] = w_ref[:, c0:c0 + 2 * LANES].astype(BF16)
        dt_tile = w_ref[:, IN_MAIN:IN_MAIN + LANES]
        lane = lax.broadcasted_iota(jnp.int32, dt_tile.shape, 1)
        w_s[:, IN_MAIN:IN_MAIN + LANES] = jnp.where(lane < N_HEADS, dt_tile, 0.0).astype(BF16)
        w_s[:, IN_MAIN + LANES:] = w_ref[:, IN_MAIN + N_HEADS:].astype(BF16)

    h = _rmsnorm(x_ref[...], g_ref[...]).astype(BF16)
    off = 0
    for o_ref, width in zip(o_refs, IN_OUT_WIDTHS):
        o_ref[...] = _dot(h, w_s[:, off:off + width]).astype(o_ref.dtype)
        off += width


def _inproj(x2d, g, w_in, layer, tm=512):
    T = x2d.shape[0]
    n_in = w_in.shape[-1]
    return pl.pallas_call(
        _inproj_kernel,
        grid=(T // tm,),
        in_specs=[pl.BlockSpec((tm, D_MODEL), lambda i: (i, 0)),
                  pl.BlockSpec((1, D_MODEL), lambda i: (0, 0)),
                  pl.BlockSpec((None, D_MODEL, n_in), lambda i: (layer, 0, 0), pipeline_mode=pl.Buffered(1))],
        out_specs=[pl.BlockSpec((tm, w), lambda i: (i, 0)) for w in IN_OUT_WIDTHS],
        out_shape=[jax.ShapeDtypeStruct((T, w), dt) for w, dt in zip(IN_OUT_WIDTHS, IN_OUT_DTYPES)],
        scratch_shapes=[pltpu.VMEM((D_MODEL, sum(IN_OUT_WIDTHS)), BF16)],
        compiler_params=_cparams(1),
        name="inproj",
    )(x2d, g, w_in)


def _scan_rows(a, u):
    n = a.shape[0]
    rows = lax.broadcasted_iota(jnp.int32, a.shape, 0)
    shift = 1
    while shift < n:
        if shift < SUBLANES:
            keep = rows >= shift
            a_s = jnp.where(keep, pltpu.roll(a, shift, 0), 1.0)
            u_s = jnp.where(keep, pltpu.roll(u, shift, 0), 0.0)
            u = a * u_s + u
            a = a * a_s
        else:
            u = jnp.concatenate([u[:shift], a[shift:] * u[:n - shift] + u[shift:]], axis=0)
            a = jnp.concatenate([a[:shift], a[shift:] * a[:n - shift]], axis=0)
        shift *= 2
    return u


def _lru_kernel(xg_ref, cw_ref, cb_ref, wbd_ref, bri_ref, lam_ref, o_ref, in_s, out_s):
    S = xg_ref.shape[0]
    W = GROUP_WIDTH
    P = SUBLANES
    NT = S // P
    for s in range(2 * W // LANES):
        in_s[s] = xg_ref[:, s * LANES:(s + 1) * LANES]
    log_sig_lam = -_softplus(-lam_ref[...])

    for s in range(W // LANES):
        cols = slice(s * LANES, (s + 1) * LANES)
        xc_all = jnp.concatenate(_phase_conv(in_s.at[s], cw_ref, cb_ref, cols), axis=0)
        w_slab = jnp.concatenate([wbd_ref[cols, cols], wbd_ref[cols, W + s * LANES:W + (s + 1) * LANES]], axis=1)
        ri = _dot(xc_all.astype(BF16), w_slab)
        r = _sigmoid(ri[:, 0:LANES] + bri_ref[:, cols])
        i = _sigmoid(ri[:, LANES:2 * LANES] + bri_ref[:, W + s * LANES:W + (s + 1) * LANES])
        log_a = (LRU_C * r) * log_sig_lam[:, cols]
        a = jnp.exp(log_a)
        u = jnp.sqrt(1.0 - jnp.exp(2.0 * log_a)) * (i * xc_all)
        loc = [u[0:NT]]
        dec = [a[0:NT]]
        for p in range(1, P):
            ap = a[p * NT:(p + 1) * NT]
            loc.append(ap * loc[-1] + u[p * NT:(p + 1) * NT])
            dec.append(ap * dec[-1])
        carry = _shift_rows_down(_scan_rows(dec[-1], loc[-1]))
        for p in range(P):
            h = loc[p] + dec[p] * carry
            out_s[s, pl.ds(p, NT, stride=P), :] = h * _gelu_tanh(in_s[W // LANES + s, pl.ds(p, NT, stride=P), :])
    for s in range(W // LANES):
        o_ref[:, s * LANES:(s + 1) * LANES] = out_s[s]


def _lru(xg, conv_w, conv_b, w_bd, b_ri, lam):
    B, S, _ = xg.shape
    W = GROUP_WIDTH
    full = lambda shape: pl.BlockSpec(shape, lambda b: (0,) * len(shape))
    return pl.pallas_call(
        _lru_kernel,
        grid=(B,),
        in_specs=[pl.BlockSpec((None, S, 2 * W), lambda b: (b, 0, 0)),
                  full((CONV_WIDTH, W)), full((1, W)), full((W, 2 * W)), full((1, 2 * W)), full((1, W))],
        out_specs=pl.BlockSpec((None, S, W), lambda b: (b, 0, 0)),
        out_shape=jax.ShapeDtypeStruct((B, S, W), F32),
        scratch_shapes=[pltpu.VMEM((2 * W // LANES, S, LANES), F32), pltpu.VMEM((W // LANES, S, LANES), F32)],
        compiler_params=_cparams(1),
        name="rglru",
    )(xg, conv_w, conv_b, w_bd, b_ri, lam)


def _sb_kernel(qkv_ref, o_ref, kt_s, v_s, acc_s, later_s):
    i = pl.program_id(1)
    W = GROUP_WIDTH
    TB = SB_BLOCK
    R = N_HEADS * TB

    NW = SB_WINDOW_BLOCKS
    KW = NW * TB
    PAD = (NW - 1) * TB
    S = qkv_ref.shape[0]

    @pl.when(i == 0)
    def _():
        kt_s[:, 0:PAD] = jnp.zeros((W, PAD), BF16)
        for r0 in range(0, S, W):
            kt_s[:, PAD + r0:PAD + r0 + W] = qkv_ref[r0:r0 + W, W:2 * W].astype(F32).T.astype(BF16)
        v_s[0:PAD, :] = jnp.zeros((PAD, W), BF16)
        v_s[PAD:PAD + S, :] = qkv_ref[:, 2 * W:3 * W].astype(BF16)

    lane = lax.broadcasted_iota(jnp.int32, (TB, W), 1)
    heads = [(lane >= h * HEAD_DIM) & (lane < (h + 1) * HEAD_DIM) for h in range(N_HEADS)]
    r_loc = lax.broadcasted_iota(jnp.int32, (R, KW), 0) & (TB - 1)
    c_loc = lax.broadcasted_iota(jnp.int32, (R, KW), 1)
    ur = lax.broadcasted_iota(jnp.int32, (TB, 2 * TB), 0)
    uc = lax.broadcasted_iota(jnp.int32, (TB, 2 * TB), 1)
    tri_ones = jnp.where((ur > uc) | (uc >= TB), 1.0, 0.0).astype(BF16)

    n_chains = o_ref.shape[0]
    blocks = [i + c * (S // TB // n_chains) for c in range(n_chains)]
    qss = []
    for blk in blocks:
        q = qkv_ref[pl.ds(pl.multiple_of(blk * TB, TB), TB), 0:W] * (HEAD_DIM ** -0.5)
        qss.append(jnp.concatenate([jnp.where(hm, q, 0.0) for hm in heads], axis=0).astype(BF16))

    acc_s[...] = jnp.zeros_like(acc_s)
    later_s[...] = jnp.zeros_like(later_s)

    def window(c, n):
        blk = blocks[c]
        first_key = (blk - n * NW - (NW - 1)) * TB
        rows = pl.ds(pl.multiple_of(jnp.maximum(first_key + PAD, 0), TB), KW)
        z = _dot(qss[c], kt_s[:, rows])
        key_abs = first_key + c_loc
        live = (key_abs < blk * TB + r_loc) & (key_abs >= 0)
        sp = _softplus(z)
        lf = jnp.where(live, -sp, 0.0)
        lf16 = lf.astype(BF16)
        order = list(range(NW - 1, -1, -1))
        stacked = jnp.concatenate([lf16[:, b * TB:(b + 1) * TB] for b in order], axis=0)
        cs_all = _dot(stacked, tri_ones)
        offset = later_s[c]
        after = [None] * NW
        for pos, b in enumerate(order):
            cs = cs_all[pos * R:(pos + 1) * R, :]
            after[b] = cs[:, 0:TB] + offset
            offset = offset + cs[:, TB:2 * TB]
        w = jnp.where(live, jnp.exp((z - sp) + jnp.concatenate(after, axis=1)), 0.0)
        acc_s[c] += _dot(w.astype(BF16), v_s[rows, :])
        later_s[c] = offset
        return jnp.where((n + 1) * NW <= blk, jnp.max(offset), SB_EXP_FLOOR)

    def cond(carry):
        return carry[1] > SB_EXP_FLOOR

    def body(carry):
        n = carry[0]
        later_max = window(0, n)
        for c in range(1, n_chains):
            later_max = jnp.maximum(later_max, window(c, n))
        return n + 1, later_max

    lax.while_loop(cond, body, (jnp.int32(0), jnp.float32(0.0)))
    for c in range(n_chains):
        out = acc_s[c, 0:TB, :]
        for h in range(1, N_HEADS):
            out = jnp.where(heads[h], acc_s[c, h * TB:(h + 1) * TB, :], out)
        o_ref[c] = out


def _sb_attention(qkv):
    B, S, _ = qkv.shape
    W = GROUP_WIDTH
    pad = (SB_WINDOW_BLOCKS - 1) * SB_BLOCK
    nc = SB_CHAINS
    rows = N_HEADS * SB_BLOCK
    out = pl.pallas_call(
        _sb_kernel,
        grid=(B, S // SB_BLOCK // nc),
        in_specs=[pl.BlockSpec((None, S, 3 * W), lambda b, i: (b, 0, 0))],
        out_specs=pl.BlockSpec((None, nc, SB_BLOCK, W), lambda b, i: (b, 0, i, 0)),
        out_shape=jax.ShapeDtypeStruct((B, nc, S // nc, W), F32),
        scratch_shapes=[pltpu.VMEM((W, S + pad), BF16), pltpu.VMEM((S + pad, W), BF16),
                        pltpu.VMEM((nc, rows, W), F32), pltpu.VMEM((nc, rows, SB_BLOCK), F32)],
        compiler_params=_cparams(2),
        name="stickbreak",
    )(qkv)
    return out.reshape(B, S, W)


def _ssd_kernel(z_ref, xbc_ref, dt_ref, cw_ref, cb_ref, dtb_ref, alog_ref, dskip_ref, o_ref, xbc_s):
    S = z_ref.shape[0]
    W = GROUP_WIDTH
    L = SSM_CHUNK
    for s in range(3 * W // LANES):
        cols = slice(s * LANES, (s + 1) * LANES)
        xbc_s[s] = _silu(_causal_conv(xbc_ref[:, cols], cw_ref, cb_ref, cols))
    a_row = -jnp.exp(alog_ref[...])

    r_i = lax.broadcasted_iota(jnp.int32, (L, L), 0)
    c_i = lax.broadcasted_iota(jnp.int32, (L, L), 1)
    tri_incl = jnp.where(c_i <= r_i, 1.0, 0.0).astype(BF16)
    lower = c_i <= r_i
    e_r = lax.broadcasted_iota(jnp.int32, (LANES, W), 0)
    e_c = lax.broadcasted_iota(jnp.int32, (LANES, W), 1)
    expand = jnp.where((e_c >= e_r * HEAD_DIM) & (e_c < (e_r + 1) * HEAD_DIM), 1.0, 0.0).astype(BF16)
    lane_l = lax.broadcasted_iota(jnp.int32, (L, LANES), 1)

    def chunk(c, states):
        rows = slice(c * L, (c + 1) * L)
        xs = jnp.concatenate([xbc_s[0, rows, :], xbc_s[1, rows, :]], axis=1)
        dt = _softplus(dt_ref[rows, :] + dtb_ref[...])
        a_dt = dt * a_row
        cs_col = _dot_wide_lhs_rhs(tri_incl, a_dt)
        cs_row = cs_col.T
        cs_full = _dot_wide_lhs(cs_col, expand)
        dt_full = _dot_wide_lhs(dt, expand)
        xd = xs * dt_full
        tot = cs_full[L - 1:L, :]
        xdec = (xd * jnp.exp(tot - cs_full)).astype(BF16)
        xd16 = xd.astype(BF16)
        ys = []
        new_states = []
        for g in range(2):
            gl = slice(g * LANES, (g + 1) * LANES)
            bm = xbc_s[2 + g, rows, :].astype(BF16)
            cm = xbc_s[4 + g, rows, :].astype(BF16)
            cb = _dot_t(cm, bm)
            prev = states[g]
            y_off = _dot(cm, prev.astype(BF16)) * jnp.exp(cs_full[:, gl])
            y_g = y_off
            for hh in range(2):
                h = 2 * g + hh
                seg = jnp.where(lower, cs_col[:, h:h + 1] - cs_row[h:h + 1, :], -jnp.inf)
                y_h = _dot((cb * jnp.exp(seg)).astype(BF16), xd16[:, gl])
                in_head = (lane_l >= hh * HEAD_DIM) & (lane_l < (hh + 1) * HEAD_DIM)
                y_g = y_g + jnp.where(in_head, y_h, 0.0)
            new_states.append(prev * jnp.exp(tot[:, gl]) + _dot_tl(bm, xdec[:, gl]))
            ys.append(y_g)
        y = jnp.concatenate(ys, axis=1) + dskip_ref[...] * xs
        o_ref[rows, :] = y * _silu(z_ref[rows, :])
        return new_states

    states = [jnp.zeros((SSM_STATE, LANES), F32) for _ in range(2)]
    for c in range(S // L):
        states = chunk(c, states)


def _dot_wide_lhs_rhs(m_bf16, x):
    hi, mid, lo = _split3(x)
    return _dot(m_bf16, hi) + _dot(m_bf16, mid) + _dot(m_bf16, lo)


def _ssd(z, xbc, dt, conv_w, conv_b, dt_bias, a_log, d_skip):
    B, S, _ = z.shape
    W = GROUP_WIDTH
    full = lambda shape: pl.BlockSpec(shape, lambda b: (0,) * len(shape))
    return pl.pallas_call(
        _ssd_kernel,
        grid=(B,),
        in_specs=[pl.BlockSpec((None, S, W), lambda b: (b, 0, 0)),
                  pl.BlockSpec((None, S, 3 * W), lambda b: (b, 0, 0)),
                  pl.BlockSpec((None, S, LANES), lambda b: (b, 0, 0)),
                  full((CONV_WIDTH, 3 * W)), full((1, 3 * W)), full((1, LANES)), full((1, LANES)),
                  full((1, W))],
        out_specs=pl.BlockSpec((None, S, W), lambda b: (b, 0, 0)),
        out_shape=jax.ShapeDtypeStruct((B, S, W), F32),
        scratch_shapes=[pltpu.VMEM((3 * W // LANES, S, LANES), F32)],
        compiler_params=_cparams(1),
        name="ssd",
    )(z, xbc, dt, conv_w, conv_b, dt_bias, a_log, d_skip)


def _rope(x, cos, sin):
    lane = lax.broadcasted_iota(jnp.int32, (x.shape[0], LANES), 1)
    first_half = (lane % HEAD_DIM) < (HEAD_DIM // 2)
    halves = []
    for p in range(x.shape[1] // LANES):
        xp = x[:, p * LANES:(p + 1) * LANES]
        fwd = pltpu.roll(xp, HEAD_DIM // 2, 1)
        bwd = pltpu.roll(xp, LANES - HEAD_DIM // 2, 1)
        halves.append(jnp.where(first_half, -bwd, fwd))
    rot = jnp.concatenate(halves, axis=1)
    return x * cos + rot * sin


def _moba_kernel(qkv_ref, cos_ref, sin_ref, o_ref, k_s, vt_s, kmean_s, acc_s, bias_s):
    i = pl.program_id(1)
    W = GROUP_WIDTH
    TB = MOBA_BLOCK
    S = qkv_ref.shape[0]
    NB = S // TB

    @pl.when(i == 0)
    def _():
        for blk in range(NB):
            rs = slice(blk * TB, (blk + 1) * TB)
            kb = _rope(qkv_ref[rs, W:2 * W], cos_ref[rs, :], sin_ref[rs, :])
            for h in range(N_HEADS):
                k_s[h, rs, :] = kb[:, h * HEAD_DIM:(h + 1) * HEAD_DIM].astype(BF16)
            kmean_s[blk:blk + 1, :] = jnp.mean(kb, axis=0, keepdims=True)
            vt_s[blk] =qkv_ref[rs, 2 * W:3 * W].T.astype(BF16)

    rows_i = pl.ds(pl.multiple_of(i * TB, TB), TB)
    q = _rope(qkv_ref[rows_i, 0:W], cos_ref[rows_i, :], sin_ref[rows_i, :])
    lane8 = lax.broadcasted_iota(jnp.int32, (NB, W), 1)
    blk_id = lax.broadcasted_iota(jnp.int32, (NB, TB), 0)
    R = N_HEADS * TB
    key_loc = lax.broadcasted_iota(jnp.int32, (TB, R), 0)
    q_loc = lax.broadcasted_iota(jnp.int32, (TB, R), 1) & (TB - 1)
    kmean = kmean_s[...]
    q_t = q.T
    qt_hi, qt_lo = _split2(q_t)
    scale = HEAD_DIM ** -0.5

    km_all = jnp.concatenate(
        [jnp.where((lane8 >= h * HEAD_DIM) & (lane8 < (h + 1) * HEAD_DIM), kmean, 0.0) for h in range(N_HEADS)],
        axis=0)
    km_hi, km_lo = _split2(km_all)
    gate_all = _dot(km_hi, qt_hi) + _dot(km_hi, qt_lo) + _dot(km_lo, qt_hi)
    qhs = []
    for h in range(N_HEADS):
        gate = gate_all[h * NB:(h + 1) * NB, :]
        cnt = jnp.zeros((NB, TB), F32)
        for jp in range(NB):
            row = gate[jp:jp + 1, :]
            beats = (row > gate) | ((row == gate) & (blk_id > jp))
            cnt = cnt + jnp.where(beats, jnp.where(jp < i, 1.0, 0.0), 0.0)
        selected = (cnt < float(MOBA_TOPK)) & (blk_id < i)
        bias_s[:, h * TB:(h + 1) * TB] = jnp.where(selected, 0.0, NEG_BIG)
        qhs.append((q_t[h * HEAD_DIM:(h + 1) * HEAD_DIM, :] * scale).astype(BF16))

    def scores(rows):
        return jnp.concatenate([_dot(k_s[h, rows, :], qhs[h]) for h in range(N_HEADS)], axis=1)

    s = jnp.where(key_loc <= q_loc, scores(rows_i), NEG_BIG)
    m = jnp.max(s, axis=0, keepdims=True)
    p = jnp.exp(s - m)
    l = jnp.sum(p, axis=0, keepdims=True)
    def weighted_values(j, p):
        p16 = p.astype(BF16)
        return jnp.concatenate(
            [_dot(vt_s[j, h * HEAD_DIM:(h + 1) * HEAD_DIM, :], p16[:, h * TB:(h + 1) * TB]) for h in range(N_HEADS)],
            axis=1)

    acc_s[...] = weighted_values(i, p)

    def body(j, carry):
        m, l = carry
        rows = pl.ds(pl.multiple_of(j * TB, TB), TB)
        s = scores(rows) + bias_s[pl.ds(j, 1), :]
        m_new = jnp.maximum(m, jnp.max(s, axis=0, keepdims=True))
        alpha = jnp.exp(m - m_new)
        p = jnp.exp(s - m_new)
        l = alpha * l + jnp.sum(p, axis=0, keepdims=True)
        acc_s[...] = alpha * acc_s[...] + weighted_values(j, p)
        return m_new, l

    m, l = lax.fori_loop(0, i, body, (m, l))
    outs = acc_s[...] / l
    o_ref[...] = jnp.concatenate([outs[:, h * TB:(h + 1) * TB].T for h in range(N_HEADS)], axis=1)


def _moba(qkv, cos, sin):
    B, S, _ = qkv.shape
    W = GROUP_WIDTH
    nb = S // MOBA_BLOCK
    return pl.pallas_call(
        _moba_kernel,
        grid=(B, S // MOBA_BLOCK),
        in_specs=[pl.BlockSpec((None, S, 3 * W), lambda b, i: (b, 0, 0)),
                  pl.BlockSpec((S, W), lambda b, i: (0, 0)),
                  pl.BlockSpec((S, W), lambda b, i: (0, 0))],
        out_specs=pl.BlockSpec((None, MOBA_BLOCK, W), lambda b, i: (b, i, 0)),
        out_shape=jax.ShapeDtypeStruct((B, S, W), F32),
        scratch_shapes=[pltpu.VMEM((N_HEADS, S, HEAD_DIM), BF16), pltpu.VMEM((nb, W, MOBA_BLOCK), BF16),
                        pltpu.VMEM((nb, W), F32), pltpu.VMEM((HEAD_DIM, N_HEADS * MOBA_BLOCK), F32),
                        pltpu.VMEM((nb, N_HEADS * MOBA_BLOCK), F32)],
        compiler_params=_cparams(2),
        name="moba",
    )(qkv, cos, sin)


def _outproj_kernel(ya_ref, yb_ref, yc_ref, yd_ref, gg_ref, w_ref, x_ref, o_ref):
    W = GROUP_WIDTH
    acc = x_ref[...]
    for g, y_ref in enumerate((ya_ref, yb_ref, yc_ref, yd_ref)):
        yn = _rmsnorm(y_ref[...], gg_ref[g:g + 1, :]).astype(BF16)
        acc = acc + _dot(yn, w_ref[g * W:(g + 1) * W, :].astype(BF16))
    o_ref[...] = acc


def _outproj(ys, gg, w_out, layer, x2d, tm=1024):
    T = x2d.shape[0]
    W = GROUP_WIDTH
    return pl.pallas_call(
        _outproj_kernel,
        grid=(T // tm,),
        in_specs=[pl.BlockSpec((tm, W), lambda i: (i, 0))] * 4
                 + [pl.BlockSpec((4, W), lambda i: (0, 0)),
                    pl.BlockSpec((None, 4 * W, D_MODEL), lambda i: (layer, 0, 0), pipeline_mode=pl.Buffered(1)),
                    pl.BlockSpec((tm, D_MODEL), lambda i: (i, 0))],
        out_specs=pl.BlockSpec((tm, D_MODEL), lambda i: (i, 0)),
        out_shape=jax.ShapeDtypeStruct((T, D_MODEL), F32),
        compiler_params=_cparams(1),
        name="outproj",
    )(*ys, gg, w_out, x2d)


def _memkv_kernel(m_ref, g_ref, w_ref, kt_ref, v_ref):
    mn = _rmsnorm(m_ref[...], g_ref[...]).astype(BF16)
    kv = _dot(mn, w_ref[...].astype(BF16))
    kt_ref[...] = kv[:, 0:D_MODEL].T.astype(BF16)
    v_ref[...] = kv[:, D_MODEL:2 * D_MODEL].astype(BF16)


def _memkv(mem, g, wkv, layer):
    B, M, _ = mem.shape
    return pl.pallas_call(
        _memkv_kernel,
        grid=(B,),
        in_specs=[pl.BlockSpec((None, M, D_MODEL), lambda b: (b, 0, 0)),
                  pl.BlockSpec((1, D_MODEL), lambda b: (0, 0)),
                  pl.BlockSpec((None, D_MODEL, 2 * D_MODEL), lambda b: (layer, 0, 0),
                               pipeline_mode=pl.Buffered(1))],
        out_specs=[pl.BlockSpec((None, D_MODEL, M), lambda b: (b, 0, 0)),
                   pl.BlockSpec((None, M, D_MODEL), lambda b: (b, 0, 0))],
        out_shape=[jax.ShapeDtypeStruct((B, D_MODEL, M), BF16), jax.ShapeDtypeStruct((B, M, D_MODEL), BF16)],
        compiler_params=_cparams(1),
        name="memkv",
    )(mem, g, wkv)


def _xattn_kernel(x_ref, g_ref, wq_ref, kt_ref, v_ref, wo_ref, o_ref):
    x = x_ref[...]
    h = _rmsnorm(x, g_ref[...]).astype(BF16)
    q = (_dot(h, wq_ref[...].astype(BF16)) * (XATTN_HEAD_DIM ** -0.5)).astype(BF16)
    acc = x
    for hd in range(XATTN_HEADS):
        cols = slice(hd * XATTN_HEAD_DIM, (hd + 1) * XATTN_HEAD_DIM)
        s = _dot(q[:, cols], kt_ref[cols, :])
        p = jnp.exp(s - jnp.max(s, axis=-1, keepdims=True))
        p = p / jnp.sum(p, axis=-1, keepdims=True)
        o = _dot(p.astype(BF16), v_ref[:, cols]).astype(BF16)
        acc = acc + _dot(o, wo_ref[cols, :].astype(BF16))
    o_ref[...] = acc


def _xattn(x3d, g, wq, kt, v, wo, layer, tm=1024):
    B, S, _ = x3d.shape
    M = v.shape[1]
    weight = pl.BlockSpec((None, D_MODEL, D_MODEL), lambda b, i: (layer, 0, 0), pipeline_mode=pl.Buffered(1))
    return pl.pallas_call(
        _xattn_kernel,
        grid=(B, S // tm),
        in_specs=[pl.BlockSpec((None, tm, D_MODEL), lambda b, i: (b, i, 0)),
                  pl.BlockSpec((1, D_MODEL), lambda b, i: (0, 0)),
                  weight,
                  pl.BlockSpec((None, D_MODEL, M), lambda b, i: (b, 0, 0)),
                  pl.BlockSpec((None, M, D_MODEL), lambda b, i: (b, 0, 0)),
                  weight],
        out_specs=pl.BlockSpec((None, tm, D_MODEL), lambda b, i: (b, i, 0)),
        out_shape=jax.ShapeDtypeStruct((B, S, D_MODEL), F32),
        compiler_params=_cparams(2),
        name="xattn",
    )(x3d, g, wq, kt, v, wo)


def _moe_kernel(x_ref, g_ref, wr_ref, br_ref, w1_ref, w3_ref, w2_ref, fg_ref, o_ref,
                t_s, comb_s, acc_s, xkeep_s, *, final_norm):
    tile = pl.program_id(0)
    grp = pl.program_id(1)
    tm = x_ref.shape[0]
    slot = lax.bitwise_and(tile, 1)

    def route(dst):
        t = _rmsnorm(x_ref[...], g_ref[...])
        t_s[dst] = t.astype(BF16)
        t_hi, t_lo = _split2(t)
        logits = (_dot(t_hi, wr_ref[0]) + _dot(t_hi, wr_ref[1]) + _dot(t_lo, wr_ref[0])) + br_ref[...]
        lt = logits.T
        gsl = SUBLANES * (MOE_EXPERTS // SUBLANES)
        g_row = lax.broadcasted_iota(jnp.int32, (SUBLANES, tm), 0).astype(F32)
        lg = jnp.where(g_row < float(MOE_GROUPS), lt[gsl:gsl + SUBLANES, :], NEG_BIG)
        gmax = jnp.max(lg, axis=0, keepdims=True)
        pg_top = 1.0 / jnp.sum(jnp.exp(lg - gmax), axis=0, keepdims=True)
        g_idx = jnp.min(jnp.where(lg == gmax, g_row, 1e9), axis=0, keepdims=True)
        e_row = lax.broadcasted_iota(jnp.int32, (MOE_EXPERTS, tm), 0).astype(F32)
        in_grp = jnp.floor(e_row * (1.0 / MOE_EPG)) == g_idx
        le = jnp.where(in_grp, lt[0:MOE_EXPERTS, :], NEG_BIG)
        e1 = jnp.max(le, axis=0, keepdims=True)
        i1 = jnp.min(jnp.where(in_grp & (le == e1), e_row, 1e9), axis=0, keepdims=True)
        le2 = jnp.where(e_row == i1, NEG_BIG, le)
        e2 = jnp.max(le2, axis=0, keepdims=True)
        i2 = jnp.min(jnp.where(in_grp & (e_row != i1) & (le2 == e2), e_row, 1e9), axis=0, keepdims=True)
        r2 = jnp.exp(e2 - e1)
        w_first = 1.0 / (1.0 + r2)
        w_second = r2 / (1.0 + r2)
        comb_t = pg_top * (jnp.where(e_row == i1, w_first, 0.0) + jnp.where(e_row == i2, w_second, 0.0))
        comb_s[dst] = jnp.concatenate([comb_t, jnp.zeros((LANES - MOE_EXPERTS, tm), F32)], axis=0).T

    def experts():
        t = t_s[slot]
        comb = comb_s[slot]
        lane = lax.broadcasted_iota(jnp.int32, (tm, LANES), 1)
        upd = jnp.zeros((tm, D_MODEL), F32)
        for e in range(MOE_EPG):
            n = grp * MOE_EPG + e
            c = jnp.sum(jnp.where(lane == n, comb, 0.0), axis=-1, keepdims=True)
            hid = _silu(_dot(t, w1_ref[e])) * _dot(t, w3_ref[e])
            upd = upd + _dot((hid * c).astype(BF16), w2_ref[e])
        return upd

    @pl.when((tile == 0) & (grp == 0))
    def _():
        route(0)

    @pl.when(grp == 0)
    def _():
        xkeep_s[...] = x_ref[...]
        acc_s[...] = experts()

    @pl.when((grp > 0) & (grp < MOE_GROUPS - 1))
    def _():
        acc_s[...] += experts()

    @pl.when(grp == MOE_GROUPS - 1)
    def _():
        y = xkeep_s[...] + (acc_s[...] + experts())
        route(1 - slot)
        if final_norm:
            y = _rmsnorm(y, fg_ref[...])
        o_ref[...] = y


def _moe(x2d, g, w_router, b_router, w1, w3, w2, layer, final_g, final_norm, tm=1024):
    T = x2d.shape[0]
    n_tiles = T // tm

    def x_window(i, e):
        return jnp.minimum(i + e // (MOE_GROUPS - 1), n_tiles - 1), 0

    return pl.pallas_call(
        functools.partial(_moe_kernel, final_norm=final_norm),
        grid=(n_tiles, MOE_GROUPS),
        in_specs=[pl.BlockSpec((tm, D_MODEL), x_window),
                  pl.BlockSpec((1, D_MODEL), lambda i, e: (0, 0)),
                  pl.BlockSpec((2, D_MODEL, LANES), lambda i, e: (0, 0, 0)),
                  pl.BlockSpec((1, LANES), lambda i, e: (0, 0)),
                  pl.BlockSpec((None, MOE_EPG, D_MODEL, MOE_FF), lambda i, e: (layer, e, 0, 0)),
                  pl.BlockSpec((None, MOE_EPG, D_MODEL, MOE_FF), lambda i, e: (layer, e, 0, 0)),
                  pl.BlockSpec((None, MOE_EPG, MOE_FF, D_MODEL), lambda i, e: (layer, e, 0, 0)),
                  pl.BlockSpec((1, D_MODEL), lambda i, e: (0, 0))],
        out_specs=pl.BlockSpec((tm, D_MODEL), lambda i, e: (i, 0)),
        out_shape=jax.ShapeDtypeStruct((T, D_MODEL), F32),
        scratch_shapes=[pltpu.VMEM((2, tm, D_MODEL), BF16), pltpu.VMEM((2, tm, LANES), F32),
                        pltpu.VMEM((tm, D_MODEL), F32), pltpu.VMEM((tm, D_MODEL), F32)],
        compiler_params=_cparams(2),
        name="moe",
    )(x2d, g, w_router, b_router, w1, w3, w2, final_g)


def _pad_lanes(v, width=LANES):
    return jnp.pad(v, (0, width - v.shape[0]))[None, :]


def _block_diag(w):
    H, n, _ = w.shape
    eye = jnp.eye(H, dtype=w.dtype)
    return (eye[:, None, :, None] * w[:, :, None, :]).reshape(H * n, H * n)


def _rope_tables(S):
    half = HEAD_DIM // 2
    inv_freq = ROPE_THETA ** (-jnp.arange(half, dtype=F32) / half)
    ang = jnp.arange(S, dtype=F32)[:, None] * inv_freq[None, :]
    reps = GROUP_WIDTH // half
    return jnp.tile(jnp.cos(ang), (1, reps)), jnp.tile(jnp.sin(ang), (1, reps))


def kernel(x, mem, mix_norm_g, w_in, lru_conv_w, lru_conv_b, lru_wr, lru_br, lru_wi, lru_bi, lru_lambda, ssm_conv_w, ssm_conv_b, ssm_dt_bias, ssm_a_log, ssm_d, group_norm_g, w_out, xattn_norm_g, mem_norm_g, xattn_wq, xattn_wkv, xattn_wo, ffn_norm_g, router_group_w, router_group_b, router_expert_w, router_expert_b, expert_w1, expert_w3, expert_w2, final_norm_g):
    B, S, D = x.shape
    T = B * S
    depth = w_in.shape[0]
    W = GROUP_WIDTH
    cos, sin = _rope_tables(S)
    x2d = x.reshape(T, D)
    w1_bf, w3_bf, w2_bf = expert_w1.astype(BF16), expert_w3.astype(BF16), expert_w2.astype(BF16)
    for l in range(depth):
        lru_xg, sb_qkv, ssm_z, ssm_xbc, ssm_dt, mb_qkv = _inproj(x2d, mix_norm_g[l][None, :], w_in, l)

        w_bd = jnp.concatenate([_block_diag(lru_wr[l]), _block_diag(lru_wi[l])], axis=1).astype(BF16)
        b_ri = jnp.concatenate([lru_br[l], lru_bi[l]])[None, :]
        y_a = _lru(lru_xg.reshape(B, S, 2 * W), lru_conv_w[l], lru_conv_b[l][None, :], w_bd, b_ri,
                   lru_lambda[l][None, :])
        y_b = _sb_attention(sb_qkv.reshape(B, S, 3 * W))
        y_c = _ssd(ssm_z.reshape(B, S, W), ssm_xbc.reshape(B, S, 3 * W), ssm_dt.reshape(B, S, LANES),
                   ssm_conv_w[l], ssm_conv_b[l][None, :], _pad_lanes(ssm_dt_bias[l]), _pad_lanes(ssm_a_log[l]),
                   jnp.repeat(ssm_d[l], HEAD_DIM)[None, :])
        y_d = _moba(mb_qkv.reshape(B, S, 3 * W), cos, sin)
        x2d = _outproj([y.reshape(T, W) for y in (y_a, y_b, y_c, y_d)], group_norm_g[l].reshape(4, W),
                       w_out, l, x2d)

        mem_kt, mem_v = _memkv(mem, mem_norm_g[l][None, :], xattn_wkv, l)
        x2d = _xattn(x2d.reshape(B, S, D), xattn_norm_g[l][None, :], xattn_wq, mem_kt, mem_v, xattn_wo,
                     l).reshape(T, D)

        w_r = jnp.pad(jnp.concatenate([router_expert_w[l], router_group_w[l]], axis=1),
                      ((0, 0), (0, LANES - MOE_EXPERTS - MOE_GROUPS)))
        w_r_hi = w_r.astype(BF16)
        w_r_lo = (w_r - w_r_hi.astype(F32)).astype(BF16)
        b_r = _pad_lanes(jnp.concatenate([router_expert_b[l], router_group_b[l]]))
        x2d = _moe(x2d, ffn_norm_g[l][None, :], jnp.stack([w_r_hi, w_r_lo]), b_r, w1_bf, w3_bf, w2_bf, l,
                   final_norm_g[None, :], final_norm=(l == depth - 1))
    return x2d.reshape(B, S, D)
```

```python
import functools
import math

import jax
import jax.numpy as jnp
from jax import lax
from jax.experimental import pallas as pl
from jax.experimental.pallas import tpu as pltpu

F32 = jnp.float32
BF16 = jnp.bfloat16

D_MODEL = 1024
GROUP_WIDTH = 256
HEAD_DIM = 64
N_HEADS = 4
NORM_EPS = 1e-6
CONV_WIDTH = 4
LRU_C = 8.0
SB_BLOCK = 128
SB_WINDOW_BLOCKS = 3
SB_CHAINS = 8
SSM_CHUNK = 128
SSM_STATE = 128
MOBA_BLOCK = 256
MOBA_TOPK = 3
ROPE_THETA = 10000.0
XATTN_HEADS = 4
XATTN_HEAD_DIM = 256
MEM_LEN = 256
MOE_GROUPS = 4
MOE_EPG = 4
MOE_EXPERTS = 16
MOE_FF = 256
LANES = 128
SUBLANES = 8
NEG_BIG = -1e30
SB_EXP_FLOOR = -104.0
IN_OUT_WIDTHS = (512, 768, 256, 768, LANES, 768)
IN_MAIN = 512 + 768 + 256 + 768
IN_OUT_DTYPES = (F32, BF16, F32, F32, F32, F32)
VMEM_LIMIT = 56 * 1024 * 1024


def _cparams(n_axes):
    return pltpu.CompilerParams(dimension_semantics=("arbitrary",) * n_axes,
                                vmem_limit_bytes=VMEM_LIMIT)


def _dot(a, b):
    return jnp.dot(a, b, preferred_element_type=F32)


def _dot_t(a, b):
    return lax.dot_general(a, b, (((1,), (1,)), ((), ())), preferred_element_type=F32)


def _dot_tl(a, b):
    return lax.dot_general(a, b, (((0,), (0,)), ((), ())), preferred_element_type=F32)


def _split2(x):
    hi = x.astype(BF16)
    lo = (x - hi.astype(F32)).astype(BF16)
    return hi, lo


def _split3(x):
    hi = x.astype(BF16)
    r = x - hi.astype(F32)
    mid = r.astype(BF16)
    lo = (r - mid.astype(F32)).astype(BF16)
    return hi, mid, lo


def _dot_wide_lhs(x, m_bf16, parts=3):
    pieces = _split3(x) if parts == 3 else _split2(x)
    out = _dot(pieces[0], m_bf16)
    for p in pieces[1:]:
        out = out + _dot(p, m_bf16)
    return out


def _rmsnorm(x, g):
    return x * lax.rsqrt(jnp.mean(x * x, axis=-1, keepdims=True) + NORM_EPS) * g


def _softplus(x):
    return jnp.maximum(x, 0.0) + jnp.log(1.0 + jnp.exp(-jnp.abs(x)))


def _sigmoid(x):
    return 1.0 / (1.0 + jnp.exp(-x))


def _silu(x):
    return x * _sigmoid(x)


def _gelu_tanh(x):
    return 0.5 * x * (1.0 + jnp.tanh(math.sqrt(2.0 / math.pi) * (x + 0.044715 * (x * x * x))))


def _shift_rows_down(x):
    rows = lax.broadcasted_iota(jnp.int32, x.shape, 0)
    return jnp.where(rows >= 1, pltpu.roll(x, 1, 0), 0.0)


def _causal_conv(x, w_ref, b_ref, cols):
    def taps(v, mask_rows):
        y = v * w_ref[CONV_WIDTH - 1:CONV_WIDTH, cols] + b_ref[:, cols]
        for s in range(1, CONV_WIDTH):
            vs = pltpu.roll(v, s, 0)
            if mask_rows is not None:
                vs = jnp.where(mask_rows >= s, vs, 0.0)
            y = y + vs * w_ref[CONV_WIDTH - 1 - s:CONV_WIDTH - s, cols]
        return y

    head = x[0:SUBLANES, :]
    y_head = taps(head, lax.broadcasted_iota(jnp.int32, head.shape, 0))
    return jnp.concatenate([y_head, taps(x, None)[SUBLANES:, :]], axis=0)


def _phase_conv(slab_ref, w_ref, b_ref, cols):
    P = SUBLANES
    nt = slab_ref.shape[0] // P
    x = [slab_ref[pl.ds(p, nt, stride=P), :] for p in range(P)]
    prev = {p: _shift_rows_down(x[p]) for p in range(P - CONV_WIDTH + 1, P)}
    out = []
    for p in range(P):
        y = x[p] * w_ref[CONV_WIDTH - 1:CONV_WIDTH, cols] + b_ref[:, cols]
        for k in range(1, CONV_WIDTH):
            src = x[p - k] if p - k >= 0 else prev[p - k + P]
            y = y + src * w_ref[CONV_WIDTH - 1 - k:CONV_WIDTH - k, cols]
        out.append(y)
    return out


def _inproj_kernel(x_ref, g_ref, w_ref, *refs):
    o_refs, w_s = refs[:-1], refs[-1]

    @pl.when(pl.program_id(0) == 0)
    def _():
        for c0 in range(0, IN_MAIN, 2 * LANES):
            w_s[:, c0:c0 + 2 * LANES] = w_ref[:, c0:c0 + 2 * LANES].astype(BF16)
        dt_tile = w_ref[:, IN_MAIN:IN_MAIN + LANES]
        lane = lax.broadcasted_iota(jnp.int32, dt_tile.shape, 1)
        w_s[:, IN_MAIN:IN_MAIN + LANES] = jnp.where(lane < N_HEADS, dt_tile, 0.0).astype(BF16)
        w_s[:, IN_MAIN + LANES:] = w_ref[:, IN_MAIN + N_HEADS:].astype(BF16)

    h = _rmsnorm(x_ref[...], g_ref[...]).astype(BF16)
    off = 0
    for o_ref, width in zip(o_refs, IN_OUT_WIDTHS):
        o_ref[...] = _dot(h, w_s[:, off:off + width]).astype(o_ref.dtype)
        off += width


def _inproj(x2d, g, w_in, layer, tm=1024):
    T = x2d.shape[0]
    n_in = w_in.shape[-1]
    return pl.pallas_call(
        _inproj_kernel,
        grid=(T // tm,),
        in_specs=[pl.BlockSpec((tm, D_MODEL), lambda i: (i, 0)),
                  pl.BlockSpec((1, D_MODEL), lambda i: (0, 0)),
                  pl.BlockSpec((None, D_MODEL, n_in), lambda i: (layer, 0, 0), pipeline_mode=pl.Buffered(1))],
        out_specs=[pl.BlockSpec((tm, w), lambda i: (i, 0)) for w in IN_OUT_WIDTHS],
        out_shape=[jax.ShapeDtypeStruct((T, w), dt) for w, dt in zip(IN_OUT_WIDTHS, IN_OUT_DTYPES)],
        scratch_shapes=[pltpu.VMEM((D_MODEL, sum(IN_OUT_WIDTHS)), BF16)],
        compiler_params=_cparams(1),
        name="inproj",
    )(x2d, g, w_in)


def _scan_rows(a, u):
    n = a.shape[0]
    rows = lax.broadcasted_iota(jnp.int32, a.shape, 0)
    shift = 1
    while shift < n:
        if shift < SUBLANES:
            keep = rows >= shift
            a_s = jnp.where(keep, pltpu.roll(a, shift, 0), 1.0)
            u_s = jnp.where(keep, pltpu.roll(u, shift, 0), 0.0)
            u = a * u_s + u
            a = a * a_s
        else:
            u = jnp.concatenate([u[:shift], a[shift:] * u[:n - shift] + u[shift:]], axis=0)
            a = jnp.concatenate([a[:shift], a[shift:] * a[:n - shift]], axis=0)
        shift *= 2
    return u


def _lru_kernel(xg_ref, cw_ref, cb_ref, wbd_ref, bri_ref, lam_ref, o_ref, in_s, out_s):
    S = xg_ref.shape[0]
    W = GROUP_WIDTH
    P = SUBLANES
    NT = S // P
    for s in range(2 * W // LANES):
        in_s[s] = xg_ref[:, s * LANES:(s + 1) * LANES]
    log_sig_lam = -_softplus(-lam_ref[...])

    for s in range(W // LANES):
        cols = slice(s * LANES, (s + 1) * LANES)
        xc_all = jnp.concatenate(_phase_conv(in_s.at[s], cw_ref, cb_ref, cols), axis=0)
        w_slab = jnp.concatenate([wbd_ref[cols, cols], wbd_ref[cols, W + s * LANES:W + (s + 1) * LANES]], axis=1)
        ri = _dot(xc_all.astype(BF16), w_slab)
        r = _sigmoid(ri[:, 0:LANES] + bri_ref[:, cols])
        i = _sigmoid(ri[:, LANES:2 * LANES] + bri_ref[:, W + s * LANES:W + (s + 1) * LANES])
        log_a = (LRU_C * r) * log_sig_lam[:, cols]
        a = jnp.exp(log_a)
        u = jnp.sqrt(1.0 - jnp.exp(2.0 * log_a)) * (i * xc_all)
        loc = [u[0:NT]]
        dec = [a[0:NT]]
        for p in range(1, P):
            ap = a[p * NT:(p + 1) * NT]
            loc.append(ap * loc[-1] + u[p * NT:(p + 1) * NT])
            dec.append(ap * dec[-1])
        carry = _shift_rows_down(_scan_rows(dec[-1], loc[-1]))
        for p in range(P):
            h = loc[p] + dec[p] * carry
            out_s[s, pl.ds(p, NT, stride=P), :] = h * _gelu_tanh(in_s[W // LANES + s, pl.ds(p, NT, stride=P), :])
    for s in range(W // LANES):
        o_ref[:, s * LANES:(s + 1) * LANES] = out_s[s]


def _lru(xg, conv_w, conv_b, w_bd, b_ri, lam):
    B, S, _ = xg.shape
    W = GROUP_WIDTH
    full = lambda shape: pl.BlockSpec(shape, lambda b: (0,) * len(shape))
    return pl.pallas_call(
        _lru_kernel,
        grid=(B,),
        in_specs=[pl.BlockSpec((None, S, 2 * W), lambda b: (b, 0, 0)),
                  full((CONV_WIDTH, W)), full((1, W)), full((W, 2 * W)), full((1, 2 * W)), full((1, W))],
        out_specs=pl.BlockSpec((None, S, W), lambda b: (b, 0, 0)),
        out_shape=jax.ShapeDtypeStruct((B, S, W), F32),
        scratch_shapes=[pltpu.VMEM((2 * W // LANES, S, LANES), F32), pltpu.VMEM((W // LANES, S, LANES), F32)],
        compiler_params=_cparams(1),
        name="rglru",
    )(xg, conv_w, conv_b, w_bd, b_ri, lam)


def _sb_kernel(qkv_ref, o_ref, kt_s, v_s, acc_s, later_s):
    i = pl.program_id(1)
    W = GROUP_WIDTH
    TB = SB_BLOCK
    R = N_HEADS * TB

    NW = SB_WINDOW_BLOCKS
    KW = NW * TB
    PAD = (NW - 1) * TB
    S = qkv_ref.shape[0]

    @pl.when(i == 0)
    def _():
        kt_s[:, 0:PAD] = jnp.zeros((W, PAD), BF16)
        for r0 in range(0, S, W):
            kt_s[:, PAD + r0:PAD + r0 + W] = qkv_ref[r0:r0 + W, W:2 * W].astype(F32).T.astype(BF16)
        v_s[0:PAD, :] = jnp.zeros((PAD, W), BF16)
        v_s[PAD:PAD + S, :] = qkv_ref[:, 2 * W:3 * W].astype(BF16)

    lane = lax.broadcasted_iota(jnp.int32, (TB, W), 1)
    heads = [(lane >= h * HEAD_DIM) & (lane < (h + 1) * HEAD_DIM) for h in range(N_HEADS)]
    r_loc = lax.broadcasted_iota(jnp.int32, (R, KW), 0) & (TB - 1)
    c_loc = lax.broadcasted_iota(jnp.int32, (R, KW), 1)
    ur = lax.broadcasted_iota(jnp.int32, (TB, 2 * TB), 0)
    uc = lax.broadcasted_iota(jnp.int32, (TB, 2 * TB), 1)
    tri_ones = jnp.where((ur > uc) | (uc >= TB), 1.0, 0.0).astype(BF16)

    n_chains = o_ref.shape[0]
    blocks = [i + c * (S // TB // n_chains) for c in range(n_chains)]
    qss = []
    for blk in blocks:
        q = qkv_ref[pl.ds(pl.multiple_of(blk * TB, TB), TB), 0:W] * (HEAD_DIM ** -0.5)
        qss.append(jnp.concatenate([jnp.where(hm, q, 0.0) for hm in heads], axis=0).astype(BF16))

    acc_s[...] = jnp.zeros_like(acc_s)
    later_s[...] = jnp.zeros_like(later_s)

    def window(c, n):
        blk = blocks[c]
        first_key = (blk - n * NW - (NW - 1)) * TB
        rows = pl.ds(pl.multiple_of(jnp.maximum(first_key + PAD, 0), TB), KW)
        z = _dot(qss[c], kt_s[:, rows])
        key_abs = first_key + c_loc
        live = (key_abs < blk * TB + r_loc) & (key_abs >= 0)
        sp = _softplus(z)
        lf = jnp.where(live, -sp, 0.0)
        lf16 = lf.astype(BF16)
        order = list(range(NW - 1, -1, -1))
        stacked = jnp.concatenate([lf16[:, b * TB:(b + 1) * TB] for b in order], axis=0)
        cs_all = _dot(stacked, tri_ones)
        offset = later_s[c]
        after = [None] * NW
        for pos, b in enumerate(order):
            cs = cs_all[pos * R:(pos + 1) * R, :]
            after[b] = cs[:, 0:TB] + offset
            offset = offset + cs[:, TB:2 * TB]
        w = jnp.where(live, jnp.exp((z - sp) + jnp.concatenate(after, axis=1)), 0.0)
        acc_s[c] += _dot(w.astype(BF16), v_s[rows, :])
        later_s[c] = offset
        return jnp.where((n + 1) * NW <= blk, jnp.max(offset), SB_EXP_FLOOR)

    def cond(carry):
        return carry[1] > SB_EXP_FLOOR

    def body(carry):
        n = carry[0]
        later_max = window(0, n)
        for c in range(1, n_chains):
            later_max = jnp.maximum(later_max, window(c, n))
        return n + 1, later_max

    lax.while_loop(cond, body, (jnp.int32(0), jnp.float32(0.0)))
    for c in range(n_chains):
        out = acc_s[c, 0:TB, :]
        for h in range(1, N_HEADS):
            out = jnp.where(heads[h], acc_s[c, h * TB:(h + 1) * TB, :], out)
        o_ref[c] = out


def _sb_attention(qkv):
    B, S, _ = qkv.shape
    W = GROUP_WIDTH
    pad = (SB_WINDOW_BLOCKS - 1) * SB_BLOCK
    nc = SB_CHAINS
    rows = N_HEADS * SB_BLOCK
    out = pl.pallas_call(
        _sb_kernel,
        grid=(B, S // SB_BLOCK // nc),
        in_specs=[pl.BlockSpec((None, S, 3 * W), lambda b, i: (b, 0, 0))],
        out_specs=pl.BlockSpec((None, nc, SB_BLOCK, W), lambda b, i: (b, 0, i, 0)),
        out_shape=jax.ShapeDtypeStruct((B, nc, S // nc, W), F32),
        scratch_shapes=[pltpu.VMEM((W, S + pad), BF16), pltpu.VMEM((S + pad, W), BF16),
                        pltpu.VMEM((nc, rows, W), F32), pltpu.VMEM((nc, rows, SB_BLOCK), F32)],
        compiler_params=_cparams(2),
        name="stickbreak",
    )(qkv)
    return out.reshape(B, S, W)


def _ssd_kernel(z_ref, xbc_ref, dt_ref, cw_ref, cb_ref, dtb_ref, alog_ref, dskip_ref, o_ref, xbc_s):
    S = z_ref.shape[0]
    W = GROUP_WIDTH
    L = SSM_CHUNK
    for s in range(3 * W // LANES):
        cols = slice(s * LANES, (s + 1) * LANES)
        xbc_s[s] = _silu(_causal_conv(xbc_ref[:, cols], cw_ref, cb_ref, cols))
    a_row = -jnp.exp(alog_ref[...])

    r_i = lax.broadcasted_iota(jnp.int32, (L, L), 0)
    c_i = lax.broadcasted_iota(jnp.int32, (L, L), 1)
    tri_incl = jnp.where(c_i <= r_i, 1.0, 0.0).astype(BF16)
    lower = c_i <= r_i
    e_r = lax.broadcasted_iota(jnp.int32, (LANES, W), 0)
    e_c = lax.broadcasted_iota(jnp.int32, (LANES, W), 1)
    expand = jnp.where((e_c >= e_r * HEAD_DIM) & (e_c < (e_r + 1) * HEAD_DIM), 1.0, 0.0).astype(BF16)
    lane_l = lax.broadcasted_iota(jnp.int32, (L, LANES), 1)

    def chunk(c, states):
        rows = slice(c * L, (c + 1) * L)
        xs = jnp.concatenate([xbc_s[0, rows, :], xbc_s[1, rows, :]], axis=1)
        dt = _softplus(dt_ref[rows, :] + dtb_ref[...])
        a_dt = dt * a_row
        cs_col = _dot_wide_lhs_rhs(tri_incl, a_dt)
        cs_row = cs_col.T
        cs_full = _dot_wide_lhs(cs_col, expand)
        dt_full = _dot_wide_lhs(dt, expand)
        xd = xs * dt_full
        tot = cs_full[L - 1:L, :]
        xdec = (xd * jnp.exp(tot - cs_full)).astype(BF16)
        xd16 = xd.astype(BF16)
        ys = []
        new_states = []
        for g in range(2):
            gl = slice(g * LANES, (g + 1) * LANES)
            bm = xbc_s[2 + g, rows, :].astype(BF16)
            cm = xbc_s[4 + g, rows, :].astype(BF16)
            cb = _dot_t(cm, bm)
            prev = states[g]
            y_off = _dot(cm, prev.astype(BF16)) * jnp.exp(cs_full[:, gl])
            y_g = y_off
            for hh in range(2):
                h = 2 * g + hh
                seg = jnp.where(lower, cs_col[:, h:h + 1] - cs_row[h:h + 1, :], -jnp.inf)
                y_h = _dot((cb * jnp.exp(seg)).astype(BF16), xd16[:, gl])
                in_head = (lane_l >= hh * HEAD_DIM) & (lane_l < (hh + 1) * HEAD_DIM)
                y_g = y_g + jnp.where(in_head, y_h, 0.0)
            new_states.append(prev * jnp.exp(tot[:, gl]) + _dot_tl(bm, xdec[:, gl]))
            ys.append(y_g)
        y = jnp.concatenate(ys, axis=1) + dskip_ref[...] * xs
        o_ref[rows, :] = y * _silu(z_ref[rows, :])
        return new_states

    states = [jnp.zeros((SSM_STATE, LANES), F32) for _ in range(2)]
    for c in range(S // L):
        states = chunk(c, states)


def _dot_wide_lhs_rhs(m_bf16, x):
    hi, mid, lo = _split3(x)
    return _dot(m_bf16, hi) + _dot(m_bf16, mid) + _dot(m_bf16, lo)


def _ssd(z, xbc, dt, conv_w, conv_b, dt_bias, a_log, d_skip):
    B, S, _ = z.shape
    W = GROUP_WIDTH
    full = lambda shape: pl.BlockSpec(shape, lambda b: (0,) * len(shape))
    return pl.pallas_call(
        _ssd_kernel,
        grid=(B,),
        in_specs=[pl.BlockSpec((None, S, W), lambda b: (b, 0, 0)),
                  pl.BlockSpec((None, S, 3 * W), lambda b: (b, 0, 0)),
                  pl.BlockSpec((None, S, LANES), lambda b: (b, 0, 0)),
                  full((CONV_WIDTH, 3 * W)), full((1, 3 * W)), full((1, LANES)), full((1, LANES)),
                  full((1, W))],
        out_specs=pl.BlockSpec((None, S, W), lambda b: (b, 0, 0)),
        out_shape=jax.ShapeDtypeStruct((B, S, W), F32),
        scratch_shapes=[pltpu.VMEM((3 * W // LANES, S, LANES), F32)],
        compiler_params=_cparams(1),
        name="ssd",
    )(z, xbc, dt, conv_w, conv_b, dt_bias, a_log, d_skip)


def _rope(x, cos, sin):
    lane = lax.broadcasted_iota(jnp.int32, (x.shape[0], LANES), 1)
    first_half = (lane % HEAD_DIM) < (HEAD_DIM // 2)
    halves = []
    for p in range(x.shape[1] // LANES):
        xp = x[:, p * LANES:(p + 1) * LANES]
        fwd = pltpu.roll(xp, HEAD_DIM // 2, 1)
        bwd = pltpu.roll(xp, LANES - HEAD_DIM // 2, 1)
        halves.append(jnp.where(first_half, -bwd, fwd))
    rot = jnp.concatenate(halves, axis=1)
    return x * cos + rot * sin


def _moba_kernel(qkv_ref, cos_ref, sin_ref, o_ref, k_s, vt_s, kmean_s, acc_s, bias_s):
    i = pl.program_id(1)
    W = GROUP_WIDTH
    TB = MOBA_BLOCK
    S = qkv_ref.shape[0]
    NB = S // TB

    @pl.when(i == 0)
    def _():
        for blk in range(NB):
            rs = slice(blk * TB, (blk + 1) * TB)
            kb = _rope(qkv_ref[rs, W:2 * W], cos_ref[rs, :], sin_ref[rs, :])
            for h in range(N_HEADS):
                k_s[h, rs, :] = kb[:, h * HEAD_DIM:(h + 1) * HEAD_DIM].astype(BF16)
            kmean_s[blk:blk + 1, :] = jnp.mean(kb, axis=0, keepdims=True)
            vt_s[blk] =qkv_ref[rs, 2 * W:3 * W].T.astype(BF16)

    rows_i = pl.ds(pl.multiple_of(i * TB, TB), TB)
    q = _rope(qkv_ref[rows_i, 0:W], cos_ref[rows_i, :], sin_ref[rows_i, :])
    lane8 = lax.broadcasted_iota(jnp.int32, (NB, W), 1)
    blk_id = lax.broadcasted_iota(jnp.int32, (NB, TB), 0)
    R = N_HEADS * TB
    key_loc = lax.broadcasted_iota(jnp.int32, (TB, R), 0)
    q_loc = lax.broadcasted_iota(jnp.int32, (TB, R), 1) & (TB - 1)
    kmean = kmean_s[...]
    q_t = q.T
    qt_hi, qt_lo = _split2(q_t)
    scale = HEAD_DIM ** -0.5

    km_all = jnp.concatenate(
        [jnp.where((lane8 >= h * HEAD_DIM) & (lane8 < (h + 1) * HEAD_DIM), kmean, 0.0) for h in range(N_HEADS)],
        axis=0)
    km_hi, km_lo = _split2(km_all)
    gate_all = _dot(km_hi, qt_hi) + _dot(km_hi, qt_lo) + _dot(km_lo, qt_hi)
    qhs = []
    for h in range(N_HEADS):
        gate = gate_all[h * NB:(h + 1) * NB, :]
        cnt = jnp.zeros((NB, TB), F32)
        for jp in range(NB):
            row = gate[jp:jp + 1, :]
            beats = (row > gate) | ((row == gate) & (blk_id > jp))
            cnt = cnt + jnp.where(beats, jnp.where(jp < i, 1.0, 0.0), 0.0)
        selected = (cnt < float(MOBA_TOPK)) & (blk_id < i)
        bias_s[:, h * TB:(h + 1) * TB] = jnp.where(selected, 0.0, NEG_BIG)
        qhs.append((q_t[h * HEAD_DIM:(h + 1) * HEAD_DIM, :] * scale).astype(BF16))

    def scores(rows):
        return jnp.concatenate([_dot(k_s[h, rows, :], qhs[h]) for h in range(N_HEADS)], axis=1)

    s = jnp.where(key_loc <= q_loc, scores(rows_i), NEG_BIG)
    m = jnp.max(s, axis=0, keepdims=True)
    p = jnp.exp(s - m)
    l = jnp.sum(p, axis=0, keepdims=True)
    def weighted_values(j, p):
        p16 = p.astype(BF16)
        return jnp.concatenate(
            [_dot(vt_s[j, h * HEAD_DIM:(h + 1) * HEAD_DIM, :], p16[:, h * TB:(h + 1) * TB]) for h in range(N_HEADS)],
            axis=1)

    acc_s[...] = weighted_values(i, p)

    def body(j, carry):
        m, l = carry
        rows = pl.ds(pl.multiple_of(j * TB, TB), TB)
        s = scores(rows) + bias_s[pl.ds(j, 1), :]
        m_new = jnp.maximum(m, jnp.max(s, axis=0, keepdims=True))
        alpha = jnp.exp(m - m_new)
        p = jnp.exp(s - m_new)
        l = alpha * l + jnp.sum(p, axis=0, keepdims=True)
        acc_s[...] = alpha * acc_s[...] + weighted_values(j, p)
        return m_new, l

    m, l = lax.fori_loop(0, i, body, (m, l))
    outs = acc_s[...] / l
    o_ref[...] = jnp.concatenate([outs[:, h * TB:(h + 1) * TB].T for h in range(N_HEADS)], axis=1)


def _moba(qkv, cos, sin):
    B, S, _ = qkv.shape
    W = GROUP_WIDTH
    nb = S // MOBA_BLOCK
    return pl.pallas_call(
        _moba_kernel,
        grid=(B, S // MOBA_BLOCK),
        in_specs=[pl.BlockSpec((None, S, 3 * W), lambda b, i: (b, 0, 0)),
                  pl.BlockSpec((S, W), lambda b, i: (0, 0)),
                  pl.BlockSpec((S, W), lambda b, i: (0, 0))],
        out_specs=pl.BlockSpec((None, MOBA_BLOCK, W), lambda b, i: (b, i, 0)),
        out_shape=jax.ShapeDtypeStruct((B, S, W), F32),
        scratch_shapes=[pltpu.VMEM((N_HEADS, S, HEAD_DIM), BF16), pltpu.VMEM((nb, W, MOBA_BLOCK), BF16),
                        pltpu.VMEM((nb, W), F32), pltpu.VMEM((HEAD_DIM, N_HEADS * MOBA_BLOCK), F32),
                        pltpu.VMEM((nb, N_HEADS * MOBA_BLOCK), F32)],
        compiler_params=_cparams(2),
        name="moba",
    )(qkv, cos, sin)


def _outproj_kernel(ya_ref, yb_ref, yc_ref, yd_ref, gg_ref, w_ref, x_ref, o_ref):
    W = GROUP_WIDTH
    acc = x_ref[...]
    for g, y_ref in enumerate((ya_ref, yb_ref, yc_ref, yd_ref)):
        yn = _rmsnorm(y_ref[...], gg_ref[g:g + 1, :]).astype(BF16)
        acc = acc + _dot(yn, w_ref[g * W:(g + 1) * W, :].astype(BF16))
    o_ref[...] = acc


def _outproj(ys, gg, w_out, layer, x2d, tm=1024):
    T = x2d.shape[0]
    W = GROUP_WIDTH
    return pl.pallas_call(
        _outproj_kernel,
        grid=(T // tm,),
        in_specs=[pl.BlockSpec((tm, W), lambda i: (i, 0))] * 4
                 + [pl.BlockSpec((4, W), lambda i: (0, 0)),
                    pl.BlockSpec((None, 4 * W, D_MODEL), lambda i: (layer, 0, 0), pipeline_mode=pl.Buffered(1)),
                    pl.BlockSpec((tm, D_MODEL), lambda i: (i, 0))],
        out_specs=pl.BlockSpec((tm, D_MODEL), lambda i: (i, 0)),
        out_shape=jax.ShapeDtypeStruct((T, D_MODEL), F32),
        compiler_params=_cparams(1),
        name="outproj",
    )(*ys, gg, w_out, x2d)


def _memkv_kernel(m_ref, g_ref, w_ref, kt_ref, v_ref):
    mn = _rmsnorm(m_ref[...], g_ref[...]).astype(BF16)
    kv = _dot(mn, w_ref[...].astype(BF16))
    kt_ref[...] = kv[:, 0:D_MODEL].T.astype(BF16)
    v_ref[...] = kv[:, D_MODEL:2 * D_MODEL].astype(BF16)


def _memkv(mem, g, wkv, layer):
    B, M, _ = mem.shape
    return pl.pallas_call(
        _memkv_kernel,
        grid=(B,),
        in_specs=[pl.BlockSpec((None, M, D_MODEL), lambda b: (b, 0, 0)),
                  pl.BlockSpec((1, D_MODEL), lambda b: (0, 0)),
                  pl.BlockSpec((None, D_MODEL, 2 * D_MODEL), lambda b: (layer, 0, 0),
                               pipeline_mode=pl.Buffered(1))],
        out_specs=[pl.BlockSpec((None, D_MODEL, M), lambda b: (b, 0, 0)),
                   pl.BlockSpec((None, M, D_MODEL), lambda b: (b, 0, 0))],
        out_shape=[jax.ShapeDtypeStruct((B, D_MODEL, M), BF16), jax.ShapeDtypeStruct((B, M, D_MODEL), BF16)],
        compiler_params=_cparams(1),
        name="memkv",
    )(mem, g, wkv)


def _xattn_kernel(x_ref, g_ref, wq_ref, kt_ref, v_ref, wo_ref, o_ref):
    x = x_ref[...]
    h = _rmsnorm(x, g_ref[...]).astype(BF16)
    q = (_dot(h, wq_ref[...].astype(BF16)) * (XATTN_HEAD_DIM ** -0.5)).astype(BF16)
    acc = x
    for hd in range(XATTN_HEADS):
        cols = slice(hd * XATTN_HEAD_DIM, (hd + 1) * XATTN_HEAD_DIM)
        s = _dot(q[:, cols], kt_ref[cols, :])
        p = jnp.exp(s - jnp.max(s, axis=-1, keepdims=True))
        p = p / jnp.sum(p, axis=-1, keepdims=True)
        o = _dot(p.astype(BF16), v_ref[:, cols]).astype(BF16)
        acc = acc + _dot(o, wo_ref[cols, :].astype(BF16))
    o_ref[...] = acc


def _xattn(x3d, g, wq, kt, v, wo, layer, tm=1024):
    B, S, _ = x3d.shape
    M = v.shape[1]
    weight = pl.BlockSpec((None, D_MODEL, D_MODEL), lambda b, i: (layer, 0, 0), pipeline_mode=pl.Buffered(1))
    return pl.pallas_call(
        _xattn_kernel,
        grid=(B, S // tm),
        in_specs=[pl.BlockSpec((None, tm, D_MODEL), lambda b, i: (b, i, 0)),
                  pl.BlockSpec((1, D_MODEL), lambda b, i: (0, 0)),
                  weight,
                  pl.BlockSpec((None, D_MODEL, M), lambda b, i: (b, 0, 0)),
                  pl.BlockSpec((None, M, D_MODEL), lambda b, i: (b, 0, 0)),
                  weight],
        out_specs=pl.BlockSpec((None, tm, D_MODEL), lambda b, i: (b, i, 0)),
        out_shape=jax.ShapeDtypeStruct((B, S, D_MODEL), F32),
        compiler_params=_cparams(2),
        name="xattn",
    )(x3d, g, wq, kt, v, wo)


def _moe_kernel(x_ref, g_ref, wr_ref, br_ref, w1_ref, w3_ref, w2_ref, fg_ref, o_ref,
                t_s, comb_s, acc_s, xkeep_s, *, final_norm):
    tile = pl.program_id(0)
    grp = pl.program_id(1)
    tm = x_ref.shape[0]
    slot = lax.bitwise_and(tile, 1)

    def route(dst):
        t = _rmsnorm(x_ref[...], g_ref[...])
        t_s[dst] = t.astype(BF16)
        t_hi, t_lo = _split2(t)
        logits = (_dot(t_hi, wr_ref[0]) + _dot(t_hi, wr_ref[1]) + _dot(t_lo, wr_ref[0])) + br_ref[...]
        lt = logits.T
        gsl = SUBLANES * (MOE_EXPERTS // SUBLANES)
        g_row = lax.broadcasted_iota(jnp.int32, (SUBLANES, tm), 0).astype(F32)
        lg = jnp.where(g_row < float(MOE_GROUPS), lt[gsl:gsl + SUBLANES, :], NEG_BIG)
        gmax = jnp.max(lg, axis=0, keepdims=True)
        pg_top = 1.0 / jnp.sum(jnp.exp(lg - gmax), axis=0, keepdims=True)
        g_idx = jnp.min(jnp.where(lg == gmax, g_row, 1e9), axis=0, keepdims=True)
        e_row = lax.broadcasted_iota(jnp.int32, (MOE_EXPERTS, tm), 0).astype(F32)
        in_grp = jnp.floor(e_row * (1.0 / MOE_EPG)) == g_idx
        le = jnp.where(in_grp, lt[0:MOE_EXPERTS, :], NEG_BIG)
        e1 = jnp.max(le, axis=0, keepdims=True)
        i1 = jnp.min(jnp.where(in_grp & (le == e1), e_row, 1e9), axis=0, keepdims=True)
        le2 = jnp.where(e_row == i1, NEG_BIG, le)
        e2 = jnp.max(le2, axis=0, keepdims=True)
        i2 = jnp.min(jnp.where(in_grp & (e_row != i1) & (le2 == e2), e_row, 1e9), axis=0, keepdims=True)
        r2 = jnp.exp(e2 - e1)
        w_first = 1.0 / (1.0 + r2)
        w_second = r2 / (1.0 + r2)
        comb_t = pg_top * (jnp.where(e_row == i1, w_first, 0.0) + jnp.where(e_row == i2, w_second, 0.0))
        comb_s[dst] = jnp.concatenate([comb_t, jnp.zeros((LANES - MOE_EXPERTS, tm), F32)], axis=0).T

    def experts():
        t = t_s[slot]
        comb = comb_s[slot]
        lane = lax.broadcasted_iota(jnp.int32, (tm, LANES), 1)
        upd = jnp.zeros((tm, D_MODEL), F32)
        for e in range(MOE_EPG):
            n = grp * MOE_EPG + e
            c = jnp.sum(jnp.where(lane == n, comb, 0.0), axis=-1, keepdims=True)
            hid = _silu(_dot(t, w1_ref[e])) * _dot(t, w3_ref[e])
            upd = upd + _dot((hid * c).astype(BF16), w2_ref[e])
        return upd

    @pl.when((tile == 0) & (grp == 0))
    def _():
        route(0)

    @pl.when(grp == 0)
    def _():
        xkeep_s[...] = x_ref[...]
        acc_s[...] = experts()

    @pl.when((grp > 0) & (grp < MOE_GROUPS - 1))
    def _():
        acc_s[...] += experts()

    @pl.when(grp == MOE_GROUPS - 1)
    def _():
        y = xkeep_s[...] + (acc_s[...] + experts())
        route(1 - slot)
        if final_norm:
            y = _rmsnorm(y, fg_ref[...])
        o_ref[...] = y


def _moe(x2d, g, w_router, b_router, w1, w3, w2, layer, final_g, final_norm, tm=1024):
    T = x2d.shape[0]
    n_tiles = T // tm

    def x_window(i, e):
        return jnp.minimum(i + e // (MOE_GROUPS - 1), n_tiles - 1), 0

    return pl.pallas_call(
        functools.partial(_moe_kernel, final_norm=final_norm),
        grid=(n_tiles, MOE_GROUPS),
        in_specs=[pl.BlockSpec((tm, D_MODEL), x_window),
                  pl.BlockSpec((1, D_MODEL), lambda i, e: (0, 0)),
                  pl.BlockSpec((2, D_MODEL, LANES), lambda i, e: (0, 0, 0)),
                  pl.BlockSpec((1, LANES), lambda i, e: (0, 0)),
                  pl.BlockSpec((None, MOE_EPG, D_MODEL, MOE_FF), lambda i, e: (layer, e, 0, 0)),
                  pl.BlockSpec((None, MOE_EPG, D_MODEL, MOE_FF), lambda i, e: (layer, e, 0, 0)),
                  pl.BlockSpec((None, MOE_EPG, MOE_FF, D_MODEL), lambda i, e: (layer, e, 0, 0)),
                  pl.BlockSpec((1, D_MODEL), lambda i, e: (0, 0))],
        out_specs=pl.BlockSpec((tm, D_MODEL), lambda i, e: (i, 0)),
        out_shape=jax.ShapeDtypeStruct((T, D_MODEL), F32),
        scratch_shapes=[pltpu.VMEM((2, tm, D_MODEL), BF16), pltpu.VMEM((2, tm, LANES), F32),
                        pltpu.VMEM((tm, D_MODEL), F32), pltpu.VMEM((tm, D_MODEL), F32)],
        compiler_params=_cparams(2),
        name="moe",
    )(x2d, g, w_router, b_router, w1, w3, w2, final_g)


def _pad_lanes(v, width=LANES):
    return jnp.pad(v, (0, width - v.shape[0]))[None, :]


def _block_diag(w):
    H, n, _ = w.shape
    eye = jnp.eye(H, dtype=w.dtype)
    return (eye[:, None, :, None] * w[:, :, None, :]).reshape(H * n, H * n)


def _rope_tables(S):
    half = HEAD_DIM // 2
    inv_freq = ROPE_THETA ** (-jnp.arange(half, dtype=F32) / half)
    ang = jnp.arange(S, dtype=F32)[:, None] * inv_freq[None, :]
    reps = GROUP_WIDTH // half
    return jnp.tile(jnp.cos(ang), (1, reps)), jnp.tile(jnp.sin(ang), (1, reps))


def kernel(x, mem, mix_norm_g, w_in, lru_conv_w, lru_conv_b, lru_wr, lru_br, lru_wi, lru_bi, lru_lambda, ssm_conv_w, ssm_conv_b, ssm_dt_bias, ssm_a_log, ssm_d, group_norm_g, w_out, xattn_norm_g, mem_norm_g, xattn_wq, xattn_wkv, xattn_wo, ffn_norm_g, router_group_w, router_group_b, router_expert_w, router_expert_b, expert_w1, expert_w3, expert_w2, final_norm_g):
    B, S, D = x.shape
    T = B * S
    depth = w_in.shape[0]
    W = GROUP_WIDTH
    cos, sin = _rope_tables(S)
    x2d = x.reshape(T, D)
    w1_bf, w3_bf, w2_bf = expert_w1.astype(BF16), expert_w3.astype(BF16), expert_w2.astype(BF16)
    for l in range(depth):
        lru_xg, sb_qkv, ssm_z, ssm_xbc, ssm_dt, mb_qkv = _inproj(x2d, mix_norm_g[l][None, :], w_in, l)

        w_bd = jnp.concatenate([_block_diag(lru_wr[l]), _block_diag(lru_wi[l])], axis=1).astype(BF16)
        b_ri = jnp.concatenate([lru_br[l], lru_bi[l]])[None, :]
        y_a = _lru(lru_xg.reshape(B, S, 2 * W), lru_conv_w[l], lru_conv_b[l][None, :], w_bd, b_ri,
                   lru_lambda[l][None, :])
        y_b = _sb_attention(sb_qkv.reshape(B, S, 3 * W))
        y_c = _ssd(ssm_z.reshape(B, S, W), ssm_xbc.reshape(B, S, 3 * W), ssm_dt.reshape(B, S, LANES),
                   ssm_conv_w[l], ssm_conv_b[l][None, :], _pad_lanes(ssm_dt_bias[l]), _pad_lanes(ssm_a_log[l]),
                   jnp.repeat(ssm_d[l], HEAD_DIM)[None, :])
        y_d = _moba(mb_qkv.reshape(B, S, 3 * W), cos, sin)
        x2d = _outproj([y.reshape(T, W) for y in (y_a, y_b, y_c, y_d)], group_norm_g[l].reshape(4, W),
                       w_out, l, x2d)

        mem_kt, mem_v = _memkv(mem, mem_norm_g[l][None, :], xattn_wkv, l)
        x2d = _xattn(x2d.reshape(B, S, D), xattn_norm_g[l][None, :], xattn_wq, mem_kt, mem_v, xattn_wo,
                     l).reshape(T, D)

        w_r = jnp.pad(jnp.concatenate([router_expert_w[l], router_group_w[l]], axis=1),
                      ((0, 0), (0, LANES - MOE_EXPERTS - MOE_GROUPS)))
        w_r_hi = w_r.astype(BF16)
        w_r_lo = (w_r - w_r_hi.astype(F32)).astype(BF16)
        b_r = _pad_lanes(jnp.concatenate([router_expert_b[l], router_group_b[l]]))
        x2d = _moe(x2d, ffn_norm_g[l][None, :], jnp.stack([w_r_hi, w_r_lo]), b_r, w1_bf, w3_bf, w2_bf, l,
                   final_norm_g[None, :], final_norm=(l == depth - 1))
    return x2d.reshape(B, S, D)
```

```python
import functools
import math

import jax
import jax.numpy as jnp
from jax import lax
from jax.experimental import pallas as pl
from jax.experimental.pallas import tpu as pltpu

F32 = jnp.float32
BF16 = jnp.bfloat16

D_MODEL = 1024
GROUP_WIDTH = 256
HEAD_DIM = 64
N_HEADS = 4
NORM_EPS = 1e-6
CONV_WIDTH = 4
LRU_C = 8.0
SB_BLOCK = 128
SB_WINDOW_BLOCKS = 3
SB_CHAINS = 8
SSM_CHUNK = 128
SSM_STATE = 128
MOBA_BLOCK = 256
MOBA_TOPK = 3
ROPE_THETA = 10000.0
XATTN_HEADS = 4
XATTN_HEAD_DIM = 256
MEM_LEN = 256
MOE_GROUPS = 4
MOE_EPG = 4
MOE_EXPERTS = 16
MOE_FF = 256
LANES = 128
SUBLANES = 8
NEG_BIG = -1e30
SB_EXP_FLOOR = -104.0
IN_OUT_WIDTHS = (512, 768, 256, 768, LANES, 768)
IN_MAIN = 512 + 768 + 256 + 768
IN_OUT_DTYPES = (F32, BF16, F32, F32, F32, F32)
VMEM_LIMIT = 56 * 1024 * 1024


def _cparams(n_axes):
    return pltpu.CompilerParams(dimension_semantics=("arbitrary",) * n_axes,
                                vmem_limit_bytes=VMEM_LIMIT)


def _dot(a, b):
    return jnp.dot(a, b, preferred_element_type=F32)


def _dot_t(a, b):
    return lax.dot_general(a, b, (((1,), (1,)), ((), ())), preferred_element_type=F32)


def _dot_tl(a, b):
    return lax.dot_general(a, b, (((0,), (0,)), ((), ())), preferred_element_type=F32)


def _split2(x):
    hi = x.astype(BF16)
    lo = (x - hi.astype(F32)).astype(BF16)
    return hi, lo


def _split3(x):
    hi = x.astype(BF16)
    r = x - hi.astype(F32)
    mid = r.astype(BF16)
    lo = (r - mid.astype(F32)).astype(BF16)
    return hi, mid, lo


def _dot_wide_lhs(x, m_bf16, parts=3):
    pieces = _split3(x) if parts == 3 else _split2(x)
    out = _dot(pieces[0], m_bf16)
    for p in pieces[1:]:
        out = out + _dot(p, m_bf16)
    return out


def _rmsnorm(x, g):
    return x * lax.rsqrt(jnp.mean(x * x, axis=-1, keepdims=True) + NORM_EPS) * g


def _softplus(x):
    return jnp.maximum(x, 0.0) + jnp.log(1.0 + jnp.exp(-jnp.abs(x)))


def _sigmoid(x):
    return 1.0 / (1.0 + jnp.exp(-x))


def _silu(x):
    return x * _sigmoid(x)


def _gelu_tanh(x):
    return 0.5 * x * (1.0 + jnp.tanh(math.sqrt(2.0 / math.pi) * (x + 0.044715 * (x * x * x))))


def _shift_rows_down(x):
    rows = lax.broadcasted_iota(jnp.int32, x.shape, 0)
    return jnp.where(rows >= 1, pltpu.roll(x, 1, 0), 0.0)


def _causal_conv(x, w_ref, b_ref, cols):
    def taps(v, mask_rows):
        y = v * w_ref[CONV_WIDTH - 1:CONV_WIDTH, cols] + b_ref[:, cols]
        for s in range(1, CONV_WIDTH):
            vs = pltpu.roll(v, s, 0)
            if mask_rows is not None:
                vs = jnp.where(mask_rows >= s, vs, 0.0)
            y = y + vs * w_ref[CONV_WIDTH - 1 - s:CONV_WIDTH - s, cols]
        return y

    head = x[0:SUBLANES, :]
    y_head = taps(head, lax.broadcasted_iota(jnp.int32, head.shape, 0))
    return jnp.concatenate([y_head, taps(x, None)[SUBLANES:, :]], axis=0)


def _phase_conv(slab_ref, w_ref, b_ref, cols):
    P = SUBLANES
    nt = slab_ref.shape[0] // P
    x = [slab_ref[pl.ds(p, nt, stride=P), :] for p in range(P)]
    prev = {p: _shift_rows_down(x[p]) for p in range(P - CONV_WIDTH + 1, P)}
    out = []
    for p in range(P):
        y = x[p] * w_ref[CONV_WIDTH - 1:CONV_WIDTH, cols] + b_ref[:, cols]
        for k in range(1, CONV_WIDTH):
            src = x[p - k] if p - k >= 0 else prev[p - k + P]
            y = y + src * w_ref[CONV_WIDTH - 1 - k:CONV_WIDTH - k, cols]
        out.append(y)
    return out


def _inproj_kernel(x_ref, g_ref, w_ref, *refs):
    o_refs, w_s = refs[:-1], refs[-1]

    @pl.when(pl.program_id(0) == 0)
    def _():
        for c0 in range(0, IN_MAIN, 2 * LANES):
            w_s[:, c0:c0 + 2 * LANES] = w_ref[:, c0:c0 + 2 * LANES].astype(BF16)
        dt_tile = w_ref[:, IN_MAIN:IN_MAIN + LANES]
        lane = lax.broadcasted_iota(jnp.int32, dt_tile.shape, 1)
        w_s[:, IN_MAIN:IN_MAIN + LANES] = jnp.where(lane < N_HEADS, dt_tile, 0.0).astype(BF16)
        w_s[:, IN_MAIN + LANES:] = w_ref[:, IN_MAIN + N_HEADS:].astype(BF16)

    h = _rmsnorm(x_ref[...], g_ref[...]).astype(BF16)
    off = 0
    for o_ref, width in zip(o_refs, IN_OUT_WIDTHS):
        o_ref[...] = _dot(h, w_s[:, off:off + width]).astype(o_ref.dtype)
        off += width


def _inproj(x2d, g, w_in, layer, tm=1024):
    T = x2d.shape[0]
    n_in = w_in.shape[-1]
    return pl.pallas_call(
        _inproj_kernel,
        grid=(T // tm,),
        in_specs=[pl.BlockSpec((tm, D_MODEL), lambda i: (i, 0)),
                  pl.BlockSpec((1, D_MODEL), lambda i: (0, 0)),
                  pl.BlockSpec((None, D_MODEL, n_in), lambda i: (layer, 0, 0), pipeline_mode=pl.Buffered(1))],
        out_specs=[pl.BlockSpec((tm, w), lambda i: (i, 0)) for w in IN_OUT_WIDTHS],
        out_shape=[jax.ShapeDtypeStruct((T, w), dt) for w, dt in zip(IN_OUT_WIDTHS, IN_OUT_DTYPES)],
        scratch_shapes=[pltpu.VMEM((D_MODEL, sum(IN_OUT_WIDTHS)), BF16)],
        compiler_params=_cparams(1),
        name="inproj",
    )(x2d, g, w_in)


def _scan_rows(a, u):
    n = a.shape[0]
    rows = lax.broadcasted_iota(jnp.int32, a.shape, 0)
    shift = 1
    while shift < n:
        if shift < SUBLANES:
            keep = rows >= shift
            a_s = jnp.where(keep, pltpu.roll(a, shift, 0), 1.0)
            u_s = jnp.where(keep, pltpu.roll(u, shift, 0), 0.0)
            u = a * u_s + u
            a = a * a_s
        else:
            u = jnp.concatenate([u[:shift], a[shift:] * u[:n - shift] + u[shift:]], axis=0)
            a = jnp.concatenate([a[:shift], a[shift:] * a[:n - shift]], axis=0)
        shift *= 2
    return u


def _lru_kernel(xg_ref, cw_ref, cb_ref, wbd_ref, bri_ref, lam_ref, o_ref, in_s, out_s):
    S = xg_ref.shape[0]
    W = GROUP_WIDTH
    P = SUBLANES
    NT = S // P
    for s in range(2 * W // LANES):
        in_s[s] = xg_ref[:, s * LANES:(s + 1) * LANES]
    log_sig_lam = -_softplus(-lam_ref[...])

    for s in range(W // LANES):
        cols = slice(s * LANES, (s + 1) * LANES)
        xc_all = jnp.concatenate(_phase_conv(in_s.at[s], cw_ref, cb_ref, cols), axis=0)
        w_slab = jnp.concatenate([wbd_ref[cols, cols], wbd_ref[cols, W + s * LANES:W + (s + 1) * LANES]], axis=1)
        ri = _dot(xc_all.astype(BF16), w_slab)
        r = _sigmoid(ri[:, 0:LANES] + bri_ref[:, cols])
        i = _sigmoid(ri[:, LANES:2 * LANES] + bri_ref[:, W + s * LANES:W + (s + 1) * LANES])
        log_a = (LRU_C * r) * log_sig_lam[:, cols]
        a = jnp.exp(log_a)
        u = jnp.sqrt(1.0 - jnp.exp(2.0 * log_a)) * (i * xc_all)
        loc = [u[0:NT]]
        dec = [a[0:NT]]
        for p in range(1, P):
            ap = a[p * NT:(p + 1) * NT]
            loc.append(ap * loc[-1] + u[p * NT:(p + 1) * NT])
            dec.append(ap * dec[-1])
        carry = _shift_rows_down(_scan_rows(dec[-1], loc[-1]))
        for p in range(P):
            h = loc[p] + dec[p] * carry
            out_s[s, pl.ds(p, NT, stride=P), :] = h * _gelu_tanh(in_s[W // LANES + s, pl.ds(p, NT, stride=P), :])
    for s in range(W // LANES):
        o_ref[:, s * LANES:(s + 1) * LANES] = out_s[s]


def _lru(xg, conv_w, conv_b, w_bd, b_ri, lam):
    B, S, _ = xg.shape
    W = GROUP_WIDTH
    full = lambda shape: pl.BlockSpec(shape, lambda b: (0,) * len(shape))
    return pl.pallas_call(
        _lru_kernel,
        grid=(B,),
        in_specs=[pl.BlockSpec((None, S, 2 * W), lambda b: (b, 0, 0)),
                  full((CONV_WIDTH, W)), full((1, W)), full((W, 2 * W)), full((1, 2 * W)), full((1, W))],
        out_specs=pl.BlockSpec((None, S, W), lambda b: (b, 0, 0)),
        out_shape=jax.ShapeDtypeStruct((B, S, W), F32),
        scratch_shapes=[pltpu.VMEM((2 * W // LANES, S, LANES), F32), pltpu.VMEM((W // LANES, S, LANES), F32)],
        compiler_params=_cparams(1),
        name="rglru",
    )(xg, conv_w, conv_b, w_bd, b_ri, lam)


def _sb_kernel(qkv_ref, o_ref, kt_s, v_s, acc_s, later_s):
    i = pl.program_id(1)
    W = GROUP_WIDTH
    TB = SB_BLOCK
    R = N_HEADS * TB

    NW = SB_WINDOW_BLOCKS
    KW = NW * TB
    PAD = (NW - 1) * TB
    S = qkv_ref.shape[0]

    @pl.when(i == 0)
    def _():
        kt_s[:, 0:PAD] = jnp.zeros((W, PAD), BF16)
        for r0 in range(0, S, W):
            kt_s[:, PAD + r0:PAD + r0 + W] = qkv_ref[r0:r0 + W, W:2 * W].astype(F32).T.astype(BF16)
        v_s[0:PAD, :] = jnp.zeros((PAD, W), BF16)
        v_s[PAD:PAD + S, :] = qkv_ref[:, 2 * W:3 * W].astype(BF16)

    lane = lax.broadcasted_iota(jnp.int32, (TB, W), 1)
    heads = [(lane >= h * HEAD_DIM) & (lane < (h + 1) * HEAD_DIM) for h in range(N_HEADS)]
    r_loc = lax.broadcasted_iota(jnp.int32, (R, KW), 0) & (TB - 1)
    c_loc = lax.broadcasted_iota(jnp.int32, (R, KW), 1)
    ur = lax.broadcasted_iota(jnp.int32, (TB, 2 * TB), 0)
    uc = lax.broadcasted_iota(jnp.int32, (TB, 2 * TB), 1)
    tri_ones = jnp.where((ur > uc) | (uc >= TB), 1.0, 0.0).astype(BF16)

    n_chains = o_ref.shape[0]
    blocks = [i + c * (S // TB // n_chains) for c in range(n_chains)]
    qss = []
    for blk in blocks:
        q = qkv_ref[pl.ds(pl.multiple_of(blk * TB, TB), TB), 0:W] * (HEAD_DIM ** -0.5)
        qss.append(jnp.concatenate([jnp.where(hm, q, 0.0) for hm in heads], axis=0).astype(BF16))

    acc_s[...] = jnp.zeros_like(acc_s)
    later_s[...] = jnp.zeros_like(later_s)

    def window(c, n):
        blk = blocks[c]
        first_key = (blk - n * NW - (NW - 1)) * TB
        rows = pl.ds(pl.multiple_of(jnp.maximum(first_key + PAD, 0), TB), KW)
        z = _dot(qss[c], kt_s[:, rows])
        key_abs = first_key + c_loc
        live = (key_abs < blk * TB + r_loc) & (key_abs >= 0)
        sp = _softplus(z)
        lf = jnp.where(live, -sp, 0.0)
        lf16 = lf.astype(BF16)
        order = list(range(NW - 1, -1, -1))
        stacked = jnp.concatenate([lf16[:, b * TB:(b + 1) * TB] for b in order], axis=0)
        cs_all = _dot(stacked, tri_ones)
        offset = later_s[c]
        after = [None] * NW
        for pos, b in enumerate(order):
            cs = cs_all[pos * R:(pos + 1) * R, :]
            after[b] = cs[:, 0:TB] + offset
            offset = offset + cs[:, TB:2 * TB]
        w = jnp.where(live, jnp.exp((z - sp) + jnp.concatenate(after, axis=1)), 0.0)
        acc_s[c] += _dot(w.astype(BF16), v_s[rows, :])
        later_s[c] = offset
        return jnp.where((n + 1) * NW <= blk, jnp.max(offset), SB_EXP_FLOOR)

    def cond(carry):
        return carry[1] > SB_EXP_FLOOR

    def body(carry):
        n = carry[0]
        later_max = window(0, n)
        for c in range(1, n_chains):
            later_max = jnp.maximum(later_max, window(c, n))
        return n + 1, later_max

    lax.while_loop(cond, body, (jnp.int32(0), jnp.float32(0.0)))
    for c in range(n_chains):
        out = acc_s[c, 0:TB, :]
        for h in range(1, N_HEADS):
            out = jnp.where(heads[h], acc_s[c, h * TB:(h + 1) * TB, :], out)
        o_ref[c] = out


def _sb_attention(qkv):
    B, S, _ = qkv.shape
    W = GROUP_WIDTH
    pad = (SB_WINDOW_BLOCKS - 1) * SB_BLOCK
    nc = SB_CHAINS
    rows = N_HEADS * SB_BLOCK
    out = pl.pallas_call(
        _sb_kernel,
        grid=(B, S // SB_BLOCK // nc),
        in_specs=[pl.BlockSpec((None, S, 3 * W), lambda b, i: (b, 0, 0))],
        out_specs=pl.BlockSpec((None, nc, SB_BLOCK, W), lambda b, i: (b, 0, i, 0)),
        out_shape=jax.ShapeDtypeStruct((B, nc, S // nc, W), F32),
        scratch_shapes=[pltpu.VMEM((W, S + pad), BF16), pltpu.VMEM((S + pad, W), BF16),
                        pltpu.VMEM((nc, rows, W), F32), pltpu.VMEM((nc, rows, SB_BLOCK), F32)],
        compiler_params=_cparams(2),
        name="stickbreak",
    )(qkv)
    return out.reshape(B, S, W)


def _ssd_kernel(z_ref, xbc_ref, dt_ref, cw_ref, cb_ref, dtb_ref, alog_ref, dskip_ref, o_ref, xbc_s):
    S = z_ref.shape[0]
    W = GROUP_WIDTH
    L = SSM_CHUNK
    for s in range(3 * W // LANES):
        cols = slice(s * LANES, (s + 1) * LANES)
        xbc_s[s] = _silu(_causal_conv(xbc_ref[:, cols], cw_ref, cb_ref, cols))
    a_row = -jnp.exp(alog_ref[...])

    r_i = lax.broadcasted_iota(jnp.int32, (L, L), 0)
    c_i = lax.broadcasted_iota(jnp.int32, (L, L), 1)
    tri_incl = jnp.where(c_i <= r_i, 1.0, 0.0).astype(BF16)
    lower = c_i <= r_i
    e_r = lax.broadcasted_iota(jnp.int32, (LANES, W), 0)
    e_c = lax.broadcasted_iota(jnp.int32, (LANES, W), 1)
    expand = jnp.where((e_c >= e_r * HEAD_DIM) & (e_c < (e_r + 1) * HEAD_DIM), 1.0, 0.0).astype(BF16)
    lane_l = lax.broadcasted_iota(jnp.int32, (L, LANES), 1)

    def chunk(c, states):
        rows = slice(c * L, (c + 1) * L)
        xs = jnp.concatenate([xbc_s[0, rows, :], xbc_s[1, rows, :]], axis=1)
        dt = _softplus(dt_ref[rows, :] + dtb_ref[...])
        a_dt = dt * a_row
        cs_col = _dot_wide_lhs_rhs(tri_incl, a_dt)
        cs_row = cs_col.T
        cs_full = _dot_wide_lhs(cs_col, expand)
        dt_full = _dot_wide_lhs(dt, expand)
        xd = xs * dt_full
        tot = cs_full[L - 1:L, :]
        xdec = (xd * jnp.exp(tot - cs_full)).astype(BF16)
        xd16 = xd.astype(BF16)
        ys = []
        new_states = []
        for g in range(2):
            gl = slice(g * LANES, (g + 1) * LANES)
            bm = xbc_s[2 + g, rows, :].astype(BF16)
            cm = xbc_s[4 + g, rows, :].astype(BF16)
            cb = _dot_t(cm, bm)
            prev = states[g]
            y_off = _dot(cm, prev.astype(BF16)) * jnp.exp(cs_full[:, gl])
            y_g = y_off
            for hh in range(2):
                h = 2 * g + hh
                seg = jnp.where(lower, cs_col[:, h:h + 1] - cs_row[h:h + 1, :], -jnp.inf)
                y_h = _dot((cb * jnp.exp(seg)).astype(BF16), xd16[:, gl])
                in_head = (lane_l >= hh * HEAD_DIM) & (lane_l < (hh + 1) * HEAD_DIM)
                y_g = y_g + jnp.where(in_head, y_h, 0.0)
            new_states.append(prev * jnp.exp(tot[:, gl]) + _dot_tl(bm, xdec[:, gl]))
            ys.append(y_g)
        y = jnp.concatenate(ys, axis=1) + dskip_ref[...] * xs
        o_ref[rows, :] = y * _silu(z_ref[rows, :])
        return new_states

    states = [jnp.zeros((SSM_STATE, LANES), F32) for _ in range(2)]
    for c in range(S // L):
        states = chunk(c, states)


def _dot_wide_lhs_rhs(m_bf16, x):
    hi, mid, lo = _split3(x)
    return _dot(m_bf16, hi) + _dot(m_bf16, mid) + _dot(m_bf16, lo)


def _ssd(z, xbc, dt, conv_w, conv_b, dt_bias, a_log, d_skip):
    B, S, _ = z.shape
    W = GROUP_WIDTH
    full = lambda shape: pl.BlockSpec(shape, lambda b: (0,) * len(shape))
    return pl.pallas_call(
        _ssd_kernel,
        grid=(B,),
        in_specs=[pl.BlockSpec((None, S, W), lambda b: (b, 0, 0)),
                  pl.BlockSpec((None, S, 3 * W), lambda b: (b, 0, 0)),
                  pl.BlockSpec((None, S, LANES), lambda b: (b, 0, 0)),
                  full((CONV_WIDTH, 3 * W)), full((1, 3 * W)), full((1, LANES)), full((1, LANES)),
                  full((1, W))],
        out_specs=pl.BlockSpec((None, S, W), lambda b: (b, 0, 0)),
        out_shape=jax.ShapeDtypeStruct((B, S, W), F32),
        scratch_shapes=[pltpu.VMEM((3 * W // LANES, S, LANES), F32)],
        compiler_params=_cparams(1),
        name="ssd",
    )(z, xbc, dt, conv_w, conv_b, dt_bias, a_log, d_skip)


def _rope(x, cos, sin):
    lane = lax.broadcasted_iota(jnp.int32, (x.shape[0], LANES), 1)
    first_half = (lane % HEAD_DIM) < (HEAD_DIM // 2)
    halves = []
    for p in range(x.shape[1] // LANES):
        xp = x[:, p * LANES:(p + 1) * LANES]
        fwd = pltpu.roll(xp, HEAD_DIM // 2, 1)
        bwd = pltpu.roll(xp, LANES - HEAD_DIM // 2, 1)
        halves.append(jnp.where(first_half, -bwd, fwd))
    rot = jnp.concatenate(halves, axis=1)
    return x * cos + rot * sin


def _moba_kernel(qkv_ref, cos_ref, sin_ref, o_ref, k_s, vt_s, kmean_s, acc_s, bias_s):
    i = pl.program_id(1)
    W = GROUP_WIDTH
    TB = MOBA_BLOCK
    S = qkv_ref.shape[0]
    NB = S // TB

    @pl.when(i == 0)
    def _():
        for blk in range(NB):
            rs = slice(blk * TB, (blk + 1) * TB)
            kb = _rope(qkv_ref[rs, W:2 * W], cos_ref[rs, :], sin_ref[rs, :])
            for h in range(N_HEADS):
                k_s[h, rs, :] = kb[:, h * HEAD_DIM:(h + 1) * HEAD_DIM].astype(BF16)
            kmean_s[blk:blk + 1, :] = jnp.mean(kb, axis=0, keepdims=True)
            vt_s[blk] =qkv_ref[rs, 2 * W:3 * W].T.astype(BF16)

    rows_i = pl.ds(pl.multiple_of(i * TB, TB), TB)
    q = _rope(qkv_ref[rows_i, 0:W], cos_ref[rows_i, :], sin_ref[rows_i, :])
    lane8 = lax.broadcasted_iota(jnp.int32, (NB, W), 1)
    blk_id = lax.broadcasted_iota(jnp.int32, (NB, TB), 0)
    R = N_HEADS * TB
    key_loc = lax.broadcasted_iota(jnp.int32, (TB, R), 0)
    q_loc = lax.broadcasted_iota(jnp.int32, (TB, R), 1) & (TB - 1)
    kmean = kmean_s[...]
    q_t = q.T
    qt_hi, qt_lo = _split2(q_t)
    scale = HEAD_DIM ** -0.5

    km_all = jnp.concatenate(
        [jnp.where((lane8 >= h * HEAD_DIM) & (lane8 < (h + 1) * HEAD_DIM), kmean, 0.0) for h in range(N_HEADS)],
        axis=0)
    km_hi, km_lo = _split2(km_all)
    gate_all = _dot(km_hi, qt_hi) + _dot(km_hi, qt_lo) + _dot(km_lo, qt_hi)
    qhs = []
    for h in range(N_HEADS):
        gate = gate_all[h * NB:(h + 1) * NB, :]
        cnt = jnp.zeros((NB, TB), F32)
        for jp in range(NB):
            row = gate[jp:jp + 1, :]
            beats = (row > gate) | ((row == gate) & (blk_id > jp))
            cnt = cnt + jnp.where(beats, jnp.where(jp < i, 1.0, 0.0), 0.0)
        selected = (cnt < float(MOBA_TOPK)) & (blk_id < i)
        bias_s[:, h * TB:(h + 1) * TB] = jnp.where(selected, 0.0, NEG_BIG)
        qhs.append((q_t[h * HEAD_DIM:(h + 1) * HEAD_DIM, :] * scale).astype(BF16))

    def scores(rows):
        return jnp.concatenate([_dot(k_s[h, rows, :], qhs[h]) for h in range(N_HEADS)], axis=1)

    s = jnp.where(key_loc <= q_loc, scores(rows_i), NEG_BIG)
    m = jnp.max(s, axis=0, keepdims=True)
    p = jnp.exp(s - m)
    l = jnp.sum(p, axis=0, keepdims=True)
    def weighted_values(j, p):
        p16 = p.astype(BF16)
        return jnp.concatenate(
            [_dot(vt_s[j, h * HEAD_DIM:(h + 1) * HEAD_DIM, :], p16[:, h * TB:(h + 1) * TB]) for h in range(N_HEADS)],
            axis=1)

    acc_s[...] = weighted_values(i, p)

    def body(j, carry):
        m, l = carry
        rows = pl.ds(pl.multiple_of(j * TB, TB), TB)
        s = scores(rows) + bias_s[pl.ds(j, 1), :]
        m_new = jnp.maximum(m, jnp.max(s, axis=0, keepdims=True))
        alpha = jnp.exp(m - m_new)
        p = jnp.exp(s - m_new)
        l = alpha * l + jnp.sum(p, axis=0, keepdims=True)
        acc_s[...] = alpha * acc_s[...] + weighted_values(j, p)
        return m_new, l

    m, l = lax.fori_loop(0, i, body, (m, l))
    outs = acc_s[...] / l
    o_ref[...] = jnp.concatenate([outs[:, h * TB:(h + 1) * TB].T for h in range(N_HEADS)], axis=1)


def _moba(qkv, cos, sin):
    B, S, _ = qkv.shape
    W = GROUP_WIDTH
    nb = S // MOBA_BLOCK
    return pl.pallas_call(
        _moba_kernel,
        grid=(B, S // MOBA_BLOCK),
        in_specs=[pl.BlockSpec((None, S, 3 * W), lambda b, i: (b, 0, 0)),
                  pl.BlockSpec((S, W), lambda b, i: (0, 0)),
                  pl.BlockSpec((S, W), lambda b, i: (0, 0))],
        out_specs=pl.BlockSpec((None, MOBA_BLOCK, W), lambda b, i: (b, i, 0)),
        out_shape=jax.ShapeDtypeStruct((B, S, W), F32),
        scratch_shapes=[pltpu.VMEM((N_HEADS, S, HEAD_DIM), BF16), pltpu.VMEM((nb, W, MOBA_BLOCK), BF16),
                        pltpu.VMEM((nb, W), F32), pltpu.VMEM((HEAD_DIM, N_HEADS * MOBA_BLOCK), F32),
                        pltpu.VMEM((nb, N_HEADS * MOBA_BLOCK), F32)],
        compiler_params=_cparams(2),
        name="moba",
    )(qkv, cos, sin)


def _outproj_kernel(ya_ref, yb_ref, yc_ref, yd_ref, gg_ref, w_ref, x_ref, o_ref):
    W = GROUP_WIDTH
    acc = x_ref[...]
    for g, y_ref in enumerate((ya_ref, yb_ref, yc_ref, yd_ref)):
        yn = _rmsnorm(y_ref[...], gg_ref[g:g + 1, :]).astype(BF16)
        acc = acc + _dot(yn, w_ref[g * W:(g + 1) * W, :].astype(BF16))
    o_ref[...] = acc


def _outproj(ys, gg, w_out, layer, x2d, tm=1024):
    T = x2d.shape[0]
    W = GROUP_WIDTH
    return pl.pallas_call(
        _outproj_kernel,
        grid=(T // tm,),
        in_specs=[pl.BlockSpec((tm, W), lambda i: (i, 0))] * 4
                 + [pl.BlockSpec((4, W), lambda i: (0, 0)),
                    pl.BlockSpec((None, 4 * W, D_MODEL), lambda i: (layer, 0, 0), pipeline_mode=pl.Buffered(1)),
                    pl.BlockSpec((tm, D_MODEL), lambda i: (i, 0))],
        out_specs=pl.BlockSpec((tm, D_MODEL), lambda i: (i, 0)),
        out_shape=jax.ShapeDtypeStruct((T, D_MODEL), F32),
        compiler_params=_cparams(1),
        name="outproj",
    )(*ys, gg, w_out, x2d)


def _memkv_kernel(m_ref, g_ref, w_ref, kt_ref, v_ref):
    mn = _rmsnorm(m_ref[...], g_ref[...]).astype(BF16)
    kv = _dot(mn, w_ref[...].astype(BF16))
    kt_ref[...] = kv[:, 0:D_MODEL].T.astype(BF16)
    v_ref[...] = kv[:, D_MODEL:2 * D_MODEL].astype(BF16)


def _memkv(mem, g, wkv, layer):
    B, M, _ = mem.shape
    return pl.pallas_call(
        _memkv_kernel,
        grid=(B,),
        in_specs=[pl.BlockSpec((None, M, D_MODEL), lambda b: (b, 0, 0)),
                  pl.BlockSpec((1, D_MODEL), lambda b: (0, 0)),
                  pl.BlockSpec((None, D_MODEL, 2 * D_MODEL), lambda b: (layer, 0, 0),
                               pipeline_mode=pl.Buffered(1))],
        out_specs=[pl.BlockSpec((None, D_MODEL, M), lambda b: (b, 0, 0)),
                   pl.BlockSpec((None, M, D_MODEL), lambda b: (b, 0, 0))],
        out_shape=[jax.ShapeDtypeStruct((B, D_MODEL, M), BF16), jax.ShapeDtypeStruct((B, M, D_MODEL), BF16)],
        compiler_params=_cparams(1),
        name="memkv",
    )(mem, g, wkv)


def _xattn_kernel(x_ref, g_ref, wq_ref, kt_ref, v_ref, wo_ref, o_ref):
    x = x_ref[...]
    h = _rmsnorm(x, g_ref[...]).astype(BF16)
    q = (_dot(h, wq_ref[...].astype(BF16)) * (XATTN_HEAD_DIM ** -0.5)).astype(BF16)
    heads = []
    for hd in range(XATTN_HEADS):
        cols = slice(hd * XATTN_HEAD_DIM, (hd + 1) * XATTN_HEAD_DIM)
        s = _dot(q[:, cols], kt_ref[cols, :])
        p = jnp.exp(s - jnp.max(s, axis=-1, keepdims=True))
        p = p / jnp.sum(p, axis=-1, keepdims=True)
        heads.append(_dot(p.astype(BF16), v_ref[:, cols]).astype(BF16))
    o_ref[...] = x + _dot(jnp.concatenate(heads, axis=1), wo_ref[...].astype(BF16))


def _xattn(x3d, g, wq, kt, v, wo, layer, tm=1024):
    B, S, _ = x3d.shape
    M = v.shape[1]
    weight = pl.BlockSpec((None, D_MODEL, D_MODEL), lambda b, i: (layer, 0, 0), pipeline_mode=pl.Buffered(1))
    return pl.pallas_call(
        _xattn_kernel,
        grid=(B, S // tm),
        in_specs=[pl.BlockSpec((None, tm, D_MODEL), lambda b, i: (b, i, 0)),
                  pl.BlockSpec((1, D_MODEL), lambda b, i: (0, 0)),
                  weight,
                  pl.BlockSpec((None, D_MODEL, M), lambda b, i: (b, 0, 0)),
                  pl.BlockSpec((None, M, D_MODEL), lambda b, i: (b, 0, 0)),
                  weight],
        out_specs=pl.BlockSpec((None, tm, D_MODEL), lambda b, i: (b, i, 0)),
        out_shape=jax.ShapeDtypeStruct((B, S, D_MODEL), F32),
        compiler_params=_cparams(2),
        name="xattn",
    )(x3d, g, wq, kt, v, wo)


def _moe_kernel(x_ref, g_ref, wr_ref, br_ref, w1_ref, w3_ref, w2_ref, fg_ref, o_ref,
                t_s, comb_s, acc_s, xkeep_s, *, final_norm):
    tile = pl.program_id(0)
    grp = pl.program_id(1)
    tm = x_ref.shape[0]
    slot = lax.bitwise_and(tile, 1)

    def route(dst):
        t = _rmsnorm(x_ref[...], g_ref[...])
        t_s[dst] = t.astype(BF16)
        t_hi, t_lo = _split2(t)
        logits = (_dot(t_hi, wr_ref[0]) + _dot(t_hi, wr_ref[1]) + _dot(t_lo, wr_ref[0])) + br_ref[...]
        lt = logits.T
        gsl = SUBLANES * (MOE_EXPERTS // SUBLANES)
        g_row = lax.broadcasted_iota(jnp.int32, (SUBLANES, tm), 0).astype(F32)
        lg = jnp.where(g_row < float(MOE_GROUPS), lt[gsl:gsl + SUBLANES, :], NEG_BIG)
        gmax = jnp.max(lg, axis=0, keepdims=True)
        pg_top = 1.0 / jnp.sum(jnp.exp(lg - gmax), axis=0, keepdims=True)
        g_idx = jnp.min(jnp.where(lg == gmax, g_row, 1e9), axis=0, keepdims=True)
        e_row = lax.broadcasted_iota(jnp.int32, (MOE_EXPERTS, tm), 0).astype(F32)
        in_grp = jnp.floor(e_row * (1.0 / MOE_EPG)) == g_idx
        le = jnp.where(in_grp, lt[0:MOE_EXPERTS, :], NEG_BIG)
        e1 = jnp.max(le, axis=0, keepdims=True)
        i1 = jnp.min(jnp.where(in_grp & (le == e1), e_row, 1e9), axis=0, keepdims=True)
        le2 = jnp.where(e_row == i1, NEG_BIG, le)
        e2 = jnp.max(le2, axis=0, keepdims=True)
        i2 = jnp.min(jnp.where(in_grp & (e_row != i1) & (le2 == e2), e_row, 1e9), axis=0, keepdims=True)
        r2 = jnp.exp(e2 - e1)
        w_first = 1.0 / (1.0 + r2)
        w_second = r2 / (1.0 + r2)
        comb_t = pg_top * (jnp.where(e_row == i1, w_first, 0.0) + jnp.where(e_row == i2, w_second, 0.0))
        comb_s[dst] = jnp.concatenate([comb_t, jnp.zeros((LANES - MOE_EXPERTS, tm), F32)], axis=0).T

    def experts():
        t = t_s[slot]
        comb = comb_s[slot]
        lane = lax.broadcasted_iota(jnp.int32, (tm, LANES), 1)
        upd = jnp.zeros((tm, D_MODEL), F32)
        for e in range(MOE_EPG):
            n = grp * MOE_EPG + e
            c = jnp.sum(jnp.where(lane == n, comb, 0.0), axis=-1, keepdims=True)
            hid = _silu(_dot(t, w1_ref[e])) * _dot(t, w3_ref[e])
            upd = upd + _dot((hid * c).astype(BF16), w2_ref[e])
        return upd

    @pl.when((tile == 0) & (grp == 0))
    def _():
        route(0)

    @pl.when(grp == 0)
    def _():
        xkeep_s[...] = x_ref[...]
        acc_s[...] = experts()

    @pl.when((grp > 0) & (grp < MOE_GROUPS - 1))
    def _():
        acc_s[...] += experts()

    @pl.when(grp == MOE_GROUPS - 1)
    def _():
        y = xkeep_s[...] + (acc_s[...] + experts())
        route(1 - slot)
        if final_norm:
            y = _rmsnorm(y, fg_ref[...])
        o_ref[...] = y


def _moe(x2d, g, w_router, b_router, w1, w3, w2, layer, final_g, final_norm, tm=1024):
    T = x2d.shape[0]
    n_tiles = T // tm

    def x_window(i, e):
        return jnp.minimum(i + e // (MOE_GROUPS - 1), n_tiles - 1), 0

    return pl.pallas_call(
        functools.partial(_moe_kernel, final_norm=final_norm),
        grid=(n_tiles, MOE_GROUPS),
        in_specs=[pl.BlockSpec((tm, D_MODEL), x_window),
                  pl.BlockSpec((1, D_MODEL), lambda i, e: (0, 0)),
                  pl.BlockSpec((2, D_MODEL, LANES), lambda i, e: (0, 0, 0)),
                  pl.BlockSpec((1, LANES), lambda i, e: (0, 0)),
                  pl.BlockSpec((None, MOE_EPG, D_MODEL, MOE_FF), lambda i, e: (layer, e, 0, 0)),
                  pl.BlockSpec((None, MOE_EPG, D_MODEL, MOE_FF), lambda i, e: (layer, e, 0, 0)),
                  pl.BlockSpec((None, MOE_EPG, MOE_FF, D_MODEL), lambda i, e: (layer, e, 0, 0)),
                  pl.BlockSpec((1, D_MODEL), lambda i, e: (0, 0))],
        out_specs=pl.BlockSpec((tm, D_MODEL), lambda i, e: (i, 0)),
        out_shape=jax.ShapeDtypeStruct((T, D_MODEL), F32),
        scratch_shapes=[pltpu.VMEM((2, tm, D_MODEL), BF16), pltpu.VMEM((2, tm, LANES), F32),
                        pltpu.VMEM((tm, D_MODEL), F32), pltpu.VMEM((tm, D_MODEL), F32)],
        compiler_params=_cparams(2),
        name="moe",
    )(x2d, g, w_router, b_router, w1, w3, w2, final_g)


def _pad_lanes(v, width=LANES):
    return jnp.pad(v, (0, width - v.shape[0]))[None, :]


def _block_diag(w):
    H, n, _ = w.shape
    eye = jnp.eye(H, dtype=w.dtype)
    return (eye[:, None, :, None] * w[:, :, None, :]).reshape(H * n, H * n)


def _rope_tables(S):
    half = HEAD_DIM // 2
    inv_freq = ROPE_THETA ** (-jnp.arange(half, dtype=F32) / half)
    ang = jnp.arange(S, dtype=F32)[:, None] * inv_freq[None, :]
    reps = GROUP_WIDTH // half
    return jnp.tile(jnp.cos(ang), (1, reps)), jnp.tile(jnp.sin(ang), (1, reps))


def kernel(x, mem, mix_norm_g, w_in, lru_conv_w, lru_conv_b, lru_wr, lru_br, lru_wi, lru_bi, lru_lambda, ssm_conv_w, ssm_conv_b, ssm_dt_bias, ssm_a_log, ssm_d, group_norm_g, w_out, xattn_norm_g, mem_norm_g, xattn_wq, xattn_wkv, xattn_wo, ffn_norm_g, router_group_w, router_group_b, router_expert_w, router_expert_b, expert_w1, expert_w3, expert_w2, final_norm_g):
    B, S, D = x.shape
    T = B * S
    depth = w_in.shape[0]
    W = GROUP_WIDTH
    cos, sin = _rope_tables(S)
    x2d = x.reshape(T, D)
    w1_bf, w3_bf, w2_bf = expert_w1.astype(BF16), expert_w3.astype(BF16), expert_w2.astype(BF16)
    for l in range(depth):
        lru_xg, sb_qkv, ssm_z, ssm_xbc, ssm_dt, mb_qkv = _inproj(x2d, mix_norm_g[l][None, :], w_in, l)

        w_bd = jnp.concatenate([_block_diag(lru_wr[l]), _block_diag(lru_wi[l])], axis=1).astype(BF16)
        b_ri = jnp.concatenate([lru_br[l], lru_bi[l]])[None, :]
        y_a = _lru(lru_xg.reshape(B, S, 2 * W), lru_conv_w[l], lru_conv_b[l][None, :], w_bd, b_ri,
                   lru_lambda[l][None, :])
        y_b = _sb_attention(sb_qkv.reshape(B, S, 3 * W))
        y_c = _ssd(ssm_z.reshape(B, S, W), ssm_xbc.reshape(B, S, 3 * W), ssm_dt.reshape(B, S, LANES),
                   ssm_conv_w[l], ssm_conv_b[l][None, :], _pad_lanes(ssm_dt_bias[l]), _pad_lanes(ssm_a_log[l]),
                   jnp.repeat(ssm_d[l], HEAD_DIM)[None, :])
        y_d = _moba(mb_qkv.reshape(B, S, 3 * W), cos, sin)
        x2d = _outproj([y.reshape(T, W) for y in (y_a, y_b, y_c, y_d)], group_norm_g[l].reshape(4, W),
                       w_out, l, x2d)

        mem_kt, mem_v = _memkv(mem, mem_norm_g[l][None, :], xattn_wkv, l)
        x2d = _xattn(x2d.reshape(B, S, D), xattn_norm_g[l][None, :], xattn_wq, mem_kt, mem_v, xattn_wo,
                     l).reshape(T, D)

        w_r = jnp.pad(jnp.concatenate([router_expert_w[l], router_group_w[l]], axis=1),
                      ((0, 0), (0, LANES - MOE_EXPERTS - MOE_GROUPS)))
        w_r_hi = w_r.astype(BF16)
        w_r_lo = (w_r - w_r_hi.astype(F32)).astype(BF16)
        b_r = _pad_lanes(jnp.concatenate([router_expert_b[l], router_group_b[l]]))
        x2d = _moe(x2d, ffn_norm_g[l][None, :], jnp.stack([w_r_hi, w_r_lo]), b_r, w1_bf, w3_bf, w2_bf, l,
                   final_norm_g[None, :], final_norm=(l == depth - 1))
    return x2d.reshape(B, S, D)
```

```python
import functools
import math

import jax
import jax.numpy as jnp
from jax import lax
from jax.experimental import pallas as pl
from jax.experimental.pallas import tpu as pltpu

F32 = jnp.float32
BF16 = jnp.bfloat16

D_MODEL = 1024
GROUP_WIDTH = 256
HEAD_DIM = 64
N_HEADS = 4
NORM_EPS = 1e-6
CONV_WIDTH = 4
LRU_C = 8.0
SB_BLOCK = 128
SB_WINDOW_BLOCKS = 3
SB_CHAINS = 8
SSM_CHUNK = 128
SSM_STATE = 128
MOBA_BLOCK = 256
MOBA_TOPK = 3
ROPE_THETA = 10000.0
XATTN_HEADS = 4
XATTN_HEAD_DIM = 256
MEM_LEN = 256
MOE_GROUPS = 4
MOE_EPG = 4
MOE_EXPERTS = 16
MOE_FF = 256
LANES = 128
SUBLANES = 8
NEG_BIG = -1e30
SB_EXP_FLOOR = -104.0
IN_OUT_WIDTHS = (512, 768, 256, 768, LANES, 768)
IN_MAIN = 512 + 768 + 256 + 768
IN_OUT_DTYPES = (F32, BF16, F32, F32, F32, F32)
VMEM_LIMIT = 56 * 1024 * 1024


def _cparams(n_axes):
    return pltpu.CompilerParams(dimension_semantics=("arbitrary",) * n_axes,
                                vmem_limit_bytes=VMEM_LIMIT)


def _dot(a, b):
    return jnp.dot(a, b, preferred_element_type=F32)


def _dot_t(a, b):
    return lax.dot_general(a, b, (((1,), (1,)), ((), ())), preferred_element_type=F32)


def _dot_tl(a, b):
    return lax.dot_general(a, b, (((0,), (0,)), ((), ())), preferred_element_type=F32)


def _split2(x):
    hi = x.astype(BF16)
    lo = (x - hi.astype(F32)).astype(BF16)
    return hi, lo


def _split3(x):
    hi = x.astype(BF16)
    r = x - hi.astype(F32)
    mid = r.astype(BF16)
    lo = (r - mid.astype(F32)).astype(BF16)
    return hi, mid, lo


def _dot_wide_lhs(x, m_bf16, parts=3):
    pieces = _split3(x) if parts == 3 else _split2(x)
    out = _dot(pieces[0], m_bf16)
    for p in pieces[1:]:
        out = out + _dot(p, m_bf16)
    return out


def _rmsnorm(x, g):
    return x * lax.rsqrt(jnp.mean(x * x, axis=-1, keepdims=True) + NORM_EPS) * g


def _softplus(x):
    return jnp.maximum(x, 0.0) + jnp.log(1.0 + jnp.exp(-jnp.abs(x)))


def _sigmoid(x):
    return 1.0 / (1.0 + jnp.exp(-x))


def _silu(x):
    return x * _sigmoid(x)


def _gelu_tanh(x):
    return 0.5 * x * (1.0 + jnp.tanh(math.sqrt(2.0 / math.pi) * (x + 0.044715 * (x * x * x))))


def _shift_rows_down(x):
    rows = lax.broadcasted_iota(jnp.int32, x.shape, 0)
    return jnp.where(rows >= 1, pltpu.roll(x, 1, 0), 0.0)


def _causal_conv(x, w_ref, b_ref, cols):
    def taps(v, mask_rows):
        y = v * w_ref[CONV_WIDTH - 1:CONV_WIDTH, cols] + b_ref[:, cols]
        for s in range(1, CONV_WIDTH):
            vs = pltpu.roll(v, s, 0)
            if mask_rows is not None:
                vs = jnp.where(mask_rows >= s, vs, 0.0)
            y = y + vs * w_ref[CONV_WIDTH - 1 - s:CONV_WIDTH - s, cols]
        return y

    head = x[0:SUBLANES, :]
    y_head = taps(head, lax.broadcasted_iota(jnp.int32, head.shape, 0))
    return jnp.concatenate([y_head, taps(x, None)[SUBLANES:, :]], axis=0)


def _phase_conv(slab_ref, w_ref, b_ref, cols):
    P = SUBLANES
    nt = slab_ref.shape[0] // P
    x = [slab_ref[pl.ds(p, nt, stride=P), :] for p in range(P)]
    prev = {p: _shift_rows_down(x[p]) for p in range(P - CONV_WIDTH + 1, P)}
    out = []
    for p in range(P):
        y = x[p] * w_ref[CONV_WIDTH - 1:CONV_WIDTH, cols] + b_ref[:, cols]
        for k in range(1, CONV_WIDTH):
            src = x[p - k] if p - k >= 0 else prev[p - k + P]
            y = y + src * w_ref[CONV_WIDTH - 1 - k:CONV_WIDTH - k, cols]
        out.append(y)
    return out


def _inproj_kernel(x_ref, g_ref, w_ref, *refs):
    o_refs, w_s = refs[:-1], refs[-1]

    @pl.when(pl.program_id(0) == 0)
    def _():
        for c0 in range(0, IN_MAIN, 2 * LANES):
            w_s[:, c0:c0 + 2 * LANES] = w_ref[:, c0:c0 + 2 * LANES].astype(BF16)
        dt_tile = w_ref[:, IN_MAIN:IN_MAIN + LANES]
        lane = lax.broadcasted_iota(jnp.int32, dt_tile.shape, 1)
        w_s[:, IN_MAIN:IN_MAIN + LANES] = jnp.where(lane < N_HEADS, dt_tile, 0.0).astype(BF16)
        w_s[:, IN_MAIN + LANES:] = w_ref[:, IN_MAIN + N_HEADS:].astype(BF16)

    h = _rmsnorm(x_ref[...], g_ref[...]).astype(BF16)
    off = 0
    for o_ref, width in zip(o_refs, IN_OUT_WIDTHS):
        o_ref[...] = _dot(h, w_s[:, off:off + width]).astype(o_ref.dtype)
        off += width


def _inproj(x2d, g, w_in, layer, tm=1024):
    T = x2d.shape[0]
    n_in = w_in.shape[-1]
    return pl.pallas_call(
        _inproj_kernel,
        grid=(T // tm,),
        in_specs=[pl.BlockSpec((tm, D_MODEL), lambda i: (i, 0)),
                  pl.BlockSpec((1, D_MODEL), lambda i: (0, 0)),
                  pl.BlockSpec((None, D_MODEL, n_in), lambda i: (layer, 0, 0), pipeline_mode=pl.Buffered(1))],
        out_specs=[pl.BlockSpec((tm, w), lambda i: (i, 0)) for w in IN_OUT_WIDTHS],
        out_shape=[jax.ShapeDtypeStruct((T, w), dt) for w, dt in zip(IN_OUT_WIDTHS, IN_OUT_DTYPES)],
        scratch_shapes=[pltpu.VMEM((D_MODEL, sum(IN_OUT_WIDTHS)), BF16)],
        compiler_params=_cparams(1),
        name="inproj",
    )(x2d, g, w_in)


def _scan_rows(a, u):
    n = a.shape[0]
    rows = lax.broadcasted_iota(jnp.int32, a.shape, 0)
    shift = 1
    while shift < n:
        if shift < SUBLANES:
            keep = rows >= shift
            a_s = jnp.where(keep, pltpu.roll(a, shift, 0), 1.0)
            u_s = jnp.where(keep, pltpu.roll(u, shift, 0), 0.0)
            u = a * u_s + u
            a = a * a_s
        else:
            u = jnp.concatenate([u[:shift], a[shift:] * u[:n - shift] + u[shift:]], axis=0)
            a = jnp.concatenate([a[:shift], a[shift:] * a[:n - shift]], axis=0)
        shift *= 2
    return u


def _lru_kernel(xg_ref, cw_ref, cb_ref, wbd_ref, bri_ref, lam_ref, o_ref, in_s, out_s):
    S = xg_ref.shape[0]
    W = GROUP_WIDTH
    P = SUBLANES
    NT = S // P
    for s in range(2 * W // LANES):
        in_s[s] = xg_ref[:, s * LANES:(s + 1) * LANES]
    log_sig_lam = -_softplus(-lam_ref[...])

    for s in range(W // LANES):
        cols = slice(s * LANES, (s + 1) * LANES)
        xc_all = jnp.concatenate(_phase_conv(in_s.at[s], cw_ref, cb_ref, cols), axis=0)
        w_slab = jnp.concatenate([wbd_ref[cols, cols], wbd_ref[cols, W + s * LANES:W + (s + 1) * LANES]], axis=1)
        ri = _dot(xc_all.astype(BF16), w_slab)
        r = _sigmoid(ri[:, 0:LANES] + bri_ref[:, cols])
        i = _sigmoid(ri[:, LANES:2 * LANES] + bri_ref[:, W + s * LANES:W + (s + 1) * LANES])
        log_a = (LRU_C * r) * log_sig_lam[:, cols]
        a = jnp.exp(log_a)
        u = jnp.sqrt(1.0 - jnp.exp(2.0 * log_a)) * (i * xc_all)
        loc = [u[0:NT]]
        dec = [a[0:NT]]
        for p in range(1, P):
            ap = a[p * NT:(p + 1) * NT]
            loc.append(ap * loc[-1] + u[p * NT:(p + 1) * NT])
            dec.append(ap * dec[-1])
        carry = _shift_rows_down(_scan_rows(dec[-1], loc[-1]))
        for p in range(P):
            h = loc[p] + dec[p] * carry
            out_s[s, pl.ds(p, NT, stride=P), :] = h * _gelu_tanh(in_s[W // LANES + s, pl.ds(p, NT, stride=P), :])
    for s in range(W // LANES):
        o_ref[:, s * LANES:(s + 1) * LANES] = out_s[s]


def _lru(xg, conv_w, conv_b, w_bd, b_ri, lam):
    B, S, _ = xg.shape
    W = GROUP_WIDTH
    full = lambda shape: pl.BlockSpec(shape, lambda b: (0,) * len(shape))
    return pl.pallas_call(
        _lru_kernel,
        grid=(B,),
        in_specs=[pl.BlockSpec((None, S, 2 * W), lambda b: (b, 0, 0)),
                  full((CONV_WIDTH, W)), full((1, W)), full((W, 2 * W)), full((1, 2 * W)), full((1, W))],
        out_specs=pl.BlockSpec((None, S, W), lambda b: (b, 0, 0)),
        out_shape=jax.ShapeDtypeStruct((B, S, W), F32),
        scratch_shapes=[pltpu.VMEM((2 * W // LANES, S, LANES), F32), pltpu.VMEM((W // LANES, S, LANES), F32)],
        compiler_params=_cparams(1),
        name="rglru",
    )(xg, conv_w, conv_b, w_bd, b_ri, lam)


def _sb_kernel(qkv_ref, o_ref, kt_s, v_s, acc_s, later_s):
    i = pl.program_id(1)
    W = GROUP_WIDTH
    TB = SB_BLOCK
    R = N_HEADS * TB

    NW = SB_WINDOW_BLOCKS
    KW = NW * TB
    PAD = (NW - 1) * TB
    S = qkv_ref.shape[0]

    @pl.when(i == 0)
    def _():
        kt_s[:, 0:PAD] = jnp.zeros((W, PAD), BF16)
        for r0 in range(0, S, W):
            kt_s[:, PAD + r0:PAD + r0 + W] = qkv_ref[r0:r0 + W, W:2 * W].astype(F32).T.astype(BF16)
        v_s[0:PAD, :] = jnp.zeros((PAD, W), BF16)
        v_s[PAD:PAD + S, :] = qkv_ref[:, 2 * W:3 * W].astype(BF16)

    lane = lax.broadcasted_iota(jnp.int32, (TB, W), 1)
    heads = [(lane >= h * HEAD_DIM) & (lane < (h + 1) * HEAD_DIM) for h in range(N_HEADS)]
    r_loc = lax.broadcasted_iota(jnp.int32, (R, KW), 0) & (TB - 1)
    c_loc = lax.broadcasted_iota(jnp.int32, (R, KW), 1)
    ur = lax.broadcasted_iota(jnp.int32, (TB, 2 * TB), 0)
    uc = lax.broadcasted_iota(jnp.int32, (TB, 2 * TB), 1)
    tri_ones = jnp.where((ur > uc) | (uc >= TB), 1.0, 0.0).astype(BF16)

    n_chains = o_ref.shape[0]
    blocks = [i + c * (S // TB // n_chains) for c in range(n_chains)]
    qss = []
    for blk in blocks:
        q = qkv_ref[pl.ds(pl.multiple_of(blk * TB, TB), TB), 0:W] * (HEAD_DIM ** -0.5)
        qss.append(jnp.concatenate([jnp.where(hm, q, 0.0) for hm in heads], axis=0).astype(BF16))

    acc_s[...] = jnp.zeros_like(acc_s)
    later_s[...] = jnp.zeros_like(later_s)

    def window(c, n):
        blk = blocks[c]
        first_key = (blk - n * NW - (NW - 1)) * TB
        rows = pl.ds(pl.multiple_of(jnp.maximum(first_key + PAD, 0), TB), KW)
        z = _dot(qss[c], kt_s[:, rows])
        key_abs = first_key + c_loc
        live = (key_abs < blk * TB + r_loc) & (key_abs >= 0)
        sp = _softplus(z)
        lf = jnp.where(live, -sp, 0.0)
        lf16 = lf.astype(BF16)
        order = list(range(NW - 1, -1, -1))
        stacked = jnp.concatenate([lf16[:, b * TB:(b + 1) * TB] for b in order], axis=0)
        cs_all = _dot(stacked, tri_ones)
        offset = later_s[c]
        after = [None] * NW
        for pos, b in enumerate(order):
            cs = cs_all[pos * R:(pos + 1) * R, :]
            after[b] = cs[:, 0:TB] + offset
            offset = offset + cs[:, TB:2 * TB]
        w = jnp.where(live, jnp.exp((z - sp) + jnp.concatenate(after, axis=1)), 0.0)
        acc_s[c] += _dot(w.astype(BF16), v_s[rows, :])
        later_s[c] = offset
        return jnp.where((n + 1) * NW <= blk, jnp.max(offset), SB_EXP_FLOOR)

    def cond(carry):
        return carry[1] > SB_EXP_FLOOR

    def body(carry):
        n = carry[0]
        later_max = window(0, n)
        for c in range(1, n_chains):
            later_max = jnp.maximum(later_max, window(c, n))
        return n + 1, later_max

    lax.while_loop(cond, body, (jnp.int32(0), jnp.float32(0.0)))
    for c in range(n_chains):
        out = acc_s[c, 0:TB, :]
        for h in range(1, N_HEADS):
            out = jnp.where(heads[h], acc_s[c, h * TB:(h + 1) * TB, :], out)
        o_ref[c] = out


def _sb_attention(qkv):
    B, S, _ = qkv.shape
    W = GROUP_WIDTH
    pad = (SB_WINDOW_BLOCKS - 1) * SB_BLOCK
    nc = SB_CHAINS
    rows = N_HEADS * SB_BLOCK
    out = pl.pallas_call(
        _sb_kernel,
        grid=(B, S // SB_BLOCK // nc),
        in_specs=[pl.BlockSpec((None, S, 3 * W), lambda b, i: (b, 0, 0))],
        out_specs=pl.BlockSpec((None, nc, SB_BLOCK, W), lambda b, i: (b, 0, i, 0)),
        out_shape=jax.ShapeDtypeStruct((B, nc, S // nc, W), F32),
        scratch_shapes=[pltpu.VMEM((W, S + pad), BF16), pltpu.VMEM((S + pad, W), BF16),
                        pltpu.VMEM((nc, rows, W), F32), pltpu.VMEM((nc, rows, SB_BLOCK), F32)],
        compiler_params=_cparams(2),
        name="stickbreak",
    )(qkv)
    return out.reshape(B, S, W)


def _ssd_kernel(z_ref, xbc_ref, dt_ref, cw_ref, cb_ref, dtb_ref, alog_ref, dskip_ref, o_ref, xbc_s):
    S = z_ref.shape[0]
    W = GROUP_WIDTH
    L = SSM_CHUNK
    for s in range(3 * W // LANES):
        cols = slice(s * LANES, (s + 1) * LANES)
        xbc_s[s] = _silu(_causal_conv(xbc_ref[:, cols], cw_ref, cb_ref, cols))
    a_row = -jnp.exp(alog_ref[...])

    r_i = lax.broadcasted_iota(jnp.int32, (L, L), 0)
    c_i = lax.broadcasted_iota(jnp.int32, (L, L), 1)
    tri_incl = jnp.where(c_i <= r_i, 1.0, 0.0).astype(BF16)
    lower = c_i <= r_i
    e_r = lax.broadcasted_iota(jnp.int32, (LANES, W), 0)
    e_c = lax.broadcasted_iota(jnp.int32, (LANES, W), 1)
    expand = jnp.where((e_c >= e_r * HEAD_DIM) & (e_c < (e_r + 1) * HEAD_DIM), 1.0, 0.0).astype(BF16)
    lane_l = lax.broadcasted_iota(jnp.int32, (L, LANES), 1)

    def chunk(c, states):
        rows = slice(c * L, (c + 1) * L)
        xs = jnp.concatenate([xbc_s[0, rows, :], xbc_s[1, rows, :]], axis=1)
        dt = _softplus(dt_ref[rows, :] + dtb_ref[...])
        a_dt = dt * a_row
        cs_col = _dot_wide_lhs_rhs(tri_incl, a_dt)
        cs_row = cs_col.T
        cs_full = _dot_wide_lhs(cs_col, expand)
        dt_full = _dot_wide_lhs(dt, expand)
        xd = xs * dt_full
        tot = cs_full[L - 1:L, :]
        xdec = (xd * jnp.exp(tot - cs_full)).astype(BF16)
        xd16 = xd.astype(BF16)
        ys = []
        new_states = []
        for g in range(2):
            gl = slice(g * LANES, (g + 1) * LANES)
            bm = xbc_s[2 + g, rows, :].astype(BF16)
            cm = xbc_s[4 + g, rows, :].astype(BF16)
            cb = _dot_t(cm, bm)
            prev = states[g]
            y_off = _dot(cm, prev.astype(BF16)) * jnp.exp(cs_full[:, gl])
            y_g = y_off
            for hh in range(2):
                h = 2 * g + hh
                seg = jnp.where(lower, cs_col[:, h:h + 1] - cs_row[h:h + 1, :], -jnp.inf)
                y_h = _dot((cb * jnp.exp(seg)).astype(BF16), xd16[:, gl])
                in_head = (lane_l >= hh * HEAD_DIM) & (lane_l < (hh + 1) * HEAD_DIM)
                y_g = y_g + jnp.where(in_head, y_h, 0.0)
            new_states.append(prev * jnp.exp(tot[:, gl]) + _dot_tl(bm, xdec[:, gl]))
            ys.append(y_g)
        y = jnp.concatenate(ys, axis=1) + dskip_ref[...] * xs
        o_ref[rows, :] = y * _silu(z_ref[rows, :])
        return new_states

    states = [jnp.zeros((SSM_STATE, LANES), F32) for _ in range(2)]
    for c in range(S // L):
        states = chunk(c, states)


def _dot_wide_lhs_rhs(m_bf16, x):
    hi, mid, lo = _split3(x)
    return _dot(m_bf16, hi) + _dot(m_bf16, mid) + _dot(m_bf16, lo)


def _ssd(z, xbc, dt, conv_w, conv_b, dt_bias, a_log, d_skip):
    B, S, _ = z.shape
    W = GROUP_WIDTH
    full = lambda shape: pl.BlockSpec(shape, lambda b: (0,) * len(shape))
    return pl.pallas_call(
        _ssd_kernel,
        grid=(B,),
        in_specs=[pl.BlockSpec((None, S, W), lambda b: (b, 0, 0)),
                  pl.BlockSpec((None, S, 3 * W), lambda b: (b, 0, 0)),
                  pl.BlockSpec((None, S, LANES), lambda b: (b, 0, 0)),
                  full((CONV_WIDTH, 3 * W)), full((1, 3 * W)), full((1, LANES)), full((1, LANES)),
                  full((1, W))],
        out_specs=pl.BlockSpec((None, S, W), lambda b: (b, 0, 0)),
        out_shape=jax.ShapeDtypeStruct((B, S, W), F32),
        scratch_shapes=[pltpu.VMEM((3 * W // LANES, S, LANES), F32)],
        compiler_params=_cparams(1),
        name="ssd",
    )(z, xbc, dt, conv_w, conv_b, dt_bias, a_log, d_skip)


def _rope(x, cos, sin):
    lane = lax.broadcasted_iota(jnp.int32, (x.shape[0], LANES), 1)
    first_half = (lane % HEAD_DIM) < (HEAD_DIM // 2)
    halves = []
    for p in range(x.shape[1] // LANES):
        xp = x[:, p * LANES:(p + 1) * LANES]
        fwd = pltpu.roll(xp, HEAD_DIM // 2, 1)
        bwd = pltpu.roll(xp, LANES - HEAD_DIM // 2, 1)
        halves.append(jnp.where(first_half, -bwd, fwd))
    rot = jnp.concatenate(halves, axis=1)
    return x * cos + rot * sin


def _moba_kernel(qkv_ref, cos_ref, sin_ref, o_ref, k_s, vt_s, kmean_s, acc_s, bias_s):
    i = pl.program_id(1)
    W = GROUP_WIDTH
    TB = MOBA_BLOCK
    S = qkv_ref.shape[0]
    NB = S // TB

    @pl.when(i == 0)
    def _():
        for blk in range(NB):
            rs = slice(blk * TB, (blk + 1) * TB)
            kb = _rope(qkv_ref[rs, W:2 * W], cos_ref[rs, :], sin_ref[rs, :])
            for h in range(N_HEADS):
                k_s[h, rs, :] = kb[:, h * HEAD_DIM:(h + 1) * HEAD_DIM].astype(BF16)
            kmean_s[blk:blk + 1, :] = jnp.mean(kb, axis=0, keepdims=True)
            vt_s[blk] =qkv_ref[rs, 2 * W:3 * W].T.astype(BF16)

    rows_i = pl.ds(pl.multiple_of(i * TB, TB), TB)
    q = _rope(qkv_ref[rows_i, 0:W], cos_ref[rows_i, :], sin_ref[rows_i, :])
    lane8 = lax.broadcasted_iota(jnp.int32, (NB, W), 1)
    blk_id = lax.broadcasted_iota(jnp.int32, (NB, TB), 0)
    R = N_HEADS * TB
    key_loc = lax.broadcasted_iota(jnp.int32, (TB, R), 0)
    q_loc = lax.broadcasted_iota(jnp.int32, (TB, R), 1) & (TB - 1)
    kmean = kmean_s[...]
    q_t = q.T
    qt_hi, qt_lo = _split2(q_t)
    scale = HEAD_DIM ** -0.5

    km_all = jnp.concatenate(
        [jnp.where((lane8 >= h * HEAD_DIM) & (lane8 < (h + 1) * HEAD_DIM), kmean, 0.0) for h in range(N_HEADS)],
        axis=0)
    km_hi, km_lo = _split2(km_all)
    gate_all = _dot(km_hi, qt_hi) + _dot(km_hi, qt_lo) + _dot(km_lo, qt_hi)
    qhs = []
    for h in range(N_HEADS):
        gate = gate_all[h * NB:(h + 1) * NB, :]
        cnt = jnp.zeros((NB, TB), F32)
        for jp in range(NB):
            row = gate[jp:jp + 1, :]
            beats = (row > gate) | ((row == gate) & (blk_id > jp))
            cnt = cnt + jnp.where(beats, jnp.where(jp < i, 1.0, 0.0), 0.0)
        selected = (cnt < float(MOBA_TOPK)) & (blk_id < i)
        bias_s[:, h * TB:(h + 1) * TB] = jnp.where(selected, 0.0, NEG_BIG)
        qhs.append((q_t[h * HEAD_DIM:(h + 1) * HEAD_DIM, :] * scale).astype(BF16))

    def scores(rows):
        return jnp.concatenate([_dot(k_s[h, rows, :], qhs[h]) for h in range(N_HEADS)], axis=1)

    s = jnp.where(key_loc <= q_loc, scores(rows_i), NEG_BIG)
    m = jnp.max(s, axis=0, keepdims=True)
    p = jnp.exp(s - m)
    l = jnp.sum(p, axis=0, keepdims=True)
    def weighted_values(j, p):
        p16 = p.astype(BF16)
        return jnp.concatenate(
            [_dot(vt_s[j, h * HEAD_DIM:(h + 1) * HEAD_DIM, :], p16[:, h * TB:(h + 1) * TB]) for h in range(N_HEADS)],
            axis=1)

    acc_s[...] = weighted_values(i, p)

    def body(j, carry):
        m, l = carry
        rows = pl.ds(pl.multiple_of(j * TB, TB), TB)
        s = scores(rows) + bias_s[pl.ds(j, 1), :]
        m_new = jnp.maximum(m, jnp.max(s, axis=0, keepdims=True))
        alpha = jnp.exp(m - m_new)
        p = jnp.exp(s - m_new)
        l = alpha * l + jnp.sum(p, axis=0, keepdims=True)
        acc_s[...] = alpha * acc_s[...] + weighted_values(j, p)
        return m_new, l

    m, l = lax.fori_loop(0, i, body, (m, l))
    outs = acc_s[...] / l
    o_ref[...] = jnp.concatenate([outs[:, h * TB:(h + 1) * TB].T for h in range(N_HEADS)], axis=1)


def _moba(qkv, cos, sin):
    B, S, _ = qkv.shape
    W = GROUP_WIDTH
    nb = S // MOBA_BLOCK
    return pl.pallas_call(
        _moba_kernel,
        grid=(B, S // MOBA_BLOCK),
        in_specs=[pl.BlockSpec((None, S, 3 * W), lambda b, i: (b, 0, 0)),
                  pl.BlockSpec((S, W), lambda b, i: (0, 0)),
                  pl.BlockSpec((S, W), lambda b, i: (0, 0))],
        out_specs=pl.BlockSpec((None, MOBA_BLOCK, W), lambda b, i: (b, i, 0)),
        out_shape=jax.ShapeDtypeStruct((B, S, W), F32),
        scratch_shapes=[pltpu.VMEM((N_HEADS, S, HEAD_DIM), BF16), pltpu.VMEM((nb, W, MOBA_BLOCK), BF16),
                        pltpu.VMEM((nb, W), F32), pltpu.VMEM((HEAD_DIM, N_HEADS * MOBA_BLOCK), F32),
                        pltpu.VMEM((nb, N_HEADS * MOBA_BLOCK), F32)],
        compiler_params=_cparams(2),
        name="moba",
    )(qkv, cos, sin)


def _outproj_kernel(ya_ref, yb_ref, yc_ref, yd_ref, gg_ref, w_ref, x_ref, o_ref):
    W = GROUP_WIDTH
    acc = x_ref[...]
    for g, y_ref in enumerate((ya_ref, yb_ref, yc_ref, yd_ref)):
        yn = _rmsnorm(y_ref[...], gg_ref[g:g + 1, :]).astype(BF16)
        acc = acc + _dot(yn, w_ref[g * W:(g + 1) * W, :].astype(BF16))
    o_ref[...] = acc


def _outproj(ys, gg, w_out, layer, x2d, tm=1024):
    T = x2d.shape[0]
    W = GROUP_WIDTH
    return pl.pallas_call(
        _outproj_kernel,
        grid=(T // tm,),
        in_specs=[pl.BlockSpec((tm, W), lambda i: (i, 0))] * 4
                 + [pl.BlockSpec((4, W), lambda i: (0, 0)),
                    pl.BlockSpec((None, 4 * W, D_MODEL), lambda i: (layer, 0, 0), pipeline_mode=pl.Buffered(1)),
                    pl.BlockSpec((tm, D_MODEL), lambda i: (i, 0))],
        out_specs=pl.BlockSpec((tm, D_MODEL), lambda i: (i, 0)),
        out_shape=jax.ShapeDtypeStruct((T, D_MODEL), F32),
        compiler_params=_cparams(1),
        name="outproj",
    )(*ys, gg, w_out, x2d)


def _memkv_kernel(m_ref, g_ref, w_ref, kt_ref, v_ref):
    mn = _rmsnorm(m_ref[...], g_ref[...]).astype(BF16)
    kv = _dot(mn, w_ref[...].astype(BF16))
    kt_ref[...] = kv[:, 0:D_MODEL].T.astype(BF16)
    v_ref[...] = kv[:, D_MODEL:2 * D_MODEL].astype(BF16)


def _memkv(mem, g, wkv, layer):
    B, M, _ = mem.shape
    return pl.pallas_call(
        _memkv_kernel,
        grid=(B,),
        in_specs=[pl.BlockSpec((None, M, D_MODEL), lambda b: (b, 0, 0)),
                  pl.BlockSpec((1, D_MODEL), lambda b: (0, 0)),
                  pl.BlockSpec((None, D_MODEL, 2 * D_MODEL), lambda b: (layer, 0, 0),
                               pipeline_mode=pl.Buffered(1))],
        out_specs=[pl.BlockSpec((None, D_MODEL, M), lambda b: (b, 0, 0)),
                   pl.BlockSpec((None, M, D_MODEL), lambda b: (b, 0, 0))],
        out_shape=[jax.ShapeDtypeStruct((B, D_MODEL, M), BF16), jax.ShapeDtypeStruct((B, M, D_MODEL), BF16)],
        compiler_params=_cparams(1),
        name="memkv",
    )(mem, g, wkv)


def _xattn_kernel(x_ref, g_ref, wq_ref, kt_ref, v_ref, wo_ref, o_ref):
    x = x_ref[...]
    h = _rmsnorm(x, g_ref[...]).astype(BF16)
    q = (_dot(h, wq_ref[...].astype(BF16)) * (XATTN_HEAD_DIM ** -0.5)).astype(BF16)
    heads = []
    for hd in range(XATTN_HEADS):
        cols = slice(hd * XATTN_HEAD_DIM, (hd + 1) * XATTN_HEAD_DIM)
        s = _dot(q[:, cols], kt_ref[cols, :])
        p = jnp.exp(s - jnp.max(s, axis=-1, keepdims=True))
        p = p / jnp.sum(p, axis=-1, keepdims=True)
        heads.append(_dot(p.astype(BF16), v_ref[:, cols]).astype(BF16))
    o_ref[...] = x + _dot(jnp.concatenate(heads, axis=1), wo_ref[...].astype(BF16))


def _xattn(x3d, g, wq, kt, v, wo, layer, tm=1024):
    B, S, _ = x3d.shape
    M = v.shape[1]
    weight = pl.BlockSpec((None, D_MODEL, D_MODEL), lambda b, i: (layer, 0, 0), pipeline_mode=pl.Buffered(1))
    return pl.pallas_call(
        _xattn_kernel,
        grid=(B, S // tm),
        in_specs=[pl.BlockSpec((None, tm, D_MODEL), lambda b, i: (b, i, 0)),
                  pl.BlockSpec((1, D_MODEL), lambda b, i: (0, 0)),
                  weight,
                  pl.BlockSpec((None, D_MODEL, M), lambda b, i: (b, 0, 0)),
                  pl.BlockSpec((None, M, D_MODEL), lambda b, i: (b, 0, 0)),
                  weight],
        out_specs=pl.BlockSpec((None, tm, D_MODEL), lambda b, i: (b, i, 0)),
        out_shape=jax.ShapeDtypeStruct((B, S, D_MODEL), F32),
        compiler_params=_cparams(2),
        name="xattn",
    )(x3d, g, wq, kt, v, wo)


def _moe_kernel(x_ref, g_ref, wr_ref, br_ref, w1_ref, w3_ref, w2_ref, fg_ref, o_ref,
                t_s, comb_s, acc_s, xkeep_s, *, final_norm):
    tile = pl.program_id(0)
    grp = pl.program_id(1)
    tm = x_ref.shape[0]
    slot = lax.bitwise_and(tile, 1)

    def route(dst):
        t = _rmsnorm(x_ref[...], g_ref[...])
        t_s[dst] = t.astype(BF16)
        t_hi, t_lo = _split2(t)
        logits = (_dot(t_hi, wr_ref[0]) + _dot(t_hi, wr_ref[1]) + _dot(t_lo, wr_ref[0])) + br_ref[...]
        lt = logits.T
        gsl = SUBLANES * (MOE_EXPERTS // SUBLANES)
        g_row = lax.broadcasted_iota(jnp.int32, (SUBLANES, tm), 0).astype(F32)
        lg = jnp.where(g_row < float(MOE_GROUPS), lt[gsl:gsl + SUBLANES, :], NEG_BIG)
        gmax = jnp.max(lg, axis=0, keepdims=True)
        pg_top = 1.0 / jnp.sum(jnp.exp(lg - gmax), axis=0, keepdims=True)
        g_idx = jnp.min(jnp.where(lg == gmax, g_row, 1e9), axis=0, keepdims=True)
        e_row = lax.broadcasted_iota(jnp.int32, (MOE_EXPERTS, tm), 0).astype(F32)
        in_grp = jnp.floor(e_row * (1.0 / MOE_EPG)) == g_idx
        le = jnp.where(in_grp, lt[0:MOE_EXPERTS, :], NEG_BIG)
        e1 = jnp.max(le, axis=0, keepdims=True)
        i1 = jnp.min(jnp.where(in_grp & (le == e1), e_row, 1e9), axis=0, keepdims=True)
        le2 = jnp.where(e_row == i1, NEG_BIG, le)
        e2 = jnp.max(le2, axis=0, keepdims=True)
        i2 = jnp.min(jnp.where(in_grp & (e_row != i1) & (le2 == e2), e_row, 1e9), axis=0, keepdims=True)
        r2 = jnp.exp(e2 - e1)
        w_first = 1.0 / (1.0 + r2)
        w_second = r2 / (1.0 + r2)
        comb_t = pg_top * (jnp.where(e_row == i1, w_first, 0.0) + jnp.where(e_row == i2, w_second, 0.0))
        comb_s[dst] = jnp.concatenate([comb_t, jnp.zeros((LANES - MOE_EXPERTS, tm), F32)], axis=0).T

    def experts():
        t = t_s[slot]
        comb = comb_s[slot]
        lane = lax.broadcasted_iota(jnp.int32, (tm, LANES), 1)
        hids = []
        for e in range(MOE_EPG):
            n = grp * MOE_EPG + e
            c = jnp.sum(jnp.where(lane == n, comb, 0.0), axis=-1, keepdims=True)
            hid = _silu(_dot(t, w1_ref[e])) * _dot(t, w3_ref[e])
            hids.append((hid * c).astype(BF16))
        return _dot(jnp.concatenate(hids, axis=1), w2_ref[...].reshape(MOE_EPG * MOE_FF, D_MODEL))

    @pl.when((tile == 0) & (grp == 0))
    def _():
        route(0)

    @pl.when(grp == 0)
    def _():
        xkeep_s[...] = x_ref[...]
        acc_s[...] = experts()

    @pl.when((grp > 0) & (grp < MOE_GROUPS - 1))
    def _():
        acc_s[...] += experts()

    @pl.when(grp == MOE_GROUPS - 1)
    def _():
        y = xkeep_s[...] + (acc_s[...] + experts())
        route(1 - slot)
        if final_norm:
            y = _rmsnorm(y, fg_ref[...])
        o_ref[...] = y


def _moe(x2d, g, w_router, b_router, w1, w3, w2, layer, final_g, final_norm, tm=1024):
    T = x2d.shape[0]
    n_tiles = T // tm

    def x_window(i, e):
        return jnp.minimum(i + e // (MOE_GROUPS - 1), n_tiles - 1), 0

    return pl.pallas_call(
        functools.partial(_moe_kernel, final_norm=final_norm),
        grid=(n_tiles, MOE_GROUPS),
        in_specs=[pl.BlockSpec((tm, D_MODEL), x_window),
                  pl.BlockSpec((1, D_MODEL), lambda i, e: (0, 0)),
                  pl.BlockSpec((2, D_MODEL, LANES), lambda i, e: (0, 0, 0)),
                  pl.BlockSpec((1, LANES), lambda i, e: (0, 0)),
                  pl.BlockSpec((None, MOE_EPG, D_MODEL, MOE_FF), lambda i, e: (layer, e, 0, 0)),
                  pl.BlockSpec((None, MOE_EPG, D_MODEL, MOE_FF), lambda i, e: (layer, e, 0, 0)),
                  pl.BlockSpec((None, MOE_EPG, MOE_FF, D_MODEL), lambda i, e: (layer, e, 0, 0)),
                  pl.BlockSpec((1, D_MODEL), lambda i, e: (0, 0))],
        out_specs=pl.BlockSpec((tm, D_MODEL), lambda i, e: (i, 0)),
        out_shape=jax.ShapeDtypeStruct((T, D_MODEL), F32),
        scratch_shapes=[pltpu.VMEM((2, tm, D_MODEL), BF16), pltpu.VMEM((2, tm, LANES), F32),
                        pltpu.VMEM((tm, D_MODEL), F32), pltpu.VMEM((tm, D_MODEL), F32)],
        compiler_params=_cparams(2),
        name="moe",
    )(x2d, g, w_router, b_router, w1, w3, w2, final_g)


def _pad_lanes(v, width=LANES):
    return jnp.pad(v, (0, width - v.shape[0]))[None, :]


def _block_diag(w):
    H, n, _ = w.shape
    eye = jnp.eye(H, dtype=w.dtype)
    return (eye[:, None, :, None] * w[:, :, None, :]).reshape(H * n, H * n)


def _rope_tables(S):
    half = HEAD_DIM // 2
    inv_freq = ROPE_THETA ** (-jnp.arange(half, dtype=F32) / half)
    ang = jnp.arange(S, dtype=F32)[:, None] * inv_freq[None, :]
    reps = GROUP_WIDTH // half
    return jnp.tile(jnp.cos(ang), (1, reps)), jnp.tile(jnp.sin(ang), (1, reps))


def kernel(x, mem, mix_norm_g, w_in, lru_conv_w, lru_conv_b, lru_wr, lru_br, lru_wi, lru_bi, lru_lambda, ssm_conv_w, ssm_conv_b, ssm_dt_bias, ssm_a_log, ssm_d, group_norm_g, w_out, xattn_norm_g, mem_norm_g, xattn_wq, xattn_wkv, xattn_wo, ffn_norm_g, router_group_w, router_group_b, router_expert_w, router_expert_b, expert_w1, expert_w3, expert_w2, final_norm_g):
    B, S, D = x.shape
    T = B * S
    depth = w_in.shape[0]
    W = GROUP_WIDTH
    cos, sin = _rope_tables(S)
    x2d = x.reshape(T, D)
    w1_bf, w3_bf, w2_bf = expert_w1.astype(BF16), expert_w3.astype(BF16), expert_w2.astype(BF16)
    for l in range(depth):
        lru_xg, sb_qkv, ssm_z, ssm_xbc, ssm_dt, mb_qkv = _inproj(x2d, mix_norm_g[l][None, :], w_in, l)

        w_bd = jnp.concatenate([_block_diag(lru_wr[l]), _block_diag(lru_wi[l])], axis=1).astype(BF16)
        b_ri = jnp.concatenate([lru_br[l], lru_bi[l]])[None, :]
        y_a = _lru(lru_xg.reshape(B, S, 2 * W), lru_conv_w[l], lru_conv_b[l][None, :], w_bd, b_ri,
                   lru_lambda[l][None, :])
        y_b = _sb_attention(sb_qkv.reshape(B, S, 3 * W))
        y_c = _ssd(ssm_z.reshape(B, S, W), ssm_xbc.reshape(B, S, 3 * W), ssm_dt.reshape(B, S, LANES),
                   ssm_conv_w[l], ssm_conv_b[l][None, :], _pad_lanes(ssm_dt_bias[l]), _pad_lanes(ssm_a_log[l]),
                   jnp.repeat(ssm_d[l], HEAD_DIM)[None, :])
        y_d = _moba(mb_qkv.reshape(B, S, 3 * W), cos, sin)
        x2d = _outproj([y.reshape(T, W) for y in (y_a, y_b, y_c, y_d)], group_norm_g[l].reshape(4, W),
                       w_out, l, x2d)

        mem_kt, mem_v = _memkv(mem, mem_norm_g[l][None, :], xattn_wkv, l)
        x2d = _xattn(x2d.reshape(B, S, D), xattn_norm_g[l][None, :], xattn_wq, mem_kt, mem_v, xattn_wo,
                     l).reshape(T, D)

        w_r = jnp.pad(jnp.concatenate([router_expert_w[l], router_group_w[l]], axis=1),
                      ((0, 0), (0, LANES - MOE_EXPERTS - MOE_GROUPS)))
        w_r_hi = w_r.astype(BF16)
        w_r_lo = (w_r - w_r_hi.astype(F32)).astype(BF16)
        b_r = _pad_lanes(jnp.concatenate([router_expert_b[l], router_group_b[l]]))
        x2d = _moe(x2d, ffn_norm_g[l][None, :], jnp.stack([w_r_hi, w_r_lo]), b_r, w1_bf, w3_bf, w2_bf, l,
                   final_norm_g[None, :], final_norm=(l == depth - 1))
    return x2d.reshape(B, S, D)
```

```python
import functools
import math

import jax
import jax.numpy as jnp
from jax import lax
from jax.experimental import pallas as pl
from jax.experimental.pallas import tpu as pltpu

F32 = jnp.float32
BF16 = jnp.bfloat16

D_MODEL = 1024
GROUP_WIDTH = 256
HEAD_DIM = 64
N_HEADS = 4
NORM_EPS = 1e-6
CONV_WIDTH = 4
LRU_C = 8.0
SB_BLOCK = 128
SB_WINDOW_BLOCKS = 3
SB_CHAINS = 8
SSM_CHUNK = 128
SSM_STATE = 128
MOBA_BLOCK = 256
MOBA_TOPK = 3
ROPE_THETA = 10000.0
XATTN_HEADS = 4
XATTN_HEAD_DIM = 256
MEM_LEN = 256
MOE_GROUPS = 4
MOE_EPG = 4
MOE_EXPERTS = 16
MOE_FF = 256
LANES = 128
SUBLANES = 8
NEG_BIG = -1e30
SB_EXP_FLOOR = -104.0
IN_OUT_WIDTHS = (512, 768, 256, 768, LANES, 768)
IN_MAIN = 512 + 768 + 256 + 768
IN_OUT_DTYPES = (F32, BF16, F32, F32, F32, F32)
VMEM_LIMIT = 56 * 1024 * 1024


def _cparams(n_axes):
    return pltpu.CompilerParams(dimension_semantics=("arbitrary",) * n_axes,
                                vmem_limit_bytes=VMEM_LIMIT)


def _dot(a, b):
    return jnp.dot(a, b, preferred_element_type=F32)


def _dot_t(a, b):
    return lax.dot_general(a, b, (((1,), (1,)), ((), ())), preferred_element_type=F32)


def _dot_tl(a, b):
    return lax.dot_general(a, b, (((0,), (0,)), ((), ())), preferred_element_type=F32)


def _split2(x):
    hi = x.astype(BF16)
    lo = (x - hi.astype(F32)).astype(BF16)
    return hi, lo


def _split3(x):
    hi = x.astype(BF16)
    r = x - hi.astype(F32)
    mid = r.astype(BF16)
    lo = (r - mid.astype(F32)).astype(BF16)
    return hi, mid, lo


def _dot_wide_lhs(x, m_bf16, parts=3):
    pieces = _split3(x) if parts == 3 else _split2(x)
    out = _dot(pieces[0], m_bf16)
    for p in pieces[1:]:
        out = out + _dot(p, m_bf16)
    return out


def _rmsnorm(x, g):
    return x * lax.rsqrt(jnp.mean(x * x, axis=-1, keepdims=True) + NORM_EPS) * g


def _softplus(x):
    return jnp.maximum(x, 0.0) + jnp.log(1.0 + jnp.exp(-jnp.abs(x)))


def _sigmoid(x):
    return 1.0 / (1.0 + jnp.exp(-x))


def _silu(x):
    return x * _sigmoid(x)


def _gelu_tanh(x):
    return 0.5 * x * (1.0 + jnp.tanh(math.sqrt(2.0 / math.pi) * (x + 0.044715 * (x * x * x))))


def _shift_rows_down(x):
    rows = lax.broadcasted_iota(jnp.int32, x.shape, 0)
    return jnp.where(rows >= 1, pltpu.roll(x, 1, 0), 0.0)


def _causal_conv(x, w_ref, b_ref, cols):
    def taps(v, mask_rows):
        y = v * w_ref[CONV_WIDTH - 1:CONV_WIDTH, cols] + b_ref[:, cols]
        for s in range(1, CONV_WIDTH):
            vs = pltpu.roll(v, s, 0)
            if mask_rows is not None:
                vs = jnp.where(mask_rows >= s, vs, 0.0)
            y = y + vs * w_ref[CONV_WIDTH - 1 - s:CONV_WIDTH - s, cols]
        return y

    head = x[0:SUBLANES, :]
    y_head = taps(head, lax.broadcasted_iota(jnp.int32, head.shape, 0))
    return jnp.concatenate([y_head, taps(x, None)[SUBLANES:, :]], axis=0)


def _phase_conv(slab_ref, w_ref, b_ref, cols):
    P = SUBLANES
    nt = slab_ref.shape[0] // P
    x = [slab_ref[pl.ds(p, nt, stride=P), :] for p in range(P)]
    prev = {p: _shift_rows_down(x[p]) for p in range(P - CONV_WIDTH + 1, P)}
    out = []
    for p in range(P):
        y = x[p] * w_ref[CONV_WIDTH - 1:CONV_WIDTH, cols] + b_ref[:, cols]
        for k in range(1, CONV_WIDTH):
            src = x[p - k] if p - k >= 0 else prev[p - k + P]
            y = y + src * w_ref[CONV_WIDTH - 1 - k:CONV_WIDTH - k, cols]
        out.append(y)
    return out


def _inproj_kernel(x_ref, g_ref, w_ref, *refs):
    o_refs, w_s = refs[:-1], refs[-1]

    @pl.when(pl.program_id(0) == 0)
    def _():
        for c0 in range(0, IN_MAIN, 2 * LANES):
            w_s[:, c0:c0 + 2 * LANES] = w_ref[:, c0:c0 + 2 * LANES].astype(BF16)
        dt_tile = w_ref[:, IN_MAIN:IN_MAIN + LANES]
        lane = lax.broadcasted_iota(jnp.int32, dt_tile.shape, 1)
        w_s[:, IN_MAIN:IN_MAIN + LANES] = jnp.where(lane < N_HEADS, dt_tile, 0.0).astype(BF16)
        w_s[:, IN_MAIN + LANES:] = w_ref[:, IN_MAIN + N_HEADS:].astype(BF16)

    h = _rmsnorm(x_ref[...], g_ref[...]).astype(BF16)
    off = 0
    for o_ref, width in zip(o_refs, IN_OUT_WIDTHS):
        o_ref[...] = _dot(h, w_s[:, off:off + width]).astype(o_ref.dtype)
        off += width


def _inproj(x2d, g, w_in, layer, tm=1024):
    T = x2d.shape[0]
    n_in = w_in.shape[-1]
    return pl.pallas_call(
        _inproj_kernel,
        grid=(T // tm,),
        in_specs=[pl.BlockSpec((tm, D_MODEL), lambda i: (i, 0)),
                  pl.BlockSpec((1, D_MODEL), lambda i: (0, 0)),
                  pl.BlockSpec((None, D_MODEL, n_in), lambda i: (layer, 0, 0), pipeline_mode=pl.Buffered(1))],
        out_specs=[pl.BlockSpec((tm, w), lambda i: (i, 0)) for w in IN_OUT_WIDTHS],
        out_shape=[jax.ShapeDtypeStruct((T, w), dt) for w, dt in zip(IN_OUT_WIDTHS, IN_OUT_DTYPES)],
        scratch_shapes=[pltpu.VMEM((D_MODEL, sum(IN_OUT_WIDTHS)), BF16)],
        compiler_params=_cparams(1),
        name="inproj",
    )(x2d, g, w_in)


def _scan_rows(a, u):
    n = a.shape[0]
    rows = lax.broadcasted_iota(jnp.int32, a.shape, 0)
    shift = 1
    while shift < n:
        if shift < SUBLANES:
            keep = rows >= shift
            a_s = jnp.where(keep, pltpu.roll(a, shift, 0), 1.0)
            u_s = jnp.where(keep, pltpu.roll(u, shift, 0), 0.0)
            u = a * u_s + u
            a = a * a_s
        else:
            u = jnp.concatenate([u[:shift], a[shift:] * u[:n - shift] + u[shift:]], axis=0)
            a = jnp.concatenate([a[:shift], a[shift:] * a[:n - shift]], axis=0)
        shift *= 2
    return u


def _lru_kernel(xg_ref, cw_ref, cb_ref, wbd_ref, bri_ref, lam_ref, o_ref, in_s, out_s):
    S = xg_ref.shape[0]
    W = GROUP_WIDTH
    P = SUBLANES
    NT = S // P
    for s in range(2 * W // LANES):
        in_s[s] = xg_ref[:, s * LANES:(s + 1) * LANES]
    log_sig_lam = -_softplus(-lam_ref[...])

    for s in range(W // LANES):
        cols = slice(s * LANES, (s + 1) * LANES)
        xc_all = jnp.concatenate(_phase_conv(in_s.at[s], cw_ref, cb_ref, cols), axis=0)
        w_slab = jnp.concatenate([wbd_ref[cols, cols], wbd_ref[cols, W + s * LANES:W + (s + 1) * LANES]], axis=1)
        ri = _dot(xc_all.astype(BF16), w_slab)
        r = _sigmoid(ri[:, 0:LANES] + bri_ref[:, cols])
        i = _sigmoid(ri[:, LANES:2 * LANES] + bri_ref[:, W + s * LANES:W + (s + 1) * LANES])
        log_a = (LRU_C * r) * log_sig_lam[:, cols]
        a = jnp.exp(log_a)
        u = jnp.sqrt(1.0 - jnp.exp(2.0 * log_a)) * (i * xc_all)
        loc = [u[0:NT]]
        dec = [a[0:NT]]
        for p in range(1, P):
            ap = a[p * NT:(p + 1) * NT]
            loc.append(ap * loc[-1] + u[p * NT:(p + 1) * NT])
            dec.append(ap * dec[-1])
        carry = _shift_rows_down(_scan_rows(dec[-1], loc[-1]))
        for p in range(P):
            h = loc[p] + dec[p] * carry
            out_s[s, pl.ds(p, NT, stride=P), :] = h * _gelu_tanh(in_s[W // LANES + s, pl.ds(p, NT, stride=P), :])
    for s in range(W // LANES):
        o_ref[:, s * LANES:(s + 1) * LANES] = out_s[s]


def _lru(xg, conv_w, conv_b, w_bd, b_ri, lam):
    B, S, _ = xg.shape
    W = GROUP_WIDTH
    full = lambda shape: pl.BlockSpec(shape, lambda b: (0,) * len(shape))
    return pl.pallas_call(
        _lru_kernel,
        grid=(B,),
        in_specs=[pl.BlockSpec((None, S, 2 * W), lambda b: (b, 0, 0)),
                  full((CONV_WIDTH, W)), full((1, W)), full((W, 2 * W)), full((1, 2 * W)), full((1, W))],
        out_specs=pl.BlockSpec((None, S, W), lambda b: (b, 0, 0)),
        out_shape=jax.ShapeDtypeStruct((B, S, W), F32),
        scratch_shapes=[pltpu.VMEM((2 * W // LANES, S, LANES), F32), pltpu.VMEM((W // LANES, S, LANES), F32)],
        compiler_params=_cparams(1),
        name="rglru",
    )(xg, conv_w, conv_b, w_bd, b_ri, lam)


def _sb_kernel(qkv_ref, o_ref, kt_s, v_s, acc_s, later_s):
    i = pl.program_id(1)
    W = GROUP_WIDTH
    TB = SB_BLOCK
    R = N_HEADS * TB

    NW = SB_WINDOW_BLOCKS
    KW = NW * TB
    PAD = (NW - 1) * TB
    S = qkv_ref.shape[0]

    @pl.when(i == 0)
    def _():
        kt_s[:, 0:PAD] = jnp.zeros((W, PAD), BF16)
        for r0 in range(0, S, W):
            kt_s[:, PAD + r0:PAD + r0 + W] = qkv_ref[r0:r0 + W, W:2 * W].astype(F32).T.astype(BF16)
        v_s[0:PAD, :] = jnp.zeros((PAD, W), BF16)
        v_s[PAD:PAD + S, :] = qkv_ref[:, 2 * W:3 * W].astype(BF16)

    lane = lax.broadcasted_iota(jnp.int32, (TB, W), 1)
    heads = [(lane >= h * HEAD_DIM) & (lane < (h + 1) * HEAD_DIM) for h in range(N_HEADS)]
    r_loc = lax.broadcasted_iota(jnp.int32, (R, KW), 0) & (TB - 1)
    c_loc = lax.broadcasted_iota(jnp.int32, (R, KW), 1)
    ur = lax.broadcasted_iota(jnp.int32, (TB, 2 * TB), 0)
    uc = lax.broadcasted_iota(jnp.int32, (TB, 2 * TB), 1)
    tri_ones = jnp.where((ur > uc) | (uc >= TB), 1.0, 0.0).astype(BF16)

    n_chains = o_ref.shape[0]
    blocks = [i + c * (S // TB // n_chains) for c in range(n_chains)]
    qss = []
    for blk in blocks:
        q = qkv_ref[pl.ds(pl.multiple_of(blk * TB, TB), TB), 0:W] * (HEAD_DIM ** -0.5)
        qss.append(jnp.concatenate([jnp.where(hm, q, 0.0) for hm in heads], axis=0).astype(BF16))

    acc_s[...] = jnp.zeros_like(acc_s)
    later_s[...] = jnp.zeros_like(later_s)

    def window(c, n):
        blk = blocks[c]
        first_key = (blk - n * NW - (NW - 1)) * TB
        rows = pl.ds(pl.multiple_of(jnp.maximum(first_key + PAD, 0), TB), KW)
        z = _dot(qss[c], kt_s[:, rows])
        key_abs = first_key + c_loc
        live = (key_abs < blk * TB + r_loc) & (key_abs >= 0)
        sp = _softplus(z)
        lf = jnp.where(live, -sp, 0.0)
        lf16 = lf.astype(BF16)
        order = list(range(NW - 1, -1, -1))
        stacked = jnp.concatenate([lf16[:, b * TB:(b + 1) * TB] for b in order], axis=0)
        cs_all = _dot(stacked, tri_ones)
        offset = later_s[c]
        after = [None] * NW
        for pos, b in enumerate(order):
            cs = cs_all[pos * R:(pos + 1) * R, :]
            after[b] = cs[:, 0:TB] + offset
            offset = offset + cs[:, TB:2 * TB]
        w = jnp.where(live, jnp.exp((z - sp) + jnp.concatenate(after, axis=1)), 0.0)
        acc_s[c] += _dot(w.astype(BF16), v_s[rows, :])
        later_s[c] = offset
        return jnp.where((n + 1) * NW <= blk, jnp.max(offset), SB_EXP_FLOOR)

    def cond(carry):
        return carry[1] > SB_EXP_FLOOR

    def body(carry):
        n = carry[0]
        later_max = window(0, n)
        for c in range(1, n_chains):
            later_max = jnp.maximum(later_max, window(c, n))
        return n + 1, later_max

    lax.while_loop(cond, body, (jnp.int32(0), jnp.float32(0.0)))
    for c in range(n_chains):
        out = acc_s[c, 0:TB, :]
        for h in range(1, N_HEADS):
            out = jnp.where(heads[h], acc_s[c, h * TB:(h + 1) * TB, :], out)
        o_ref[c] = out


def _sb_attention(qkv):
    B, S, _ = qkv.shape
    W = GROUP_WIDTH
    pad = (SB_WINDOW_BLOCKS - 1) * SB_BLOCK
    nc = SB_CHAINS
    rows = N_HEADS * SB_BLOCK
    out = pl.pallas_call(
        _sb_kernel,
        grid=(B, S // SB_BLOCK // nc),
        in_specs=[pl.BlockSpec((None, S, 3 * W), lambda b, i: (b, 0, 0))],
        out_specs=pl.BlockSpec((None, nc, SB_BLOCK, W), lambda b, i: (b, 0, i, 0)),
        out_shape=jax.ShapeDtypeStruct((B, nc, S // nc, W), F32),
        scratch_shapes=[pltpu.VMEM((W, S + pad), BF16), pltpu.VMEM((S + pad, W), BF16),
                        pltpu.VMEM((nc, rows, W), F32), pltpu.VMEM((nc, rows, SB_BLOCK), F32)],
        compiler_params=_cparams(2),
        name="stickbreak",
    )(qkv)
    return out.reshape(B, S, W)


def _ssd_kernel(z_ref, xbc_ref, dt_ref, cw_ref, cb_ref, dtb_ref, alog_ref, dskip_ref, o_ref, xbc_s):
    S = z_ref.shape[0]
    W = GROUP_WIDTH
    L = SSM_CHUNK
    for s in range(3 * W // LANES):
        cols = slice(s * LANES, (s + 1) * LANES)
        xbc_s[s] = _silu(_causal_conv(xbc_ref[:, cols], cw_ref, cb_ref, cols))
    a_row = -jnp.exp(alog_ref[...])

    r_i = lax.broadcasted_iota(jnp.int32, (L, L), 0)
    c_i = lax.broadcasted_iota(jnp.int32, (L, L), 1)
    tri_incl = jnp.where(c_i <= r_i, 1.0, 0.0).astype(BF16)
    lower = c_i <= r_i
    e_r = lax.broadcasted_iota(jnp.int32, (LANES, W), 0)
    e_c = lax.broadcasted_iota(jnp.int32, (LANES, W), 1)
    expand = jnp.where((e_c >= e_r * HEAD_DIM) & (e_c < (e_r + 1) * HEAD_DIM), 1.0, 0.0).astype(BF16)
    lane_l = lax.broadcasted_iota(jnp.int32, (L, LANES), 1)

    def chunk(c, states):
        rows = slice(c * L, (c + 1) * L)
        xs = jnp.concatenate([xbc_s[0, rows, :], xbc_s[1, rows, :]], axis=1)
        dt = _softplus(dt_ref[rows, :] + dtb_ref[...])
        a_dt = dt * a_row
        cs_col = _dot_wide_lhs_rhs(tri_incl, a_dt)
        cs_row = cs_col.T
        cs_full = _dot_wide_lhs(cs_col, expand)
        dt_full = _dot_wide_lhs(dt, expand)
        xd = xs * dt_full
        tot = cs_full[L - 1:L, :]
        xdec = (xd * jnp.exp(tot - cs_full)).astype(BF16)
        xd16 = xd.astype(BF16)
        ys = []
        new_states = []
        for g in range(2):
            gl = slice(g * LANES, (g + 1) * LANES)
            bm = xbc_s[2 + g, rows, :].astype(BF16)
            cm = xbc_s[4 + g, rows, :].astype(BF16)
            cb = _dot_t(cm, bm)
            prev = states[g]
            y_off = _dot(cm, prev.astype(BF16)) * jnp.exp(cs_full[:, gl])
            y_g = y_off
            for hh in range(2):
                h = 2 * g + hh
                seg = jnp.where(lower, cs_col[:, h:h + 1] - cs_row[h:h + 1, :], -jnp.inf)
                y_h = _dot((cb * jnp.exp(seg)).astype(BF16), xd16[:, gl])
                in_head = (lane_l >= hh * HEAD_DIM) & (lane_l < (hh + 1) * HEAD_DIM)
                y_g = y_g + jnp.where(in_head, y_h, 0.0)
            new_states.append(prev * jnp.exp(tot[:, gl]) + _dot_tl(bm, xdec[:, gl]))
            ys.append(y_g)
        y = jnp.concatenate(ys, axis=1) + dskip_ref[...] * xs
        o_ref[rows, :] = y * _silu(z_ref[rows, :])
        return new_states

    states = [jnp.zeros((SSM_STATE, LANES), F32) for _ in range(2)]
    for c in range(S // L):
        states = chunk(c, states)


def _dot_wide_lhs_rhs(m_bf16, x):
    hi, mid, lo = _split3(x)
    return _dot(m_bf16, hi) + _dot(m_bf16, mid) + _dot(m_bf16, lo)


def _ssd(z, xbc, dt, conv_w, conv_b, dt_bias, a_log, d_skip):
    B, S, _ = z.shape
    W = GROUP_WIDTH
    full = lambda shape: pl.BlockSpec(shape, lambda b: (0,) * len(shape))
    return pl.pallas_call(
        _ssd_kernel,
        grid=(B,),
        in_specs=[pl.BlockSpec((None, S, W), lambda b: (b, 0, 0)),
                  pl.BlockSpec((None, S, 3 * W), lambda b: (b, 0, 0)),
                  pl.BlockSpec((None, S, LANES), lambda b: (b, 0, 0)),
                  full((CONV_WIDTH, 3 * W)), full((1, 3 * W)), full((1, LANES)), full((1, LANES)),
                  full((1, W))],
        out_specs=pl.BlockSpec((None, S, W), lambda b: (b, 0, 0)),
        out_shape=jax.ShapeDtypeStruct((B, S, W), F32),
        scratch_shapes=[pltpu.VMEM((3 * W // LANES, S, LANES), F32)],
        compiler_params=_cparams(1),
        name="ssd",
    )(z, xbc, dt, conv_w, conv_b, dt_bias, a_log, d_skip)


def _rope(x, cos, sin):
    lane = lax.broadcasted_iota(jnp.int32, (x.shape[0], LANES), 1)
    first_half = (lane % HEAD_DIM) < (HEAD_DIM // 2)
    halves = []
    for p in range(x.shape[1] // LANES):
        xp = x[:, p * LANES:(p + 1) * LANES]
        fwd = pltpu.roll(xp, HEAD_DIM // 2, 1)
        bwd = pltpu.roll(xp, LANES - HEAD_DIM // 2, 1)
        halves.append(jnp.where(first_half, -bwd, fwd))
    rot = jnp.concatenate(halves, axis=1)
    return x * cos + rot * sin


def _moba_kernel(qkv_ref, cos_ref, sin_ref, o_ref, k_s, vt_s, kmean_s, acc_s, bias_s):
    i = pl.program_id(1)
    W = GROUP_WIDTH
    TB = MOBA_BLOCK
    S = qkv_ref.shape[0]
    NB = S // TB

    @pl.when(i == 0)
    def _():
        for blk in range(NB):
            rs = slice(blk * TB, (blk + 1) * TB)
            kb = _rope(qkv_ref[rs, W:2 * W], cos_ref[rs, :], sin_ref[rs, :])
            for h in range(N_HEADS):
                k_s[h, rs, :] = kb[:, h * HEAD_DIM:(h + 1) * HEAD_DIM].astype(BF16)
            kmean_s[blk:blk + 1, :] = jnp.mean(kb, axis=0, keepdims=True)
            vt_s[blk] =qkv_ref[rs, 2 * W:3 * W].T.astype(BF16)

    rows_i = pl.ds(pl.multiple_of(i * TB, TB), TB)
    q = _rope(qkv_ref[rows_i, 0:W], cos_ref[rows_i, :], sin_ref[rows_i, :])
    lane8 = lax.broadcasted_iota(jnp.int32, (NB, W), 1)
    blk_id = lax.broadcasted_iota(jnp.int32, (NB, TB), 0)
    R = N_HEADS * TB
    key_loc = lax.broadcasted_iota(jnp.int32, (TB, R), 0)
    q_loc = lax.broadcasted_iota(jnp.int32, (TB, R), 1) & (TB - 1)
    kmean = kmean_s[...]
    q_t = q.T
    qt_hi, qt_lo = _split2(q_t)
    scale = HEAD_DIM ** -0.5

    km_all = jnp.concatenate(
        [jnp.where((lane8 >= h * HEAD_DIM) & (lane8 < (h + 1) * HEAD_DIM), kmean, 0.0) for h in range(N_HEADS)],
        axis=0)
    km_hi, km_lo = _split2(km_all)
    gate_all = _dot(km_hi, qt_hi) + _dot(km_hi, qt_lo) + _dot(km_lo, qt_hi)
    qhs = []
    for h in range(N_HEADS):
        gate = gate_all[h * NB:(h + 1) * NB, :]
        cnt = jnp.zeros((NB, TB), F32)
        for jp in range(NB):
            row = gate[jp:jp + 1, :]
            beats = (row > gate) | ((row == gate) & (blk_id > jp))
            cnt = cnt + jnp.where(beats, jnp.where(jp < i, 1.0, 0.0), 0.0)
        selected = (cnt < float(MOBA_TOPK)) & (blk_id < i)
        bias_s[:, h * TB:(h + 1) * TB] = jnp.where(selected, 0.0, NEG_BIG)
        qhs.append((q_t[h * HEAD_DIM:(h + 1) * HEAD_DIM, :] * scale).astype(BF16))

    def scores(rows):
        return jnp.concatenate([_dot(k_s[h, rows, :], qhs[h]) for h in range(N_HEADS)], axis=1)

    s = jnp.where(key_loc <= q_loc, scores(rows_i), NEG_BIG)
    m = jnp.max(s, axis=0, keepdims=True)
    p = jnp.exp(s - m)
    l = jnp.sum(p, axis=0, keepdims=True)
    def weighted_values(j, p):
        p16 = p.astype(BF16)
        return jnp.concatenate(
            [_dot(vt_s[j, h * HEAD_DIM:(h + 1) * HEAD_DIM, :], p16[:, h * TB:(h + 1) * TB]) for h in range(N_HEADS)],
            axis=1)

    acc_s[...] = weighted_values(i, p)

    def body(j, carry):
        m, l = carry
        rows = pl.ds(pl.multiple_of(j * TB, TB), TB)
        s = scores(rows) + bias_s[pl.ds(j, 1), :]
        m_new = jnp.maximum(m, jnp.max(s, axis=0, keepdims=True))
        alpha = jnp.exp(m - m_new)
        p = jnp.exp(s - m_new)
        l = alpha * l + jnp.sum(p, axis=0, keepdims=True)
        acc_s[...] = alpha * acc_s[...] + weighted_values(j, p)
        return m_new, l

    m, l = lax.fori_loop(0, i, body, (m, l))
    outs = acc_s[...] / l
    o_ref[...] = jnp.concatenate([outs[:, h * TB:(h + 1) * TB].T for h in range(N_HEADS)], axis=1)


def _moba(qkv, cos, sin):
    B, S, _ = qkv.shape
    W = GROUP_WIDTH
    nb = S // MOBA_BLOCK
    return pl.pallas_call(
        _moba_kernel,
        grid=(B, S // MOBA_BLOCK),
        in_specs=[pl.BlockSpec((None, S, 3 * W), lambda b, i: (b, 0, 0)),
                  pl.BlockSpec((S, W), lambda b, i: (0, 0)),
                  pl.BlockSpec((S, W), lambda b, i: (0, 0))],
        out_specs=pl.BlockSpec((None, MOBA_BLOCK, W), lambda b, i: (b, i, 0)),
        out_shape=jax.ShapeDtypeStruct((B, S, W), F32),
        scratch_shapes=[pltpu.VMEM((N_HEADS, S, HEAD_DIM), BF16), pltpu.VMEM((nb, W, MOBA_BLOCK), BF16),
                        pltpu.VMEM((nb, W), F32), pltpu.VMEM((HEAD_DIM, N_HEADS * MOBA_BLOCK), F32),
                        pltpu.VMEM((nb, N_HEADS * MOBA_BLOCK), F32)],
        compiler_params=_cparams(2),
        name="moba",
    )(qkv, cos, sin)


def _outproj_kernel(ya_ref, yb_ref, yc_ref, yd_ref, gg_ref, w_ref, x_ref, o_ref):
    yn = [_rmsnorm(y_ref[...], gg_ref[g:g + 1, :]).astype(BF16)
          for g, y_ref in enumerate((ya_ref, yb_ref, yc_ref, yd_ref))]
    o_ref[...] = x_ref[...] + _dot(jnp.concatenate(yn, axis=1), w_ref[...].astype(BF16))


def _outproj(ys, gg, w_out, layer, x2d, tm=1024):
    T = x2d.shape[0]
    W = GROUP_WIDTH
    return pl.pallas_call(
        _outproj_kernel,
        grid=(T // tm,),
        in_specs=[pl.BlockSpec((tm, W), lambda i: (i, 0))] * 4
                 + [pl.BlockSpec((4, W), lambda i: (0, 0)),
                    pl.BlockSpec((None, 4 * W, D_MODEL), lambda i: (layer, 0, 0), pipeline_mode=pl.Buffered(1)),
                    pl.BlockSpec((tm, D_MODEL), lambda i: (i, 0))],
        out_specs=pl.BlockSpec((tm, D_MODEL), lambda i: (i, 0)),
        out_shape=jax.ShapeDtypeStruct((T, D_MODEL), F32),
        compiler_params=_cparams(1),
        name="outproj",
    )(*ys, gg, w_out, x2d)


def _memkv_kernel(m_ref, g_ref, w_ref, kt_ref, v_ref):
    mn = _rmsnorm(m_ref[...], g_ref[...]).astype(BF16)
    kv = _dot(mn, w_ref[...].astype(BF16))
    kt_ref[...] = kv[:, 0:D_MODEL].T.astype(BF16)
    v_ref[...] = kv[:, D_MODEL:2 * D_MODEL].astype(BF16)


def _memkv(mem, g, wkv, layer):
    B, M, _ = mem.shape
    return pl.pallas_call(
        _memkv_kernel,
        grid=(B,),
        in_specs=[pl.BlockSpec((None, M, D_MODEL), lambda b: (b, 0, 0)),
                  pl.BlockSpec((1, D_MODEL), lambda b: (0, 0)),
                  pl.BlockSpec((None, D_MODEL, 2 * D_MODEL), lambda b: (layer, 0, 0),
                               pipeline_mode=pl.Buffered(1))],
        out_specs=[pl.BlockSpec((None, D_MODEL, M), lambda b: (b, 0, 0)),
                   pl.BlockSpec((None, M, D_MODEL), lambda b: (b, 0, 0))],
        out_shape=[jax.ShapeDtypeStruct((B, D_MODEL, M), BF16), jax.ShapeDtypeStruct((B, M, D_MODEL), BF16)],
        compiler_params=_cparams(1),
        name="memkv",
    )(mem, g, wkv)


def _xattn_kernel(x_ref, g_ref, wq_ref, kt_ref, v_ref, wo_ref, o_ref):
    x = x_ref[...]
    h = _rmsnorm(x, g_ref[...]).astype(BF16)
    q = (_dot(h, wq_ref[...].astype(BF16)) * (XATTN_HEAD_DIM ** -0.5)).astype(BF16)
    heads = []
    for hd in range(XATTN_HEADS):
        cols = slice(hd * XATTN_HEAD_DIM, (hd + 1) * XATTN_HEAD_DIM)
        s = _dot(q[:, cols], kt_ref[cols, :])
        p = jnp.exp(s - jnp.max(s, axis=-1, keepdims=True))
        p = p / jnp.sum(p, axis=-1, keepdims=True)
        heads.append(_dot(p.astype(BF16), v_ref[:, cols]).astype(BF16))
    o_ref[...] = x + _dot(jnp.concatenate(heads, axis=1), wo_ref[...].astype(BF16))


def _xattn(x3d, g, wq, kt, v, wo, layer, tm=1024):
    B, S, _ = x3d.shape
    M = v.shape[1]
    weight = pl.BlockSpec((None, D_MODEL, D_MODEL), lambda b, i: (layer, 0, 0), pipeline_mode=pl.Buffered(1))
    return pl.pallas_call(
        _xattn_kernel,
        grid=(B, S // tm),
        in_specs=[pl.BlockSpec((None, tm, D_MODEL), lambda b, i: (b, i, 0)),
                  pl.BlockSpec((1, D_MODEL), lambda b, i: (0, 0)),
                  weight,
                  pl.BlockSpec((None, D_MODEL, M), lambda b, i: (b, 0, 0)),
                  pl.BlockSpec((None, M, D_MODEL), lambda b, i: (b, 0, 0)),
                  weight],
        out_specs=pl.BlockSpec((None, tm, D_MODEL), lambda b, i: (b, i, 0)),
        out_shape=jax.ShapeDtypeStruct((B, S, D_MODEL), F32),
        compiler_params=_cparams(2),
        name="xattn",
    )(x3d, g, wq, kt, v, wo)


def _moe_kernel(x_ref, g_ref, wr_ref, br_ref, w1_ref, w3_ref, w2_ref, fg_ref, o_ref,
                t_s, comb_s, acc_s, xkeep_s, *, final_norm):
    tile = pl.program_id(0)
    grp = pl.program_id(1)
    tm = x_ref.shape[0]
    slot = lax.bitwise_and(tile, 1)

    def route(dst):
        t = _rmsnorm(x_ref[...], g_ref[...])
        t_s[dst] = t.astype(BF16)
        t_hi, t_lo = _split2(t)
        logits = (_dot(t_hi, wr_ref[0]) + _dot(t_hi, wr_ref[1]) + _dot(t_lo, wr_ref[0])) + br_ref[...]
        lt = logits.T
        gsl = SUBLANES * (MOE_EXPERTS // SUBLANES)
        g_row = lax.broadcasted_iota(jnp.int32, (SUBLANES, tm), 0).astype(F32)
        lg = jnp.where(g_row < float(MOE_GROUPS), lt[gsl:gsl + SUBLANES, :], NEG_BIG)
        gmax = jnp.max(lg, axis=0, keepdims=True)
        pg_top = 1.0 / jnp.sum(jnp.exp(lg - gmax), axis=0, keepdims=True)
        g_idx = jnp.min(jnp.where(lg == gmax, g_row, 1e9), axis=0, keepdims=True)
        e_row = lax.broadcasted_iota(jnp.int32, (MOE_EXPERTS, tm), 0).astype(F32)
        in_grp = jnp.floor(e_row * (1.0 / MOE_EPG)) == g_idx
        le = jnp.where(in_grp, lt[0:MOE_EXPERTS, :], NEG_BIG)
        e1 = jnp.max(le, axis=0, keepdims=True)
        i1 = jnp.min(jnp.where(in_grp & (le == e1), e_row, 1e9), axis=0, keepdims=True)
        le2 = jnp.where(e_row == i1, NEG_BIG, le)
        e2 = jnp.max(le2, axis=0, keepdims=True)
        i2 = jnp.min(jnp.where(in_grp & (e_row != i1) & (le2 == e2), e_row, 1e9), axis=0, keepdims=True)
        r2 = jnp.exp(e2 - e1)
        w_first = 1.0 / (1.0 + r2)
        w_second = r2 / (1.0 + r2)
        comb_t = pg_top * (jnp.where(e_row == i1, w_first, 0.0) + jnp.where(e_row == i2, w_second, 0.0))
        comb_s[dst] = jnp.concatenate([comb_t, jnp.zeros((LANES - MOE_EXPERTS, tm), F32)], axis=0).T

    def experts():
        t = t_s[slot]
        comb = comb_s[slot]
        lane = lax.broadcasted_iota(jnp.int32, (tm, LANES), 1)
        hids = []
        for e in range(MOE_EPG):
            n = grp * MOE_EPG + e
            c = jnp.sum(jnp.where(lane == n, comb, 0.0), axis=-1, keepdims=True)
            hid = _silu(_dot(t, w1_ref[e])) * _dot(t, w3_ref[e])
            hids.append((hid * c).astype(BF16))
        return _dot(jnp.concatenate(hids, axis=1), w2_ref[...].reshape(MOE_EPG * MOE_FF, D_MODEL))

    @pl.when((tile == 0) & (grp == 0))
    def _():
        route(0)

    @pl.when(grp == 0)
    def _():
        xkeep_s[...] = x_ref[...]
        acc_s[...] = experts()

    @pl.when((grp > 0) & (grp < MOE_GROUPS - 1))
    def _():
        acc_s[...] += experts()

    @pl.when(grp == MOE_GROUPS - 1)
    def _():
        y = xkeep_s[...] + (acc_s[...] + experts())
        route(1 - slot)
        if final_norm:
            y = _rmsnorm(y, fg_ref[...])
        o_ref[...] = y


def _moe(x2d, g, w_router, b_router, w1, w3, w2, layer, final_g, final_norm, tm=1024):
    T = x2d.shape[0]
    n_tiles = T // tm

    def x_window(i, e):
        return jnp.minimum(i + e // (MOE_GROUPS - 1), n_tiles - 1), 0

    return pl.pallas_call(
        functools.partial(_moe_kernel, final_norm=final_norm),
        grid=(n_tiles, MOE_GROUPS),
        in_specs=[pl.BlockSpec((tm, D_MODEL), x_window),
                  pl.BlockSpec((1, D_MODEL), lambda i, e: (0, 0)),
                  pl.BlockSpec((2, D_MODEL, LANES), lambda i, e: (0, 0, 0)),
                  pl.BlockSpec((1, LANES), lambda i, e: (0, 0)),
                  pl.BlockSpec((None, MOE_EPG, D_MODEL, MOE_FF), lambda i, e: (layer, e, 0, 0)),
                  pl.BlockSpec((None, MOE_EPG, D_MODEL, MOE_FF), lambda i, e: (layer, e, 0, 0)),
                  pl.BlockSpec((None, MOE_EPG, MOE_FF, D_MODEL), lambda i, e: (layer, e, 0, 0)),
                  pl.BlockSpec((1, D_MODEL), lambda i, e: (0, 0))],
        out_specs=pl.BlockSpec((tm, D_MODEL), lambda i, e: (i, 0)),
        out_shape=jax.ShapeDtypeStruct((T, D_MODEL), F32),
        scratch_shapes=[pltpu.VMEM((2, tm, D_MODEL), BF16), pltpu.VMEM((2, tm, LANES), F32),
                        pltpu.VMEM((tm, D_MODEL), F32), pltpu.VMEM((tm, D_MODEL), F32)],
        compiler_params=_cparams(2),
        name="moe",
    )(x2d, g, w_router, b_router, w1, w3, w2, final_g)


def _pad_lanes(v, width=LANES):
    return jnp.pad(v, (0, width - v.shape[0]))[None, :]


def _block_diag(w):
    H, n, _ = w.shape
    eye = jnp.eye(H, dtype=w.dtype)
    return (eye[:, None, :, None] * w[:, :, None, :]).reshape(H * n, H * n)


def _rope_tables(S):
    half = HEAD_DIM // 2
    inv_freq = ROPE_THETA ** (-jnp.arange(half, dtype=F32) / half)
    ang = jnp.arange(S, dtype=F32)[:, None] * inv_freq[None, :]
    reps = GROUP_WIDTH // half
    return jnp.tile(jnp.cos(ang), (1, reps)), jnp.tile(jnp.sin(ang), (1, reps))


def kernel(x, mem, mix_norm_g, w_in, lru_conv_w, lru_conv_b, lru_wr, lru_br, lru_wi, lru_bi, lru_lambda, ssm_conv_w, ssm_conv_b, ssm_dt_bias, ssm_a_log, ssm_d, group_norm_g, w_out, xattn_norm_g, mem_norm_g, xattn_wq, xattn_wkv, xattn_wo, ffn_norm_g, router_group_w, router_group_b, router_expert_w, router_expert_b, expert_w1, expert_w3, expert_w2, final_norm_g):
    B, S, D = x.shape
    T = B * S
    depth = w_in.shape[0]
    W = GROUP_WIDTH
    cos, sin = _rope_tables(S)
    x2d = x.reshape(T, D)
    w1_bf, w3_bf, w2_bf = expert_w1.astype(BF16), expert_w3.astype(BF16), expert_w2.astype(BF16)
    for l in range(depth):
        lru_xg, sb_qkv, ssm_z, ssm_xbc, ssm_dt, mb_qkv = _inproj(x2d, mix_norm_g[l][None, :], w_in, l)

        w_bd = jnp.concatenate([_block_diag(lru_wr[l]), _block_diag(lru_wi[l])], axis=1).astype(BF16)
        b_ri = jnp.concatenate([lru_br[l], lru_bi[l]])[None, :]
        y_a = _lru(lru_xg.reshape(B, S, 2 * W), lru_conv_w[l], lru_conv_b[l][None, :], w_bd, b_ri,
                   lru_lambda[l][None, :])
        y_b = _sb_attention(sb_qkv.reshape(B, S, 3 * W))
        y_c = _ssd(ssm_z.reshape(B, S, W), ssm_xbc.reshape(B, S, 3 * W), ssm_dt.reshape(B, S, LANES),
                   ssm_conv_w[l], ssm_conv_b[l][None, :], _pad_lanes(ssm_dt_bias[l]), _pad_lanes(ssm_a_log[l]),
                   jnp.repeat(ssm_d[l], HEAD_DIM)[None, :])
        y_d = _moba(mb_qkv.reshape(B, S, 3 * W), cos, sin)
        x2d = _outproj([y.reshape(T, W) for y in (y_a, y_b, y_c, y_d)], group_norm_g[l].reshape(4, W),
                       w_out, l, x2d)

        mem_kt, mem_v = _memkv(mem, mem_norm_g[l][None, :], xattn_wkv, l)
        x2d = _xattn(x2d.reshape(B, S, D), xattn_norm_g[l][None, :], xattn_wq, mem_kt, mem_v, xattn_wo,
                     l).reshape(T, D)

        w_r = jnp.pad(jnp.concatenate([router_expert_w[l], router_group_w[l]], axis=1),
                      ((0, 0), (0, LANES - MOE_EXPERTS - MOE_GROUPS)))
        w_r_hi = w_r.astype(BF16)
        w_r_lo = (w_r - w_r_hi.astype(F32)).astype(BF16)
        b_r = _pad_lanes(jnp.concatenate([router_expert_b[l], router_group_b[l]]))
        x2d = _moe(x2d, ffn_norm_g[l][None, :], jnp.stack([w_r_hi, w_r_lo]), b_r, w1_bf, w3_bf, w2_bf, l,
                   final_norm_g[None, :], final_norm=(l == depth - 1))
    return x2d.reshape(B, S, D)
```

```python
import functools
import math

import jax
import jax.numpy as jnp
from jax import lax
from jax.experimental import pallas as pl
from jax.experimental.pallas import tpu as pltpu

F32 = jnp.float32
BF16 = jnp.bfloat16

D_MODEL = 1024
GROUP_WIDTH = 256
HEAD_DIM = 64
N_HEADS = 4
NORM_EPS = 1e-6
CONV_WIDTH = 4
LRU_C = 8.0
SB_BLOCK = 128
SB_WINDOW_BLOCKS = 3
SB_CHAINS = 8
SSM_CHUNK = 128
SSM_STATE = 128
MOBA_BLOCK = 256
MOBA_TOPK = 3
ROPE_THETA = 10000.0
XATTN_HEADS = 4
XATTN_HEAD_DIM = 256
MEM_LEN = 256
MOE_GROUPS = 4
MOE_EPG = 4
MOE_EXPERTS = 16
MOE_FF = 256
LANES = 128
SUBLANES = 8
NEG_BIG = -1e30
SB_EXP_FLOOR = -104.0
IN_OUT_WIDTHS = (512, 768, 256, 768, LANES, 768)
IN_MAIN = 512 + 768 + 256 + 768
IN_OUT_DTYPES = (F32, BF16, F32, F32, F32, F32)
VMEM_LIMIT = 56 * 1024 * 1024


def _cparams(n_axes):
    return pltpu.CompilerParams(dimension_semantics=("arbitrary",) * n_axes,
                                vmem_limit_bytes=VMEM_LIMIT)


def _dot(a, b):
    return jnp.dot(a, b, preferred_element_type=F32)


def _dot_t(a, b):
    return lax.dot_general(a, b, (((1,), (1,)), ((), ())), preferred_element_type=F32)


def _dot_tl(a, b):
    return lax.dot_general(a, b, (((0,), (0,)), ((), ())), preferred_element_type=F32)


def _split2(x):
    hi = x.astype(BF16)
    lo = (x - hi.astype(F32)).astype(BF16)
    return hi, lo


def _split3(x):
    hi = x.astype(BF16)
    r = x - hi.astype(F32)
    mid = r.astype(BF16)
    lo = (r - mid.astype(F32)).astype(BF16)
    return hi, mid, lo


def _dot_wide_lhs(x, m_bf16, parts=3):
    pieces = _split3(x) if parts == 3 else _split2(x)
    out = _dot(pieces[0], m_bf16)
    for p in pieces[1:]:
        out = out + _dot(p, m_bf16)
    return out


def _rmsnorm(x, g):
    return x * lax.rsqrt(jnp.mean(x * x, axis=-1, keepdims=True) + NORM_EPS) * g


def _softplus(x):
    return jnp.maximum(x, 0.0) + jnp.log(1.0 + jnp.exp(-jnp.abs(x)))


def _sigmoid(x):
    return 1.0 / (1.0 + jnp.exp(-x))


def _silu(x):
    return x * _sigmoid(x)


def _gelu_tanh(x):
    return 0.5 * x * (1.0 + jnp.tanh(math.sqrt(2.0 / math.pi) * (x + 0.044715 * (x * x * x))))


def _shift_rows_down(x):
    rows = lax.broadcasted_iota(jnp.int32, x.shape, 0)
    return jnp.where(rows >= 1, pltpu.roll(x, 1, 0), 0.0)


def _causal_conv(x, w_ref, b_ref, cols):
    def taps(v, mask_rows):
        y = v * w_ref[CONV_WIDTH - 1:CONV_WIDTH, cols] + b_ref[:, cols]
        for s in range(1, CONV_WIDTH):
            vs = pltpu.roll(v, s, 0)
            if mask_rows is not None:
                vs = jnp.where(mask_rows >= s, vs, 0.0)
            y = y + vs * w_ref[CONV_WIDTH - 1 - s:CONV_WIDTH - s, cols]
        return y

    head = x[0:SUBLANES, :]
    y_head = taps(head, lax.broadcasted_iota(jnp.int32, head.shape, 0))
    return jnp.concatenate([y_head, taps(x, None)[SUBLANES:, :]], axis=0)


def _phase_conv(slab_ref, w_ref, b_ref, cols):
    P = SUBLANES
    nt = slab_ref.shape[0] // P
    x = [slab_ref[pl.ds(p, nt, stride=P), :] for p in range(P)]
    prev = {p: _shift_rows_down(x[p]) for p in range(P - CONV_WIDTH + 1, P)}
    out = []
    for p in range(P):
        y = x[p] * w_ref[CONV_WIDTH - 1:CONV_WIDTH, cols] + b_ref[:, cols]
        for k in range(1, CONV_WIDTH):
            src = x[p - k] if p - k >= 0 else prev[p - k + P]
            y = y + src * w_ref[CONV_WIDTH - 1 - k:CONV_WIDTH - k, cols]
        out.append(y)
    return out


def _inproj_kernel(x_ref, g_ref, w_ref, *refs):
    o_refs, w_s = refs[:-1], refs[-1]

    @pl.when(pl.program_id(0) == 0)
    def _():
        for c0 in range(0, IN_MAIN, 2 * LANES):
            w_s[:, c0:c0 + 2 * LANES] = w_ref[:, c0:c0 + 2 * LANES].astype(BF16)
        dt_tile = w_ref[:, IN_MAIN:IN_MAIN + LANES]
        lane = lax.broadcasted_iota(jnp.int32, dt_tile.shape, 1)
        w_s[:, IN_MAIN:IN_MAIN + LANES] = jnp.where(lane < N_HEADS, dt_tile, 0.0).astype(BF16)
        w_s[:, IN_MAIN + LANES:] = w_ref[:, IN_MAIN + N_HEADS:].astype(BF16)

    h = _rmsnorm(x_ref[...], g_ref[...]).astype(BF16)
    off = 0
    for o_ref, width in zip(o_refs, IN_OUT_WIDTHS):
        o_ref[...] = _dot(h, w_s[:, off:off + width]).astype(o_ref.dtype)
        off += width


def _inproj(x2d, g, w_in, layer, tm=1024):
    T = x2d.shape[0]
    n_in = w_in.shape[-1]
    return pl.pallas_call(
        _inproj_kernel,
        grid=(T // tm,),
        in_specs=[pl.BlockSpec((tm, D_MODEL), lambda i: (i, 0)),
                  pl.BlockSpec((1, D_MODEL), lambda i: (0, 0)),
                  pl.BlockSpec((None, D_MODEL, n_in), lambda i: (layer, 0, 0), pipeline_mode=pl.Buffered(1))],
        out_specs=[pl.BlockSpec((tm, w), lambda i: (i, 0)) for w in IN_OUT_WIDTHS],
        out_shape=[jax.ShapeDtypeStruct((T, w), dt) for w, dt in zip(IN_OUT_WIDTHS, IN_OUT_DTYPES)],
        scratch_shapes=[pltpu.VMEM((D_MODEL, sum(IN_OUT_WIDTHS)), BF16)],
        compiler_params=_cparams(1),
        name="inproj",
    )(x2d, g, w_in)


def _scan_rows(a, u):
    n = a.shape[0]
    rows = lax.broadcasted_iota(jnp.int32, a.shape, 0)
    shift = 1
    while shift < n:
        if shift < SUBLANES:
            keep = rows >= shift
            a_s = jnp.where(keep, pltpu.roll(a, shift, 0), 1.0)
            u_s = jnp.where(keep, pltpu.roll(u, shift, 0), 0.0)
            u = a * u_s + u
            a = a * a_s
        else:
            u = jnp.concatenate([u[:shift], a[shift:] * u[:n - shift] + u[shift:]], axis=0)
            a = jnp.concatenate([a[:shift], a[shift:] * a[:n - shift]], axis=0)
        shift *= 2
    return u


def _lru_kernel(xg_ref, cw_ref, cb_ref, wbd_ref, bri_ref, lam_ref, o_ref, in_s, out_s):
    S = xg_ref.shape[0]
    W = GROUP_WIDTH
    P = SUBLANES
    NT = S // P
    for s in range(2 * W // LANES):
        in_s[s] = xg_ref[:, s * LANES:(s + 1) * LANES]
    log_sig_lam = -_softplus(-lam_ref[...])

    for s in range(W // LANES):
        cols = slice(s * LANES, (s + 1) * LANES)
        xc_all = jnp.concatenate(_phase_conv(in_s.at[s], cw_ref, cb_ref, cols), axis=0)
        w_slab = jnp.concatenate([wbd_ref[cols, cols], wbd_ref[cols, W + s * LANES:W + (s + 1) * LANES]], axis=1)
        ri = _dot(xc_all.astype(BF16), w_slab)
        r = _sigmoid(ri[:, 0:LANES] + bri_ref[:, cols])
        i = _sigmoid(ri[:, LANES:2 * LANES] + bri_ref[:, W + s * LANES:W + (s + 1) * LANES])
        log_a = (LRU_C * r) * log_sig_lam[:, cols]
        a = jnp.exp(log_a)
        u = jnp.sqrt(1.0 - jnp.exp(2.0 * log_a)) * (i * xc_all)
        loc = [u[0:NT]]
        dec = [a[0:NT]]
        for p in range(1, P):
            ap = a[p * NT:(p + 1) * NT]
            loc.append(ap * loc[-1] + u[p * NT:(p + 1) * NT])
            dec.append(ap * dec[-1])
        carry = _shift_rows_down(_scan_rows(dec[-1], loc[-1]))
        for p in range(P):
            h = loc[p] + dec[p] * carry
            out_s[s, pl.ds(p, NT, stride=P), :] = h * _gelu_tanh(in_s[W // LANES + s, pl.ds(p, NT, stride=P), :])
    for s in range(W // LANES):
        o_ref[:, s * LANES:(s + 1) * LANES] = out_s[s]


def _lru(xg, conv_w, conv_b, w_bd, b_ri, lam):
    B, S, _ = xg.shape
    W = GROUP_WIDTH
    full = lambda shape: pl.BlockSpec(shape, lambda b: (0,) * len(shape))
    return pl.pallas_call(
        _lru_kernel,
        grid=(B,),
        in_specs=[pl.BlockSpec((None, S, 2 * W), lambda b: (b, 0, 0)),
                  full((CONV_WIDTH, W)), full((1, W)), full((W, 2 * W)), full((1, 2 * W)), full((1, W))],
        out_specs=pl.BlockSpec((None, S, W), lambda b: (b, 0, 0)),
        out_shape=jax.ShapeDtypeStruct((B, S, W), F32),
        scratch_shapes=[pltpu.VMEM((2 * W // LANES, S, LANES), F32), pltpu.VMEM((W // LANES, S, LANES), F32)],
        compiler_params=_cparams(1),
        name="rglru",
    )(xg, conv_w, conv_b, w_bd, b_ri, lam)


def _sb_kernel(qkv_ref, o_ref, kt_s, v_s, acc_s, later_s):
    i = pl.program_id(1)
    W = GROUP_WIDTH
    TB = SB_BLOCK
    R = N_HEADS * TB

    NW = SB_WINDOW_BLOCKS
    KW = NW * TB
    PAD = (NW - 1) * TB
    S = qkv_ref.shape[0]

    @pl.when(i == 0)
    def _():
        kt_s[:, 0:PAD] = jnp.zeros((W, PAD), BF16)
        for r0 in range(0, S, W):
            kt_s[:, PAD + r0:PAD + r0 + W] = qkv_ref[r0:r0 + W, W:2 * W].astype(F32).T.astype(BF16)
        v_s[0:PAD, :] = jnp.zeros((PAD, W), BF16)
        v_s[PAD:PAD + S, :] = qkv_ref[:, 2 * W:3 * W].astype(BF16)

    lane = lax.broadcasted_iota(jnp.int32, (TB, W), 1)
    heads = [(lane >= h * HEAD_DIM) & (lane < (h + 1) * HEAD_DIM) for h in range(N_HEADS)]
    r_loc = lax.broadcasted_iota(jnp.int32, (R, KW), 0) & (TB - 1)
    c_loc = lax.broadcasted_iota(jnp.int32, (R, KW), 1)
    ur = lax.broadcasted_iota(jnp.int32, (TB, 2 * TB), 0)
    uc = lax.broadcasted_iota(jnp.int32, (TB, 2 * TB), 1)
    tri_ones = jnp.where((ur > uc) | (uc >= TB), 1.0, 0.0).astype(BF16)

    n_chains = o_ref.shape[0]
    blocks = [i + c * (S // TB // n_chains) for c in range(n_chains)]
    qss = []
    for blk in blocks:
        q = qkv_ref[pl.ds(pl.multiple_of(blk * TB, TB), TB), 0:W] * (HEAD_DIM ** -0.5)
        qss.append(jnp.concatenate([jnp.where(hm, q, 0.0) for hm in heads], axis=0).astype(BF16))

    acc_s[...] = jnp.zeros_like(acc_s)
    later_s[...] = jnp.zeros_like(later_s)

    def window(c, n):
        blk = blocks[c]
        first_key = (blk - n * NW - (NW - 1)) * TB
        rows = pl.ds(pl.multiple_of(jnp.maximum(first_key + PAD, 0), TB), KW)
        z = _dot(qss[c], kt_s[:, rows])
        key_abs = first_key + c_loc
        live = (key_abs < blk * TB + r_loc) & (key_abs >= 0)
        sp = _softplus(z)
        lf = jnp.where(live, -sp, 0.0)
        lf16 = lf.astype(BF16)
        order = list(range(NW - 1, -1, -1))
        stacked = jnp.concatenate([lf16[:, b * TB:(b + 1) * TB] for b in order], axis=0)
        cs_all = _dot(stacked, tri_ones)
        offset = later_s[c]
        after = [None] * NW
        for pos, b in enumerate(order):
            cs = cs_all[pos * R:(pos + 1) * R, :]
            after[b] = cs[:, 0:TB] + offset
            offset = offset + cs[:, TB:2 * TB]
        w = jnp.where(live, jnp.exp((z - sp) + jnp.concatenate(after, axis=1)), 0.0)
        acc_s[c] += _dot(w.astype(BF16), v_s[rows, :])
        later_s[c] = offset
        return jnp.where((n + 1) * NW <= blk, jnp.max(offset), SB_EXP_FLOOR)

    def cond(carry):
        return carry[1] > SB_EXP_FLOOR

    def body(carry):
        n = carry[0]
        later_max = window(0, n)
        for c in range(1, n_chains):
            later_max = jnp.maximum(later_max, window(c, n))
        return n + 1, later_max

    lax.while_loop(cond, body, (jnp.int32(0), jnp.float32(0.0)))
    for c in range(n_chains):
        out = acc_s[c, 0:TB, :]
        for h in range(1, N_HEADS):
            out = jnp.where(heads[h], acc_s[c, h * TB:(h + 1) * TB, :], out)
        o_ref[c] = out


def _sb_attention(qkv):
    B, S, _ = qkv.shape
    W = GROUP_WIDTH
    pad = (SB_WINDOW_BLOCKS - 1) * SB_BLOCK
    nc = SB_CHAINS
    rows = N_HEADS * SB_BLOCK
    out = pl.pallas_call(
        _sb_kernel,
        grid=(B, S // SB_BLOCK // nc),
        in_specs=[pl.BlockSpec((None, S, 3 * W), lambda b, i: (b, 0, 0))],
        out_specs=pl.BlockSpec((None, nc, SB_BLOCK, W), lambda b, i: (b, 0, i, 0)),
        out_shape=jax.ShapeDtypeStruct((B, nc, S // nc, W), F32),
        scratch_shapes=[pltpu.VMEM((W, S + pad), BF16), pltpu.VMEM((S + pad, W), BF16),
                        pltpu.VMEM((nc, rows, W), F32), pltpu.VMEM((nc, rows, SB_BLOCK), F32)],
        compiler_params=_cparams(2),
        name="stickbreak",
    )(qkv)
    return out.reshape(B, S, W)


def _ssd_kernel(z_ref, xbc_ref, dt_ref, cw_ref, cb_ref, dtb_ref, alog_ref, dskip_ref, o_ref, xbc_s):
    S = z_ref.shape[0]
    W = GROUP_WIDTH
    L = SSM_CHUNK
    for s in range(3 * W // LANES):
        cols = slice(s * LANES, (s + 1) * LANES)
        xbc_s[s] = _silu(_causal_conv(xbc_ref[:, cols], cw_ref, cb_ref, cols))
    a_row = -jnp.exp(alog_ref[...])

    r_i = lax.broadcasted_iota(jnp.int32, (L, L), 0)
    c_i = lax.broadcasted_iota(jnp.int32, (L, L), 1)
    tri_incl = jnp.where(c_i <= r_i, 1.0, 0.0).astype(BF16)
    lower = c_i <= r_i
    e_r = lax.broadcasted_iota(jnp.int32, (LANES, W), 0)
    e_c = lax.broadcasted_iota(jnp.int32, (LANES, W), 1)
    expand = jnp.where((e_c >= e_r * HEAD_DIM) & (e_c < (e_r + 1) * HEAD_DIM), 1.0, 0.0).astype(BF16)
    lane_l = lax.broadcasted_iota(jnp.int32, (L, LANES), 1)

    def chunk(c, states):
        rows = slice(c * L, (c + 1) * L)
        xs = jnp.concatenate([xbc_s[0, rows, :], xbc_s[1, rows, :]], axis=1)
        dt = _softplus(dt_ref[rows, :] + dtb_ref[...])
        a_dt = dt * a_row
        cs_col = _dot_wide_lhs_rhs(tri_incl, a_dt)
        cs_row = cs_col.T
        cs_full = _dot_wide_lhs(cs_col, expand)
        dt_full = _dot_wide_lhs(dt, expand)
        xd = xs * dt_full
        tot = cs_full[L - 1:L, :]
        xdec = (xd * jnp.exp(tot - cs_full)).astype(BF16)
        xd16 = xd.astype(BF16)
        ys = []
        new_states = []
        for g in range(2):
            gl = slice(g * LANES, (g + 1) * LANES)
            bm = xbc_s[2 + g, rows, :].astype(BF16)
            cm = xbc_s[4 + g, rows, :].astype(BF16)
            cb = _dot_t(cm, bm)
            prev = states[g]
            y_off = _dot(cm, prev.astype(BF16)) * jnp.exp(cs_full[:, gl])
            y_g = y_off
            for hh in range(2):
                h = 2 * g + hh
                seg = jnp.where(lower, cs_col[:, h:h + 1] - cs_row[h:h + 1, :], -jnp.inf)
                y_h = _dot((cb * jnp.exp(seg)).astype(BF16), xd16[:, gl])
                in_head = (lane_l >= hh * HEAD_DIM) & (lane_l < (hh + 1) * HEAD_DIM)
                y_g = y_g + jnp.where(in_head, y_h, 0.0)
            new_states.append(prev * jnp.exp(tot[:, gl]) + _dot_tl(bm, xdec[:, gl]))
            ys.append(y_g)
        y = jnp.concatenate(ys, axis=1) + dskip_ref[...] * xs
        o_ref[rows, :] = y * _silu(z_ref[rows, :])
        return new_states

    states = [jnp.zeros((SSM_STATE, LANES), F32) for _ in range(2)]
    for c in range(S // L):
        states = chunk(c, states)


def _dot_wide_lhs_rhs(m_bf16, x):
    hi, mid, lo = _split3(x)
    return _dot(m_bf16, hi) + _dot(m_bf16, mid) + _dot(m_bf16, lo)


def _ssd(z, xbc, dt, conv_w, conv_b, dt_bias, a_log, d_skip):
    B, S, _ = z.shape
    W = GROUP_WIDTH
    full = lambda shape: pl.BlockSpec(shape, lambda b: (0,) * len(shape))
    return pl.pallas_call(
        _ssd_kernel,
        grid=(B,),
        in_specs=[pl.BlockSpec((None, S, W), lambda b: (b, 0, 0)),
                  pl.BlockSpec((None, S, 3 * W), lambda b: (b, 0, 0)),
                  pl.BlockSpec((None, S, LANES), lambda b: (b, 0, 0)),
                  full((CONV_WIDTH, 3 * W)), full((1, 3 * W)), full((1, LANES)), full((1, LANES)),
                  full((1, W))],
        out_specs=pl.BlockSpec((None, S, W), lambda b: (b, 0, 0)),
        out_shape=jax.ShapeDtypeStruct((B, S, W), F32),
        scratch_shapes=[pltpu.VMEM((3 * W // LANES, S, LANES), F32)],
        compiler_params=_cparams(1),
        name="ssd",
    )(z, xbc, dt, conv_w, conv_b, dt_bias, a_log, d_skip)


def _rope(x, cos, sin):
    lane = lax.broadcasted_iota(jnp.int32, (x.shape[0], LANES), 1)
    first_half = (lane % HEAD_DIM) < (HEAD_DIM // 2)
    halves = []
    for p in range(x.shape[1] // LANES):
        xp = x[:, p * LANES:(p + 1) * LANES]
        fwd = pltpu.roll(xp, HEAD_DIM // 2, 1)
        bwd = pltpu.roll(xp, LANES - HEAD_DIM // 2, 1)
        halves.append(jnp.where(first_half, -bwd, fwd))
    rot = jnp.concatenate(halves, axis=1)
    return x * cos + rot * sin


def _moba_kernel(qkv_ref, cos_ref, sin_ref, o_ref, k_s, vt_s, kmean_s, acc_s, bias_s):
    i = pl.program_id(1)
    W = GROUP_WIDTH
    TB = MOBA_BLOCK
    S = qkv_ref.shape[0]
    NB = S // TB

    @pl.when(i == 0)
    def _():
        for blk in range(NB):
            rs = slice(blk * TB, (blk + 1) * TB)
            kb = _rope(qkv_ref[rs, W:2 * W], cos_ref[rs, :], sin_ref[rs, :])
            for h in range(N_HEADS):
                k_s[h, rs, :] = kb[:, h * HEAD_DIM:(h + 1) * HEAD_DIM].astype(BF16)
            kmean_s[blk:blk + 1, :] = jnp.mean(kb, axis=0, keepdims=True)
            vt_s[blk] =qkv_ref[rs, 2 * W:3 * W].T.astype(BF16)

    rows_i = pl.ds(pl.multiple_of(i * TB, TB), TB)
    q = _rope(qkv_ref[rows_i, 0:W], cos_ref[rows_i, :], sin_ref[rows_i, :])
    lane8 = lax.broadcasted_iota(jnp.int32, (NB, W), 1)
    blk_id = lax.broadcasted_iota(jnp.int32, (NB, TB), 0)
    R = N_HEADS * TB
    key_loc = lax.broadcasted_iota(jnp.int32, (TB, R), 0)
    q_loc = lax.broadcasted_iota(jnp.int32, (TB, R), 1) & (TB - 1)
    kmean = kmean_s[...]
    q_t = q.T
    qt_hi, qt_lo = _split2(q_t)
    scale = HEAD_DIM ** -0.5

    km_all = jnp.concatenate(
        [jnp.where((lane8 >= h * HEAD_DIM) & (lane8 < (h + 1) * HEAD_DIM), kmean, 0.0) for h in range(N_HEADS)],
        axis=0)
    km_hi, km_lo = _split2(km_all)
    gate_all = _dot(km_hi, qt_hi) + _dot(km_hi, qt_lo) + _dot(km_lo, qt_hi)
    qhs = []
    for h in range(N_HEADS):
        gate = gate_all[h * NB:(h + 1) * NB, :]
        cnt = jnp.zeros((NB, TB), F32)
        for jp in range(NB):
            row = gate[jp:jp + 1, :]
            beats = (row > gate) | ((row == gate) & (blk_id > jp))
            cnt = cnt + jnp.where(beats, jnp.where(jp < i, 1.0, 0.0), 0.0)
        selected = (cnt < float(MOBA_TOPK)) & (blk_id < i)
        bias_s[:, h * TB:(h + 1) * TB] = jnp.where(selected, 0.0, NEG_BIG)
        qhs.append((q_t[h * HEAD_DIM:(h + 1) * HEAD_DIM, :] * scale).astype(BF16))

    def scores(rows):
        return jnp.concatenate([_dot(k_s[h, rows, :], qhs[h]) for h in range(N_HEADS)], axis=1)

    s = jnp.where(key_loc <= q_loc, scores(rows_i), NEG_BIG)
    m = jnp.max(s, axis=0, keepdims=True)
    p = jnp.exp(s - m)
    l = jnp.sum(p, axis=0, keepdims=True)
    def weighted_values(j, p):
        p16 = p.astype(BF16)
        return jnp.concatenate(
            [_dot(vt_s[j, h * HEAD_DIM:(h + 1) * HEAD_DIM, :], p16[:, h * TB:(h + 1) * TB]) for h in range(N_HEADS)],
            axis=1)

    acc_s[...] = weighted_values(i, p)

    def body(j, carry):
        m, l = carry
        rows = pl.ds(pl.multiple_of(j * TB, TB), TB)
        s = scores(rows) + bias_s[pl.ds(j, 1), :]
        m_new = jnp.maximum(m, jnp.max(s, axis=0, keepdims=True))
        alpha = jnp.exp(m - m_new)
        p = jnp.exp(s - m_new)
        l = alpha * l + jnp.sum(p, axis=0, keepdims=True)
        acc_s[...] = alpha * acc_s[...] + weighted_values(j, p)
        return m_new, l

    m, l = lax.fori_loop(0, i, body, (m, l))
    outs = acc_s[...] / l
    o_ref[...] = jnp.concatenate([outs[:, h * TB:(h + 1) * TB].T for h in range(N_HEADS)], axis=1)


def _moba(qkv, cos, sin):
    B, S, _ = qkv.shape
    W = GROUP_WIDTH
    nb = S // MOBA_BLOCK
    return pl.pallas_call(
        _moba_kernel,
        grid=(B, S // MOBA_BLOCK),
        in_specs=[pl.BlockSpec((None, S, 3 * W), lambda b, i: (b, 0, 0)),
                  pl.BlockSpec((S, W), lambda b, i: (0, 0)),
                  pl.BlockSpec((S, W), lambda b, i: (0, 0))],
        out_specs=pl.BlockSpec((None, MOBA_BLOCK, W), lambda b, i: (b, i, 0)),
        out_shape=jax.ShapeDtypeStruct((B, S, W), F32),
        scratch_shapes=[pltpu.VMEM((N_HEADS, S, HEAD_DIM), BF16), pltpu.VMEM((nb, W, MOBA_BLOCK), BF16),
                        pltpu.VMEM((nb, W), F32), pltpu.VMEM((HEAD_DIM, N_HEADS * MOBA_BLOCK), F32),
                        pltpu.VMEM((nb, N_HEADS * MOBA_BLOCK), F32)],
        compiler_params=_cparams(2),
        name="moba",
    )(qkv, cos, sin)


def _memkv_kernel(m_ref, g_ref, w_ref, kt_ref, v_ref):
    mn = _rmsnorm(m_ref[...], g_ref[...]).astype(BF16)
    kv = _dot(mn, w_ref[...].astype(BF16))
    kt_ref[...] = kv[:, 0:D_MODEL].T.astype(BF16)
    v_ref[...] = kv[:, D_MODEL:2 * D_MODEL].astype(BF16)


def _memkv(mem, g, wkv, layer):
    B, M, _ = mem.shape
    return pl.pallas_call(
        _memkv_kernel,
        grid=(B,),
        in_specs=[pl.BlockSpec((None, M, D_MODEL), lambda b: (b, 0, 0)),
                  pl.BlockSpec((1, D_MODEL), lambda b: (0, 0)),
                  pl.BlockSpec((None, D_MODEL, 2 * D_MODEL), lambda b: (layer, 0, 0),
                               pipeline_mode=pl.Buffered(1))],
        out_specs=[pl.BlockSpec((None, D_MODEL, M), lambda b: (b, 0, 0)),
                   pl.BlockSpec((None, M, D_MODEL), lambda b: (b, 0, 0))],
        out_shape=[jax.ShapeDtypeStruct((B, D_MODEL, M), BF16), jax.ShapeDtypeStruct((B, M, D_MODEL), BF16)],
        compiler_params=_cparams(1),
        name="memkv",
    )(mem, g, wkv)


def _xattn_kernel(ya_ref, yb_ref, yc_ref, yd_ref, gg_ref, wout_ref, x_ref, g_ref, wq_ref, kt_ref, v_ref, wo_ref,
                  o_ref):
    yn = [_rmsnorm(y_ref[...], gg_ref[k:k + 1, :]).astype(BF16)
          for k, y_ref in enumerate((ya_ref, yb_ref, yc_ref, yd_ref))]
    x = x_ref[...] + _dot(jnp.concatenate(yn, axis=1), wout_ref[...].astype(BF16))
    h = _rmsnorm(x, g_ref[...]).astype(BF16)
    q = (_dot(h, wq_ref[...].astype(BF16)) * (XATTN_HEAD_DIM ** -0.5)).astype(BF16)
    heads = []
    for hd in range(XATTN_HEADS):
        cols = slice(hd * XATTN_HEAD_DIM, (hd + 1) * XATTN_HEAD_DIM)
        s = _dot(q[:, cols], kt_ref[cols, :])
        p = jnp.exp(s - jnp.max(s, axis=-1, keepdims=True))
        p = p / jnp.sum(p, axis=-1, keepdims=True)
        heads.append(_dot(p.astype(BF16), v_ref[:, cols]).astype(BF16))
    o_ref[...] = x + _dot(jnp.concatenate(heads, axis=1), wo_ref[...].astype(BF16))


def _xattn(ys, gg, w_out, x3d, g, wq, kt, v, wo, layer, tm=1024):
    B, S, _ = x3d.shape
    M = v.shape[1]
    W = GROUP_WIDTH
    weight = pl.BlockSpec((None, D_MODEL, D_MODEL), lambda b, i: (layer, 0, 0), pipeline_mode=pl.Buffered(1))
    return pl.pallas_call(
        _xattn_kernel,
        grid=(B, S // tm),
        in_specs=[pl.BlockSpec((None, tm, W), lambda b, i: (b, i, 0))] * 4
                 + [pl.BlockSpec((4, W), lambda b, i: (0, 0)),
                    weight,
                  pl.BlockSpec((None, tm, D_MODEL), lambda b, i: (b, i, 0)),
                  pl.BlockSpec((1, D_MODEL), lambda b, i: (0, 0)),
                  weight,
                  pl.BlockSpec((None, D_MODEL, M), lambda b, i: (b, 0, 0)),
                  pl.BlockSpec((None, M, D_MODEL), lambda b, i: (b, 0, 0)),
                  weight],
        out_specs=pl.BlockSpec((None, tm, D_MODEL), lambda b, i: (b, i, 0)),
        out_shape=jax.ShapeDtypeStruct((B, S, D_MODEL), F32),
        compiler_params=_cparams(2),
        name="xattn",
    )(*ys, gg, w_out, x3d, g, wq, kt, v, wo)


def _moe_kernel(x_ref, g_ref, wr_ref, br_ref, w1_ref, w3_ref, w2_ref, fg_ref, o_ref,
                t_s, comb_s, acc_s, xkeep_s, *, final_norm):
    tile = pl.program_id(0)
    grp = pl.program_id(1)
    tm = x_ref.shape[0]
    slot = lax.bitwise_and(tile, 1)

    def route(dst):
        t = _rmsnorm(x_ref[...], g_ref[...])
        t_s[dst] = t.astype(BF16)
        t_hi, t_lo = _split2(t)
        logits = (_dot(t_hi, wr_ref[0]) + _dot(t_hi, wr_ref[1]) + _dot(t_lo, wr_ref[0])) + br_ref[...]
        lt = logits.T
        gsl = SUBLANES * (MOE_EXPERTS // SUBLANES)
        g_row = lax.broadcasted_iota(jnp.int32, (SUBLANES, tm), 0).astype(F32)
        lg = jnp.where(g_row < float(MOE_GROUPS), lt[gsl:gsl + SUBLANES, :], NEG_BIG)
        gmax = jnp.max(lg, axis=0, keepdims=True)
        pg_top = 1.0 / jnp.sum(jnp.exp(lg - gmax), axis=0, keepdims=True)
        g_idx = jnp.min(jnp.where(lg == gmax, g_row, 1e9), axis=0, keepdims=True)
        e_row = lax.broadcasted_iota(jnp.int32, (MOE_EXPERTS, tm), 0).astype(F32)
        in_grp = jnp.floor(e_row * (1.0 / MOE_EPG)) == g_idx
        le = jnp.where(in_grp, lt[0:MOE_EXPERTS, :], NEG_BIG)
        e1 = jnp.max(le, axis=0, keepdims=True)
        i1 = jnp.min(jnp.where(in_grp & (le == e1), e_row, 1e9), axis=0, keepdims=True)
        le2 = jnp.where(e_row == i1, NEG_BIG, le)
        e2 = jnp.max(le2, axis=0, keepdims=True)
        i2 = jnp.min(jnp.where(in_grp & (e_row != i1) & (le2 == e2), e_row, 1e9), axis=0, keepdims=True)
        r2 = jnp.exp(e2 - e1)
        w_first = 1.0 / (1.0 + r2)
        w_second = r2 / (1.0 + r2)
        comb_t = pg_top * (jnp.where(e_row == i1, w_first, 0.0) + jnp.where(e_row == i2, w_second, 0.0))
        comb_s[dst] = jnp.concatenate([comb_t, jnp.zeros((LANES - MOE_EXPERTS, tm), F32)], axis=0).T

    def experts():
        t = t_s[slot]
        comb = comb_s[slot]
        lane = lax.broadcasted_iota(jnp.int32, (tm, LANES), 1)
        hids = []
        for e in range(MOE_EPG):
            n = grp * MOE_EPG + e
            c = jnp.sum(jnp.where(lane == n, comb, 0.0), axis=-1, keepdims=True)
            hid = _silu(_dot(t, w1_ref[e])) * _dot(t, w3_ref[e])
            hids.append((hid * c).astype(BF16))
        return _dot(jnp.concatenate(hids, axis=1), w2_ref[...].reshape(MOE_EPG * MOE_FF, D_MODEL))

    @pl.when((tile == 0) & (grp == 0))
    def _():
        route(0)

    @pl.when(grp == 0)
    def _():
        xkeep_s[...] = x_ref[...]
        acc_s[...] = experts()

    @pl.when((grp > 0) & (grp < MOE_GROUPS - 1))
    def _():
        acc_s[...] += experts()

    @pl.when(grp == MOE_GROUPS - 1)
    def _():
        y = xkeep_s[...] + (acc_s[...] + experts())
        route(1 - slot)
        if final_norm:
            y = _rmsnorm(y, fg_ref[...])
        o_ref[...] = y


def _moe(x2d, g, w_router, b_router, w1, w3, w2, layer, final_g, final_norm, tm=1024):
    T = x2d.shape[0]
    n_tiles = T // tm

    def x_window(i, e):
        return jnp.minimum(i + e // (MOE_GROUPS - 1), n_tiles - 1), 0

    return pl.pallas_call(
        functools.partial(_moe_kernel, final_norm=final_norm),
        grid=(n_tiles, MOE_GROUPS),
        in_specs=[pl.BlockSpec((tm, D_MODEL), x_window),
                  pl.BlockSpec((1, D_MODEL), lambda i, e: (0, 0)),
                  pl.BlockSpec((2, D_MODEL, LANES), lambda i, e: (0, 0, 0)),
                  pl.BlockSpec((1, LANES), lambda i, e: (0, 0)),
                  pl.BlockSpec((None, MOE_EPG, D_MODEL, MOE_FF), lambda i, e: (layer, e, 0, 0)),
                  pl.BlockSpec((None, MOE_EPG, D_MODEL, MOE_FF), lambda i, e: (layer, e, 0, 0)),
                  pl.BlockSpec((None, MOE_EPG, MOE_FF, D_MODEL), lambda i, e: (layer, e, 0, 0)),
                  pl.BlockSpec((1, D_MODEL), lambda i, e: (0, 0))],
        out_specs=pl.BlockSpec((tm, D_MODEL), lambda i, e: (i, 0)),
        out_shape=jax.ShapeDtypeStruct((T, D_MODEL), F32),
        scratch_shapes=[pltpu.VMEM((2, tm, D_MODEL), BF16), pltpu.VMEM((2, tm, LANES), F32),
                        pltpu.VMEM((tm, D_MODEL), F32), pltpu.VMEM((tm, D_MODEL), F32)],
        compiler_params=_cparams(2),
        name="moe",
    )(x2d, g, w_router, b_router, w1, w3, w2, final_g)


def _pad_lanes(v, width=LANES):
    return jnp.pad(v, (0, width - v.shape[0]))[None, :]


def _block_diag(w):
    H, n, _ = w.shape
    eye = jnp.eye(H, dtype=w.dtype)
    return (eye[:, None, :, None] * w[:, :, None, :]).reshape(H * n, H * n)


def _rope_tables(S):
    half = HEAD_DIM // 2
    inv_freq = ROPE_THETA ** (-jnp.arange(half, dtype=F32) / half)
    ang = jnp.arange(S, dtype=F32)[:, None] * inv_freq[None, :]
    reps = GROUP_WIDTH // half
    return jnp.tile(jnp.cos(ang), (1, reps)), jnp.tile(jnp.sin(ang), (1, reps))


def kernel(x, mem, mix_norm_g, w_in, lru_conv_w, lru_conv_b, lru_wr, lru_br, lru_wi, lru_bi, lru_lambda, ssm_conv_w, ssm_conv_b, ssm_dt_bias, ssm_a_log, ssm_d, group_norm_g, w_out, xattn_norm_g, mem_norm_g, xattn_wq, xattn_wkv, xattn_wo, ffn_norm_g, router_group_w, router_group_b, router_expert_w, router_expert_b, expert_w1, expert_w3, expert_w2, final_norm_g):
    B, S, D = x.shape
    T = B * S
    depth = w_in.shape[0]
    W = GROUP_WIDTH
    cos, sin = _rope_tables(S)
    x2d = x.reshape(T, D)
    w1_bf, w3_bf, w2_bf = expert_w1.astype(BF16), expert_w3.astype(BF16), expert_w2.astype(BF16)
    for l in range(depth):
        lru_xg, sb_qkv, ssm_z, ssm_xbc, ssm_dt, mb_qkv = _inproj(x2d, mix_norm_g[l][None, :], w_in, l)

        w_bd = jnp.concatenate([_block_diag(lru_wr[l]), _block_diag(lru_wi[l])], axis=1).astype(BF16)
        b_ri = jnp.concatenate([lru_br[l], lru_bi[l]])[None, :]
        y_a = _lru(lru_xg.reshape(B, S, 2 * W), lru_conv_w[l], lru_conv_b[l][None, :], w_bd, b_ri,
                   lru_lambda[l][None, :])
        y_b = _sb_attention(sb_qkv.reshape(B, S, 3 * W))
        y_c = _ssd(ssm_z.reshape(B, S, W), ssm_xbc.reshape(B, S, 3 * W), ssm_dt.reshape(B, S, LANES),
                   ssm_conv_w[l], ssm_conv_b[l][None, :], _pad_lanes(ssm_dt_bias[l]), _pad_lanes(ssm_a_log[l]),
                   jnp.repeat(ssm_d[l], HEAD_DIM)[None, :])
        y_d = _moba(mb_qkv.reshape(B, S, 3 * W), cos, sin)
        mem_kt, mem_v = _memkv(mem, mem_norm_g[l][None, :], xattn_wkv, l)
        x2d = _xattn([y_a, y_b, y_c, y_d], group_norm_g[l].reshape(4, W), w_out, x2d.reshape(B, S, D),
                     xattn_norm_g[l][None, :], xattn_wq, mem_kt, mem_v, xattn_wo, l).reshape(T, D)

        w_r = jnp.pad(jnp.concatenate([router_expert_w[l], router_group_w[l]], axis=1),
                      ((0, 0), (0, LANES - MOE_EXPERTS - MOE_GROUPS)))
        w_r_hi = w_r.astype(BF16)
        w_r_lo = (w_r - w_r_hi.astype(F32)).astype(BF16)
        b_r = _pad_lanes(jnp.concatenate([router_expert_b[l], router_group_b[l]]))
        x2d = _moe(x2d, ffn_norm_g[l][None, :], jnp.stack([w_r_hi, w_r_lo]), b_r, w1_bf, w3_bf, w2_bf, l,
                   final_norm_g[None, :], final_norm=(l == depth - 1))
    return x2d.reshape(B, S, D)
```
